```python
import math
import jax, jax.numpy as jnp
from jax import lax
import numpy as np

D_MODEL = 1024
BATCH = 32
SEQ = 256
DEPTH = 2
DEC_BATCH = 4
DEC_SEQ = 1024
PAST_LEN = 512

GRID_W = 64
HEAD_DIM = 64
MIX_WIDTH = D_MODEL
DIFF_WIDTH = MIX_WIDTH // 2
GQA_WIDTH = MIX_WIDTH - DIFF_WIDTH
DIFF_HEADS = DIFF_WIDTH // (2 * HEAD_DIM)
GQA_Q_HEADS = GQA_WIDTH // HEAD_DIM
GQA_KV_HEADS = 2
GQA_REP = GQA_Q_HEADS // GQA_KV_HEADS
DIFF_QK_COLS = 2 * DIFF_HEADS * HEAD_DIM
DIFF_V_COLS = DIFF_HEADS * 2 * HEAD_DIM
GQA_Q_COLS = GQA_Q_HEADS * HEAD_DIM
GQA_KV_COLS = GQA_KV_HEADS * HEAD_DIM
IN_COLS = 2 * DIFF_QK_COLS + DIFF_V_COLS + GQA_Q_COLS + 2 * GQA_KV_COLS
IN_SPLITS = (DIFF_QK_COLS,
             2 * DIFF_QK_COLS,
             2 * DIFF_QK_COLS + DIFF_V_COLS,
             2 * DIFF_QK_COLS + DIFF_V_COLS + GQA_Q_COLS,
             2 * DIFF_QK_COLS + DIFF_V_COLS + GQA_Q_COLS + GQA_KV_COLS)
FFN_HIDDEN = ((8 * D_MODEL + 3 * 256 - 1) // (3 * 256)) * 256
Q_BLOCK = 128
ROPE_THETA = 10000.0
EPS = 1e-6

kernel_name = "hybrid_diffattn_gqa_dit_step"


def rms_norm(x, gain):
    xf = x.astype(jnp.float32)
    y = xf * lax.rsqrt(jnp.mean(xf * xf, axis=-1, keepdims=True) + EPS)
    return (y * gain.astype(jnp.float32)).astype(x.dtype)


def axial_rope_angles(n_tokens):
    t = jnp.arange(n_tokens, dtype=jnp.int32)
    row = (t // GRID_W).astype(jnp.float32)
    col = (t % GRID_W).astype(jnp.float32)
    axis_dim = HEAD_DIM // 2
    freqs = ROPE_THETA ** (-jnp.arange(0, axis_dim, 2, dtype=jnp.float32) / axis_dim)
    ang = jnp.concatenate([row[:, None] * freqs, col[:, None] * freqs], axis=-1)
    return jnp.cos(ang), jnp.sin(ang)


def apply_rope(x, cos, sin):
    half = HEAD_DIM // 2
    xf = x.astype(jnp.float32)
    x1, x2 = xf[..., :half], xf[..., half:]
    shape = (1, cos.shape[0]) + (1,) * (x.ndim - 3) + (half,)
    c = cos.reshape(shape)
    s = sin.reshape(shape)
    return jnp.concatenate([x1 * c - x2 * s, x2 * c + x1 * s], axis=-1).astype(x.dtype)


def sweep_query_blocks(attend, q):
    b, t = q.shape[0], q.shape[1]
    nb = t // Q_BLOCK
    qb = jnp.moveaxis(q.reshape((b, nb, Q_BLOCK) + q.shape[2:]), 1, 0)
    out = lax.map(attend, qb)
    out = jnp.moveaxis(out, 0, 1)
    return out.reshape((b, t) + out.shape[3:])


def diff_attention(q, k, v, lam, lam_init, subln_w):
    scale = HEAD_DIM ** -0.5

    def attend(qb):
        s = jnp.einsum('bqhcd,bshcd->bhcqs', qb, k).astype(jnp.float32) * scale
        p = jax.nn.softmax(s, axis=-1)
        w = p[:, :, 0] - lam * p[:, :, 1]
        return jnp.einsum('bhqs,bshe->bqhe', w.astype(v.dtype), v)

    o = sweep_query_blocks(attend, q)
    return rms_norm(o, subln_w) * (1.0 - lam_init)


def gqa_attention(q, k, v):
    scale = HEAD_DIM ** -0.5

    def attend(qb):
        s = jnp.einsum('bqgrd,bsgd->bgrqs', qb, k).astype(jnp.float32) * scale
        p = jax.nn.softmax(s, axis=-1)
        return jnp.einsum('bgrqs,bsge->bqgre', p.astype(v.dtype), v)

    return sweep_query_blocks(attend, q)


def modulation(cond, w_mod, b_mod):
    m = jax.nn.silu(cond) @ w_mod + b_mod
    return jnp.split(m[:, None, :], 6, axis=-1)


def trunk_layer(x, cond, rope, ctx, layer_idx,
                w_mod, b_mod, norm_attn, w_in, q_norm_a, k_norm_a,
                lambda_q1, lambda_k1, lambda_q2, lambda_k2, subln,
                q_norm_b, k_norm_b, w_out, norm_ffn, w_gate_up, w_down):
    b, t, _ = x.shape
    sh_a, sc_a, g_a, sh_f, sc_f, g_f = modulation(cond, w_mod, b_mod)

    h = rms_norm(x, norm_attn) * (1.0 + sc_a) + sh_a
    proj = h @ w_in
    qa, ka, va, qb, kb, vb = jnp.split(proj, IN_SPLITS, axis=-1)
    qa = rms_norm(qa.reshape(b, t, DIFF_HEADS, 2, HEAD_DIM), q_norm_a)
    ka = rms_norm(ka.reshape(b, t, DIFF_HEADS, 2, HEAD_DIM), k_norm_a)
    va = va.reshape(b, t, DIFF_HEADS, 2 * HEAD_DIM)
    qb = rms_norm(qb.reshape(b, t, GQA_KV_HEADS, GQA_REP, HEAD_DIM), q_norm_b)
    kb = rms_norm(kb.reshape(b, t, GQA_KV_HEADS, HEAD_DIM), k_norm_b)
    vb = vb.reshape(b, t, GQA_KV_HEADS, HEAD_DIM)
    own_ctx = (ka, va, kb, vb)

    if ctx is None:
        keys_a, vals_a, keys_b, vals_b = ka, va, kb, vb
    else:
        cos, sin = rope
        qa, ka = apply_rope(qa, cos, sin), apply_rope(ka, cos, sin)
        qb, kb = apply_rope(qb, cos, sin), apply_rope(kb, cos, sin)
        c_ka, c_va, c_kb, c_vb = ctx
        keys_a = jnp.concatenate([ka, c_ka.astype(ka.dtype)], axis=1)
        vals_a = jnp.concatenate([va, c_va.astype(va.dtype)], axis=1)
        keys_b = jnp.concatenate([kb, c_kb.astype(kb.dtype)], axis=1)
        vals_b = jnp.concatenate([vb, c_vb.astype(vb.dtype)], axis=1)

    lam_init = 0.8 - 0.6 * math.exp(-0.3 * layer_idx)
    lam = (jnp.exp(jnp.sum(lambda_q1.astype(jnp.float32) * lambda_k1.astype(jnp.float32)))
           - jnp.exp(jnp.sum(lambda_q2.astype(jnp.float32) * lambda_k2.astype(jnp.float32)))
           + lam_init)
    out_a = diff_attention(qa, keys_a, vals_a, lam, lam_init, subln).reshape(b, t, DIFF_WIDTH)
    out_b = gqa_attention(qb, keys_b, vals_b).reshape(b, t, GQA_WIDTH)
    mix = jnp.concatenate([out_a, out_b], axis=-1) @ w_out
    x = x + g_a * mix

    h = rms_norm(x, norm_ffn) * (1.0 + sc_f) + sh_f
    gate, up = jnp.split(h @ w_gate_up, 2, axis=-1)
    x = x + g_f * ((jax.nn.silu(gate) * up) @ w_down)
    return x, own_ctx


def setup_inputs(seed: int = 0) -> dict:
    key = jax.random.key(seed)
    ks = jax.random.split(key, 32)

    def nrm(k, shape, s):
        return jax.random.normal(k, shape, jnp.float32) * s

    d = D_MODEL
    return {
        "x_prompt": nrm(ks[0], (BATCH, SEQ, d), 1.0),
        "x_sample": nrm(ks[1], (DEC_BATCH, DEC_SEQ, d), 1.0),
        "cache_diff_k": nrm(ks[2], (DEC_BATCH, DEPTH, PAST_LEN, DIFF_HEADS, 2, HEAD_DIM), 1.0),
        "cache_diff_v": nrm(ks[3], (DEC_BATCH, DEPTH, PAST_LEN, DIFF_HEADS, 2 * HEAD_DIM), 1.0),
        "cache_gqa_k": nrm(ks[4], (DEC_BATCH, DEPTH, PAST_LEN, GQA_KV_HEADS, HEAD_DIM), 1.0),
        "cache_gqa_v": nrm(ks[5], (DEC_BATCH, DEPTH, PAST_LEN, GQA_KV_HEADS, HEAD_DIM), 1.0),
        "c": nrm(ks[6], (DEC_BATCH, d), 1.0),
        "c_ctx": nrm(ks[7], (d,), 1.0),
        "w_mod": nrm(ks[8], (DEPTH, d, 6 * d), 0.5 * d ** -0.5),
        "b_mod": nrm(ks[9], (DEPTH, 6 * d), 0.02),
        "norm_attn": 1.0 + nrm(ks[10], (DEPTH, d), 0.02),
        "w_in": nrm(ks[11], (DEPTH, d, IN_COLS), d ** -0.5),
        "q_norm_a": 1.0 + nrm(ks[12], (DEPTH, HEAD_DIM), 0.02),
        "k_norm_a": 1.0 + nrm(ks[13], (DEPTH, HEAD_DIM), 0.02),
        "lambda_q1": nrm(ks[14], (DEPTH, HEAD_DIM), 0.1),
        "lambda_k1": nrm(ks[15], (DEPTH, HEAD_DIM), 0.1),
        "lambda_q2": nrm(ks[16], (DEPTH, HEAD_DIM), 0.1),
        "lambda_k2": nrm(ks[17], (DEPTH, HEAD_DIM), 0.1),
        "subln": 1.0 + nrm(ks[18], (DEPTH, 2 * HEAD_DIM), 0.02),
        "q_norm_b": 1.0 + nrm(ks[19], (DEPTH, HEAD_DIM), 0.02),
        "k_norm_b": 1.0 + nrm(ks[20], (DEPTH, HEAD_DIM), 0.02),
        "w_out": nrm(ks[21], (DEPTH, MIX_WIDTH, d), MIX_WIDTH ** -0.5),
        "norm_ffn": 1.0 + nrm(ks[22], (DEPTH, d), 0.02),
        "w_gate_up": nrm(ks[23], (DEPTH, d, 2 * FFN_HIDDEN), d ** -0.5),
        "w_down": nrm(ks[24], (DEPTH, FFN_HIDDEN, d), FFN_HIDDEN ** -0.5),
    }


def reference(x_prompt, x_sample, cache_diff_k, cache_diff_v, cache_gqa_k, cache_gqa_v,
              c, c_ctx, w_mod, b_mod, norm_attn, w_in, q_norm_a, k_norm_a,
              lambda_q1, lambda_k1, lambda_q2, lambda_k2, subln, q_norm_b, k_norm_b,
              w_out, norm_ffn, w_gate_up, w_down):
    rope = axial_rope_angles(x_sample.shape[1])
    cond_ctx = c_ctx[None, :]

    yp = x_prompt
    ys = x_sample
    dk, dv, gk, gv = [], [], [], []
    for l in range(DEPTH):
        lw = (w_mod[l], b_mod[l], norm_attn[l], w_in[l], q_norm_a[l], k_norm_a[l],
              lambda_q1[l], lambda_k1[l], lambda_q2[l], lambda_k2[l], subln[l],
              q_norm_b[l], k_norm_b[l], w_out[l], norm_ffn[l], w_gate_up[l], w_down[l])
        yp, (ka, va, kb, vb) = trunk_layer(yp, cond_ctx, None, None, l, *lw)
        dk.append(ka)
        dv.append(va)
        gk.append(kb)
        gv.append(vb)
        ctx = (cache_diff_k[:, l], cache_diff_v[:, l], cache_gqa_k[:, l], cache_gqa_v[:, l])
        ys, _ = trunk_layer(ys, c, rope, ctx, l, *lw)

    new_diff_k = jnp.stack(dk, axis=1)
    new_diff_v = jnp.stack(dv, axis=1)
    new_gqa_k = jnp.stack(gk, axis=1)
    new_gqa_v = jnp.stack(gv, axis=1)
    return (yp, ys, new_diff_k, new_diff_v, new_gqa_k, new_gqa_v)
```

```python
import functools
import math

import jax
import jax.numpy as jnp
from jax import lax
from jax.experimental import pallas as pl
from jax.experimental.pallas import tpu as pltpu

F32 = jnp.float32
BF16 = jnp.bfloat16

D_MODEL = 1024
HEAD_DIM = 64
GRID_W = 64
DIFF_HEADS = 4
GQA_KV_HEADS = 2
FFN_HIDDEN = 2816
IN_COLS = 2304
ROPE_THETA = 10000.0
EPS = 1e-6
LANES = 128
MOD_ROWS = 8
TOKEN_TILE = 256
Q_TILE = 256
VMEM_LIMIT = 56 * 1024 * 1024


def _dot(a, b):
    return jnp.dot(a, b, preferred_element_type=F32)


def _dot_nt(a, b):
    return lax.dot_general(a, b, (((1,), (1,)), ((), ())), preferred_element_type=F32)


def _rms(x, gain):
    ms = jnp.mean(x * x, axis=-1, keepdims=True)
    return x * lax.rsqrt(ms + EPS) * gain


def _const_spec(shape):
    n = len(shape)
    return pl.BlockSpec(shape, lambda *_: (0,) * n, pipeline_mode=pl.Buffered(1))


def _mod_kernel(cond_ref, w_ref, b_ref, o_ref):
    c = cond_ref[...]
    s = c * jax.nn.sigmoid(c)
    o_ref[...] = _dot(s.astype(BF16), w_ref[...].astype(BF16)) + b_ref[...]


def _modulation(cond, w_mod, b_mod):
    depth = w_mod.shape[0]
    tn = 1536
    return pl.pallas_call(
        _mod_kernel,
        grid=(depth, 6 * D_MODEL // tn),
        in_specs=[
            pl.BlockSpec((MOD_ROWS, D_MODEL), lambda l, j: (0, 0)),
            pl.BlockSpec((None, D_MODEL, tn), lambda l, j: (l, 0, j)),
            pl.BlockSpec((None, 1, tn), lambda l, j: (l, 0, j)),
        ],
        out_specs=pl.BlockSpec((None, MOD_ROWS, tn), lambda l, j: (l, 0, j)),
        out_shape=jax.ShapeDtypeStruct((depth, MOD_ROWS, 6 * D_MODEL), F32),
        compiler_params=pltpu.CompilerParams(
            dimension_semantics=("arbitrary", "arbitrary"), vmem_limit_bytes=VMEM_LIMIT),
        name="modulation",
    )(cond, w_mod, b_mod.reshape(depth, 1, 6 * D_MODEL))


def _head_norm(x, seg_ones, gain):
    sq = x * x
    hi = sq.astype(BF16)
    lo = (sq - hi.astype(F32)).astype(BF16)
    ss = _dot(hi, seg_ones) + _dot(lo, seg_ones)
    return x * lax.rsqrt(ss * (1.0 / HEAD_DIM) + EPS) * gain


def _swap_halves(x, first_half):
    return jnp.where(first_half, pltpu.roll(x, 96, 1), pltpu.roll(x, 32, 1))


def _rope(x, cos, sin, first_half):
    out = []
    for j in range(x.shape[1] // LANES):
        xj = x[:, j * LANES:(j + 1) * LANES]
        out.append(xj * cos + _swap_halves(xj, first_half) * sin)
    return out[0] if len(out) == 1 else jnp.concatenate(out, axis=1)


def _dup_heads(x, low):
    r = pltpu.roll(x, HEAD_DIM, 1)
    return jnp.concatenate([jnp.where(low, x, r), jnp.where(low, r, x)], axis=1)


def _qkv_kernel(latent, *refs):
    if latent:
        (x_ref, mod_ref, na_ref, w_ref, seg_ref, gqa_ref, gka_ref, gqb_ref, gkb_ref,
         cos_ref, sin_ref, qa_o, ka_o, va_o, qb_o, kb_o, vb_o) = refs
    else:
        (x_ref, mod_ref, na_ref, w_ref, seg_ref, gqa_ref, gka_ref, gqb_ref, gkb_ref,
         qa_o, ka_o, va_o, qb_o, kb_o, vb_o, kaf_o, vaf_o, kbf_o, vbf_o) = refs
    x = x_ref[...]
    sh = mod_ref[:, 0:D_MODEL]
    sc = mod_ref[:, D_MODEL:2 * D_MODEL]
    h = _rms(x, na_ref[...]) * (1.0 + sc) + sh
    proj = _dot(h.astype(BF16), w_ref[...])
    seg = seg_ref[...]
    lane = lax.broadcasted_iota(jnp.int32, (1, LANES), 1)
    low = lane < HEAD_DIM
    first_half = (lane % HEAD_DIM) < (HEAD_DIM // 2)

    def qk(lo_col, width, gain_ref):
        out = []
        for c0 in range(0, width, 256):
            w = min(256, width - c0)
            out.append(_head_norm(proj[:, lo_col + c0:lo_col + c0 + w], seg[:w, :w], gain_ref[:, :w]))
        y = out[0] if len(out) == 1 else jnp.concatenate(out, axis=1)
        if latent:
            y = _rope(y, cos_ref[...], sin_ref[...], first_half)
        return y

    qa = qk(0, 512, gqa_ref)
    ka = qk(512, 512, gka_ref)
    va = proj[:, 1024:1536]
    qb = qk(1536, 512, gqb_ref)
    kb = qk(2048, 128, gkb_ref)
    vb = proj[:, 2176:2304]
    scale = HEAD_DIM ** -0.5
    qa_o[...] = (qa * scale).astype(BF16)
    ka_o[...] = ka.astype(BF16)
    va_o[...] = va.astype(BF16)
    qb_o[...] = (qb * scale).astype(BF16)
    kb_o[...] = _dup_heads(kb, low).astype(BF16)
    vb_o[...] = _dup_heads(vb, low).astype(BF16)
    if not latent:
        kaf_o[...] = ka
        vaf_o[...] = va
        kbf_o[...] = kb
        vbf_o[...] = vb


def _qkv(x, mod3, layer, latent, tiles_per_request, norm_attn, w_in, seg_ones, gains, rope):
    n = x.shape[0]
    tm = TOKEN_TILE
    if latent:
        mod_map = lambda i: (1 + i // tiles_per_request, 0, 0)
    else:
        mod_map = lambda i: (0, 0, 0)
    tok = lambda w: pl.BlockSpec((tm, w), lambda i: (i, 0))
    in_specs = [
        tok(D_MODEL),
        pl.BlockSpec((None, 1, 6 * D_MODEL), mod_map),
        _const_spec((1, D_MODEL)),
        _const_spec((D_MODEL, IN_COLS)),
        _const_spec((256, 256)),
        _const_spec((1, 256)), _const_spec((1, 256)), _const_spec((1, 256)), _const_spec((1, 256)),
    ]
    args = [x, mod3[layer], norm_attn, w_in, seg_ones, *gains]
    out_specs = [tok(512), tok(512), tok(512), tok(512), tok(256), tok(256)]
    out_shape = [jax.ShapeDtypeStruct((n, w), BF16) for w in (512, 512, 512, 512, 256, 256)]
    if latent:
        rope_spec = pl.BlockSpec((tm, LANES), lambda i: (i % tiles_per_request, 0))
        in_specs += [rope_spec, rope_spec]
        args += list(rope)
    else:
        out_specs += [tok(512), tok(512), tok(128), tok(128)]
        out_shape += [jax.ShapeDtypeStruct((n, w), F32) for w in (512, 512, 128, 128)]
    return pl.pallas_call(
        functools.partial(_qkv_kernel, latent),
        grid=(n // tm,),
        in_specs=in_specs,
        out_specs=out_specs,
        out_shape=out_shape,
        compiler_params=pltpu.CompilerParams(
            dimension_semantics=("arbitrary",), vmem_limit_bytes=VMEM_LIMIT),
        name="qkv_latent" if latent else "qkv_context",
    )(*args)


def _stack_halves(q, low):
    zero = jnp.zeros_like(q)
    return jnp.concatenate([jnp.where(low, q, zero), jnp.where(low, zero, q)], axis=0)


def _softmax_parts(qs, ks):
    ss = [_dot_nt(qs, k) for k in ks]
    m = ss[0].max(axis=-1, keepdims=True)
    for s in ss[1:]:
        m = jnp.maximum(m, s.max(axis=-1, keepdims=True))
    es = [jnp.exp(s - m) for s in ss]
    l = es[0].sum(axis=-1, keepdims=True)
    for e in es[1:]:
        l = l + e.sum(axis=-1, keepdims=True)
    return es, l


def _diff_head(q, ks, vs, lam, low):
    tq = q.shape[0]
    es, l = _softmax_parts(_stack_halves(q, low), ks)
    r = 1.0 / l
    r0 = r[:tq]
    r1 = r[tq:] * lam
    o = None
    for e, v in zip(es, vs):
        w = e[:tq] * r0 - e[tq:] * r1
        t = _dot(w.astype(BF16), v)
        o = t if o is None else o + t
    return o


def _gqa_pair(q, ks, vs, low):
    tq = q.shape[0]
    es, l = _softmax_parts(_stack_halves(q, low), ks)
    o = None
    for e, v in zip(es, vs):
        t = _dot(e.astype(BF16), v)
        o = t if o is None else o + t
    o = o * (1.0 / l)
    return jnp.where(low, o[:tq], o[tq:])


def _attn_kernel(n_kv, lam_init, *refs):
    qa_ref, qb_ref = refs[0], refs[1]
    kv = refs[2:2 + 4 * n_kv]
    lq1, lk1, lq2, lk2, subln_ref, o_ref = refs[2 + 4 * n_kv:]
    ka_refs, va_refs, kb_refs, vb_refs = kv[0::4], kv[1::4], kv[2::4], kv[3::4]
    lane = lax.broadcasted_iota(jnp.int32, (1, LANES), 1)
    low = lane < HEAD_DIM
    lam = (jnp.exp(jnp.sum(lq1[...] * lk1[...], axis=-1, keepdims=True))
           - jnp.exp(jnp.sum(lq2[...] * lk2[...], axis=-1, keepdims=True)) + lam_init)
    subln = subln_ref[...]
    for h in range(DIFF_HEADS):
        cs = slice(h * LANES, (h + 1) * LANES)
        o = _diff_head(qa_ref[:, cs], [r[:, cs] for r in ka_refs], [r[:, cs] for r in va_refs], lam, low)
        o_ref[:, cs] = (_rms(o, subln) * (1.0 - lam_init)).astype(o_ref.dtype)
    for g in range(GQA_KV_HEADS):
        cs = slice(g * LANES, (g + 1) * LANES)
        for rp in range(2):
            qs = slice((2 * g + rp) * LANES, (2 * g + rp + 1) * LANES)
            o = _gqa_pair(qb_ref[:, qs], [r[:, cs] for r in kb_refs], [r[:, cs] for r in vb_refs], low)
            o_ref[:, 512 + qs.start:512 + qs.stop] = o.astype(o_ref.dtype)


def _attention(layer, qa, qb, own, cache, seq, lam_vecs, subln):
    n = qa.shape[0]
    requests = n // seq
    tq = min(Q_TILE, seq)
    nq = seq // tq
    lam_init = 0.8 - 0.6 * math.exp(-0.3 * layer)
    qspec = pl.BlockSpec((tq, 512), lambda b, i: (b * nq + i, 0))
    in_specs = [qspec, qspec]
    args = [qa, qb]
    for a in own:
        in_specs.append(pl.BlockSpec((seq, a.shape[1]), lambda b, i: (b, 0)))
        args.append(a)
    n_kv = 1
    if cache is not None:
        n_kv = 2
        for a in cache:
            in_specs.append(pl.BlockSpec((None, None) + a.shape[2:], lambda b, i: (b, layer, 0, 0)))
            args.append(a)
    in_specs += [_const_spec((1, HEAD_DIM))] * 4 + [_const_spec((1, 2 * HEAD_DIM))]
    args += list(lam_vecs) + [subln]
    return pl.pallas_call(
        functools.partial(_attn_kernel, n_kv, lam_init),
        grid=(requests, nq),
        in_specs=in_specs,
        out_specs=pl.BlockSpec((tq, D_MODEL), lambda b, i: (b * nq + i, 0)),
        out_shape=jax.ShapeDtypeStruct((n, D_MODEL), BF16),
        compiler_params=pltpu.CompilerParams(
            dimension_semantics=("arbitrary", "arbitrary"), vmem_limit_bytes=VMEM_LIMIT),
        name="attn_latent" if cache is not None else "attn_context",
    )(*args)


def _ffn_kernel(x_ref, mix_ref, mod_ref, wo_ref, nf_ref, wgu_ref, wd_ref, o_ref):
    g_a = mod_ref[:, 2 * D_MODEL:3 * D_MODEL]
    sh_f = mod_ref[:, 3 * D_MODEL:4 * D_MODEL]
    sc_f = mod_ref[:, 4 * D_MODEL:5 * D_MODEL]
    g_f = mod_ref[:, 5 * D_MODEL:6 * D_MODEL]
    x = x_ref[...] + g_a * _dot(mix_ref[...], wo_ref[...])
    h = (_rms(x, nf_ref[...]) * (1.0 + sc_f) + sh_f).astype(BF16)
    gu = _dot(h, wgu_ref[...])
    gate = gu[:, :FFN_HIDDEN]
    up = gu[:, FFN_HIDDEN:]
    act = (gate * jax.nn.sigmoid(gate) * up).astype(BF16)
    o_ref[...] = x + g_f * _dot(act, wd_ref[...])


def _ffn(x, mix, mod3, layer, latent, tiles_per_request, w_out, norm_ffn, w_gu, w_down):
    n = x.shape[0]
    tm = TOKEN_TILE
    if latent:
        mod_map = lambda i: (1 + i // tiles_per_request, 0, 0)
    else:
        mod_map = lambda i: (0, 0, 0)
    return pl.pallas_call(
        _ffn_kernel,
        grid=(n // tm,),
        in_specs=[
            pl.BlockSpec((tm, D_MODEL), lambda i: (i, 0)),
            pl.BlockSpec((tm, D_MODEL), lambda i: (i, 0)),
            pl.BlockSpec((None, 1, 6 * D_MODEL), mod_map),
            _const_spec((D_MODEL, D_MODEL)),
            _const_spec((1, D_MODEL)),
            _const_spec((D_MODEL, 2 * FFN_HIDDEN)),
            _const_spec((FFN_HIDDEN, D_MODEL)),
        ],
        out_specs=pl.BlockSpec((tm, D_MODEL), lambda i: (i, 0)),
        out_shape=jax.ShapeDtypeStruct((n, D_MODEL), F32),
        compiler_params=pltpu.CompilerParams(
            dimension_semantics=("arbitrary",), vmem_limit_bytes=VMEM_LIMIT),
        name="ffn_latent" if latent else "ffn_context",
    )(x, mix, mod3[layer], w_out, norm_ffn, w_gu, w_down)


def _rope_tables(n_tokens):
    t = jnp.arange(n_tokens, dtype=jnp.int32)
    row = (t // GRID_W).astype(F32)
    col = (t % GRID_W).astype(F32)
    axis_dim = HEAD_DIM // 2
    freqs = ROPE_THETA ** (-jnp.arange(0, axis_dim, 2, dtype=F32) / axis_dim)
    ang = jnp.concatenate([row[:, None] * freqs, col[:, None] * freqs], axis=-1)
    c, s = jnp.cos(ang), jnp.sin(ang)
    return jnp.concatenate([c, c, c, c], axis=-1), jnp.concatenate([-s, s, -s, s], axis=-1)


def kernel(x_prompt, x_sample, cache_diff_k, cache_diff_v, cache_gqa_k, cache_gqa_v, c, c_ctx, w_mod, b_mod, norm_attn, w_in, q_norm_a, k_norm_a, lambda_q1, lambda_k1, lambda_q2, lambda_k2, subln, q_norm_b, k_norm_b, w_out, norm_ffn, w_gate_up, w_down):
    batch, seq, d = x_prompt.shape
    dec_batch, dec_seq, _ = x_sample.shape
    depth = w_mod.shape[0]
    past = cache_diff_k.shape[2]

    cond = jnp.zeros((MOD_ROWS, d), F32).at[0].set(c_ctx).at[1:1 + dec_batch].set(c)
    mod = _modulation(cond, w_mod, b_mod)
    mod3 = mod.reshape(depth, MOD_ROWS, 1, 6 * d)

    w_in_b = w_in.astype(BF16)
    w_out_b = w_out.astype(BF16)
    w_gu_b = w_gate_up.astype(BF16)
    w_down_b = w_down.astype(BF16)
    seg_ones = jnp.kron(jnp.eye(256 // HEAD_DIM, dtype=F32), jnp.ones((HEAD_DIM, HEAD_DIM), F32)).astype(BF16)
    rope = _rope_tables(dec_seq)

    def dup(a):
        g0, g1 = a[..., 0, :], a[..., 1, :]
        return jnp.concatenate([g0, g0, g1, g1], axis=-1)

    cache = (cache_diff_k.reshape(dec_batch, depth, past, 512).astype(BF16),
             cache_diff_v.reshape(dec_batch, depth, past, 512).astype(BF16),
             dup(cache_gqa_k).astype(BF16),
             dup(cache_gqa_v).astype(BF16))

    yp = x_prompt.reshape(batch * seq, d)
    ys = x_sample.reshape(dec_batch * dec_seq, d)
    ctx_tiles = seq // TOKEN_TILE
    lat_tiles = dec_seq // TOKEN_TILE
    dk, dv, gk, gv = [], [], [], []
    for l in range(depth):
        tile4 = lambda g: jnp.tile(g[l], 256 // HEAD_DIM)[None]
        gains = (tile4(q_norm_a), tile4(k_norm_a), tile4(q_norm_b), tile4(k_norm_b))
        lam_vecs = (lambda_q1[l][None], lambda_k1[l][None], lambda_q2[l][None], lambda_k2[l][None])
        na, nf, sl = norm_attn[l][None], norm_ffn[l][None], subln[l][None]

        qa, ka, va, qb, kb, vb, kaf, vaf, kbf, vbf = _qkv(
            yp, mod3, l, False, ctx_tiles, na, w_in_b[l], seg_ones, gains, None)
        dk.append(kaf)
        dv.append(vaf)
        gk.append(kbf)
        gv.append(vbf)
        mix = _attention(l, qa, qb, (ka, va, kb, vb), None, seq, lam_vecs, sl)
        yp = _ffn(yp, mix, mod3, l, False, ctx_tiles, w_out_b[l], nf, w_gu_b[l], w_down_b[l])

        qa, ka, va, qb, kb, vb = _qkv(
            ys, mod3, l, True, lat_tiles, na, w_in_b[l], seg_ones, gains, rope)
        mix = _attention(l, qa, qb, (ka, va, kb, vb), cache, dec_seq, lam_vecs, sl)
        ys = _ffn(ys, mix, mod3, l, True, lat_tiles, w_out_b[l], nf, w_gu_b[l], w_down_b[l])

    def stack(parts, tail):
        return jnp.stack([p.reshape(batch, seq, *tail) for p in parts], axis=1)

    return (yp.reshape(batch, seq, d),
            ys.reshape(dec_batch, dec_seq, d),
            stack(dk, (DIFF_HEADS, 2, HEAD_DIM)),
            stack(dv, (DIFF_HEADS, 2 * HEAD_DIM)),
            stack(gk, (GQA_KV_HEADS, HEAD_DIM)),
            stack(gv, (GQA_KV_HEADS, HEAD_DIM)))
```

```python
import functools
import math

import jax
import jax.numpy as jnp
from jax import lax
from jax.experimental import pallas as pl
from jax.experimental.pallas import tpu as pltpu

F32 = jnp.float32
BF16 = jnp.bfloat16

D_MODEL = 1024
HEAD_DIM = 64
GRID_W = 64
DIFF_HEADS = 4
GQA_KV_HEADS = 2
FFN_HIDDEN = 2816
IN_COLS = 2304
ROPE_THETA = 10000.0
EPS = 1e-6
LANES = 128
MOD_ROWS = 8
TOKEN_TILE = 256
Q_TILE = 256
VMEM_LIMIT = 56 * 1024 * 1024


def _dot(a, b):
    return jnp.dot(a, b, preferred_element_type=F32)


def _dot_nt(a, b):
    return lax.dot_general(a, b, (((1,), (1,)), ((), ())), preferred_element_type=F32)


def _rms(x, gain):
    ms = jnp.mean(x * x, axis=-1, keepdims=True)
    return x * lax.rsqrt(ms + EPS) * gain


def _layer_spec(shape, layer):
    n = len(shape)
    return pl.BlockSpec((None,) + tuple(shape), lambda *_: (layer,) + (0,) * n,
                        pipeline_mode=pl.Buffered(1))


def _const_spec(shape):
    n = len(shape)
    return pl.BlockSpec(shape, lambda *_: (0,) * n, pipeline_mode=pl.Buffered(1))


def _mod_spec(layer, latent, tiles_per_request):
    if latent:
        return pl.BlockSpec((None, None, 1, 6 * D_MODEL),
                            lambda i: (layer, 1 + i // tiles_per_request, 0, 0))
    return pl.BlockSpec((None, None, 1, 6 * D_MODEL), lambda i: (layer, 0, 0, 0))


def _mod_kernel(cond_ref, w_ref, b_ref, o_ref):
    c = cond_ref[...]
    s = c * jax.nn.sigmoid(c)
    o_ref[...] = _dot(s.astype(BF16), w_ref[...].astype(BF16)) + b_ref[...]


def _modulation(cond, w_mod, b_mod):
    depth = w_mod.shape[0]
    tn = 1536
    return pl.pallas_call(
        _mod_kernel,
        grid=(depth, 6 * D_MODEL // tn),
        in_specs=[
            pl.BlockSpec((MOD_ROWS, D_MODEL), lambda l, j: (0, 0)),
            pl.BlockSpec((None, D_MODEL, tn), lambda l, j: (l, 0, j)),
            pl.BlockSpec((None, 1, tn), lambda l, j: (l, 0, j)),
        ],
        out_specs=pl.BlockSpec((None, MOD_ROWS, tn), lambda l, j: (l, 0, j)),
        out_shape=jax.ShapeDtypeStruct((depth, MOD_ROWS, 6 * D_MODEL), F32),
        compiler_params=pltpu.CompilerParams(
            dimension_semantics=("arbitrary", "arbitrary"), vmem_limit_bytes=VMEM_LIMIT),
        name="modulation",
    )(cond, w_mod, b_mod.reshape(depth, 1, 6 * D_MODEL))


def _head_norm(x, seg_ones, gain):
    sq = x * x
    hi = sq.astype(BF16)
    lo = (sq - hi.astype(F32)).astype(BF16)
    ss = _dot(hi, seg_ones) + _dot(lo, seg_ones)
    return x * lax.rsqrt(ss * (1.0 / HEAD_DIM) + EPS) * gain


def _swap_halves(x, first_half):
    return jnp.where(first_half, pltpu.roll(x, 96, 1), pltpu.roll(x, 32, 1))


def _rope(x, cos, sin, first_half):
    out = []
    for j in range(x.shape[1] // LANES):
        xj = x[:, j * LANES:(j + 1) * LANES]
        out.append(xj * cos + _swap_halves(xj, first_half) * sin)
    return out[0] if len(out) == 1 else jnp.concatenate(out, axis=1)


def _dup_heads(x, low):
    r = pltpu.roll(x, HEAD_DIM, 1)
    return jnp.concatenate([jnp.where(low, x, r), jnp.where(low, r, x)], axis=1)


def _tile_gain(g):
    return jnp.concatenate([g] * (256 // HEAD_DIM), axis=1)


def _qkv_kernel(latent, *refs):
    if latent:
        (x_ref, mod_ref, na_ref, w_ref, seg_ref, gqa_ref, gka_ref, gqb_ref, gkb_ref,
         cos_ref, sin_ref, qa_o, ka_o, va_o, qb_o, kb_o, vb_o) = refs
    else:
        (x_ref, mod_ref, na_ref, w_ref, seg_ref, gqa_ref, gka_ref, gqb_ref, gkb_ref) = refs[:9]
        qa_o, ka_o, va_o, qb_o, kb_o, vb_o, kaf_o, vaf_o, kbf_o, vbf_o = refs[-10:]
    x = x_ref[...]
    sh = mod_ref[:, 0:D_MODEL]
    sc = mod_ref[:, D_MODEL:2 * D_MODEL]
    h = _rms(x, na_ref[...]) * (1.0 + sc) + sh
    proj = _dot(h.astype(BF16), w_ref[...])
    seg = seg_ref[...]
    lane = lax.broadcasted_iota(jnp.int32, (1, LANES), 1)
    low = lane < HEAD_DIM
    first_half = (lane % HEAD_DIM) < (HEAD_DIM // 2)

    def qk(lo_col, width, gain_ref):
        gain = _tile_gain(gain_ref[...])
        out = []
        for c0 in range(0, width, 256):
            w = min(256, width - c0)
            out.append(_head_norm(proj[:, lo_col + c0:lo_col + c0 + w], seg[:w, :w], gain[:, :w]))
        y = out[0] if len(out) == 1 else jnp.concatenate(out, axis=1)
        if latent:
            y = _rope(y, cos_ref[...], sin_ref[...], first_half)
        return y

    qa = qk(0, 512, gqa_ref)
    ka = qk(512, 512, gka_ref)
    va = proj[:, 1024:1536]
    qb = qk(1536, 512, gqb_ref)
    kb = qk(2048, 128, gkb_ref)
    vb = proj[:, 2176:2304]
    scale = HEAD_DIM ** -0.5
    qa_o[...] = (qa * scale).astype(BF16)
    va_o[...] = va.astype(BF16)
    qb_o[...] = (qb * scale).astype(BF16)
    vb_o[...] = _dup_heads(vb, low).astype(BF16)
    if latent:
        ka_o[...] = ka.astype(BF16)
        kb_o[...] = _dup_heads(kb, low).astype(BF16)
    else:
        ka_t = ka.T
        kb_t = kb.T
        ka_o[...] = ka_t.astype(BF16)
        kb_o[...] = kb_t.astype(BF16)
        kaf_o[...] = ka_t
        kbf_o[...] = kb_t
        vbf_o[...] = vb.T
        for h in range(DIFF_HEADS):
            vaf_o[:, h, :] = va[:, h * LANES:(h + 1) * LANES]


def _qkv(x, mod4, layer, latent, tiles_per_request, norm_attn, w_in, seg_ones, gains, rope, prev_caches):
    n = x.shape[0]
    tm = TOKEN_TILE
    depth = w_in.shape[0]
    tok = lambda w: pl.BlockSpec((tm, w), lambda i: (i, 0))
    in_specs = [
        tok(D_MODEL),
        _mod_spec(layer, latent, tiles_per_request),
        _layer_spec((1, D_MODEL), layer),
        _layer_spec((D_MODEL, IN_COLS), layer),
        _const_spec((256, 256)),
    ] + [_layer_spec((1, HEAD_DIM), layer)] * 4
    args = [x, mod4, norm_attn, w_in, seg_ones, *gains]
    aliases = {}
    if latent:
        out_specs = [tok(512), tok(512), tok(512), tok(512), tok(256), tok(256)]
        out_shape = [jax.ShapeDtypeStruct((n, w), BF16) for w in (512, 512, 512, 512, 256, 256)]
        rope_spec = pl.BlockSpec((tm, LANES), lambda i: (i % tiles_per_request, 0))
        in_specs += [rope_spec, rope_spec]
        args += list(rope)
    else:
        assert tiles_per_request == 1, "context requests must be one token tile long"
        req = n // tm
        feat = lambda rows: pl.BlockSpec((rows, tm), lambda i: (i, 0))
        slab = lambda *dims: pl.BlockSpec((None, None) + dims, lambda i: (i, layer) + (0,) * len(dims))
        out_specs = [tok(512), feat(512), tok(512), tok(512), feat(LANES), tok(256),
                     slab(512, tm), slab(tm, DIFF_HEADS, LANES), slab(LANES, tm), slab(LANES, tm)]
        out_shape = [jax.ShapeDtypeStruct((n, 512), BF16), jax.ShapeDtypeStruct((req * 512, tm), BF16),
                     jax.ShapeDtypeStruct((n, 512), BF16), jax.ShapeDtypeStruct((n, 512), BF16),
                     jax.ShapeDtypeStruct((req * LANES, tm), BF16), jax.ShapeDtypeStruct((n, 256), BF16),
                     jax.ShapeDtypeStruct((req, depth, 512, tm), F32),
                     jax.ShapeDtypeStruct((req, depth, tm, DIFF_HEADS, LANES), F32),
                     jax.ShapeDtypeStruct((req, depth, LANES, tm), F32),
                     jax.ShapeDtypeStruct((req, depth, LANES, tm), F32)]
        if prev_caches is not None:
            aliases = {len(args) + k: 6 + k for k in range(4)}
            in_specs += [pl.BlockSpec(memory_space=pl.ANY)] * 4
            args += list(prev_caches)
    return pl.pallas_call(
        functools.partial(_qkv_kernel, latent),
        grid=(n // tm,),
        in_specs=in_specs,
        out_specs=out_specs,
        out_shape=out_shape,
        input_output_aliases=aliases,
        compiler_params=pltpu.CompilerParams(
            dimension_semantics=("arbitrary",), vmem_limit_bytes=VMEM_LIMIT),
        name="qkv_latent" if latent else "qkv_context",
    )(*args)


def _stack_halves(q, low):
    zero = jnp.zeros_like(q)
    return jnp.concatenate([jnp.where(low, q, zero), jnp.where(low, zero, q)], axis=0)


def _softmax_parts(ss):
    m = ss[0].max(axis=-1, keepdims=True)
    for s in ss[1:]:
        m = jnp.maximum(m, s.max(axis=-1, keepdims=True))
    es = [jnp.exp(s - m) for s in ss]
    l = es[0].sum(axis=-1, keepdims=True)
    for e in es[1:]:
        l = l + e.sum(axis=-1, keepdims=True)
    return es, l


def _diff_head(qs, scores, pvs, lam):
    tq = qs.shape[0] // 2
    es, l = _softmax_parts([f(qs) for f in scores])
    r = 1.0 / l
    r0 = r[:tq]
    r1 = r[tq:] * lam
    o = None
    for e, pv in zip(es, pvs):
        t = pv((e[:tq] * r0 - e[tq:] * r1).astype(BF16))
        o = t if o is None else o + t
    return o


def _gqa_pair(qs, scores, pvs, low):
    tq = qs.shape[0] // 2
    es, l = _softmax_parts([f(qs) for f in scores])
    o = None
    for e, pv in zip(es, pvs):
        t = pv(e.astype(BF16))
        o = t if o is None else o + t
    o = o * (1.0 / l)
    return jnp.where(low, o[:tq], o[tq:])


def _attn_kernel(has_cache, lam_init, *refs):
    qa_ref, qb_ref, ka_ref, va_ref, kb_ref, vb_ref = refs[:6]
    if has_cache:
        cka_ref, cva_ref, ckb_ref, cvb_ref = refs[6:10]
        refs = refs[10:]
    else:
        refs = refs[6:]
    lq1, lk1, lq2, lk2, subln_ref, o_ref = refs
    lane = lax.broadcasted_iota(jnp.int32, (1, LANES), 1)
    low = lane < HEAD_DIM
    lam = (jnp.exp(jnp.sum(lq1[...] * lk1[...], axis=-1, keepdims=True))
           - jnp.exp(jnp.sum(lq2[...] * lk2[...], axis=-1, keepdims=True)) + lam_init)
    subln = subln_ref[...]

    def twice(x):
        return jnp.concatenate([x, x], axis=0)

    for h in range(DIFF_HEADS):
        cs = slice(h * LANES, (h + 1) * LANES)
        pvs = [lambda w, cs=cs: _dot(w, va_ref[:, cs])]
        if has_cache:
            scores = [lambda q, cs=cs: _dot_nt(q, ka_ref[:, cs]),
                      lambda q, h=h: _dot(q, cka_ref[h].astype(BF16))]
            pvs.append(lambda w, h=h: _dot(w, cva_ref[:, h, :].astype(BF16)))
        else:
            scores = [lambda q, cs=cs: _dot(q, ka_ref[cs, :])]
        o = _diff_head(_stack_halves(qa_ref[:, cs], low), scores, pvs, lam)
        o_ref[:, cs] = (_rms(o, subln) * (1.0 - lam_init)).astype(o_ref.dtype)
    for g in range(GQA_KV_HEADS):
        cs = slice(g * LANES, (g + 1) * LANES)
        gs = slice(g * HEAD_DIM, (g + 1) * HEAD_DIM)
        pvs = [lambda p, cs=cs: _dot(p, vb_ref[:, cs])]
        if has_cache:
            scores = [lambda q, cs=cs: _dot_nt(q, kb_ref[:, cs]),
                      lambda q, g=g: _dot(q, twice(ckb_ref[g]).astype(BF16))]
            pvs.append(lambda p, g=g: _dot_nt(p, twice(cvb_ref[g]).astype(BF16)))
        else:
            scores = [lambda q, gs=gs: _dot(q, twice(kb_ref[gs, :]))]
        for rp in range(2):
            qs = slice((2 * g + rp) * LANES, (2 * g + rp + 1) * LANES)
            o = _gqa_pair(_stack_halves(qb_ref[:, qs], low), scores, pvs, low)
            o_ref[:, 512 + qs.start:512 + qs.stop] = o.astype(o_ref.dtype)


def _attention(layer, qa, qb, own, cache, seq, lam_vecs, subln):
    n = qa.shape[0]
    requests = n // seq
    tq = min(Q_TILE, seq)
    nq = seq // tq
    lam_init = 0.8 - 0.6 * math.exp(-0.3 * layer)
    qspec = pl.BlockSpec((tq, 512), lambda b, i: (b * nq + i, 0))
    in_specs = [qspec, qspec]
    args = [qa, qb]
    for a in own:
        in_specs.append(pl.BlockSpec((a.shape[0] // requests, a.shape[1]), lambda b, i: (b, 0)))
        args.append(a)
    if cache is not None:
        for a in cache:
            nd = a.ndim - 2
            in_specs.append(pl.BlockSpec((None, None) + a.shape[2:],
                                         lambda b, i, nd=nd: (b, layer) + (0,) * nd))
            args.append(a)
    in_specs += [_layer_spec((1, HEAD_DIM), layer)] * 4 + [_layer_spec((1, 2 * HEAD_DIM), layer)]
    args += list(lam_vecs) + [subln]
    return pl.pallas_call(
        functools.partial(_attn_kernel, cache is not None, lam_init),
        grid=(requests, nq),
        in_specs=in_specs,
        out_specs=pl.BlockSpec((tq, D_MODEL), lambda b, i: (b * nq + i, 0)),
        out_shape=jax.ShapeDtypeStruct((n, D_MODEL), BF16),
        compiler_params=pltpu.CompilerParams(
            dimension_semantics=("arbitrary", "arbitrary"), vmem_limit_bytes=VMEM_LIMIT),
        name="attn_latent" if cache is not None else "attn_context",
    )(*args)


def _ffn_kernel(x_ref, mix_ref, mod_ref, wo_ref, nf_ref, wgu_ref, wd_ref, o_ref):
    g_a = mod_ref[:, 2 * D_MODEL:3 * D_MODEL]
    sh_f = mod_ref[:, 3 * D_MODEL:4 * D_MODEL]
    sc_f = mod_ref[:, 4 * D_MODEL:5 * D_MODEL]
    g_f = mod_ref[:, 5 * D_MODEL:6 * D_MODEL]
    x = x_ref[...] + g_a * _dot(mix_ref[...], wo_ref[...])
    h = (_rms(x, nf_ref[...]) * (1.0 + sc_f) + sh_f).astype(BF16)
    gu = _dot(h, wgu_ref[...])
    gate = gu[:, :FFN_HIDDEN]
    up = gu[:, FFN_HIDDEN:]
    act = (gate * jax.nn.sigmoid(gate) * up).astype(BF16)
    o_ref[...] = x + g_f * _dot(act, wd_ref[...])


def _ffn(x, mix, mod4, layer, latent, tiles_per_request, w_out, norm_ffn, w_gu, w_down):
    n = x.shape[0]
    tm = TOKEN_TILE
    return pl.pallas_call(
        _ffn_kernel,
        grid=(n // tm,),
        in_specs=[
            pl.BlockSpec((tm, D_MODEL), lambda i: (i, 0)),
            pl.BlockSpec((tm, D_MODEL), lambda i: (i, 0)),
            _mod_spec(layer, latent, tiles_per_request),
            _layer_spec((D_MODEL, D_MODEL), layer),
            _layer_spec((1, D_MODEL), layer),
            _layer_spec((D_MODEL, 2 * FFN_HIDDEN), layer),
            _layer_spec((FFN_HIDDEN, D_MODEL), layer),
        ],
        out_specs=pl.BlockSpec((tm, D_MODEL), lambda i: (i, 0)),
        out_shape=jax.ShapeDtypeStruct((n, D_MODEL), F32),
        compiler_params=pltpu.CompilerParams(
            dimension_semantics=("arbitrary",), vmem_limit_bytes=VMEM_LIMIT),
        name="ffn_latent" if latent else "ffn_context",
    )(x, mix, mod4, w_out, norm_ffn, w_gu, w_down)


def _rope_tables(n_tokens):
    t = jnp.arange(n_tokens, dtype=jnp.int32)
    row = (t // GRID_W).astype(F32)
    col = (t % GRID_W).astype(F32)
    axis_dim = HEAD_DIM // 2
    freqs = ROPE_THETA ** (-jnp.arange(0, axis_dim, 2, dtype=F32) / axis_dim)
    ang = jnp.concatenate([row[:, None] * freqs, col[:, None] * freqs], axis=-1)
    c, s = jnp.cos(ang), jnp.sin(ang)
    return jnp.concatenate([c, c, c, c], axis=-1), jnp.concatenate([-s, s, -s, s], axis=-1)


def kernel(x_prompt, x_sample, cache_diff_k, cache_diff_v, cache_gqa_k, cache_gqa_v, c, c_ctx, w_mod, b_mod, norm_attn, w_in, q_norm_a, k_norm_a, lambda_q1, lambda_k1, lambda_q2, lambda_k2, subln, q_norm_b, k_norm_b, w_out, norm_ffn, w_gate_up, w_down):
    batch, seq, d = x_prompt.shape
    dec_batch, dec_seq, _ = x_sample.shape
    depth = w_mod.shape[0]
    past = cache_diff_k.shape[2]

    cond = jnp.concatenate([c_ctx[None], c, jnp.zeros((MOD_ROWS - 1 - dec_batch, d), F32)], axis=0)
    mod = _modulation(cond, w_mod, b_mod)
    mod4 = mod.reshape(depth, MOD_ROWS, 1, 6 * d)

    w_in_b = w_in.astype(BF16)
    w_out_b = w_out.astype(BF16)
    w_gu_b = w_gate_up.astype(BF16)
    w_down_b = w_down.astype(BF16)
    seg_ones = jnp.kron(jnp.eye(256 // HEAD_DIM, dtype=F32), jnp.ones((HEAD_DIM, HEAD_DIM), F32)).astype(BF16)
    rope = _rope_tables(dec_seq)

    cache = (jnp.transpose(cache_diff_k, (0, 1, 3, 4, 5, 2)).reshape(dec_batch, depth, DIFF_HEADS, 2 * HEAD_DIM, past),
             cache_diff_v,
             jnp.transpose(cache_gqa_k, (0, 1, 3, 4, 2)),
             jnp.transpose(cache_gqa_v, (0, 1, 3, 4, 2)))

    row = lambda a: a.reshape(depth, 1, a.shape[-1])
    gains = (row(q_norm_a), row(k_norm_a), row(q_norm_b), row(k_norm_b))
    lam_vecs = (row(lambda_q1), row(lambda_k1), row(lambda_q2), row(lambda_k2))
    na, nf, sl = row(norm_attn), row(norm_ffn), row(subln)

    yp = x_prompt.reshape(batch * seq, d)
    ys = x_sample.reshape(dec_batch * dec_seq, d)
    ctx_tiles = seq // TOKEN_TILE
    lat_tiles = dec_seq // TOKEN_TILE
    new_caches = None
    for l in range(depth):
        qa, ka, va, qb, kb, vb, *new_caches = _qkv(
            yp, mod4, l, False, ctx_tiles, na, w_in_b, seg_ones, gains, None, new_caches)
        mix = _attention(l, qa, qb, (ka, va, kb, vb), None, seq, lam_vecs, sl)
        yp = _ffn(yp, mix, mod4, l, False, ctx_tiles, w_out_b, nf, w_gu_b, w_down_b)

        qa, ka, va, qb, kb, vb = _qkv(
            ys, mod4, l, True, lat_tiles, na, w_in_b, seg_ones, gains, rope, None)
        mix = _attention(l, qa, qb, (ka, va, kb, vb), cache, dec_seq, lam_vecs, sl)
        ys = _ffn(ys, mix, mod4, l, True, lat_tiles, w_out_b, nf, w_gu_b, w_down_b)

    ka_t, va_n, kb_t, vb_t = new_caches
    new_diff_k = jnp.transpose(ka_t.reshape(batch, depth, DIFF_HEADS, 2, HEAD_DIM, seq), (0, 1, 5, 2, 3, 4))
    new_gqa_k = jnp.transpose(kb_t.reshape(batch, depth, GQA_KV_HEADS, HEAD_DIM, seq), (0, 1, 4, 2, 3))
    new_gqa_v = jnp.transpose(vb_t.reshape(batch, depth, GQA_KV_HEADS, HEAD_DIM, seq), (0, 1, 4, 2, 3))
    return (yp.reshape(batch, seq, d), ys.reshape(dec_batch, dec_seq, d),
            new_diff_k, va_n, new_gqa_k, new_gqa_v)
```

```python
import functools
import math

import jax
import jax.numpy as jnp
from jax import lax
from jax.experimental import pallas as pl
from jax.experimental.pallas import tpu as pltpu

F32 = jnp.float32
BF16 = jnp.bfloat16

D_MODEL = 1024
HEAD_DIM = 64
GRID_W = 64
DIFF_HEADS = 4
GQA_KV_HEADS = 2
FFN_HIDDEN = 2816
IN_COLS = 2304
ROPE_THETA = 10000.0
EPS = 1e-6
LANES = 128
MOD_ROWS = 8
TOKEN_TILE = 256
Q_TILE = 256
VMEM_LIMIT = 56 * 1024 * 1024


def _dot(a, b):
    return jnp.dot(a, b, preferred_element_type=F32)


def _dot_nt(a, b):
    return lax.dot_general(a, b, (((1,), (1,)), ((), ())), preferred_element_type=F32)


def _rms(x, gain):
    ms = jnp.mean(x * x, axis=-1, keepdims=True)
    return x * lax.rsqrt(ms + EPS) * gain


def _layer_spec(shape, layer):
    n = len(shape)
    return pl.BlockSpec((None,) + tuple(shape), lambda *_: (layer,) + (0,) * n,
                        pipeline_mode=pl.Buffered(1))


def _const_spec(shape):
    n = len(shape)
    return pl.BlockSpec(shape, lambda *_: (0,) * n, pipeline_mode=pl.Buffered(1))


def _mod_spec(layer, latent, tiles_per_request):
    if latent:
        return pl.BlockSpec((None, None, 1, 6 * D_MODEL),
                            lambda i: (layer, 1 + i // tiles_per_request, 0, 0))
    return pl.BlockSpec((None, None, 1, 6 * D_MODEL), lambda i: (layer, 0, 0, 0))


def _mod_kernel(cond_ref, w_ref, b_ref, o_ref):
    c = cond_ref[...]
    s = c * jax.nn.sigmoid(c)
    o_ref[...] = _dot(s.astype(BF16), w_ref[...].astype(BF16)) + b_ref[...]


def _modulation(cond, w_mod, b_mod):
    depth = w_mod.shape[0]
    tn = 1536
    return pl.pallas_call(
        _mod_kernel,
        grid=(depth, 6 * D_MODEL // tn),
        in_specs=[
            pl.BlockSpec((MOD_ROWS, D_MODEL), lambda l, j: (0, 0)),
            pl.BlockSpec((None, D_MODEL, tn), lambda l, j: (l, 0, j)),
            pl.BlockSpec((None, 1, tn), lambda l, j: (l, 0, j)),
        ],
        out_specs=pl.BlockSpec((None, MOD_ROWS, tn), lambda l, j: (l, 0, j)),
        out_shape=jax.ShapeDtypeStruct((depth, MOD_ROWS, 6 * D_MODEL), F32),
        compiler_params=pltpu.CompilerParams(
            dimension_semantics=("arbitrary", "arbitrary"), vmem_limit_bytes=VMEM_LIMIT),
        name="modulation",
    )(cond, w_mod, b_mod.reshape(depth, 1, 6 * D_MODEL))


def _head_norm(x, seg_ones, gain):
    sq = x * x
    hi = sq.astype(BF16)
    lo = (sq - hi.astype(F32)).astype(BF16)
    ss = _dot(hi, seg_ones) + _dot(lo, seg_ones)
    return x * lax.rsqrt(ss * (1.0 / HEAD_DIM) + EPS) * gain


def _swap_halves(x, first_half):
    return jnp.where(first_half, pltpu.roll(x, 96, 1), pltpu.roll(x, 32, 1))


def _rope(x, cos, sin, first_half):
    out = []
    for j in range(x.shape[1] // LANES):
        xj = x[:, j * LANES:(j + 1) * LANES]
        out.append(xj * cos + _swap_halves(xj, first_half) * sin)
    return out[0] if len(out) == 1 else jnp.concatenate(out, axis=1)


def _dup_heads(x, low):
    r = pltpu.roll(x, HEAD_DIM, 1)
    return jnp.concatenate([jnp.where(low, x, r), jnp.where(low, r, x)], axis=1)


def _tile_gain(g):
    return jnp.concatenate([g] * (256 // HEAD_DIM), axis=1)


def _qkv_kernel(latent, *refs):
    if latent:
        (x_ref, mod_ref, na_ref, w_ref, seg_ref, gqa_ref, gka_ref, gqb_ref, gkb_ref,
         cos_ref, sin_ref, qa_o, ka_o, va_o, qb_o, kb_o, vb_o) = refs
    else:
        (x_ref, mod_ref, na_ref, w_ref, seg_ref, gqa_ref, gka_ref, gqb_ref, gkb_ref) = refs[:9]
        qa_o, ka_o, va_o, qb_o, kb_o, vb_o, kaf_o, vaf_o, kbf_o, vbf_o = refs[-10:]
    x = x_ref[...]
    sh = mod_ref[:, 0:D_MODEL]
    sc = mod_ref[:, D_MODEL:2 * D_MODEL]
    h = _rms(x, na_ref[...]) * (1.0 + sc) + sh
    proj = _dot(h.astype(BF16), w_ref[...])
    seg = seg_ref[...]
    lane = lax.broadcasted_iota(jnp.int32, (1, LANES), 1)
    low = lane < HEAD_DIM
    first_half = (lane % HEAD_DIM) < (HEAD_DIM // 2)

    def qk(lo_col, width, gain_ref):
        gain = _tile_gain(gain_ref[...])
        out = []
        for c0 in range(0, width, 256):
            w = min(256, width - c0)
            out.append(_head_norm(proj[:, lo_col + c0:lo_col + c0 + w], seg[:w, :w], gain[:, :w]))
        y = out[0] if len(out) == 1 else jnp.concatenate(out, axis=1)
        if latent:
            y = _rope(y, cos_ref[...], sin_ref[...], first_half)
        return y

    qa = qk(0, 512, gqa_ref)
    ka = qk(512, 512, gka_ref)
    va = proj[:, 1024:1536]
    qb = qk(1536, 512, gqb_ref)
    kb = qk(2048, 128, gkb_ref)
    vb = proj[:, 2176:2304]
    scale = HEAD_DIM ** -0.5 * math.log2(math.e)
    qa_o[...] = (qa * scale).astype(BF16)
    va_o[...] = va.astype(BF16)
    qb_o[...] = (qb * scale).astype(BF16)
    vb_o[...] = _dup_heads(vb, low).astype(BF16)
    if latent:
        ka_o[...] = ka.astype(BF16)
        kb_o[...] = _dup_heads(kb, low).astype(BF16)
    else:
        ka_t = ka.T
        kb_t = kb.T
        ka_o[...] = ka_t.astype(BF16)
        kb_o[...] = kb_t.astype(BF16)
        kaf_o[...] = ka_t
        kbf_o[...] = kb_t
        vbf_o[...] = vb.T
        for h in range(DIFF_HEADS):
            vaf_o[pl.ds(h, x.shape[0], stride=DIFF_HEADS), :] = va[:, h * LANES:(h + 1) * LANES]


def _qkv(x, mod4, layer, latent, tiles_per_request, norm_attn, w_in, seg_ones, gains, rope, prev_caches):
    n = x.shape[0]
    tm = TOKEN_TILE
    depth = w_in.shape[0]
    tok = lambda w: pl.BlockSpec((tm, w), lambda i: (i, 0))
    in_specs = [
        tok(D_MODEL),
        _mod_spec(layer, latent, tiles_per_request),
        _layer_spec((1, D_MODEL), layer),
        _layer_spec((D_MODEL, IN_COLS), layer),
        _const_spec((256, 256)),
    ] + [_layer_spec((1, HEAD_DIM), layer)] * 4
    args = [x, mod4, norm_attn, w_in, seg_ones, *gains]
    aliases = {}
    if latent:
        out_specs = [tok(512), tok(512), tok(512), tok(512), tok(256), tok(256)]
        out_shape = [jax.ShapeDtypeStruct((n, w), BF16) for w in (512, 512, 512, 512, 256, 256)]
        rope_spec = pl.BlockSpec((tm, LANES), lambda i: (i % tiles_per_request, 0))
        in_specs += [rope_spec, rope_spec]
        args += list(rope)
    else:
        assert tiles_per_request == 1, "context requests must be one token tile long"
        req = n // tm
        feat = lambda rows: pl.BlockSpec((rows, tm), lambda i: (i, 0))
        slab = lambda *dims: pl.BlockSpec((None, None) + dims, lambda i: (i, layer) + (0,) * len(dims))
        out_specs = [tok(512), feat(512), tok(512), tok(512), feat(LANES), tok(256),
                     slab(512, tm), slab(tm * DIFF_HEADS, LANES), slab(LANES, tm), slab(LANES, tm)]
        out_shape = [jax.ShapeDtypeStruct((n, 512), BF16), jax.ShapeDtypeStruct((req * 512, tm), BF16),
                     jax.ShapeDtypeStruct((n, 512), BF16), jax.ShapeDtypeStruct((n, 512), BF16),
                     jax.ShapeDtypeStruct((req * LANES, tm), BF16), jax.ShapeDtypeStruct((n, 256), BF16),
                     jax.ShapeDtypeStruct((req, depth, 512, tm), F32),
                     jax.ShapeDtypeStruct((req, depth, tm * DIFF_HEADS, LANES), F32),
                     jax.ShapeDtypeStruct((req, depth, LANES, tm), F32),
                     jax.ShapeDtypeStruct((req, depth, LANES, tm), F32)]
        if prev_caches is not None:
            aliases = {len(args) + k: 6 + k for k in range(4)}
            in_specs += [pl.BlockSpec(memory_space=pl.ANY)] * 4
            args += list(prev_caches)
    return pl.pallas_call(
        functools.partial(_qkv_kernel, latent),
        grid=(n // tm,),
        in_specs=in_specs,
        out_specs=out_specs,
        out_shape=out_shape,
        input_output_aliases=aliases,
        compiler_params=pltpu.CompilerParams(
            dimension_semantics=("arbitrary",), vmem_limit_bytes=VMEM_LIMIT),
        name="qkv_latent" if latent else "qkv_context",
    )(*args)


def _stack_halves(q, low):
    zero = jnp.zeros_like(q)
    return jnp.concatenate([jnp.where(low, q, zero), jnp.where(low, zero, q)], axis=0)


def _exp_parts(ss):
    m = ss[0].max(axis=-1, keepdims=True)
    for s in ss[1:]:
        m = jnp.maximum(m, s.max(axis=-1, keepdims=True))
    return [jnp.exp2(s - m) for s in ss]


def _diff_head(ss, pvs, lam):
    tq = ss[0].shape[0] // 2
    es = _exp_parts(ss)
    l = es[0].sum(axis=-1, keepdims=True)
    for e in es[1:]:
        l = l + e.sum(axis=-1, keepdims=True)
    r = 1.0 / l
    r0 = r[:tq]
    r1 = r[tq:] * lam
    o = None
    for e, pv in zip(es, pvs):
        t = pv((e[:tq] * r0 - e[tq:] * r1).astype(BF16))
        o = t if o is None else o + t
    return o


def _gqa_pair(ss, pvs, low):
    tq = ss[0].shape[0] // 2
    es = _exp_parts(ss)
    r = None
    for e, pv in zip(es, pvs):
        t = pv(e.astype(BF16))
        r = t if r is None else r + t
    o = r[:, :LANES] * (1.0 / r[:, LANES:])
    return jnp.where(low, o[:tq], o[tq:])


def _attn_kernel(has_cache, lam_init, *refs):
    qa_ref, qb_ref, ka_ref, va_ref, kb_ref, vb_ref = refs[:6]
    if has_cache:
        cka_ref, cva_ref, ckb_ref, cvb_ref = refs[6:10]
        refs = refs[10:]
    else:
        refs = refs[6:]
    lq1, lk1, lq2, lk2, subln_ref, o_ref = refs
    lane = lax.broadcasted_iota(jnp.int32, (1, LANES), 1)
    low = lane < HEAD_DIM
    lam = (jnp.exp(jnp.sum(lq1[...] * lk1[...], axis=-1, keepdims=True))
           - jnp.exp(jnp.sum(lq2[...] * lk2[...], axis=-1, keepdims=True)) + lam_init)
    subln = subln_ref[...]

    def twice(x):
        return jnp.concatenate([x, x], axis=0)

    tasks = []

    def diff_finish(ss, pvs, cs):
        o = _diff_head(ss, pvs, lam)
        o_ref[:, cs] = (_rms(o, subln) * (1.0 - lam_init)).astype(o_ref.dtype)

    def gqa_finish(ss, pvs, cs):
        o_ref[:, cs] = _gqa_pair(ss, pvs, low).astype(o_ref.dtype)

    for h in range(DIFF_HEADS):
        cs = slice(h * LANES, (h + 1) * LANES)
        pvs = [lambda w, cs=cs: _dot(w, va_ref[:, cs])]
        if has_cache:
            scores = [lambda q, cs=cs: _dot_nt(q, ka_ref[:, cs]),
                      lambda q, h=h: _dot(q, cka_ref[h].astype(BF16))]
            pvs.append(lambda w, h=h: _dot(
                w, cva_ref[pl.ds(h, cva_ref.shape[0] // DIFF_HEADS, stride=DIFF_HEADS), :].astype(BF16)))
        else:
            scores = [lambda q, cs=cs: _dot(q, ka_ref[cs, :])]
        tasks.append((
            lambda cs=cs, scores=scores: [f(_stack_halves(qa_ref[:, cs], low)) for f in scores],
            lambda ss, pvs=pvs, cs=cs: diff_finish(ss, pvs, cs)))
    for g in range(GQA_KV_HEADS):
        cs = slice(g * LANES, (g + 1) * LANES)
        gs = slice(g * HEAD_DIM, (g + 1) * HEAD_DIM)
        pvs = [lambda p, cs=cs: _dot(p, jnp.concatenate(
            [vb_ref[:, cs], jnp.ones((vb_ref.shape[0], LANES), BF16)], axis=1))]
        if has_cache:
            scores = [lambda q, cs=cs: _dot_nt(q, kb_ref[:, cs]),
                      lambda q, g=g: _dot(q, twice(ckb_ref[g]).astype(BF16))]
            pvs.append(lambda p, g=g: _dot_nt(p, jnp.concatenate(
                [twice(cvb_ref[g]).astype(BF16), jnp.ones((LANES, cvb_ref.shape[2]), BF16)], axis=0)))
        else:
            scores = [lambda q, gs=gs: _dot(q, twice(kb_ref[gs, :]))]
        for rp in range(2):
            qs = slice((2 * g + rp) * LANES, (2 * g + rp + 1) * LANES)
            os = slice(512 + qs.start, 512 + qs.stop)
            tasks.append((
                lambda qs=qs, scores=scores: [f(_stack_halves(qb_ref[:, qs], low)) for f in scores],
                lambda ss, pvs=pvs, os=os: gqa_finish(ss, pvs, os)))

    ahead = 2
    pending = [t[0]() for t in tasks[:ahead]]
    for i, (_, finish) in enumerate(tasks):
        if i + ahead < len(tasks):
            pending.append(tasks[i + ahead][0]())
        finish(pending.pop(0))


def _attention(layer, qa, qb, own, cache, seq, lam_vecs, subln):
    n = qa.shape[0]
    requests = n // seq
    tq = min(Q_TILE, seq)
    nq = seq // tq
    lam_init = 0.8 - 0.6 * math.exp(-0.3 * layer)
    qspec = pl.BlockSpec((tq, 512), lambda b, i: (b * nq + i, 0))
    in_specs = [qspec, qspec]
    args = [qa, qb]
    for a in own:
        in_specs.append(pl.BlockSpec((a.shape[0] // requests, a.shape[1]), lambda b, i: (b, 0)))
        args.append(a)
    if cache is not None:
        for a in cache:
            nd = a.ndim - 2
            in_specs.append(pl.BlockSpec((None, None) + a.shape[2:],
                                         lambda b, i, nd=nd: (b, layer) + (0,) * nd))
            args.append(a)
    in_specs += [_layer_spec((1, HEAD_DIM), layer)] * 4 + [_layer_spec((1, 2 * HEAD_DIM), layer)]
    args += list(lam_vecs) + [subln]
    return pl.pallas_call(
        functools.partial(_attn_kernel, cache is not None, lam_init),
        grid=(requests, nq),
        in_specs=in_specs,
        out_specs=pl.BlockSpec((tq, D_MODEL), lambda b, i: (b * nq + i, 0)),
        out_shape=jax.ShapeDtypeStruct((n, D_MODEL), BF16),
        compiler_params=pltpu.CompilerParams(
            dimension_semantics=("arbitrary", "arbitrary"), vmem_limit_bytes=VMEM_LIMIT),
        name="attn_latent" if cache is not None else "attn_context",
    )(*args)


def _ffn_kernel(x_ref, mix_ref, mod_ref, wo_ref, nf_ref, wgu_ref, wd_ref, o_ref):
    g_a = mod_ref[:, 2 * D_MODEL:3 * D_MODEL]
    sh_f = mod_ref[:, 3 * D_MODEL:4 * D_MODEL]
    sc_f = mod_ref[:, 4 * D_MODEL:5 * D_MODEL]
    g_f = mod_ref[:, 5 * D_MODEL:6 * D_MODEL]
    x = x_ref[...] + g_a * _dot(mix_ref[...], wo_ref[...])
    h = (_rms(x, nf_ref[...]) * (1.0 + sc_f) + sh_f).astype(BF16)
    gu = _dot(h, wgu_ref[...])
    gate = gu[:, :FFN_HIDDEN]
    up = gu[:, FFN_HIDDEN:]
    act = (gate * jax.nn.sigmoid(gate) * up).astype(BF16)
    o_ref[...] = x + g_f * _dot(act, wd_ref[...])


def _ffn(x, mix, mod4, layer, latent, tiles_per_request, w_out, norm_ffn, w_gu, w_down):
    n = x.shape[0]
    tm = TOKEN_TILE
    return pl.pallas_call(
        _ffn_kernel,
        grid=(n // tm,),
        in_specs=[
            pl.BlockSpec((tm, D_MODEL), lambda i: (i, 0)),
            pl.BlockSpec((tm, D_MODEL), lambda i: (i, 0)),
            _mod_spec(layer, latent, tiles_per_request),
            _layer_spec((D_MODEL, D_MODEL), layer),
            _layer_spec((1, D_MODEL), layer),
            _layer_spec((D_MODEL, 2 * FFN_HIDDEN), layer),
            _layer_spec((FFN_HIDDEN, D_MODEL), layer),
        ],
        out_specs=pl.BlockSpec((tm, D_MODEL), lambda i: (i, 0)),
        out_shape=jax.ShapeDtypeStruct((n, D_MODEL), F32),
        compiler_params=pltpu.CompilerParams(
            dimension_semantics=("arbitrary",), vmem_limit_bytes=VMEM_LIMIT),
        name="ffn_latent" if latent else "ffn_context",
    )(x, mix, mod4, w_out, norm_ffn, w_gu, w_down)


def _rope_tables(n_tokens):
    t = jnp.arange(n_tokens, dtype=jnp.int32)
    row = (t // GRID_W).astype(F32)
    col = (t % GRID_W).astype(F32)
    axis_dim = HEAD_DIM // 2
    freqs = ROPE_THETA ** (-jnp.arange(0, axis_dim, 2, dtype=F32) / axis_dim)
    ang = jnp.concatenate([row[:, None] * freqs, col[:, None] * freqs], axis=-1)
    c, s = jnp.cos(ang), jnp.sin(ang)
    return jnp.concatenate([c, c, c, c], axis=-1), jnp.concatenate([-s, s, -s, s], axis=-1)


def kernel(x_prompt, x_sample, cache_diff_k, cache_diff_v, cache_gqa_k, cache_gqa_v, c, c_ctx, w_mod, b_mod, norm_attn, w_in, q_norm_a, k_norm_a, lambda_q1, lambda_k1, lambda_q2, lambda_k2, subln, q_norm_b, k_norm_b, w_out, norm_ffn, w_gate_up, w_down):
    batch, seq, d = x_prompt.shape
    dec_batch, dec_seq, _ = x_sample.shape
    depth = w_mod.shape[0]
    past = cache_diff_k.shape[2]

    cond = jnp.concatenate([c_ctx[None], c, jnp.zeros((MOD_ROWS - 1 - dec_batch, d), F32)], axis=0)
    mod = _modulation(cond, w_mod, b_mod)
    mod4 = mod.reshape(depth, MOD_ROWS, 1, 6 * d)

    w_in_b = w_in.astype(BF16)
    w_out_b = w_out.astype(BF16)
    w_gu_b = w_gate_up.astype(BF16)
    w_down_b = w_down.astype(BF16)
    seg_ones = jnp.kron(jnp.eye(256 // HEAD_DIM, dtype=F32), jnp.ones((HEAD_DIM, HEAD_DIM), F32)).astype(BF16)
    rope = _rope_tables(dec_seq)

    cache = (jnp.transpose(cache_diff_k, (0, 1, 3, 4, 5, 2)).reshape(dec_batch, depth, DIFF_HEADS, 2 * HEAD_DIM, past),
             cache_diff_v.reshape(dec_batch, depth, past * DIFF_HEADS, 2 * HEAD_DIM),
             jnp.transpose(cache_gqa_k, (0, 1, 3, 4, 2)),
             jnp.transpose(cache_gqa_v, (0, 1, 3, 4, 2)))

    row = lambda a: a.reshape(depth, 1, a.shape[-1])
    gains = (row(q_norm_a), row(k_norm_a), row(q_norm_b), row(k_norm_b))
    lam_vecs = (row(lambda_q1), row(lambda_k1), row(lambda_q2), row(lambda_k2))
    na, nf, sl = row(norm_attn), row(norm_ffn), row(subln)

    yp = x_prompt.reshape(batch * seq, d)
    ys = x_sample.reshape(dec_batch * dec_seq, d)
    ctx_tiles = seq // TOKEN_TILE
    lat_tiles = dec_seq // TOKEN_TILE
    new_caches = None
    for l in range(depth):
        qa, ka, va, qb, kb, vb, *new_caches = _qkv(
            yp, mod4, l, False, ctx_tiles, na, w_in_b, seg_ones, gains, None, new_caches)
        mix = _attention(l, qa, qb, (ka, va, kb, vb), None, seq, lam_vecs, sl)
        yp = _ffn(yp, mix, mod4, l, False, ctx_tiles, w_out_b, nf, w_gu_b, w_down_b)

        qa, ka, va, qb, kb, vb = _qkv(
            ys, mod4, l, True, lat_tiles, na, w_in_b, seg_ones, gains, rope, None)
        mix = _attention(l, qa, qb, (ka, va, kb, vb), cache, dec_seq, lam_vecs, sl)
        ys = _ffn(ys, mix, mod4, l, True, lat_tiles, w_out_b, nf, w_gu_b, w_down_b)

    ka_t, va_n, kb_t, vb_t = new_caches
    new_diff_k = jnp.transpose(ka_t.reshape(batch, depth, DIFF_HEADS, 2, HEAD_DIM, seq), (0, 1, 5, 2, 3, 4))
    new_gqa_k = jnp.transpose(kb_t.reshape(batch, depth, GQA_KV_HEADS, HEAD_DIM, seq), (0, 1, 4, 2, 3))
    new_gqa_v = jnp.transpose(vb_t.reshape(batch, depth, GQA_KV_HEADS, HEAD_DIM, seq), (0, 1, 4, 2, 3))
    return (yp.reshape(batch, seq, d), ys.reshape(dec_batch, dec_seq, d),
            new_diff_k, va_n.reshape(batch, depth, seq, DIFF_HEADS, 2 * HEAD_DIM), new_gqa_k, new_gqa_v)
```

```python
import functools
import math

import jax
import jax.numpy as jnp
from jax import lax
from jax.experimental import pallas as pl
from jax.experimental.pallas import tpu as pltpu

F32 = jnp.float32
BF16 = jnp.bfloat16

D_MODEL = 1024
HEAD_DIM = 64
GRID_W = 64
DIFF_HEADS = 4
GQA_KV_HEADS = 2
FFN_HIDDEN = 2816
IN_COLS = 2304
ROPE_THETA = 10000.0
EPS = 1e-6
LANES = 128
MOD_ROWS = 8
TOKEN_TILE = 256
Q_TILE = 256
VMEM_LIMIT = 56 * 1024 * 1024


def _dot(a, b):
    return jnp.dot(a, b, preferred_element_type=F32)


def _dot_nt(a, b):
    return lax.dot_general(a, b, (((1,), (1,)), ((), ())), preferred_element_type=F32)


def _rms(x, gain):
    ms = jnp.mean(x * x, axis=-1, keepdims=True)
    return x * lax.rsqrt(ms + EPS) * gain


def _layer_spec(shape, layer):
    n = len(shape)
    return pl.BlockSpec((None,) + tuple(shape), lambda *_: (layer,) + (0,) * n,
                        pipeline_mode=pl.Buffered(1))


def _const_spec(shape):
    n = len(shape)
    return pl.BlockSpec(shape, lambda *_: (0,) * n, pipeline_mode=pl.Buffered(1))


def _mod_spec(layer, latent, tiles_per_request):
    if latent:
        return pl.BlockSpec((None, None, 1, 6 * D_MODEL),
                            lambda i: (layer, 1 + i // tiles_per_request, 0, 0))
    return pl.BlockSpec((None, None, 1, 6 * D_MODEL), lambda i: (layer, 0, 0, 0))


def _mod_kernel(cond_ref, w_ref, b_ref, o_ref):
    c = cond_ref[...]
    s = c * jax.nn.sigmoid(c)
    o_ref[...] = _dot(s.astype(BF16), w_ref[...].astype(BF16)) + b_ref[...]


def _modulation(cond, w_mod, b_mod):
    depth = w_mod.shape[0]
    tn = 1536
    return pl.pallas_call(
        _mod_kernel,
        grid=(depth, 6 * D_MODEL // tn),
        in_specs=[
            pl.BlockSpec((MOD_ROWS, D_MODEL), lambda l, j: (0, 0)),
            pl.BlockSpec((None, D_MODEL, tn), lambda l, j: (l, 0, j)),
            pl.BlockSpec((None, 1, tn), lambda l, j: (l, 0, j)),
        ],
        out_specs=pl.BlockSpec((None, MOD_ROWS, tn), lambda l, j: (l, 0, j)),
        out_shape=jax.ShapeDtypeStruct((depth, MOD_ROWS, 6 * D_MODEL), F32),
        compiler_params=pltpu.CompilerParams(
            dimension_semantics=("arbitrary", "arbitrary"), vmem_limit_bytes=VMEM_LIMIT),
        name="modulation",
    )(cond, w_mod, b_mod.reshape(depth, 1, 6 * D_MODEL))


def _head_norm(x, seg_ones, gain):
    ss = _dot((x * x).astype(BF16), seg_ones)
    return x * lax.rsqrt(ss * (1.0 / HEAD_DIM) + EPS) * gain


def _swap_halves(x, first_half):
    return jnp.where(first_half, pltpu.roll(x, 96, 1), pltpu.roll(x, 32, 1))


def _rope(x, cos, sin, first_half):
    out = []
    for j in range(x.shape[1] // LANES):
        xj = x[:, j * LANES:(j + 1) * LANES]
        out.append(xj * cos + _swap_halves(xj, first_half) * sin)
    return out[0] if len(out) == 1 else jnp.concatenate(out, axis=1)


def _dup_heads(x, low):
    r = pltpu.roll(x, HEAD_DIM, 1)
    return jnp.concatenate([jnp.where(low, x, r), jnp.where(low, r, x)], axis=1)


def _tile_gain(g):
    return jnp.concatenate([g] * (256 // HEAD_DIM), axis=1)


def _qkv_kernel(latent, *refs):
    if latent:
        (x_ref, mod_ref, na_ref, w_ref, seg_ref, gqa_ref, gka_ref, gqb_ref, gkb_ref,
         cos_ref, sin_ref, qa_o, ka_o, va_o, qb_o, kb_o, vb_o) = refs
    else:
        (x_ref, mod_ref, na_ref, w_ref, seg_ref, gqa_ref, gka_ref, gqb_ref, gkb_ref) = refs[:9]
        qa_o, ka_o, va_o, qb_o, kb_o, vb_o, kaf_o, vaf_o, kbf_o, vbf_o = refs[-10:]
    tm = TOKEN_TILE
    sh = mod_ref[:, 0:D_MODEL]
    sc = mod_ref[:, D_MODEL:2 * D_MODEL]
    seg = seg_ref[...]
    lane = lax.broadcasted_iota(jnp.int32, (1, LANES), 1)
    low = lane < HEAD_DIM
    first_half = (lane % HEAD_DIM) < (HEAD_DIM // 2)
    scale = HEAD_DIM ** -0.5 * math.log2(math.e)

    def project(r):
        h = _rms(x_ref[r, :], na_ref[...]) * (1.0 + sc) + sh
        return _dot(h.astype(BF16), w_ref[...])

    def finish(j, r, proj):
        def qk(lo_col, width, gain_ref):
            gain = _tile_gain(gain_ref[...])
            out = []
            for c0 in range(0, width, 256):
                w = min(256, width - c0)
                out.append(_head_norm(proj[:, lo_col + c0:lo_col + c0 + w], seg[:w, :w], gain[:, :w]))
            y = out[0] if len(out) == 1 else jnp.concatenate(out, axis=1)
            if latent:
                y = _rope(y, cos_ref[r, :], sin_ref[r, :], first_half)
            return y

        qa = qk(0, 512, gqa_ref)
        ka = qk(512, 512, gka_ref)
        va = proj[:, 1024:1536]
        qb = qk(1536, 512, gqb_ref)
        kb = qk(2048, 128, gkb_ref)
        vb = proj[:, 2176:2304]
        qa_o[r, :] = (qa * scale).astype(BF16)
        va_o[r, :] = va.astype(BF16)
        qb_o[r, :] = (qb * scale).astype(BF16)
        vb_o[r, :] = _dup_heads(vb, low).astype(BF16)
        if latent:
            ka_o[r, :] = ka.astype(BF16)
            kb_o[r, :] = _dup_heads(kb, low).astype(BF16)
        else:
            ka_t = ka.T
            kb_t = kb.T
            ka_o[j * 512:(j + 1) * 512, :] = ka_t.astype(BF16)
            kb_o[j * LANES:(j + 1) * LANES, :] = kb_t.astype(BF16)
            kaf_o[j] = ka_t
            kbf_o[j] = kb_t
            vbf_o[j] = vb.T
            for h in range(DIFF_HEADS):
                vaf_o[j, pl.ds(h, tm, stride=DIFF_HEADS), :] = va[:, h * LANES:(h + 1) * LANES]

    rows = [slice(j * tm, (j + 1) * tm) for j in range(x_ref.shape[0] // tm)]
    projs = [project(r) for r in rows]
    for j, (r, proj) in enumerate(zip(rows, projs)):
        finish(j, r, proj)


def _qkv(x, mod4, layer, latent, tiles_per_request, norm_attn, w_in, seg_ones, gains, rope, prev_caches):
    n = x.shape[0]
    sub = 2
    tm = sub * TOKEN_TILE
    depth = w_in.shape[0]
    tok = lambda w: pl.BlockSpec((tm, w), lambda i: (i, 0))
    in_specs = [
        tok(D_MODEL),
        _mod_spec(layer, latent, max(tiles_per_request // sub, 1)),
        _layer_spec((1, D_MODEL), layer),
        _layer_spec((D_MODEL, IN_COLS), layer),
        _const_spec((256, 256)),
    ] + [_layer_spec((1, HEAD_DIM), layer)] * 4
    args = [x, mod4, norm_attn, w_in, seg_ones, *gains]
    aliases = {}
    if latent:
        out_specs = [tok(512), tok(512), tok(512), tok(512), tok(256), tok(256)]
        out_shape = [jax.ShapeDtypeStruct((n, w), BF16) for w in (512, 512, 512, 512, 256, 256)]
        rope_spec = pl.BlockSpec((tm, LANES), lambda i: (i % (tiles_per_request // sub), 0))
        in_specs += [rope_spec, rope_spec]
        args += list(rope)
    else:
        assert tiles_per_request == 1, "context requests must be one token tile long"
        seq = TOKEN_TILE
        req = n // seq
        feat = lambda rows: pl.BlockSpec((sub * rows, seq), lambda i: (i, 0))
        slab = lambda *dims: pl.BlockSpec((sub, None) + dims, lambda i: (i, layer) + (0,) * len(dims))
        out_specs = [tok(512), feat(512), tok(512), tok(512), feat(LANES), tok(256),
                     slab(512, seq), slab(seq * DIFF_HEADS, LANES), slab(LANES, seq), slab(LANES, seq)]
        out_shape = [jax.ShapeDtypeStruct((n, 512), BF16), jax.ShapeDtypeStruct((req * 512, seq), BF16),
                     jax.ShapeDtypeStruct((n, 512), BF16), jax.ShapeDtypeStruct((n, 512), BF16),
                     jax.ShapeDtypeStruct((req * LANES, seq), BF16), jax.ShapeDtypeStruct((n, 256), BF16),
                     jax.ShapeDtypeStruct((req, depth, 512, seq), F32),
                     jax.ShapeDtypeStruct((req, depth, seq * DIFF_HEADS, LANES), F32),
                     jax.ShapeDtypeStruct((req, depth, LANES, seq), F32),
                     jax.ShapeDtypeStruct((req, depth, LANES, seq), F32)]
        if prev_caches is not None:
            aliases = {len(args) + k: 6 + k for k in range(4)}
            in_specs += [pl.BlockSpec(memory_space=pl.ANY)] * 4
            args += list(prev_caches)
    return pl.pallas_call(
        functools.partial(_qkv_kernel, latent),
        grid=(n // tm,),
        in_specs=in_specs,
        out_specs=out_specs,
        out_shape=out_shape,
        input_output_aliases=aliases,
        compiler_params=pltpu.CompilerParams(
            dimension_semantics=("arbitrary",), vmem_limit_bytes=VMEM_LIMIT),
        name="qkv_latent" if latent else "qkv_context",
    )(*args)


def _stack_halves(q, low):
    zero = jnp.zeros_like(q)
    return jnp.concatenate([jnp.where(low, q, zero), jnp.where(low, zero, q)], axis=0)


def _exp_parts(ss):
    m = ss[0].max(axis=-1, keepdims=True)
    for s in ss[1:]:
        m = jnp.maximum(m, s.max(axis=-1, keepdims=True))
    return [jnp.exp2(s - m) for s in ss]


def _diff_head(ss, pvs, lam):
    tq = ss[0].shape[0] // 2
    es = _exp_parts(ss)
    l = es[0].sum(axis=-1, keepdims=True)
    for e in es[1:]:
        l = l + e.sum(axis=-1, keepdims=True)
    r = 1.0 / l
    r0 = r[:tq]
    r1 = r[tq:] * lam
    o = None
    for e, pv in zip(es, pvs):
        t = pv((e[:tq] * r0 - e[tq:] * r1).astype(BF16))
        o = t if o is None else o + t
    return o


def _gqa_pair(ss, pvs, low):
    tq = ss[0].shape[0] // 2
    es = _exp_parts(ss)
    r = None
    for e, pv in zip(es, pvs):
        t = pv(e.astype(BF16))
        r = t if r is None else r + t
    o = r[:, :LANES] * (1.0 / r[:, LANES:])
    return jnp.where(low, o[:tq], o[tq:])


def _attn_kernel(has_cache, lam_init, *refs):
    qa_ref, qb_ref, ka_ref, va_ref, kb_ref, vb_ref = refs[:6]
    if has_cache:
        cka_ref, cva_ref, ckb_ref, cvb_ref = refs[6:10]
        refs = refs[10:]
    else:
        refs = refs[6:]
    lq1, lk1, lq2, lk2, subln_ref, o_ref = refs
    lane = lax.broadcasted_iota(jnp.int32, (1, LANES), 1)
    low = lane < HEAD_DIM
    lam = (jnp.exp(jnp.sum(lq1[...] * lk1[...], axis=-1, keepdims=True))
           - jnp.exp(jnp.sum(lq2[...] * lk2[...], axis=-1, keepdims=True)) + lam_init)
    subln = subln_ref[...]

    def twice(x):
        return jnp.concatenate([x, x], axis=0)

    tasks = []

    def diff_finish(ss, pvs, cs):
        o = _diff_head(ss, pvs, lam)
        o_ref[:, cs] = (_rms(o, subln) * (1.0 - lam_init)).astype(o_ref.dtype)

    def gqa_finish(ss, pvs, cs):
        o_ref[:, cs] = _gqa_pair(ss, pvs, low).astype(o_ref.dtype)

    for h in range(DIFF_HEADS):
        cs = slice(h * LANES, (h + 1) * LANES)
        pvs = [lambda w, cs=cs: _dot(w, va_ref[:, cs])]
        if has_cache:
            scores = [lambda q, cs=cs: _dot_nt(q, ka_ref[:, cs]),
                      lambda q, h=h: _dot(q, cka_ref[h].astype(BF16))]
            pvs.append(lambda w, h=h: _dot(
                w, cva_ref[pl.ds(h, cva_ref.shape[0] // DIFF_HEADS, stride=DIFF_HEADS), :].astype(BF16)))
        else:
            scores = [lambda q, cs=cs: _dot(q, ka_ref[cs, :])]
        tasks.append((
            lambda cs=cs, scores=scores: [f(_stack_halves(qa_ref[:, cs], low)) for f in scores],
            lambda ss, pvs=pvs, cs=cs: diff_finish(ss, pvs, cs)))
    for g in range(GQA_KV_HEADS):
        cs = slice(g * LANES, (g + 1) * LANES)
        gs = slice(g * HEAD_DIM, (g + 1) * HEAD_DIM)
        pvs = [lambda p, cs=cs: _dot(p, jnp.concatenate(
            [vb_ref[:, cs], jnp.ones((vb_ref.shape[0], LANES), BF16)], axis=1))]
        if has_cache:
            scores = [lambda q, cs=cs: _dot_nt(q, kb_ref[:, cs]),
                      lambda q, g=g: _dot(q, twice(ckb_ref[g]).astype(BF16))]
            pvs.append(lambda p, g=g: _dot_nt(p, jnp.concatenate(
                [twice(cvb_ref[g]).astype(BF16), jnp.ones((LANES, cvb_ref.shape[2]), BF16)], axis=0)))
        else:
            scores = [lambda q, gs=gs: _dot(q, twice(kb_ref[gs, :]))]
        for rp in range(2):
            qs = slice((2 * g + rp) * LANES, (2 * g + rp + 1) * LANES)
            os = slice(512 + qs.start, 512 + qs.stop)
            tasks.append((
                lambda qs=qs, scores=scores: [f(_stack_halves(qb_ref[:, qs], low)) for f in scores],
                lambda ss, pvs=pvs, os=os: gqa_finish(ss, pvs, os)))

    ahead = 2
    pending = [t[0]() for t in tasks[:ahead]]
    for i, (_, finish) in enumerate(tasks):
        if i + ahead < len(tasks):
            pending.append(tasks[i + ahead][0]())
        finish(pending.pop(0))


def _attention(layer, qa, qb, own, cache, seq, lam_vecs, subln):
    n = qa.shape[0]
    requests = n // seq
    tq = min(Q_TILE, seq)
    nq = seq // tq
    lam_init = 0.8 - 0.6 * math.exp(-0.3 * layer)
    qspec = pl.BlockSpec((tq, 512), lambda b, i: (b * nq + i, 0))
    in_specs = [qspec, qspec]
    args = [qa, qb]
    for a in own:
        in_specs.append(pl.BlockSpec((a.shape[0] // requests, a.shape[1]), lambda b, i: (b, 0)))
        args.append(a)
    if cache is not None:
        for a in cache:
            nd = a.ndim - 2
            in_specs.append(pl.BlockSpec((None, None) + a.shape[2:],
                                         lambda b, i, nd=nd: (b, layer) + (0,) * nd))
            args.append(a)
    in_specs += [_layer_spec((1, HEAD_DIM), layer)] * 4 + [_layer_spec((1, 2 * HEAD_DIM), layer)]
    args += list(lam_vecs) + [subln]
    return pl.pallas_call(
        functools.partial(_attn_kernel, cache is not None, lam_init),
        grid=(requests, nq),
        in_specs=in_specs,
        out_specs=pl.BlockSpec((tq, D_MODEL), lambda b, i: (b * nq + i, 0)),
        out_shape=jax.ShapeDtypeStruct((n, D_MODEL), BF16),
        compiler_params=pltpu.CompilerParams(
            dimension_semantics=("arbitrary", "arbitrary"), vmem_limit_bytes=VMEM_LIMIT),
        name="attn_latent" if cache is not None else "attn_context",
    )(*args)


def _ffn_kernel(x_ref, mix_ref, mod_ref, wo_ref, nf_ref, wgu_ref, wd_ref, o_ref):
    g_a = mod_ref[:, 2 * D_MODEL:3 * D_MODEL]
    sh_f = mod_ref[:, 3 * D_MODEL:4 * D_MODEL]
    sc_f = mod_ref[:, 4 * D_MODEL:5 * D_MODEL]
    g_f = mod_ref[:, 5 * D_MODEL:6 * D_MODEL]
    half = x_ref.shape[0] // 2
    rows = (slice(0, half), slice(half, 2 * half))

    def attn_residual(r):
        return x_ref[r, :] + g_a * _dot(mix_ref[r, :], wo_ref[...])

    def gate_up(x):
        h = (_rms(x, nf_ref[...]) * (1.0 + sc_f) + sh_f).astype(BF16)
        return _dot(h, wgu_ref[...])

    def ffn_residual(r, x, gu):
        gate = gu[:, :FFN_HIDDEN]
        up = gu[:, FFN_HIDDEN:]
        act = (gate * jax.nn.sigmoid(gate) * up).astype(BF16)
        o_ref[r, :] = x + g_f * _dot(act, wd_ref[...])

    xs = [attn_residual(r) for r in rows]
    gus = [gate_up(x) for x in xs]
    for r, x, gu in zip(rows, xs, gus):
        ffn_residual(r, x, gu)


def _ffn(x, mix, mod4, layer, latent, tiles_per_request, w_out, norm_ffn, w_gu, w_down):
    n = x.shape[0]
    tm = 2 * TOKEN_TILE
    tiles_per_request = max(tiles_per_request // 2, 1)
    return pl.pallas_call(
        _ffn_kernel,
        grid=(n // tm,),
        in_specs=[
            pl.BlockSpec((tm, D_MODEL), lambda i: (i, 0)),
            pl.BlockSpec((tm, D_MODEL), lambda i: (i, 0)),
            _mod_spec(layer, latent, tiles_per_request),
            _layer_spec((D_MODEL, D_MODEL), layer),
            _layer_spec((1, D_MODEL), layer),
            _layer_spec((D_MODEL, 2 * FFN_HIDDEN), layer),
            _layer_spec((FFN_HIDDEN, D_MODEL), layer),
        ],
        out_specs=pl.BlockSpec((tm, D_MODEL), lambda i: (i, 0)),
        out_shape=jax.ShapeDtypeStruct((n, D_MODEL), F32),
        compiler_params=pltpu.CompilerParams(
            dimension_semantics=("arbitrary",), vmem_limit_bytes=VMEM_LIMIT),
        name="ffn_latent" if latent else "ffn_context",
    )(x, mix, mod4, w_out, norm_ffn, w_gu, w_down)


def _rope_tables(n_tokens):
    t = jnp.arange(n_tokens, dtype=jnp.int32)
    row = (t // GRID_W).astype(F32)
    col = (t % GRID_W).astype(F32)
    axis_dim = HEAD_DIM // 2
    freqs = ROPE_THETA ** (-jnp.arange(0, axis_dim, 2, dtype=F32) / axis_dim)
    ang = jnp.concatenate([row[:, None] * freqs, col[:, None] * freqs], axis=-1)
    c, s = jnp.cos(ang), jnp.sin(ang)
    return jnp.concatenate([c, c, c, c], axis=-1), jnp.concatenate([-s, s, -s, s], axis=-1)


def kernel(x_prompt, x_sample, cache_diff_k, cache_diff_v, cache_gqa_k, cache_gqa_v, c, c_ctx, w_mod, b_mod, norm_attn, w_in, q_norm_a, k_norm_a, lambda_q1, lambda_k1, lambda_q2, lambda_k2, subln, q_norm_b, k_norm_b, w_out, norm_ffn, w_gate_up, w_down):
    batch, seq, d = x_prompt.shape
    dec_batch, dec_seq, _ = x_sample.shape
    depth = w_mod.shape[0]
    past = cache_diff_k.shape[2]

    cond = jnp.concatenate([c_ctx[None], c, jnp.zeros((MOD_ROWS - 1 - dec_batch, d), F32)], axis=0)
    mod = _modulation(cond, w_mod, b_mod)
    mod4 = mod.reshape(depth, MOD_ROWS, 1, 6 * d)

    w_in_b = w_in.astype(BF16)
    w_out_b = w_out.astype(BF16)
    w_gu_b = w_gate_up.astype(BF16)
    w_down_b = w_down.astype(BF16)
    seg_ones = jnp.kron(jnp.eye(256 // HEAD_DIM, dtype=F32), jnp.ones((HEAD_DIM, HEAD_DIM), F32)).astype(BF16)
    rope = _rope_tables(dec_seq)

    cache = (jnp.transpose(cache_diff_k, (0, 1, 3, 4, 5, 2)).reshape(dec_batch, depth, DIFF_HEADS, 2 * HEAD_DIM, past),
             cache_diff_v.reshape(dec_batch, depth, past * DIFF_HEADS, 2 * HEAD_DIM),
             jnp.transpose(cache_gqa_k, (0, 1, 3, 4, 2)),
             jnp.transpose(cache_gqa_v, (0, 1, 3, 4, 2)))

    row = lambda a: a.reshape(depth, 1, a.shape[-1])
    gains = (row(q_norm_a), row(k_norm_a), row(q_norm_b), row(k_norm_b))
    lam_vecs = (row(lambda_q1), row(lambda_k1), row(lambda_q2), row(lambda_k2))
    na, nf, sl = row(norm_attn), row(norm_ffn), row(subln)

    yp = x_prompt.reshape(batch * seq, d)
    ys = x_sample.reshape(dec_batch * dec_seq, d)
    ctx_tiles = seq // TOKEN_TILE
    lat_tiles = dec_seq // TOKEN_TILE
    new_caches = None
    for l in range(depth):
        qa, ka, va, qb, kb, vb, *new_caches = _qkv(
            yp, mod4, l, False, ctx_tiles, na, w_in_b, seg_ones, gains, None, new_caches)
        mix = _attention(l, qa, qb, (ka, va, kb, vb), None, seq, lam_vecs, sl)
        yp = _ffn(yp, mix, mod4, l, False, ctx_tiles, w_out_b, nf, w_gu_b, w_down_b)

        qa, ka, va, qb, kb, vb = _qkv(
            ys, mod4, l, True, lat_tiles, na, w_in_b, seg_ones, gains, rope, None)
        mix = _attention(l, qa, qb, (ka, va, kb, vb), cache, dec_seq, lam_vecs, sl)
        ys = _ffn(ys, mix, mod4, l, True, lat_tiles, w_out_b, nf, w_gu_b, w_down_b)

    ka_t, va_n, kb_t, vb_t = new_caches
    new_diff_k = jnp.transpose(ka_t.reshape(batch, depth, DIFF_HEADS, 2, HEAD_DIM, seq), (0, 1, 5, 2, 3, 4))
    new_gqa_k = jnp.transpose(kb_t.reshape(batch, depth, GQA_KV_HEADS, HEAD_DIM, seq), (0, 1, 4, 2, 3))
    new_gqa_v = jnp.transpose(vb_t.reshape(batch, depth, GQA_KV_HEADS, HEAD_DIM, seq), (0, 1, 4, 2, 3))
    return (yp.reshape(batch, seq, d), ys.reshape(dec_batch, dec_seq, d),
            new_diff_k, va_n.reshape(batch, depth, seq, DIFF_HEADS, 2 * HEAD_DIM), new_gqa_k, new_gqa_v)
```

```python
import functools
import math

import jax
import jax.numpy as jnp
from jax import lax
from jax.experimental import pallas as pl
from jax.experimental.pallas import tpu as pltpu

F32 = jnp.float32
BF16 = jnp.bfloat16

D_MODEL = 1024
HEAD_DIM = 64
GRID_W = 64
DIFF_HEADS = 4
GQA_KV_HEADS = 2
GQA_REP = 4
FFN_HIDDEN = 2816
IN_COLS = 2304
ROPE_THETA = 10000.0
EPS = 1e-6
LANES = 128
MOD_ROWS = 8
TOKEN_TILE = 256
Q_TILE = 256
VMEM_LIMIT = 56 * 1024 * 1024


def _dot(a, b):
    return jnp.dot(a, b, preferred_element_type=F32)


def _dot_nt(a, b):
    return lax.dot_general(a, b, (((1,), (1,)), ((), ())), preferred_element_type=F32)


def _rms(x, gain):
    ms = jnp.mean(x * x, axis=-1, keepdims=True)
    return x * lax.rsqrt(ms + EPS) * gain


def _layer_spec(shape, layer):
    n = len(shape)
    return pl.BlockSpec((None,) + tuple(shape), lambda *_: (layer,) + (0,) * n,
                        pipeline_mode=pl.Buffered(1))


def _const_spec(shape):
    n = len(shape)
    return pl.BlockSpec(shape, lambda *_: (0,) * n, pipeline_mode=pl.Buffered(1))


def _mod_spec(layer, latent, tiles_per_request):
    if latent:
        return pl.BlockSpec((None, None, 1, 6 * D_MODEL),
                            lambda i: (layer, 1 + i // tiles_per_request, 0, 0))
    return pl.BlockSpec((None, None, 1, 6 * D_MODEL), lambda i: (layer, 0, 0, 0))


def _mod_kernel(cond_ref, w_ref, b_ref, o_ref):
    c = cond_ref[...]
    s = c * jax.nn.sigmoid(c)
    o_ref[...] = _dot(s.astype(BF16), w_ref[...].astype(BF16)) + b_ref[...]


def _modulation(cond, w_mod, b_mod):
    depth = w_mod.shape[0]
    tn = 1536
    return pl.pallas_call(
        _mod_kernel,
        grid=(depth, 6 * D_MODEL // tn),
        in_specs=[
            pl.BlockSpec((MOD_ROWS, D_MODEL), lambda l, j: (0, 0)),
            pl.BlockSpec((None, D_MODEL, tn), lambda l, j: (l, 0, j)),
            pl.BlockSpec((None, 1, tn), lambda l, j: (l, 0, j)),
        ],
        out_specs=pl.BlockSpec((None, MOD_ROWS, tn), lambda l, j: (l, 0, j)),
        out_shape=jax.ShapeDtypeStruct((depth, MOD_ROWS, 6 * D_MODEL), F32),
        compiler_params=pltpu.CompilerParams(
            dimension_semantics=("arbitrary", "arbitrary"), vmem_limit_bytes=VMEM_LIMIT),
        name="modulation",
    )(cond, w_mod, b_mod.reshape(depth, 1, 6 * D_MODEL))


def _head_norm(x, seg_ones, gain):
    ss = _dot((x * x).astype(BF16), seg_ones)
    return x * lax.rsqrt(ss * (1.0 / HEAD_DIM) + EPS) * gain


def _swap_halves(x, first_half):
    return jnp.where(first_half, pltpu.roll(x, 96, 1), pltpu.roll(x, 32, 1))


def _rope(x, cos, sin, first_half):
    out = []
    for j in range(x.shape[1] // LANES):
        xj = x[:, j * LANES:(j + 1) * LANES]
        out.append(xj * cos + _swap_halves(xj, first_half) * sin)
    return out[0] if len(out) == 1 else jnp.concatenate(out, axis=1)


def _tile_gain(g):
    return jnp.concatenate([g] * (256 // HEAD_DIM), axis=1)


def _qkv_kernel(latent, *refs):
    if latent:
        (x_ref, mod_ref, na_ref, w_ref, seg_ref, gqa_ref, gka_ref, gqb_ref, gkb_ref,
         cos_ref, sin_ref, qa_o, ka_o, va_o, qb_o, kb_o, vb_o) = refs
    else:
        (x_ref, mod_ref, na_ref, w_ref, seg_ref, gqa_ref, gka_ref, gqb_ref, gkb_ref) = refs[:9]
        qa_o, ka_o, va_o, qb_o, kb_o, vb_o, kaf_o, vaf_o, kbf_o, vbf_o = refs[-10:]
    tm = TOKEN_TILE
    sh = mod_ref[:, 0:D_MODEL]
    sc = mod_ref[:, D_MODEL:2 * D_MODEL]
    seg = seg_ref[...]
    lane = lax.broadcasted_iota(jnp.int32, (1, LANES), 1)
    low = lane < HEAD_DIM
    first_half = (lane % HEAD_DIM) < (HEAD_DIM // 2)
    scale = HEAD_DIM ** -0.5 * math.log2(math.e)

    def project(r):
        h = _rms(x_ref[r, :], na_ref[...]) * (1.0 + sc) + sh
        return _dot(h.astype(BF16), w_ref[...])

    def finish(j, r, proj):
        def qk(lo_col, width, gain_ref):
            gain = _tile_gain(gain_ref[...])
            out = []
            for c0 in range(0, width, 256):
                w = min(256, width - c0)
                out.append(_head_norm(proj[:, lo_col + c0:lo_col + c0 + w], seg[:w, :w], gain[:, :w]))
            y = out[0] if len(out) == 1 else jnp.concatenate(out, axis=1)
            if latent:
                y = _rope(y, cos_ref[r, :], sin_ref[r, :], first_half)
            return y

        qa = qk(0, 512, gqa_ref)
        ka = qk(512, 512, gka_ref)
        va = proj[:, 1024:1536]
        qb = qk(1536, 512, gqb_ref)
        kb = qk(2048, 128, gkb_ref)
        vb = proj[:, 2176:2304]
        qa_o[r, :] = (qa * scale).astype(BF16)
        ka_o[r, :] = ka.astype(BF16)
        va_o[r, :] = va.astype(BF16)
        qb_o[r, :] = (qb * scale).astype(BF16)
        kb_o[r, :] = kb.astype(BF16)
        vb_o[r, :] = vb.astype(BF16)
        if not latent:
            kaf_o[j] = ka.T
            kbf_o[j] = kb.T
            vbf_o[j] = vb.T
            for h in range(DIFF_HEADS):
                vaf_o[j, pl.ds(h, tm, stride=DIFF_HEADS), :] = va[:, h * LANES:(h + 1) * LANES]

    rows = [slice(j * tm, (j + 1) * tm) for j in range(x_ref.shape[0] // tm)]
    projs = [project(r) for r in rows]
    for j, (r, proj) in enumerate(zip(rows, projs)):
        finish(j, r, proj)


def _qkv(x, mod4, layer, latent, tiles_per_request, norm_attn, w_in, seg_ones, gains, rope, prev_caches):
    n = x.shape[0]
    sub = 2
    tm = sub * TOKEN_TILE
    depth = w_in.shape[0]
    tok = lambda w: pl.BlockSpec((tm, w), lambda i: (i, 0))
    in_specs = [
        tok(D_MODEL),
        _mod_spec(layer, latent, max(tiles_per_request // sub, 1)),
        _layer_spec((1, D_MODEL), layer),
        _layer_spec((D_MODEL, IN_COLS), layer),
        _const_spec((256, 256)),
    ] + [_layer_spec((1, HEAD_DIM), layer)] * 4
    args = [x, mod4, norm_attn, w_in, seg_ones, *gains]
    aliases = {}
    out_specs = [tok(512), tok(512), tok(512), tok(512), tok(LANES), tok(LANES)]
    out_shape = [jax.ShapeDtypeStruct((n, w), BF16) for w in (512, 512, 512, 512, LANES, LANES)]
    if latent:
        rope_spec = pl.BlockSpec((tm, LANES), lambda i: (i % (tiles_per_request // sub), 0))
        in_specs += [rope_spec, rope_spec]
        args += list(rope)
    else:
        assert tiles_per_request == 1, "context requests must be one token tile long"
        seq = TOKEN_TILE
        req = n // seq
        slab = lambda *dims: pl.BlockSpec((sub, None) + dims, lambda i: (i, layer) + (0,) * len(dims))
        out_specs += [slab(512, seq), slab(seq * DIFF_HEADS, LANES), slab(LANES, seq), slab(LANES, seq)]
        out_shape += [jax.ShapeDtypeStruct((req, depth, 512, seq), F32),
                      jax.ShapeDtypeStruct((req, depth, seq * DIFF_HEADS, LANES), F32),
                      jax.ShapeDtypeStruct((req, depth, LANES, seq), F32),
                      jax.ShapeDtypeStruct((req, depth, LANES, seq), F32)]
        if prev_caches is not None:
            aliases = {len(args) + k: 6 + k for k in range(4)}
            in_specs += [pl.BlockSpec(memory_space=pl.ANY)] * 4
            args += list(prev_caches)
    return pl.pallas_call(
        functools.partial(_qkv_kernel, latent),
        grid=(n // tm,),
        in_specs=in_specs,
        out_specs=out_specs,
        out_shape=out_shape,
        input_output_aliases=aliases,
        compiler_params=pltpu.CompilerParams(
            dimension_semantics=("arbitrary",), vmem_limit_bytes=VMEM_LIMIT),
        name="qkv_latent" if latent else "qkv_context",
    )(*args)


def _pad_rows(x, block):
    z = jnp.zeros_like(x)
    return jnp.concatenate([x, z] if block == 0 else [z, x], axis=0)


ONES_ROWS = 16


def _softmax_pv(vt, st):
    d = vt.shape[0]
    e = jnp.exp2(st - st.max(axis=0, keepdims=True)).astype(BF16)
    ext = _dot(jnp.concatenate([vt, jnp.ones((ONES_ROWS, vt.shape[1]), BF16)], axis=0), e)
    return ext[:d], 1.0 / ext[d:d + 1]


def _attn_kernel(has_cache, lam_init, *refs):
    qa_ref, qb_ref, ka_ref, va_ref, kb_ref, vb_ref = refs[:6]
    if has_cache:
        cka_ref, cva_ref, ckb_ref, cvb_ref = refs[6:10]
        refs = refs[10:]
    else:
        refs = refs[6:]
    lq1, lk1, lq2, lk2, subln_ref, o_ref, k_all, vt_all = refs
    seq = ka_ref.shape[0]
    tq = qa_ref.shape[0]
    ka_w = DIFF_HEADS * LANES

    @pl.when(pl.program_id(1) == 0)
    def _():
        k_all[0:seq, 0:ka_w] = ka_ref[...]
        k_all[0:seq, ka_w:] = kb_ref[...]
        vt_all[0:ka_w, 0:seq] = va_ref[...].astype(F32).T.astype(BF16)
        vt_all[ka_w:, 0:seq] = vb_ref[...].astype(F32).T.astype(BF16)
        if has_cache:
            past = cka_ref.shape[2]
            for h in range(DIFF_HEADS):
                cs = slice(h * LANES, (h + 1) * LANES)
                k_all[seq:, cs] = cka_ref[h].T.astype(BF16)
                vt_all[cs, seq:] = cva_ref[pl.ds(h, past, stride=DIFF_HEADS), :].T.astype(BF16)
            k_all[seq:, ka_w:] = jnp.concatenate([ckb_ref[0], ckb_ref[1]], axis=0).T.astype(BF16)
            for g in range(GQA_KV_HEADS):
                vt_all[ka_w + g * HEAD_DIM:ka_w + (g + 1) * HEAD_DIM, seq:] = cvb_ref[g].astype(BF16)

    lam = (jnp.exp(jnp.sum(lq1[...] * lk1[...], axis=-1, keepdims=True))
           - jnp.exp(jnp.sum(lq2[...] * lk2[...], axis=-1, keepdims=True)) + lam_init)
    subln = subln_ref[...]
    qat = qa_ref[...].astype(F32).T.astype(BF16)
    qbt = qb_ref[...].astype(F32).T.astype(BF16)

    tasks = []

    def diff_finish(st, h, cs):
        num, r = _softmax_pv(vt_all[cs, :], st)
        ot = num[:, :tq] * r[:, :tq] - num[:, tq:] * (r[:, tq:] * lam)
        ms = jnp.mean(ot * ot, axis=0, keepdims=True)
        ot = ot * lax.rsqrt(ms + EPS) * subln * (1.0 - lam_init)
        o_ref[:, cs] = ot.T.astype(o_ref.dtype)

    def gqa_finish(st, g, os):
        num, r = _softmax_pv(vt_all[ka_w + g * HEAD_DIM:ka_w + (g + 1) * HEAD_DIM, :], st)
        ot = num * r
        o_ref[:, os] = jnp.concatenate([ot[:, :tq], ot[:, tq:]], axis=0).T.astype(o_ref.dtype)

    for h in range(DIFF_HEADS):
        cs = slice(h * LANES, (h + 1) * LANES)

        def logits(h=h, cs=cs):
            c0 = qat[h * LANES:h * LANES + HEAD_DIM]
            c1 = qat[h * LANES + HEAD_DIM:(h + 1) * LANES]
            return _dot(k_all[:, cs], jnp.concatenate([_pad_rows(c0, 0), _pad_rows(c1, 1)], axis=1))

        tasks.append((logits, lambda st, h=h, cs=cs: diff_finish(st, h, cs)))
    for g in range(GQA_KV_HEADS):
        for rp in range(2):
            r0 = (GQA_REP * g + 2 * rp) * HEAD_DIM
            os = slice(ka_w + r0, ka_w + r0 + LANES)

            def logits(g=g, r0=r0):
                a = qbt[r0:r0 + HEAD_DIM]
                b = qbt[r0 + HEAD_DIM:r0 + LANES]
                return _dot(k_all[:, ka_w:], jnp.concatenate([_pad_rows(a, g), _pad_rows(b, g)], axis=1))

            tasks.append((logits, lambda st, g=g, os=os: gqa_finish(st, g, os)))

    ahead = 2
    pending = [t[0]() for t in tasks[:ahead]]
    for i, (_, finish) in enumerate(tasks):
        if i + ahead < len(tasks):
            pending.append(tasks[i + ahead][0]())
        finish(pending.pop(0))


def _attention(layer, qa, qb, own, cache, seq, lam_vecs, subln):
    n = qa.shape[0]
    requests = n // seq
    tq = min(Q_TILE, seq)
    nq = seq // tq
    lam_init = 0.8 - 0.6 * math.exp(-0.3 * layer)
    qspec = pl.BlockSpec((tq, 512), lambda b, i: (b * nq + i, 0))
    in_specs = [qspec, qspec]
    args = [qa, qb]
    for a in own:
        in_specs.append(pl.BlockSpec((a.shape[0] // requests, a.shape[1]), lambda b, i: (b, 0)))
        args.append(a)
    if cache is not None:
        for a in cache:
            nd = a.ndim - 2
            in_specs.append(pl.BlockSpec((None, None) + a.shape[2:],
                                         lambda b, i, nd=nd: (b, layer) + (0,) * nd))
            args.append(a)
    in_specs += [_layer_spec((1, HEAD_DIM), layer)] * 4 + [_layer_spec((2 * HEAD_DIM, 1), layer)]
    args += list(lam_vecs) + [subln]
    keys = seq + (cache[0].shape[-1] if cache is not None else 0)
    kv_w = own[0].shape[1] + own[2].shape[1]
    return pl.pallas_call(
        functools.partial(_attn_kernel, cache is not None, lam_init),
        grid=(requests, nq),
        in_specs=in_specs,
        out_specs=pl.BlockSpec((tq, D_MODEL), lambda b, i: (b * nq + i, 0)),
        out_shape=jax.ShapeDtypeStruct((n, D_MODEL), BF16),
        scratch_shapes=[pltpu.VMEM((keys, kv_w), BF16), pltpu.VMEM((kv_w, keys), BF16)],
        compiler_params=pltpu.CompilerParams(
            dimension_semantics=("arbitrary", "arbitrary"), vmem_limit_bytes=VMEM_LIMIT),
        name="attn_latent" if cache is not None else "attn_context",
    )(*args)


def _ffn_kernel(x_ref, mix_ref, mod_ref, wo_ref, nf_ref, wgu_ref, wd_ref, o_ref):
    g_a = mod_ref[:, 2 * D_MODEL:3 * D_MODEL]
    sh_f = mod_ref[:, 3 * D_MODEL:4 * D_MODEL]
    sc_f = mod_ref[:, 4 * D_MODEL:5 * D_MODEL]
    g_f = mod_ref[:, 5 * D_MODEL:6 * D_MODEL]
    half = x_ref.shape[0] // 2
    rows = (slice(0, half), slice(half, 2 * half))

    def attn_residual(r):
        return x_ref[r, :] + g_a * _dot(mix_ref[r, :], wo_ref[...])

    def gate_up(x):
        h = (_rms(x, nf_ref[...]) * (1.0 + sc_f) + sh_f).astype(BF16)
        return _dot(h, wgu_ref[...])

    def ffn_residual(r, x, gu):
        gate = gu[:, :FFN_HIDDEN]
        up = gu[:, FFN_HIDDEN:]
        act = (gate * jax.nn.sigmoid(gate) * up).astype(BF16)
        o_ref[r, :] = x + g_f * _dot(act, wd_ref[...])

    xs = [attn_residual(r) for r in rows]
    gus = [gate_up(x) for x in xs]
    for r, x, gu in zip(rows, xs, gus):
        ffn_residual(r, x, gu)


def _ffn(x, mix, mod4, layer, latent, tiles_per_request, w_out, norm_ffn, w_gu, w_down):
    n = x.shape[0]
    tm = 2 * TOKEN_TILE
    tiles_per_request = max(tiles_per_request // 2, 1)
    return pl.pallas_call(
        _ffn_kernel,
        grid=(n // tm,),
        in_specs=[
            pl.BlockSpec((tm, D_MODEL), lambda i: (i, 0)),
            pl.BlockSpec((tm, D_MODEL), lambda i: (i, 0)),
            _mod_spec(layer, latent, tiles_per_request),
            _layer_spec((D_MODEL, D_MODEL), layer),
            _layer_spec((1, D_MODEL), layer),
            _layer_spec((D_MODEL, 2 * FFN_HIDDEN), layer),
            _layer_spec((FFN_HIDDEN, D_MODEL), layer),
        ],
        out_specs=pl.BlockSpec((tm, D_MODEL), lambda i: (i, 0)),
        out_shape=jax.ShapeDtypeStruct((n, D_MODEL), F32),
        compiler_params=pltpu.CompilerParams(
            dimension_semantics=("arbitrary",), vmem_limit_bytes=VMEM_LIMIT),
        name="ffn_latent" if latent else "ffn_context",
    )(x, mix, mod4, w_out, norm_ffn, w_gu, w_down)


def _rope_tables(n_tokens):
    t = jnp.arange(n_tokens, dtype=jnp.int32)
    row = (t // GRID_W).astype(F32)
    col = (t % GRID_W).astype(F32)
    axis_dim = HEAD_DIM // 2
    freqs = ROPE_THETA ** (-jnp.arange(0, axis_dim, 2, dtype=F32) / axis_dim)
    ang = jnp.concatenate([row[:, None] * freqs, col[:, None] * freqs], axis=-1)
    c, s = jnp.cos(ang), jnp.sin(ang)
    return jnp.concatenate([c, c, c, c], axis=-1), jnp.concatenate([-s, s, -s, s], axis=-1)


def kernel(x_prompt, x_sample, cache_diff_k, cache_diff_v, cache_gqa_k, cache_gqa_v, c, c_ctx, w_mod, b_mod, norm_attn, w_in, q_norm_a, k_norm_a, lambda_q1, lambda_k1, lambda_q2, lambda_k2, subln, q_norm_b, k_norm_b, w_out, norm_ffn, w_gate_up, w_down):
    batch, seq, d = x_prompt.shape
    dec_batch, dec_seq, _ = x_sample.shape
    depth = w_mod.shape[0]
    past = cache_diff_k.shape[2]

    cond = jnp.concatenate([c_ctx[None], c, jnp.zeros((MOD_ROWS - 1 - dec_batch, d), F32)], axis=0)
    mod = _modulation(cond, w_mod, b_mod)
    mod4 = mod.reshape(depth, MOD_ROWS, 1, 6 * d)

    w_in_b = w_in.astype(BF16)
    w_out_b = w_out.astype(BF16)
    w_gu_b = w_gate_up.astype(BF16)
    w_down_b = w_down.astype(BF16)
    seg_ones = jnp.kron(jnp.eye(256 // HEAD_DIM, dtype=F32), jnp.ones((HEAD_DIM, HEAD_DIM), F32)).astype(BF16)
    rope = _rope_tables(dec_seq)

    cache = (jnp.transpose(cache_diff_k, (0, 1, 3, 4, 5, 2)).reshape(dec_batch, depth, DIFF_HEADS, 2 * HEAD_DIM, past),
             cache_diff_v.reshape(dec_batch, depth, past * DIFF_HEADS, 2 * HEAD_DIM),
             jnp.transpose(cache_gqa_k, (0, 1, 3, 4, 2)),
             jnp.transpose(cache_gqa_v, (0, 1, 3, 4, 2)))

    row = lambda a: a.reshape(depth, 1, a.shape[-1])
    gains = (row(q_norm_a), row(k_norm_a), row(q_norm_b), row(k_norm_b))
    lam_vecs = (row(lambda_q1), row(lambda_k1), row(lambda_q2), row(lambda_k2))
    na, nf, sl = row(norm_attn), row(norm_ffn), subln.reshape(depth, 2 * HEAD_DIM, 1)

    yp = x_prompt.reshape(batch * seq, d)
    ys = x_sample.reshape(dec_batch * dec_seq, d)
    ctx_tiles = seq // TOKEN_TILE
    lat_tiles = dec_seq // TOKEN_TILE
    new_caches = None
    for l in range(depth):
        qa, ka, va, qb, kb, vb, *new_caches = _qkv(
            yp, mod4, l, False, ctx_tiles, na, w_in_b, seg_ones, gains, None, new_caches)
        mix = _attention(l, qa, qb, (ka, va, kb, vb), None, seq, lam_vecs, sl)
        yp = _ffn(yp, mix, mod4, l, False, ctx_tiles, w_out_b, nf, w_gu_b, w_down_b)

        qa, ka, va, qb, kb, vb = _qkv(
            ys, mod4, l, True, lat_tiles, na, w_in_b, seg_ones, gains, rope, None)
        mix = _attention(l, qa, qb, (ka, va, kb, vb), cache, dec_seq, lam_vecs, sl)
        ys = _ffn(ys, mix, mod4, l, True, lat_tiles, w_out_b, nf, w_gu_b, w_down_b)

    ka_t, va_n, kb_t, vb_t = new_caches
    new_diff_k = jnp.transpose(ka_t.reshape(batch, depth, DIFF_HEADS, 2, HEAD_DIM, seq), (0, 1, 5, 2, 3, 4))
    new_gqa_k = jnp.transpose(kb_t.reshape(batch, depth, GQA_KV_HEADS, HEAD_DIM, seq), (0, 1, 4, 2, 3))
    new_gqa_v = jnp.transpose(vb_t.reshape(batch, depth, GQA_KV_HEADS, HEAD_DIM, seq), (0, 1, 4, 2, 3))
    return (yp.reshape(batch, seq, d), ys.reshape(dec_batch, dec_seq, d),
            new_diff_k, va_n.reshape(batch, depth, seq, DIFF_HEADS, 2 * HEAD_DIM), new_gqa_k, new_gqa_v)
```

```python
import functools
import math

import jax
import jax.numpy as jnp
from jax import lax
from jax.experimental import pallas as pl
from jax.experimental.pallas import tpu as pltpu

F32 = jnp.float32
BF16 = jnp.bfloat16

D_MODEL = 1024
HEAD_DIM = 64
GRID_W = 64
DIFF_HEADS = 4
GQA_KV_HEADS = 2
GQA_REP = 4
FFN_HIDDEN = 2816
IN_COLS = 2304
ROPE_THETA = 10000.0
EPS = 1e-6
LANES = 128
MOD_ROWS = 8
TOKEN_TILE = 256
Q_TILE = 256
VMEM_LIMIT = 56 * 1024 * 1024


def _dot(a, b):
    return jnp.dot(a, b, preferred_element_type=F32)


def _dot_nt(a, b):
    return lax.dot_general(a, b, (((1,), (1,)), ((), ())), preferred_element_type=F32)


def _rms(x, gain):
    ms = jnp.mean(x * x, axis=-1, keepdims=True)
    return x * lax.rsqrt(ms + EPS) * gain


def _layer_spec(shape, layer):
    n = len(shape)
    return pl.BlockSpec((None,) + tuple(shape), lambda *_: (layer,) + (0,) * n,
                        pipeline_mode=pl.Buffered(1))


def _const_spec(shape):
    n = len(shape)
    return pl.BlockSpec(shape, lambda *_: (0,) * n, pipeline_mode=pl.Buffered(1))


def _mod_spec(layer, latent, tiles_per_request):
    if latent:
        return pl.BlockSpec((None, None, 1, 6 * D_MODEL),
                            lambda i: (layer, 1 + i // tiles_per_request, 0, 0))
    return pl.BlockSpec((None, None, 1, 6 * D_MODEL), lambda i: (layer, 0, 0, 0))


def _mod_kernel(cond_ref, w_ref, b_ref, o_ref):
    c = cond_ref[...]
    s = c * jax.nn.sigmoid(c)
    o_ref[...] = _dot(s.astype(BF16), w_ref[...].astype(BF16)) + b_ref[...]


def _modulation(cond, w_mod, b_mod):
    depth = w_mod.shape[0]
    tn = 1536
    return pl.pallas_call(
        _mod_kernel,
        grid=(depth, 6 * D_MODEL // tn),
        in_specs=[
            pl.BlockSpec((MOD_ROWS, D_MODEL), lambda l, j: (0, 0)),
            pl.BlockSpec((None, D_MODEL, tn), lambda l, j: (l, 0, j)),
            pl.BlockSpec((None, 1, tn), lambda l, j: (l, 0, j)),
        ],
        out_specs=pl.BlockSpec((None, MOD_ROWS, tn), lambda l, j: (l, 0, j)),
        out_shape=jax.ShapeDtypeStruct((depth, MOD_ROWS, 6 * D_MODEL), F32),
        compiler_params=pltpu.CompilerParams(
            dimension_semantics=("arbitrary", "arbitrary"), vmem_limit_bytes=VMEM_LIMIT),
        name="modulation",
    )(cond, w_mod, b_mod.reshape(depth, 1, 6 * D_MODEL))


def _head_norm(x, seg_ones, gain):
    ss = _dot((x * x).astype(BF16), seg_ones)
    return x * lax.rsqrt(ss * (1.0 / HEAD_DIM) + EPS) * gain


def _swap_halves(x, first_half):
    return jnp.where(first_half, pltpu.roll(x, 96, 1), pltpu.roll(x, 32, 1))


def _rope(x, cos, sin, first_half):
    out = []
    for j in range(x.shape[1] // LANES):
        xj = x[:, j * LANES:(j + 1) * LANES]
        out.append(xj * cos + _swap_halves(xj, first_half) * sin)
    return out[0] if len(out) == 1 else jnp.concatenate(out, axis=1)


def _tile_gain(g):
    return jnp.concatenate([g] * (256 // HEAD_DIM), axis=1)


def _qkv_kernel(latent, *refs):
    if latent:
        (x_ref, mod_ref, na_ref, w_ref, seg_ref, gqa_ref, gka_ref, gqb_ref, gkb_ref,
         cos_ref, sin_ref, qa_o, ka_o, va_o, qb_o, kb_o, vb_o) = refs
    else:
        (x_ref, mod_ref, na_ref, w_ref, seg_ref, gqa_ref, gka_ref, gqb_ref, gkb_ref) = refs[:9]
        qa_o, ka_o, va_o, qb_o, kb_o, vb_o, kaf_o, vaf_o, kbf_o, vbf_o = refs[-10:]
    tm = TOKEN_TILE
    sh = mod_ref[:, 0:D_MODEL]
    sc = mod_ref[:, D_MODEL:2 * D_MODEL]
    seg = seg_ref[...]
    lane = lax.broadcasted_iota(jnp.int32, (1, LANES), 1)
    low = lane < HEAD_DIM
    first_half = (lane % HEAD_DIM) < (HEAD_DIM // 2)
    scale = HEAD_DIM ** -0.5 * math.log2(math.e)

    def project(r):
        h = _rms(x_ref[r, :], na_ref[...]) * (1.0 + sc) + sh
        return _dot(h.astype(BF16), w_ref[...])

    def finish(j, r, proj):
        def qk(lo_col, width, gain_ref):
            gain = _tile_gain(gain_ref[...])
            out = []
            for c0 in range(0, width, 256):
                w = min(256, width - c0)
                out.append(_head_norm(proj[:, lo_col + c0:lo_col + c0 + w], seg[:w, :w], gain[:, :w]))
            y = out[0] if len(out) == 1 else jnp.concatenate(out, axis=1)
            if latent:
                y = _rope(y, cos_ref[r, :], sin_ref[r, :], first_half)
            return y

        qa = qk(0, 512, gqa_ref)
        ka = qk(512, 512, gka_ref)
        va = proj[:, 1024:1536]
        qb = qk(1536, 512, gqb_ref)
        kb = qk(2048, 128, gkb_ref)
        vb = proj[:, 2176:2304]
        qa_o[r, :] = (qa * scale).astype(BF16)
        ka_o[r, :] = ka.astype(BF16)
        va_o[r, :] = va.astype(BF16)
        qb_o[r, :] = (qb * scale).astype(BF16)
        kb_o[r, :] = kb.astype(BF16)
        vb_o[r, :] = vb.astype(BF16)
        if not latent:
            kaf_o[j] = ka.T
            kbf_o[j] = kb.T
            vbf_o[j] = vb.T
            for h in range(DIFF_HEADS):
                vaf_o[j, pl.ds(h, tm, stride=DIFF_HEADS), :] = va[:, h * LANES:(h + 1) * LANES]

    rows = [slice(j * tm, (j + 1) * tm) for j in range(x_ref.shape[0] // tm)]
    pending = [project(rows[0])]
    for j, r in enumerate(rows):
        if j + 1 < len(rows):
            pending.append(project(rows[j + 1]))
        finish(j, r, pending.pop(0))


def _qkv(x, mod4, layer, latent, tiles_per_request, norm_attn, w_in, seg_ones, gains, rope, prev_caches):
    n = x.shape[0]
    sub = 4
    tm = sub * TOKEN_TILE
    depth = w_in.shape[0]
    tok = lambda w: pl.BlockSpec((tm, w), lambda i: (i, 0))
    in_specs = [
        tok(D_MODEL),
        _mod_spec(layer, latent, max(tiles_per_request // sub, 1)),
        _layer_spec((1, D_MODEL), layer),
        _layer_spec((D_MODEL, IN_COLS), layer),
        _const_spec((256, 256)),
    ] + [_layer_spec((1, HEAD_DIM), layer)] * 4
    args = [x, mod4, norm_attn, w_in, seg_ones, *gains]
    aliases = {}
    out_specs = [tok(512), tok(512), tok(512), tok(512), tok(LANES), tok(LANES)]
    out_shape = [jax.ShapeDtypeStruct((n, w), BF16) for w in (512, 512, 512, 512, LANES, LANES)]
    if latent:
        rope_spec = pl.BlockSpec((tm, LANES), lambda i: (i % (tiles_per_request // sub), 0))
        in_specs += [rope_spec, rope_spec]
        args += list(rope)
    else:
        assert tiles_per_request == 1, "context requests must be one token tile long"
        seq = TOKEN_TILE
        req = n // seq
        slab = lambda *dims: pl.BlockSpec((sub, None) + dims, lambda i: (i, layer) + (0,) * len(dims))
        out_specs += [slab(512, seq), slab(seq * DIFF_HEADS, LANES), slab(LANES, seq), slab(LANES, seq)]
        out_shape += [jax.ShapeDtypeStruct((req, depth, 512, seq), F32),
                      jax.ShapeDtypeStruct((req, depth, seq * DIFF_HEADS, LANES), F32),
                      jax.ShapeDtypeStruct((req, depth, LANES, seq), F32),
                      jax.ShapeDtypeStruct((req, depth, LANES, seq), F32)]
        if prev_caches is not None:
            aliases = {len(args) + k: 6 + k for k in range(4)}
            in_specs += [pl.BlockSpec(memory_space=pl.ANY)] * 4
            args += list(prev_caches)
    return pl.pallas_call(
        functools.partial(_qkv_kernel, latent),
        grid=(n // tm,),
        in_specs=in_specs,
        out_specs=out_specs,
        out_shape=out_shape,
        input_output_aliases=aliases,
        compiler_params=pltpu.CompilerParams(
            dimension_semantics=("arbitrary",), vmem_limit_bytes=VMEM_LIMIT),
        name="qkv_latent" if latent else "qkv_context",
    )(*args)


def _pad_rows(x, block):
    z = jnp.zeros_like(x)
    return jnp.concatenate([x, z] if block == 0 else [z, x], axis=0)


ONES_ROWS = 16


def _softmax_pv(vt, st):
    d = vt.shape[0]
    e = jnp.exp2(st - st.max(axis=0, keepdims=True)).astype(BF16)
    ext = _dot(jnp.concatenate([vt, jnp.ones((ONES_ROWS, vt.shape[1]), BF16)], axis=0), e)
    return ext[:d], 1.0 / ext[d:d + 1]


def _attn_kernel(has_cache, lam_init, *refs):
    qa_ref, qb_ref, ka_ref, va_ref, kb_ref, vb_ref = refs[:6]
    if has_cache:
        cka_ref, cva_ref, ckb_ref, cvb_ref = refs[6:10]
        refs = refs[10:]
    else:
        refs = refs[6:]
    lq1, lk1, lq2, lk2, subln_ref, o_ref, k_all, vt_all = refs
    seq = ka_ref.shape[0]
    tq = qa_ref.shape[0]
    ka_w = DIFF_HEADS * LANES

    @pl.when(pl.program_id(1) == 0)
    def _():
        k_all[0:seq, 0:ka_w] = ka_ref[...]
        k_all[0:seq, ka_w:] = kb_ref[...]
        vt_all[0:ka_w, 0:seq] = va_ref[...].astype(F32).T.astype(BF16)
        vt_all[ka_w:, 0:seq] = vb_ref[...].astype(F32).T.astype(BF16)
        if has_cache:
            past = cka_ref.shape[2]
            for h in range(DIFF_HEADS):
                cs = slice(h * LANES, (h + 1) * LANES)
                k_all[seq:, cs] = cka_ref[h].T.astype(BF16)
                vt_all[cs, seq:] = cva_ref[pl.ds(h, past, stride=DIFF_HEADS), :].T.astype(BF16)
            k_all[seq:, ka_w:] = jnp.concatenate([ckb_ref[0], ckb_ref[1]], axis=0).T.astype(BF16)
            for g in range(GQA_KV_HEADS):
                vt_all[ka_w + g * HEAD_DIM:ka_w + (g + 1) * HEAD_DIM, seq:] = cvb_ref[g].astype(BF16)

    lam = (jnp.exp(jnp.sum(lq1[...] * lk1[...], axis=-1, keepdims=True))
           - jnp.exp(jnp.sum(lq2[...] * lk2[...], axis=-1, keepdims=True)) + lam_init)
    subln = subln_ref[...]
    qat = qa_ref[...].astype(F32).T.astype(BF16)
    qbt = qb_ref[...].astype(F32).T.astype(BF16)

    tasks = []

    def diff_finish(st, h, cs):
        num, r = _softmax_pv(vt_all[cs, :], st)
        ot = num[:, :tq] * r[:, :tq] - num[:, tq:] * (r[:, tq:] * lam)
        ms = jnp.mean(ot * ot, axis=0, keepdims=True)
        ot = ot * lax.rsqrt(ms + EPS) * subln * (1.0 - lam_init)
        o_ref[:, cs] = ot.T.astype(o_ref.dtype)

    def gqa_finish(st, g, os):
        num, r = _softmax_pv(vt_all[ka_w + g * HEAD_DIM:ka_w + (g + 1) * HEAD_DIM, :], st)
        ot = num * r
        o_ref[:, os] = jnp.concatenate([ot[:, :tq], ot[:, tq:]], axis=0).T.astype(o_ref.dtype)

    for h in range(DIFF_HEADS):
        cs = slice(h * LANES, (h + 1) * LANES)

        def logits(h=h, cs=cs):
            c0 = qat[h * LANES:h * LANES + HEAD_DIM]
            c1 = qat[h * LANES + HEAD_DIM:(h + 1) * LANES]
            return _dot(k_all[:, cs], jnp.concatenate([_pad_rows(c0, 0), _pad_rows(c1, 1)], axis=1))

        tasks.append((logits, lambda st, h=h, cs=cs: diff_finish(st, h, cs)))
    for g in range(GQA_KV_HEADS):
        for rp in range(2):
            r0 = (GQA_REP * g + 2 * rp) * HEAD_DIM
            os = slice(ka_w + r0, ka_w + r0 + LANES)

            def logits(g=g, r0=r0):
                a = qbt[r0:r0 + HEAD_DIM]
                b = qbt[r0 + HEAD_DIM:r0 + LANES]
                return _dot(k_all[:, ka_w:], jnp.concatenate([_pad_rows(a, g), _pad_rows(b, g)], axis=1))

            tasks.append((logits, lambda st, g=g, os=os: gqa_finish(st, g, os)))

    ahead = 2
    pending = [t[0]() for t in tasks[:ahead]]
    for i, (_, finish) in enumerate(tasks):
        if i + ahead < len(tasks):
            pending.append(tasks[i + ahead][0]())
        finish(pending.pop(0))


def _attention(layer, qa, qb, own, cache, seq, lam_vecs, subln):
    n = qa.shape[0]
    requests = n // seq
    tq = min(Q_TILE, seq)
    nq = seq // tq
    lam_init = 0.8 - 0.6 * math.exp(-0.3 * layer)
    qspec = pl.BlockSpec((tq, 512), lambda b, i: (b * nq + i, 0))
    in_specs = [qspec, qspec]
    args = [qa, qb]
    for a in own:
        in_specs.append(pl.BlockSpec((a.shape[0] // requests, a.shape[1]), lambda b, i: (b, 0)))
        args.append(a)
    if cache is not None:
        for a in cache:
            nd = a.ndim - 2
            in_specs.append(pl.BlockSpec((None, None) + a.shape[2:],
                                         lambda b, i, nd=nd: (b, layer) + (0,) * nd))
            args.append(a)
    in_specs += [_layer_spec((1, HEAD_DIM), layer)] * 4 + [_layer_spec((2 * HEAD_DIM, 1), layer)]
    args += list(lam_vecs) + [subln]
    keys = seq + (cache[0].shape[-1] if cache is not None else 0)
    kv_w = own[0].shape[1] + own[2].shape[1]
    return pl.pallas_call(
        functools.partial(_attn_kernel, cache is not None, lam_init),
        grid=(requests, nq),
        in_specs=in_specs,
        out_specs=pl.BlockSpec((tq, D_MODEL), lambda b, i: (b * nq + i, 0)),
        out_shape=jax.ShapeDtypeStruct((n, D_MODEL), BF16),
        scratch_shapes=[pltpu.VMEM((keys, kv_w), BF16), pltpu.VMEM((kv_w, keys), BF16)],
        compiler_params=pltpu.CompilerParams(
            dimension_semantics=("arbitrary", "arbitrary"), vmem_limit_bytes=VMEM_LIMIT),
        name="attn_latent" if cache is not None else "attn_context",
    )(*args)


def _ffn_kernel(x_ref, mix_ref, mod_ref, wo_ref, nf_ref, wgu_ref, wd_ref, o_ref):
    g_a = mod_ref[:, 2 * D_MODEL:3 * D_MODEL]
    sh_f = mod_ref[:, 3 * D_MODEL:4 * D_MODEL]
    sc_f = mod_ref[:, 4 * D_MODEL:5 * D_MODEL]
    g_f = mod_ref[:, 5 * D_MODEL:6 * D_MODEL]
    half = x_ref.shape[0] // 2
    rows = (slice(0, half), slice(half, 2 * half))

    def attn_residual(r):
        return x_ref[r, :] + g_a * _dot(mix_ref[r, :], wo_ref[...])

    def gate_up(x):
        h = (_rms(x, nf_ref[...]) * (1.0 + sc_f) + sh_f).astype(BF16)
        return _dot(h, wgu_ref[...])

    def ffn_residual(r, x, gu):
        gate = gu[:, :FFN_HIDDEN]
        up = gu[:, FFN_HIDDEN:]
        act = (gate * jax.nn.sigmoid(gate) * up).astype(BF16)
        o_ref[r, :] = x + g_f * _dot(act, wd_ref[...])

    xs = [attn_residual(r) for r in rows]
    gus = [gate_up(x) for x in xs]
    for r, x, gu in zip(rows, xs, gus):
        ffn_residual(r, x, gu)


def _ffn(x, mix, mod4, layer, latent, tiles_per_request, w_out, norm_ffn, w_gu, w_down):
    n = x.shape[0]
    tm = 2 * TOKEN_TILE
    tiles_per_request = max(tiles_per_request // 2, 1)
    return pl.pallas_call(
        _ffn_kernel,
        grid=(n // tm,),
        in_specs=[
            pl.BlockSpec((tm, D_MODEL), lambda i: (i, 0)),
            pl.BlockSpec((tm, D_MODEL), lambda i: (i, 0)),
            _mod_spec(layer, latent, tiles_per_request),
            _layer_spec((D_MODEL, D_MODEL), layer),
            _layer_spec((1, D_MODEL), layer),
            _layer_spec((D_MODEL, 2 * FFN_HIDDEN), layer),
            _layer_spec((FFN_HIDDEN, D_MODEL), layer),
        ],
        out_specs=pl.BlockSpec((tm, D_MODEL), lambda i: (i, 0)),
        out_shape=jax.ShapeDtypeStruct((n, D_MODEL), F32),
        compiler_params=pltpu.CompilerParams(
            dimension_semantics=("arbitrary",), vmem_limit_bytes=VMEM_LIMIT),
        name="ffn_latent" if latent else "ffn_context",
    )(x, mix, mod4, w_out, norm_ffn, w_gu, w_down)


def _rope_tables(n_tokens):
    t = jnp.arange(n_tokens, dtype=jnp.int32)
    row = (t // GRID_W).astype(F32)
    col = (t % GRID_W).astype(F32)
    axis_dim = HEAD_DIM // 2
    freqs = ROPE_THETA ** (-jnp.arange(0, axis_dim, 2, dtype=F32) / axis_dim)
    ang = jnp.concatenate([row[:, None] * freqs, col[:, None] * freqs], axis=-1)
    c, s = jnp.cos(ang), jnp.sin(ang)
    return jnp.concatenate([c, c, c, c], axis=-1), jnp.concatenate([-s, s, -s, s], axis=-1)


def kernel(x_prompt, x_sample, cache_diff_k, cache_diff_v, cache_gqa_k, cache_gqa_v, c, c_ctx, w_mod, b_mod, norm_attn, w_in, q_norm_a, k_norm_a, lambda_q1, lambda_k1, lambda_q2, lambda_k2, subln, q_norm_b, k_norm_b, w_out, norm_ffn, w_gate_up, w_down):
    batch, seq, d = x_prompt.shape
    dec_batch, dec_seq, _ = x_sample.shape
    depth = w_mod.shape[0]
    past = cache_diff_k.shape[2]

    cond = jnp.concatenate([c_ctx[None], c, jnp.zeros((MOD_ROWS - 1 - dec_batch, d), F32)], axis=0)
    mod = _modulation(cond, w_mod, b_mod)
    mod4 = mod.reshape(depth, MOD_ROWS, 1, 6 * d)

    w_in_b = w_in.astype(BF16)
    w_out_b = w_out.astype(BF16)
    w_gu_b = w_gate_up.astype(BF16)
    w_down_b = w_down.astype(BF16)
    seg_ones = jnp.kron(jnp.eye(256 // HEAD_DIM, dtype=F32), jnp.ones((HEAD_DIM, HEAD_DIM), F32)).astype(BF16)
    rope = _rope_tables(dec_seq)

    cache = (jnp.transpose(cache_diff_k, (0, 1, 3, 4, 5, 2)).reshape(dec_batch, depth, DIFF_HEADS, 2 * HEAD_DIM, past),
             cache_diff_v.reshape(dec_batch, depth, past * DIFF_HEADS, 2 * HEAD_DIM),
             jnp.transpose(cache_gqa_k, (0, 1, 3, 4, 2)),
             jnp.transpose(cache_gqa_v, (0, 1, 3, 4, 2)))

    row = lambda a: a.reshape(depth, 1, a.shape[-1])
    gains = (row(q_norm_a), row(k_norm_a), row(q_norm_b), row(k_norm_b))
    lam_vecs = (row(lambda_q1), row(lambda_k1), row(lambda_q2), row(lambda_k2))
    na, nf, sl = row(norm_attn), row(norm_ffn), subln.reshape(depth, 2 * HEAD_DIM, 1)

    yp = x_prompt.reshape(batch * seq, d)
    ys = x_sample.reshape(dec_batch * dec_seq, d)
    ctx_tiles = seq // TOKEN_TILE
    lat_tiles = dec_seq // TOKEN_TILE
    new_caches = None
    for l in range(depth):
        qa, ka, va, qb, kb, vb, *new_caches = _qkv(
            yp, mod4, l, False, ctx_tiles, na, w_in_b, seg_ones, gains, None, new_caches)
        mix = _attention(l, qa, qb, (ka, va, kb, vb), None, seq, lam_vecs, sl)
        yp = _ffn(yp, mix, mod4, l, False, ctx_tiles, w_out_b, nf, w_gu_b, w_down_b)

        qa, ka, va, qb, kb, vb = _qkv(
            ys, mod4, l, True, lat_tiles, na, w_in_b, seg_ones, gains, rope, None)
        mix = _attention(l, qa, qb, (ka, va, kb, vb), cache, dec_seq, lam_vecs, sl)
        ys = _ffn(ys, mix, mod4, l, True, lat_tiles, w_out_b, nf, w_gu_b, w_down_b)

    ka_t, va_n, kb_t, vb_t = new_caches
    new_diff_k = jnp.transpose(ka_t.reshape(batch, depth, DIFF_HEADS, 2, HEAD_DIM, seq), (0, 1, 5, 2, 3, 4))
    new_gqa_k = jnp.transpose(kb_t.reshape(batch, depth, GQA_KV_HEADS, HEAD_DIM, seq), (0, 1, 4, 2, 3))
    new_gqa_v = jnp.transpose(vb_t.reshape(batch, depth, GQA_KV_HEADS, HEAD_DIM, seq), (0, 1, 4, 2, 3))
    return (yp.reshape(batch, seq, d), ys.reshape(dec_batch, dec_seq, d),
            new_diff_k, va_n.reshape(batch, depth, seq, DIFF_HEADS, 2 * HEAD_DIM), new_gqa_k, new_gqa_v)
```

```python
import functools
import math

import jax
import jax.numpy as jnp
from jax import lax
from jax.experimental import pallas as pl
from jax.experimental.pallas import tpu as pltpu

F32 = jnp.float32
BF16 = jnp.bfloat16

D_MODEL = 1024
HEAD_DIM = 64
GRID_W = 64
DIFF_HEADS = 4
GQA_KV_HEADS = 2
GQA_REP = 4
FFN_HIDDEN = 2816
IN_COLS = 2304
ROPE_THETA = 10000.0
EPS = 1e-6
LANES = 128
MOD_ROWS = 8
TOKEN_TILE = 256
Q_TILE = 256
VMEM_LIMIT = 56 * 1024 * 1024


def _dot(a, b):
    return jnp.dot(a, b, preferred_element_type=F32)


def _dot_nt(a, b):
    return lax.dot_general(a, b, (((1,), (1,)), ((), ())), preferred_element_type=F32)


def _rms(x, gain):
    ms = jnp.mean(x * x, axis=-1, keepdims=True)
    return x * lax.rsqrt(ms + EPS) * gain


def _layer_spec(shape, layer):
    n = len(shape)
    return pl.BlockSpec((None,) + tuple(shape), lambda *_: (layer,) + (0,) * n,
                        pipeline_mode=pl.Buffered(1))


def _const_spec(shape):
    n = len(shape)
    return pl.BlockSpec(shape, lambda *_: (0,) * n, pipeline_mode=pl.Buffered(1))


def _mod_spec(layer, latent, tiles_per_request):
    if latent:
        return pl.BlockSpec((None, None, 1, 6 * D_MODEL),
                            lambda i: (layer, 1 + i // tiles_per_request, 0, 0))
    return pl.BlockSpec((None, None, 1, 6 * D_MODEL), lambda i: (layer, 0, 0, 0))


def _mod_kernel(cond_ref, w_ref, b_ref, o_ref):
    c = cond_ref[...]
    s = c * jax.nn.sigmoid(c)
    o_ref[...] = _dot(s.astype(BF16), w_ref[...].astype(BF16)) + b_ref[...]


def _modulation(cond, w_mod, b_mod):
    depth = w_mod.shape[0]
    tn = 1536
    return pl.pallas_call(
        _mod_kernel,
        grid=(depth, 6 * D_MODEL // tn),
        in_specs=[
            pl.BlockSpec((MOD_ROWS, D_MODEL), lambda l, j: (0, 0)),
            pl.BlockSpec((None, D_MODEL, tn), lambda l, j: (l, 0, j)),
            pl.BlockSpec((None, 1, tn), lambda l, j: (l, 0, j)),
        ],
        out_specs=pl.BlockSpec((None, MOD_ROWS, tn), lambda l, j: (l, 0, j)),
        out_shape=jax.ShapeDtypeStruct((depth, MOD_ROWS, 6 * D_MODEL), F32),
        compiler_params=pltpu.CompilerParams(
            dimension_semantics=("arbitrary", "arbitrary"), vmem_limit_bytes=VMEM_LIMIT),
        name="modulation",
    )(cond, w_mod, b_mod.reshape(depth, 1, 6 * D_MODEL))


def _head_norm(x, seg_ones, gain):
    ss = _dot((x * x).astype(BF16), seg_ones)
    return x * lax.rsqrt(ss * (1.0 / HEAD_DIM) + EPS) * gain


def _swap_halves(x, first_half):
    return jnp.where(first_half, pltpu.roll(x, 96, 1), pltpu.roll(x, 32, 1))


def _rope(x, cos, sin, first_half):
    out = []
    for j in range(x.shape[1] // LANES):
        xj = x[:, j * LANES:(j + 1) * LANES]
        out.append(xj * cos + _swap_halves(xj, first_half) * sin)
    return out[0] if len(out) == 1 else jnp.concatenate(out, axis=1)


def _tile_gain(g):
    return jnp.concatenate([g] * (256 // HEAD_DIM), axis=1)


def _qkv_kernel(latent, *refs):
    if latent:
        (x_ref, mod_ref, na_ref, w_ref, seg_ref, gqa_ref, gka_ref, gqb_ref, gkb_ref,
         cos_ref, sin_ref, qa_o, ka_o, va_o, qb_o, kb_o, vb_o) = refs
    else:
        (x_ref, mod_ref, na_ref, w_ref, seg_ref, gqa_ref, gka_ref, gqb_ref, gkb_ref) = refs[:9]
        qa_o, ka_o, va_o, qb_o, kb_o, vb_o, kaf_o, vaf_o, kbf_o, vbf_o = refs[-10:]
    tm = TOKEN_TILE
    sh = mod_ref[:, 0:D_MODEL]
    sc = mod_ref[:, D_MODEL:2 * D_MODEL]
    seg = seg_ref[...]
    lane = lax.broadcasted_iota(jnp.int32, (1, LANES), 1)
    low = lane < HEAD_DIM
    first_half = (lane % HEAD_DIM) < (HEAD_DIM // 2)
    scale = HEAD_DIM ** -0.5 * math.log2(math.e)

    def project(r):
        h = _rms(x_ref[r, :], na_ref[...]) * (1.0 + sc) + sh
        return _dot(h.astype(BF16), w_ref[...])

    def finish(j, r, proj):
        def qk(lo_col, width, gain_ref):
            gain = _tile_gain(gain_ref[...])
            out = []
            for c0 in range(0, width, 256):
                w = min(256, width - c0)
                out.append(_head_norm(proj[:, lo_col + c0:lo_col + c0 + w], seg[:w, :w], gain[:, :w]))
            y = out[0] if len(out) == 1 else jnp.concatenate(out, axis=1)
            if latent:
                y = _rope(y, cos_ref[r, :], sin_ref[r, :], first_half)
            return y

        qa = qk(0, 512, gqa_ref)
        ka = qk(512, 512, gka_ref)
        va = proj[:, 1024:1536]
        qb = qk(1536, 512, gqb_ref)
        kb = qk(2048, 128, gkb_ref)
        vb = proj[:, 2176:2304]
        qa_o[r, :] = (qa * scale).astype(BF16)
        ka_o[r, :] = ka.astype(BF16)
        va_o[r, :] = va.astype(BF16)
        qb_o[r, :] = (qb * scale).astype(BF16)
        kb_o[r, :] = kb.astype(BF16)
        vb_o[r, :] = vb.astype(BF16)
        if not latent:
            kaf_o[j] = ka.T
            kbf_o[j] = kb.T
            vbf_o[j] = vb.T
            for h in range(DIFF_HEADS):
                vaf_o[j, pl.ds(h, tm, stride=DIFF_HEADS), :] = va[:, h * LANES:(h + 1) * LANES]

    rows = [slice(j * tm, (j + 1) * tm) for j in range(x_ref.shape[0] // tm)]
    pending = [project(rows[0])]
    for j, r in enumerate(rows):
        if j + 1 < len(rows):
            pending.append(project(rows[j + 1]))
        finish(j, r, pending.pop(0))


def _qkv(x, mod4, layer, latent, tiles_per_request, norm_attn, w_in, seg_ones, gains, rope, prev_caches):
    n = x.shape[0]
    sub = 4
    tm = sub * TOKEN_TILE
    depth = w_in.shape[0]
    tok = lambda w: pl.BlockSpec((tm, w), lambda i: (i, 0))
    in_specs = [
        tok(D_MODEL),
        _mod_spec(layer, latent, max(tiles_per_request // sub, 1)),
        _layer_spec((1, D_MODEL), layer),
        _layer_spec((D_MODEL, IN_COLS), layer),
        _const_spec((256, 256)),
    ] + [_layer_spec((1, HEAD_DIM), layer)] * 4
    args = [x, mod4, norm_attn, w_in, seg_ones, *gains]
    aliases = {}
    out_specs = [tok(512), tok(512), tok(512), tok(512), tok(LANES), tok(LANES)]
    out_shape = [jax.ShapeDtypeStruct((n, w), BF16) for w in (512, 512, 512, 512, LANES, LANES)]
    if latent:
        rope_spec = pl.BlockSpec((tm, LANES), lambda i: (i % (tiles_per_request // sub), 0))
        in_specs += [rope_spec, rope_spec]
        args += list(rope)
    else:
        assert tiles_per_request == 1, "context requests must be one token tile long"
        seq = TOKEN_TILE
        req = n // seq
        slab = lambda *dims: pl.BlockSpec((sub, None) + dims, lambda i: (i, layer) + (0,) * len(dims))
        out_specs += [slab(512, seq), slab(seq * DIFF_HEADS, LANES), slab(LANES, seq), slab(LANES, seq)]
        out_shape += [jax.ShapeDtypeStruct((req, depth, 512, seq), F32),
                      jax.ShapeDtypeStruct((req, depth, seq * DIFF_HEADS, LANES), F32),
                      jax.ShapeDtypeStruct((req, depth, LANES, seq), F32),
                      jax.ShapeDtypeStruct((req, depth, LANES, seq), F32)]
        if prev_caches is not None:
            aliases = {len(args) + k: 6 + k for k in range(4)}
            in_specs += [pl.BlockSpec(memory_space=pl.ANY)] * 4
            args += list(prev_caches)
    return pl.pallas_call(
        functools.partial(_qkv_kernel, latent),
        grid=(n // tm,),
        in_specs=in_specs,
        out_specs=out_specs,
        out_shape=out_shape,
        input_output_aliases=aliases,
        compiler_params=pltpu.CompilerParams(
            dimension_semantics=("arbitrary",), vmem_limit_bytes=VMEM_LIMIT),
        name="qkv_latent" if latent else "qkv_context",
    )(*args)


def _pad_rows(x, block):
    z = jnp.zeros_like(x)
    return jnp.concatenate([x, z] if block == 0 else [z, x], axis=0)


ONES_ROWS = 16


def _softmax_pv(vt, st):
    d = vt.shape[0]
    e = jnp.exp2(st - st.max(axis=0, keepdims=True)).astype(BF16)
    ext = _dot(jnp.concatenate([vt, jnp.ones((ONES_ROWS, vt.shape[1]), BF16)], axis=0), e)
    return ext[:d], 1.0 / ext[d:d + 1]


KA_W = DIFF_HEADS * LANES


def _stage_kv(k_all, vt_all, ka, va, kb, vb, cache_refs):
    seq = ka.shape[0]
    k_all[0:seq, 0:KA_W] = ka
    k_all[0:seq, KA_W:] = kb
    vt_all[0:KA_W, 0:seq] = va.astype(F32).T.astype(BF16)
    vt_all[KA_W:, 0:seq] = vb.astype(F32).T.astype(BF16)
    if cache_refs is not None:
        cka_ref, cva_ref, ckb_ref, cvb_ref = cache_refs
        past = cka_ref.shape[2]
        for h in range(DIFF_HEADS):
            cs = slice(h * LANES, (h + 1) * LANES)
            k_all[seq:, cs] = cka_ref[h].T.astype(BF16)
            vt_all[cs, seq:] = cva_ref[pl.ds(h, past, stride=DIFF_HEADS), :].T.astype(BF16)
        k_all[seq:, KA_W:] = jnp.concatenate([ckb_ref[0], ckb_ref[1]], axis=0).T.astype(BF16)
        for g in range(GQA_KV_HEADS):
            vt_all[KA_W + g * HEAD_DIM:KA_W + (g + 1) * HEAD_DIM, seq:] = cvb_ref[g].astype(BF16)


def _lambda(lq1, lk1, lq2, lk2, lam_init):
    return (jnp.exp(jnp.sum(lq1[...] * lk1[...], axis=-1, keepdims=True))
            - jnp.exp(jnp.sum(lq2[...] * lk2[...], axis=-1, keepdims=True)) + lam_init)


def _attention_tasks(qa, qb, k_all, vt_all, lam, subln, lam_init, store):
    tq = qa.shape[0]
    qat = qa.astype(F32).T.astype(BF16)
    qbt = qb.astype(F32).T.astype(BF16)
    tasks = []

    def diff_finish(st, h, cs):
        num, r = _softmax_pv(vt_all[cs, :], st)
        ot = num[:, :tq] * r[:, :tq] - num[:, tq:] * (r[:, tq:] * lam)
        ms = jnp.mean(ot * ot, axis=0, keepdims=True)
        ot = ot * lax.rsqrt(ms + EPS) * subln * (1.0 - lam_init)
        store(cs, ot.T)

    def gqa_finish(st, g, os):
        num, r = _softmax_pv(vt_all[KA_W + g * HEAD_DIM:KA_W + (g + 1) * HEAD_DIM, :], st)
        ot = num * r
        store(os, jnp.concatenate([ot[:, :tq], ot[:, tq:]], axis=0).T)

    for h in range(DIFF_HEADS):
        cs = slice(h * LANES, (h + 1) * LANES)

        def logits(h=h, cs=cs):
            c0 = qat[h * LANES:h * LANES + HEAD_DIM]
            c1 = qat[h * LANES + HEAD_DIM:(h + 1) * LANES]
            return _dot(k_all[:, cs], jnp.concatenate([_pad_rows(c0, 0), _pad_rows(c1, 1)], axis=1))

        tasks.append((logits, lambda st, h=h, cs=cs: diff_finish(st, h, cs)))
    for g in range(GQA_KV_HEADS):
        for rp in range(2):
            r0 = (GQA_REP * g + 2 * rp) * HEAD_DIM
            os = slice(KA_W + r0, KA_W + r0 + LANES)

            def logits(g=g, r0=r0):
                a = qbt[r0:r0 + HEAD_DIM]
                b = qbt[r0 + HEAD_DIM:r0 + LANES]
                return _dot(k_all[:, KA_W:], jnp.concatenate([_pad_rows(a, g), _pad_rows(b, g)], axis=1))

            tasks.append((logits, lambda st, g=g, os=os: gqa_finish(st, g, os)))
    return tasks


LOGITS_AHEAD = 2
N_TASKS = DIFF_HEADS + 2 * GQA_KV_HEADS


def _attention_units(tasks):
    pending = []

    def prefill(t):
        pending.append(t[0]())

    def step(i):
        if i + LOGITS_AHEAD < len(tasks):
            pending.append(tasks[i + LOGITS_AHEAD][0]())
        tasks[i][1](pending.pop(0))

    units = [functools.partial(prefill, t) for t in tasks[:LOGITS_AHEAD]]
    return units + [functools.partial(step, i) for i in range(len(tasks))]


def _attn_kernel(has_cache, lam_init, *refs):
    qa_ref, qb_ref, ka_ref, va_ref, kb_ref, vb_ref = refs[:6]
    cache_refs = refs[6:10] if has_cache else None
    lq1, lk1, lq2, lk2, subln_ref, o_ref, k_all, vt_all = refs[10 if has_cache else 6:]

    @pl.when(pl.program_id(1) == 0)
    def _():
        _stage_kv(k_all, vt_all, ka_ref[...], va_ref[...], kb_ref[...], vb_ref[...], cache_refs)

    def store(cols, tile):
        o_ref[:, cols] = tile.astype(o_ref.dtype)

    tasks = _attention_tasks(qa_ref[...], qb_ref[...], k_all, vt_all,
                             _lambda(lq1, lk1, lq2, lk2, lam_init), subln_ref[...], lam_init, store)
    for unit in _attention_units(tasks):
        unit()


def _attention(layer, qa, qb, own, cache, seq, lam_vecs, subln):
    n = qa.shape[0]
    requests = n // seq
    tq = min(Q_TILE, seq)
    nq = seq // tq
    lam_init = 0.8 - 0.6 * math.exp(-0.3 * layer)
    qspec = pl.BlockSpec((tq, 512), lambda b, i: (b * nq + i, 0))
    in_specs = [qspec, qspec]
    args = [qa, qb]
    for a in own:
        in_specs.append(pl.BlockSpec((a.shape[0] // requests, a.shape[1]), lambda b, i: (b, 0)))
        args.append(a)
    if cache is not None:
        for a in cache:
            nd = a.ndim - 2
            in_specs.append(pl.BlockSpec((None, None) + a.shape[2:],
                                         lambda b, i, nd=nd: (b, layer) + (0,) * nd))
            args.append(a)
    in_specs += [_layer_spec((1, HEAD_DIM), layer)] * 4 + [_layer_spec((2 * HEAD_DIM, 1), layer)]
    args += list(lam_vecs) + [subln]
    keys = seq + (cache[0].shape[-1] if cache is not None else 0)
    kv_w = own[0].shape[1] + own[2].shape[1]
    return pl.pallas_call(
        functools.partial(_attn_kernel, cache is not None, lam_init),
        grid=(requests, nq),
        in_specs=in_specs,
        out_specs=pl.BlockSpec((tq, D_MODEL), lambda b, i: (b * nq + i, 0)),
        out_shape=jax.ShapeDtypeStruct((n, D_MODEL), BF16),
        scratch_shapes=[pltpu.VMEM((keys, kv_w), BF16), pltpu.VMEM((kv_w, keys), BF16)],
        compiler_params=pltpu.CompilerParams(
            dimension_semantics=("arbitrary", "arbitrary"), vmem_limit_bytes=VMEM_LIMIT),
        name="attn_latent" if cache is not None else "attn_context",
    )(*args)


def _ffn_units(r, x_ref, mix_ref, mod_ref, wo_ref, nf_ref, wgu_ref, wd_ref, o_ref):
    g_a = mod_ref[:, 2 * D_MODEL:3 * D_MODEL]
    sh_f = mod_ref[:, 3 * D_MODEL:4 * D_MODEL]
    sc_f = mod_ref[:, 4 * D_MODEL:5 * D_MODEL]
    g_f = mod_ref[:, 5 * D_MODEL:6 * D_MODEL]
    state = {}

    def attn_residual():
        state["x"] = x_ref[r, :] + g_a * _dot(mix_ref[r, :], wo_ref[...])

    def gate():
        state["h"] = (_rms(state["x"], nf_ref[...]) * (1.0 + sc_f) + sh_f).astype(BF16)
        state["gate"] = _dot(state["h"], wgu_ref[:, :FFN_HIDDEN])

    def up():
        g = state.pop("gate")
        state["act"] = (g * jax.nn.sigmoid(g) * _dot(state.pop("h"), wgu_ref[:, FFN_HIDDEN:])).astype(BF16)

    def ffn_residual():
        o_ref[r, :] = state.pop("x") + g_f * _dot(state.pop("act"), wd_ref[...])

    return [attn_residual, gate, up, ffn_residual]


def _interleave(a, b):
    out, ia, ib = [], 0, 0
    while ia < len(a) or ib < len(b):
        if ib >= len(b) or (ia < len(a) and ia * len(b) <= ib * len(a)):
            out.append(a[ia])
            ia += 1
        else:
            out.append(b[ib])
            ib += 1
    return out


def _ffn_kernel(x_ref, mix_ref, mod_ref, wo_ref, nf_ref, wgu_ref, wd_ref, o_ref):
    half = x_ref.shape[0] // 2
    a, b = [_ffn_units(slice(j * half, (j + 1) * half), x_ref, mix_ref, mod_ref, wo_ref, nf_ref,
                       wgu_ref, wd_ref, o_ref) for j in range(2)]
    for unit in _interleave(a, b):
        unit()


def _attn_ffn_kernel(lam_init, sub, *refs):
    (qa_ref, qb_ref, ka_ref, va_ref, kb_ref, vb_ref, lq1, lk1, lq2, lk2, subln_ref,
     x_ref, mod_ref, wo_ref, nf_ref, wgu_ref, wd_ref, o_ref, k_all, vt_all, mix) = refs
    seq = ka_ref.shape[0] // sub
    lam = _lambda(lq1, lk1, lq2, lk2, lam_init)
    subln = subln_ref[...]
    attn, ffn = [], []
    for j in range(sub):
        r = slice(j * seq, (j + 1) * seq)

        later = []

        def store(cols, tile, r=r):
            mix[r, cols] = tile.astype(mix.dtype)

        def stage(j=j, r=r, later=later, store=store):
            _stage_kv(k_all.at[j], vt_all.at[j], ka_ref[r, :], va_ref[r, :], kb_ref[r, :], vb_ref[r, :], None)
            later.extend(_attention_units(_attention_tasks(
                qa_ref[r, :], qb_ref[r, :], k_all.at[j], vt_all.at[j], lam, subln, lam_init, store)))

        attn.append([stage] + [lambda k=k, later=later: later[k]() for k in range(LOGITS_AHEAD + N_TASKS)])
        ffn.append(_ffn_units(r, x_ref, mix, mod_ref, wo_ref, nf_ref, wgu_ref, wd_ref, o_ref))
    units = attn[0]
    for j in range(1, sub):
        units = units + _interleave(attn[j], ffn[j - 1])
    for unit in units + ffn[sub - 1]:
        unit()


def _attn_ffn(layer, x, qa, qb, own, seq, lam_vecs, subln, mod4, w_out, norm_ffn, w_gu, w_down):
    n = x.shape[0]
    sub = 4
    tm = sub * seq
    lam_init = 0.8 - 0.6 * math.exp(-0.3 * layer)
    tok = lambda w: pl.BlockSpec((tm, w), lambda i: (i, 0))
    kv_w = own[0].shape[1] + own[2].shape[1]
    in_specs = ([tok(512), tok(512)] + [tok(a.shape[1]) for a in own]
                + [_layer_spec((1, HEAD_DIM), layer)] * 4 + [_layer_spec((2 * HEAD_DIM, 1), layer)]
                + [tok(D_MODEL), _mod_spec(layer, False, 1),
                   _layer_spec((D_MODEL, D_MODEL), layer), _layer_spec((1, D_MODEL), layer),
                   _layer_spec((D_MODEL, 2 * FFN_HIDDEN), layer), _layer_spec((FFN_HIDDEN, D_MODEL), layer)])
    return pl.pallas_call(
        functools.partial(_attn_ffn_kernel, lam_init, sub),
        grid=(n // tm,),
        in_specs=in_specs,
        out_specs=tok(D_MODEL),
        out_shape=jax.ShapeDtypeStruct((n, D_MODEL), F32),
        scratch_shapes=[pltpu.VMEM((sub, seq, kv_w), BF16), pltpu.VMEM((sub, kv_w, seq), BF16),
                        pltpu.VMEM((tm, D_MODEL), BF16)],
        compiler_params=pltpu.CompilerParams(
            dimension_semantics=("arbitrary",), vmem_limit_bytes=VMEM_LIMIT),
        name="attn_ffn_context",
    )(qa, qb, *own, *lam_vecs, subln, x, mod4, w_out, norm_ffn, w_gu, w_down)


def _ffn(x, mix, mod4, layer, latent, tiles_per_request, w_out, norm_ffn, w_gu, w_down):
    n = x.shape[0]
    tm = 2 * TOKEN_TILE
    tiles_per_request = max(tiles_per_request // 2, 1)
    return pl.pallas_call(
        _ffn_kernel,
        grid=(n // tm,),
        in_specs=[
            pl.BlockSpec((tm, D_MODEL), lambda i: (i, 0)),
            pl.BlockSpec((tm, D_MODEL), lambda i: (i, 0)),
            _mod_spec(layer, latent, tiles_per_request),
            _layer_spec((D_MODEL, D_MODEL), layer),
            _layer_spec((1, D_MODEL), layer),
            _layer_spec((D_MODEL, 2 * FFN_HIDDEN), layer),
            _layer_spec((FFN_HIDDEN, D_MODEL), layer),
        ],
        out_specs=pl.BlockSpec((tm, D_MODEL), lambda i: (i, 0)),
        out_shape=jax.ShapeDtypeStruct((n, D_MODEL), F32),
        compiler_params=pltpu.CompilerParams(
            dimension_semantics=("arbitrary",), vmem_limit_bytes=VMEM_LIMIT),
        name="ffn_latent" if latent else "ffn_context",
    )(x, mix, mod4, w_out, norm_ffn, w_gu, w_down)


def _rope_tables(n_tokens):
    t = jnp.arange(n_tokens, dtype=jnp.int32)
    row = (t // GRID_W).astype(F32)
    col = (t % GRID_W).astype(F32)
    axis_dim = HEAD_DIM // 2
    freqs = ROPE_THETA ** (-jnp.arange(0, axis_dim, 2, dtype=F32) / axis_dim)
    ang = jnp.concatenate([row[:, None] * freqs, col[:, None] * freqs], axis=-1)
    c, s = jnp.cos(ang), jnp.sin(ang)
    return jnp.concatenate([c, c, c, c], axis=-1), jnp.concatenate([-s, s, -s, s], axis=-1)


def kernel(x_prompt, x_sample, cache_diff_k, cache_diff_v, cache_gqa_k, cache_gqa_v, c, c_ctx, w_mod, b_mod, norm_attn, w_in, q_norm_a, k_norm_a, lambda_q1, lambda_k1, lambda_q2, lambda_k2, subln, q_norm_b, k_norm_b, w_out, norm_ffn, w_gate_up, w_down):
    batch, seq, d = x_prompt.shape
    dec_batch, dec_seq, _ = x_sample.shape
    depth = w_mod.shape[0]
    past = cache_diff_k.shape[2]

    cond = jnp.concatenate([c_ctx[None], c, jnp.zeros((MOD_ROWS - 1 - dec_batch, d), F32)], axis=0)
    mod = _modulation(cond, w_mod, b_mod)
    mod4 = mod.reshape(depth, MOD_ROWS, 1, 6 * d)

    w_in_b = w_in.astype(BF16)
    w_out_b = w_out.astype(BF16)
    w_gu_b = w_gate_up.astype(BF16)
    w_down_b = w_down.astype(BF16)
    seg_ones = jnp.kron(jnp.eye(256 // HEAD_DIM, dtype=F32), jnp.ones((HEAD_DIM, HEAD_DIM), F32)).astype(BF16)
    rope = _rope_tables(dec_seq)

    cache = (jnp.transpose(cache_diff_k, (0, 1, 3, 4, 5, 2)).reshape(dec_batch, depth, DIFF_HEADS, 2 * HEAD_DIM, past),
             cache_diff_v.reshape(dec_batch, depth, past * DIFF_HEADS, 2 * HEAD_DIM),
             jnp.transpose(cache_gqa_k, (0, 1, 3, 4, 2)),
             jnp.transpose(cache_gqa_v, (0, 1, 3, 4, 2)))

    row = lambda a: a.reshape(depth, 1, a.shape[-1])
    gains = (row(q_norm_a), row(k_norm_a), row(q_norm_b), row(k_norm_b))
    lam_vecs = (row(lambda_q1), row(lambda_k1), row(lambda_q2), row(lambda_k2))
    na, nf, sl = row(norm_attn), row(norm_ffn), subln.reshape(depth, 2 * HEAD_DIM, 1)

    yp = x_prompt.reshape(batch * seq, d)
    ys = x_sample.reshape(dec_batch * dec_seq, d)
    ctx_tiles = seq // TOKEN_TILE
    lat_tiles = dec_seq // TOKEN_TILE
    new_caches = None
    for l in range(depth):
        qa, ka, va, qb, kb, vb, *new_caches = _qkv(
            yp, mod4, l, False, ctx_tiles, na, w_in_b, seg_ones, gains, None, new_caches)
        yp = _attn_ffn(l, yp, qa, qb, (ka, va, kb, vb), seq, lam_vecs, sl, mod4, w_out_b, nf, w_gu_b, w_down_b)

        qa, ka, va, qb, kb, vb = _qkv(
            ys, mod4, l, True, lat_tiles, na, w_in_b, seg_ones, gains, rope, None)
        mix = _attention(l, qa, qb, (ka, va, kb, vb), cache, dec_seq, lam_vecs, sl)
        ys = _ffn(ys, mix, mod4, l, True, lat_tiles, w_out_b, nf, w_gu_b, w_down_b)

    ka_t, va_n, kb_t, vb_t = new_caches
    new_diff_k = jnp.transpose(ka_t.reshape(batch, depth, DIFF_HEADS, 2, HEAD_DIM, seq), (0, 1, 5, 2, 3, 4))
    new_gqa_k = jnp.transpose(kb_t.reshape(batch, depth, GQA_KV_HEADS, HEAD_DIM, seq), (0, 1, 4, 2, 3))
    new_gqa_v = jnp.transpose(vb_t.reshape(batch, depth, GQA_KV_HEADS, HEAD_DIM, seq), (0, 1, 4, 2, 3))
    return (yp.reshape(batch, seq, d), ys.reshape(dec_batch, dec_seq, d),
            new_diff_k, va_n.reshape(batch, depth, seq, DIFF_HEADS, 2 * HEAD_DIM), new_gqa_k, new_gqa_v)
```

```python
import functools
import math

import jax
import jax.numpy as jnp
from jax import lax
from jax.experimental import pallas as pl
from jax.experimental.pallas import tpu as pltpu

F32 = jnp.float32
BF16 = jnp.bfloat16

D_MODEL = 1024
HEAD_DIM = 64
GRID_W = 64
DIFF_HEADS = 4
GQA_KV_HEADS = 2
GQA_REP = 4
FFN_HIDDEN = 2816
IN_COLS = 2304
ROPE_THETA = 10000.0
EPS = 1e-6
LANES = 128
MOD_ROWS = 8
TOKEN_TILE = 256
Q_TILE = 256
VMEM_LIMIT = 56 * 1024 * 1024


def _dot(a, b):
    return jnp.dot(a, b, preferred_element_type=F32)


def _dot_nt(a, b):
    return lax.dot_general(a, b, (((1,), (1,)), ((), ())), preferred_element_type=F32)


def _rms(x, gain):
    ms = jnp.mean(x * x, axis=-1, keepdims=True)
    return x * lax.rsqrt(ms + EPS) * gain


def _layer_spec(shape, layer):
    n = len(shape)
    return pl.BlockSpec((None,) + tuple(shape), lambda *_: (layer,) + (0,) * n,
                        pipeline_mode=pl.Buffered(1))


def _const_spec(shape):
    n = len(shape)
    return pl.BlockSpec(shape, lambda *_: (0,) * n, pipeline_mode=pl.Buffered(1))


def _mod_spec(layer, latent, tiles_per_request):
    if latent:
        return pl.BlockSpec((None, None, 1, 6 * D_MODEL),
                            lambda i: (layer, 1 + i // tiles_per_request, 0, 0))
    return pl.BlockSpec((None, None, 1, 6 * D_MODEL), lambda i: (layer, 0, 0, 0))


def _mod_kernel(cond_ref, w_ref, b_ref, o_ref):
    c = cond_ref[...]
    s = c * jax.nn.sigmoid(c)
    o_ref[...] = _dot(s.astype(BF16), w_ref[...].astype(BF16)) + b_ref[...]


def _modulation(cond, w_mod, b_mod):
    depth = w_mod.shape[0]
    tn = 1536
    return pl.pallas_call(
        _mod_kernel,
        grid=(depth, 6 * D_MODEL // tn),
        in_specs=[
            pl.BlockSpec((MOD_ROWS, D_MODEL), lambda l, j: (0, 0)),
            pl.BlockSpec((None, D_MODEL, tn), lambda l, j: (l, 0, j)),
            pl.BlockSpec((None, 1, tn), lambda l, j: (l, 0, j)),
        ],
        out_specs=pl.BlockSpec((None, MOD_ROWS, tn), lambda l, j: (l, 0, j)),
        out_shape=jax.ShapeDtypeStruct((depth, MOD_ROWS, 6 * D_MODEL), F32),
        compiler_params=pltpu.CompilerParams(
            dimension_semantics=("arbitrary", "arbitrary"), vmem_limit_bytes=VMEM_LIMIT),
        name="modulation",
    )(cond, w_mod, b_mod.reshape(depth, 1, 6 * D_MODEL))


def _head_norm(x, seg_ones, gain):
    ss = _dot((x * x).astype(BF16), seg_ones)
    return x * lax.rsqrt(ss * (1.0 / HEAD_DIM) + EPS) * gain


def _swap_halves(x, first_half):
    return jnp.where(first_half, pltpu.roll(x, 96, 1), pltpu.roll(x, 32, 1))


def _rope(x, cos, sin, first_half):
    out = []
    for j in range(x.shape[1] // LANES):
        xj = x[:, j * LANES:(j + 1) * LANES]
        out.append(xj * cos + _swap_halves(xj, first_half) * sin)
    return out[0] if len(out) == 1 else jnp.concatenate(out, axis=1)


def _tile_gain(g):
    return jnp.concatenate([g] * (256 // HEAD_DIM), axis=1)


def _qkv_kernel(latent, *refs):
    if latent:
        (x_ref, mod_ref, na_ref, w_ref, seg_ref, gqa_ref, gka_ref, gqb_ref, gkb_ref,
         cos_ref, sin_ref, qa_o, ka_o, va_o, qb_o, kb_o, vb_o) = refs
    else:
        (x_ref, mod_ref, na_ref, w_ref, seg_ref, gqa_ref, gka_ref, gqb_ref, gkb_ref) = refs[:9]
        qa_o, ka_o, va_o, qb_o, kb_o, vb_o, kaf_o, vaf_o, kbf_o, vbf_o = refs[-10:]
    tm = TOKEN_TILE
    sh = mod_ref[:, 0:D_MODEL]
    sc = mod_ref[:, D_MODEL:2 * D_MODEL]
    seg = seg_ref[...]
    lane = lax.broadcasted_iota(jnp.int32, (1, LANES), 1)
    low = lane < HEAD_DIM
    first_half = (lane % HEAD_DIM) < (HEAD_DIM // 2)
    scale = HEAD_DIM ** -0.5 * math.log2(math.e)

    def project(r):
        h = _rms(x_ref[r, :], na_ref[...]) * (1.0 + sc) + sh
        return _dot(h.astype(BF16), w_ref[...])

    def finish(j, r, proj):
        def qk(lo_col, width, gain_ref):
            gain = _tile_gain(gain_ref[...])
            out = []
            for c0 in range(0, width, 256):
                w = min(256, width - c0)
                out.append(_head_norm(proj[:, lo_col + c0:lo_col + c0 + w], seg[:w, :w], gain[:, :w]))
            y = out[0] if len(out) == 1 else jnp.concatenate(out, axis=1)
            if latent:
                y = _rope(y, cos_ref[r, :], sin_ref[r, :], first_half)
            return y

        qa = qk(0, 512, gqa_ref)
        ka = qk(512, 512, gka_ref)
        va = proj[:, 1024:1536]
        qb = qk(1536, 512, gqb_ref)
        kb = qk(2048, 128, gkb_ref)
        vb = proj[:, 2176:2304]
        qa_o[r, :] = (qa * scale).astype(BF16)
        ka_o[r, :] = ka.astype(BF16)
        va_o[r, :] = va.astype(BF16)
        qb_o[r, :] = (qb * scale).astype(BF16)
        kb_o[r, :] = kb.astype(BF16)
        vb_o[r, :] = vb.astype(BF16)
        if not latent:
            kaf_o[j] = ka.T
            kbf_o[j] = kb.T
            vbf_o[j] = vb.T
            for h in range(DIFF_HEADS):
                vaf_o[j, pl.ds(h, tm, stride=DIFF_HEADS), :] = va[:, h * LANES:(h + 1) * LANES]

    rows = [slice(j * tm, (j + 1) * tm) for j in range(x_ref.shape[0] // tm)]
    pending = [project(rows[0])]
    for j, r in enumerate(rows):
        if j + 1 < len(rows):
            pending.append(project(rows[j + 1]))
        finish(j, r, pending.pop(0))


def _qkv(x, mod4, layer, latent, tiles_per_request, norm_attn, w_in, seg_ones, gains, rope, prev_caches):
    n = x.shape[0]
    sub = 4
    tm = sub * TOKEN_TILE
    depth = w_in.shape[0]
    tok = lambda w: pl.BlockSpec((tm, w), lambda i: (i, 0))
    in_specs = [
        tok(D_MODEL),
        _mod_spec(layer, latent, max(tiles_per_request // sub, 1)),
        _layer_spec((1, D_MODEL), layer),
        _layer_spec((D_MODEL, IN_COLS), layer),
        _const_spec((256, 256)),
    ] + [_layer_spec((1, HEAD_DIM), layer)] * 4
    args = [x, mod4, norm_attn, w_in, seg_ones, *gains]
    aliases = {}
    out_specs = [tok(512), tok(512), tok(512), tok(512), tok(LANES), tok(LANES)]
    out_shape = [jax.ShapeDtypeStruct((n, w), BF16) for w in (512, 512, 512, 512, LANES, LANES)]
    if latent:
        rope_spec = pl.BlockSpec((tm, LANES), lambda i: (i % (tiles_per_request // sub), 0))
        in_specs += [rope_spec, rope_spec]
        args += list(rope)
    else:
        assert tiles_per_request == 1, "context requests must be one token tile long"
        seq = TOKEN_TILE
        req = n // seq
        slab = lambda *dims: pl.BlockSpec((sub, None) + dims, lambda i: (i, layer) + (0,) * len(dims))
        out_specs += [slab(512, seq), slab(seq * DIFF_HEADS, LANES), slab(LANES, seq), slab(LANES, seq)]
        out_shape += [jax.ShapeDtypeStruct((req, depth, 512, seq), F32),
                      jax.ShapeDtypeStruct((req, depth, seq * DIFF_HEADS, LANES), F32),
                      jax.ShapeDtypeStruct((req, depth, LANES, seq), F32),
                      jax.ShapeDtypeStruct((req, depth, LANES, seq), F32)]
        if prev_caches is not None:
            aliases = {len(args) + k: 6 + k for k in range(4)}
            in_specs += [pl.BlockSpec(memory_space=pl.ANY)] * 4
            args += list(prev_caches)
    return pl.pallas_call(
        functools.partial(_qkv_kernel, latent),
        grid=(n // tm,),
        in_specs=in_specs,
        out_specs=out_specs,
        out_shape=out_shape,
        input_output_aliases=aliases,
        compiler_params=pltpu.CompilerParams(
            dimension_semantics=("arbitrary",), vmem_limit_bytes=VMEM_LIMIT),
        name="qkv_latent" if latent else "qkv_context",
    )(*args)


def _pad_rows(x, block):
    z = jnp.zeros_like(x)
    return jnp.concatenate([x, z] if block == 0 else [z, x], axis=0)


ONES_ROWS = 16


def _softmax_pv(vt, st):
    d = vt.shape[0]
    e = jnp.exp2(st - st.max(axis=0, keepdims=True)).astype(BF16)
    ext = _dot(jnp.concatenate([vt, jnp.ones((ONES_ROWS, vt.shape[1]), BF16)], axis=0), e)
    return ext[:d], 1.0 / ext[d:d + 1]


KA_W = DIFF_HEADS * LANES


def _stage_kv(k_all, vt_all, ka, va, kb, vb, cache_refs):
    seq = ka.shape[0]
    k_all[0:seq, 0:KA_W] = ka
    k_all[0:seq, KA_W:] = kb
    vt_all[0:KA_W, 0:seq] = va.astype(F32).T.astype(BF16)
    vt_all[KA_W:, 0:seq] = vb.astype(F32).T.astype(BF16)
    if cache_refs is not None:
        cka_ref, cva_ref, ckb_ref, cvb_ref = cache_refs
        past = cka_ref.shape[2]
        for h in range(DIFF_HEADS):
            cs = slice(h * LANES, (h + 1) * LANES)
            k_all[seq:, cs] = cka_ref[h].T.astype(BF16)
            vt_all[cs, seq:] = cva_ref[pl.ds(h, past, stride=DIFF_HEADS), :].T.astype(BF16)
        k_all[seq:, KA_W:] = jnp.concatenate([ckb_ref[0], ckb_ref[1]], axis=0).T.astype(BF16)
        for g in range(GQA_KV_HEADS):
            vt_all[KA_W + g * HEAD_DIM:KA_W + (g + 1) * HEAD_DIM, seq:] = cvb_ref[g].astype(BF16)


def _lambda(lq1, lk1, lq2, lk2, lam_init):
    return (jnp.exp(jnp.sum(lq1[...] * lk1[...], axis=-1, keepdims=True))
            - jnp.exp(jnp.sum(lq2[...] * lk2[...], axis=-1, keepdims=True)) + lam_init)


def _attention_tasks(qa, qb, k_all, vt_all, lam, subln, lam_init, store):
    tq = qa.shape[0]
    qat = qa.astype(F32).T.astype(BF16)
    qbt = qb.astype(F32).T.astype(BF16)
    tasks = []

    def diff_finish(st, h, cs):
        num, r = _softmax_pv(vt_all[cs, :], st)
        ot = num[:, :tq] * r[:, :tq] - num[:, tq:] * (r[:, tq:] * lam)
        ms = jnp.mean(ot * ot, axis=0, keepdims=True)
        ot = ot * lax.rsqrt(ms + EPS) * subln * (1.0 - lam_init)
        store(cs, ot.T)

    def gqa_finish(st, g, os):
        num, r = _softmax_pv(vt_all[KA_W + g * HEAD_DIM:KA_W + (g + 1) * HEAD_DIM, :], st)
        ot = num * r
        store(os, jnp.concatenate([ot[:, :tq], ot[:, tq:]], axis=0).T)

    for h in range(DIFF_HEADS):
        cs = slice(h * LANES, (h + 1) * LANES)

        def logits(h=h, cs=cs):
            c0 = qat[h * LANES:h * LANES + HEAD_DIM]
            c1 = qat[h * LANES + HEAD_DIM:(h + 1) * LANES]
            return _dot(k_all[:, cs], jnp.concatenate([_pad_rows(c0, 0), _pad_rows(c1, 1)], axis=1))

        tasks.append((logits, lambda st, h=h, cs=cs: diff_finish(st, h, cs)))
    for g in range(GQA_KV_HEADS):
        for rp in range(2):
            r0 = (GQA_REP * g + 2 * rp) * HEAD_DIM
            os = slice(KA_W + r0, KA_W + r0 + LANES)

            def logits(g=g, r0=r0):
                a = qbt[r0:r0 + HEAD_DIM]
                b = qbt[r0 + HEAD_DIM:r0 + LANES]
                return _dot(k_all[:, KA_W:], jnp.concatenate([_pad_rows(a, g), _pad_rows(b, g)], axis=1))

            tasks.append((logits, lambda st, g=g, os=os: gqa_finish(st, g, os)))
    return tasks


LOGITS_AHEAD = 2
N_TASKS = DIFF_HEADS + 2 * GQA_KV_HEADS


def _attention_units(tasks):
    pending = []

    def prefill(t):
        pending.append(t[0]())

    def step(i):
        if i + LOGITS_AHEAD < len(tasks):
            pending.append(tasks[i + LOGITS_AHEAD][0]())
        tasks[i][1](pending.pop(0))

    units = [functools.partial(prefill, t) for t in tasks[:LOGITS_AHEAD]]
    return units + [functools.partial(step, i) for i in range(len(tasks))]


def _attn_kernel(has_cache, lam_init, *refs):
    qa_ref, qb_ref, ka_ref, va_ref, kb_ref, vb_ref = refs[:6]
    cache_refs = refs[6:10] if has_cache else None
    lq1, lk1, lq2, lk2, subln_ref, o_ref, k_all, vt_all = refs[10 if has_cache else 6:]

    @pl.when(pl.program_id(1) == 0)
    def _():
        _stage_kv(k_all, vt_all, ka_ref[...], va_ref[...], kb_ref[...], vb_ref[...], cache_refs)

    def store(cols, tile):
        o_ref[:, cols] = tile.astype(o_ref.dtype)

    tasks = _attention_tasks(qa_ref[...], qb_ref[...], k_all, vt_all,
                             _lambda(lq1, lk1, lq2, lk2, lam_init), subln_ref[...], lam_init, store)
    for unit in _attention_units(tasks):
        unit()


def _attention(layer, qa, qb, own, cache, seq, lam_vecs, subln):
    n = qa.shape[0]
    requests = n // seq
    tq = min(Q_TILE, seq)
    nq = seq // tq
    lam_init = 0.8 - 0.6 * math.exp(-0.3 * layer)
    qspec = pl.BlockSpec((tq, 512), lambda b, i: (b * nq + i, 0))
    in_specs = [qspec, qspec]
    args = [qa, qb]
    for a in own:
        in_specs.append(pl.BlockSpec((a.shape[0] // requests, a.shape[1]), lambda b, i: (b, 0)))
        args.append(a)
    if cache is not None:
        for a in cache:
            nd = a.ndim - 2
            in_specs.append(pl.BlockSpec((None, None) + a.shape[2:],
                                         lambda b, i, nd=nd: (b, layer) + (0,) * nd))
            args.append(a)
    in_specs += [_layer_spec((1, HEAD_DIM), layer)] * 4 + [_layer_spec((2 * HEAD_DIM, 1), layer)]
    args += list(lam_vecs) + [subln]
    keys = seq + (cache[0].shape[-1] if cache is not None else 0)
    kv_w = own[0].shape[1] + own[2].shape[1]
    return pl.pallas_call(
        functools.partial(_attn_kernel, cache is not None, lam_init),
        grid=(requests, nq),
        in_specs=in_specs,
        out_specs=pl.BlockSpec((tq, D_MODEL), lambda b, i: (b * nq + i, 0)),
        out_shape=jax.ShapeDtypeStruct((n, D_MODEL), BF16),
        scratch_shapes=[pltpu.VMEM((keys, kv_w), BF16), pltpu.VMEM((kv_w, keys), BF16)],
        compiler_params=pltpu.CompilerParams(
            dimension_semantics=("arbitrary", "arbitrary"), vmem_limit_bytes=VMEM_LIMIT),
        name="attn_latent" if cache is not None else "attn_context",
    )(*args)


def _ffn_units(r, x_ref, mix_ref, mod_ref, wo_ref, nf_ref, wgu_ref, wd_ref, o_ref):
    g_a = mod_ref[:, 2 * D_MODEL:3 * D_MODEL]
    sh_f = mod_ref[:, 3 * D_MODEL:4 * D_MODEL]
    sc_f = mod_ref[:, 4 * D_MODEL:5 * D_MODEL]
    g_f = mod_ref[:, 5 * D_MODEL:6 * D_MODEL]
    state = {}

    def attn_residual():
        state["x"] = x_ref[r, :] + g_a * _dot(mix_ref[r, :], wo_ref[...])

    def gate():
        state["h"] = (_rms(state["x"], nf_ref[...]) * (1.0 + sc_f) + sh_f).astype(BF16)
        state["gate"] = _dot(state["h"], wgu_ref[:, :FFN_HIDDEN])

    def up():
        g = state.pop("gate")
        state["act"] = (g * jax.nn.sigmoid(g) * _dot(state.pop("h"), wgu_ref[:, FFN_HIDDEN:])).astype(BF16)

    def ffn_residual():
        o_ref[r, :] = state.pop("x") + g_f * _dot(state.pop("act"), wd_ref[...])

    return [attn_residual, gate, up, ffn_residual]


def _interleave(a, b):
    out, ia, ib = [], 0, 0
    while ia < len(a) or ib < len(b):
        if ib >= len(b) or (ia < len(a) and ia * len(b) <= ib * len(a)):
            out.append(a[ia])
            ia += 1
        else:
            out.append(b[ib])
            ib += 1
    return out


def _ffn_kernel(x_ref, mix_ref, mod_ref, wo_ref, nf_ref, wgu_ref, wd_ref, o_ref):
    half = x_ref.shape[0] // 2
    a, b = [_ffn_units(slice(j * half, (j + 1) * half), x_ref, mix_ref, mod_ref, wo_ref, nf_ref,
                       wgu_ref, wd_ref, o_ref) for j in range(2)]
    for unit in _interleave(a, b):
        unit()


def _attn_ffn_kernel(lam_init, sub, *refs):
    (qa_ref, qb_ref, ka_ref, va_ref, kb_ref, vb_ref, lq1, lk1, lq2, lk2, subln_ref,
     x_ref, mod_ref, wo_ref, nf_ref, wgu_ref, wd_ref, o_ref, k_all, vt_all, mix) = refs
    seq = ka_ref.shape[0] // sub
    lam = _lambda(lq1, lk1, lq2, lk2, lam_init)
    subln = subln_ref[...]
    attn, ffn = [], []
    for j in range(sub):
        r = slice(j * seq, (j + 1) * seq)

        later = []

        def store(cols, tile, r=r):
            mix[r, cols] = tile.astype(mix.dtype)

        def stage(j=j, r=r, later=later, store=store):
            _stage_kv(k_all.at[j], vt_all.at[j], ka_ref[r, :], va_ref[r, :], kb_ref[r, :], vb_ref[r, :], None)
            later.extend(_attention_units(_attention_tasks(
                qa_ref[r, :], qb_ref[r, :], k_all.at[j], vt_all.at[j], lam, subln, lam_init, store)))

        attn.append([stage] + [lambda k=k, later=later: later[k]() for k in range(LOGITS_AHEAD + N_TASKS)])
        ffn.append(_ffn_units(r, x_ref, mix, mod_ref, wo_ref, nf_ref, wgu_ref, wd_ref, o_ref))
    units = attn[0]
    for j in range(1, sub):
        units = units + _interleave(attn[j], ffn[j - 1])
    for unit in units + ffn[sub - 1]:
        unit()


def _attn_ffn(layer, x, qa, qb, own, seq, lam_vecs, subln, mod4, w_out, norm_ffn, w_gu, w_down):
    n = x.shape[0]
    sub = 4
    tm = sub * seq
    lam_init = 0.8 - 0.6 * math.exp(-0.3 * layer)
    tok = lambda w: pl.BlockSpec((tm, w), lambda i: (i, 0))
    kv_w = own[0].shape[1] + own[2].shape[1]
    in_specs = ([tok(512), tok(512)] + [tok(a.shape[1]) for a in own]
                + [_layer_spec((1, HEAD_DIM), layer)] * 4 + [_layer_spec((2 * HEAD_DIM, 1), layer)]
                + [tok(D_MODEL), _mod_spec(layer, False, 1),
                   _layer_spec((D_MODEL, D_MODEL), layer), _layer_spec((1, D_MODEL), layer),
                   _layer_spec((D_MODEL, 2 * FFN_HIDDEN), layer), _layer_spec((FFN_HIDDEN, D_MODEL), layer)])
    return pl.pallas_call(
        functools.partial(_attn_ffn_kernel, lam_init, sub),
        grid=(n // tm,),
        in_specs=in_specs,
        out_specs=tok(D_MODEL),
        out_shape=jax.ShapeDtypeStruct((n, D_MODEL), F32),
        scratch_shapes=[pltpu.VMEM((sub, seq, kv_w), BF16), pltpu.VMEM((sub, kv_w, seq), BF16),
                        pltpu.VMEM((tm, D_MODEL), BF16)],
        compiler_params=pltpu.CompilerParams(
            dimension_semantics=("arbitrary",), vmem_limit_bytes=VMEM_LIMIT),
        name="attn_ffn_context",
    )(qa, qb, *own, *lam_vecs, subln, x, mod4, w_out, norm_ffn, w_gu, w_down)


def _attn_ffn_lagged_kernel(lam_init, tiles_per_request, n_tiles, *refs):
    (qa_ref, qb_ref, ka_ref, va_ref, kb_ref, vb_ref, cka_ref, cva_ref, ckb_ref, cvb_ref,
     lq1, lk1, lq2, lk2, subln_ref, x_ref, mod_ref, wo_ref, nf_ref, wgu_ref, wd_ref,
     o_ref, k_all, vt_all, mix) = refs
    s = pl.program_id(0)
    tq = qa_ref.shape[0]
    lam = _lambda(lq1, lk1, lq2, lk2, lam_init)
    subln = subln_ref[...]

    def attention_units(slot):
        def store(cols, tile):
            mix[slot, :, cols] = tile.astype(mix.dtype)

        return _attention_units(_attention_tasks(
            qa_ref[...], qb_ref[...], k_all, vt_all, lam, subln, lam_init, store))

    def ffn_units(slot):
        return _ffn_units(slice(0, tq), x_ref, mix.at[slot], mod_ref, wo_ref, nf_ref, wgu_ref, wd_ref, o_ref)

    @pl.when(jnp.logical_and(s % tiles_per_request == 0, s < n_tiles))
    def _():
        _stage_kv(k_all, vt_all, ka_ref[...], va_ref[...], kb_ref[...], vb_ref[...],
                  (cka_ref, cva_ref, ckb_ref, cvb_ref))

    @pl.when(s == 0)
    def _():
        for unit in attention_units(0):
            unit()

    @pl.when(jnp.logical_and(s > 0, s < n_tiles))
    def _():
        for unit in _interleave(attention_units(s % 2), ffn_units((s - 1) % 2)):
            unit()

    @pl.when(s == n_tiles)
    def _():
        for unit in ffn_units((n_tiles - 1) % 2):
            unit()


def _attn_ffn_lagged(layer, x, qa, qb, own, cache, seq, lam_vecs, subln, mod4, w_out, norm_ffn, w_gu, w_down):
    n = x.shape[0]
    tq = Q_TILE
    tpr = seq // tq
    n_tiles = n // tq
    lam_init = 0.8 - 0.6 * math.exp(-0.3 * layer)
    att = lambda s: jnp.minimum(s, n_tiles - 1)
    ffn = lambda s: jnp.maximum(s - 1, 0)
    once = pl.Buffered(1)
    in_specs = [pl.BlockSpec((tq, 512), lambda s: (att(s), 0))] * 2
    in_specs += [pl.BlockSpec((seq, a.shape[1]), lambda s: (att(s) // tpr, 0), pipeline_mode=once) for a in own]
    for a in cache:
        nd = a.ndim - 2
        in_specs.append(pl.BlockSpec((None, None) + a.shape[2:],
                                     lambda s, nd=nd: (att(s) // tpr, layer) + (0,) * nd, pipeline_mode=once))
    in_specs += [_layer_spec((1, HEAD_DIM), layer)] * 4 + [_layer_spec((2 * HEAD_DIM, 1), layer)]
    in_specs += [pl.BlockSpec((tq, D_MODEL), lambda s: (ffn(s), 0)),
                 pl.BlockSpec((None, None, 1, 6 * D_MODEL), lambda s: (layer, 1 + ffn(s) // tpr, 0, 0)),
                 _layer_spec((D_MODEL, D_MODEL), layer), _layer_spec((1, D_MODEL), layer),
                 _layer_spec((D_MODEL, 2 * FFN_HIDDEN), layer), _layer_spec((FFN_HIDDEN, D_MODEL), layer)]
    keys = seq + cache[0].shape[-1]
    kv_w = own[0].shape[1] + own[2].shape[1]
    return pl.pallas_call(
        functools.partial(_attn_ffn_lagged_kernel, lam_init, tpr, n_tiles),
        grid=(n_tiles + 1,),
        in_specs=in_specs,
        out_specs=pl.BlockSpec((tq, D_MODEL), lambda s: (ffn(s), 0)),
        out_shape=jax.ShapeDtypeStruct((n, D_MODEL), F32),
        scratch_shapes=[pltpu.VMEM((keys, kv_w), BF16), pltpu.VMEM((kv_w, keys), BF16),
                        pltpu.VMEM((2, tq, D_MODEL), BF16)],
        compiler_params=pltpu.CompilerParams(
            dimension_semantics=("arbitrary",), vmem_limit_bytes=VMEM_LIMIT),
        name="attn_ffn_latent",
    )(qa, qb, *own, *cache, *lam_vecs, subln, x, mod4, w_out, norm_ffn, w_gu, w_down)


def _ffn(x, mix, mod4, layer, latent, tiles_per_request, w_out, norm_ffn, w_gu, w_down):
    n = x.shape[0]
    tm = 2 * TOKEN_TILE
    tiles_per_request = max(tiles_per_request // 2, 1)
    return pl.pallas_call(
        _ffn_kernel,
        grid=(n // tm,),
        in_specs=[
            pl.BlockSpec((tm, D_MODEL), lambda i: (i, 0)),
            pl.BlockSpec((tm, D_MODEL), lambda i: (i, 0)),
            _mod_spec(layer, latent, tiles_per_request),
            _layer_spec((D_MODEL, D_MODEL), layer),
            _layer_spec((1, D_MODEL), layer),
            _layer_spec((D_MODEL, 2 * FFN_HIDDEN), layer),
            _layer_spec((FFN_HIDDEN, D_MODEL), layer),
        ],
        out_specs=pl.BlockSpec((tm, D_MODEL), lambda i: (i, 0)),
        out_shape=jax.ShapeDtypeStruct((n, D_MODEL), F32),
        compiler_params=pltpu.CompilerParams(
            dimension_semantics=("arbitrary",), vmem_limit_bytes=VMEM_LIMIT),
        name="ffn_latent" if latent else "ffn_context",
    )(x, mix, mod4, w_out, norm_ffn, w_gu, w_down)


def _rope_tables(n_tokens):
    t = jnp.arange(n_tokens, dtype=jnp.int32)
    row = (t // GRID_W).astype(F32)
    col = (t % GRID_W).astype(F32)
    axis_dim = HEAD_DIM // 2
    freqs = ROPE_THETA ** (-jnp.arange(0, axis_dim, 2, dtype=F32) / axis_dim)
    ang = jnp.concatenate([row[:, None] * freqs, col[:, None] * freqs], axis=-1)
    c, s = jnp.cos(ang), jnp.sin(ang)
    return jnp.concatenate([c, c, c, c], axis=-1), jnp.concatenate([-s, s, -s, s], axis=-1)


def kernel(x_prompt, x_sample, cache_diff_k, cache_diff_v, cache_gqa_k, cache_gqa_v, c, c_ctx, w_mod, b_mod, norm_attn, w_in, q_norm_a, k_norm_a, lambda_q1, lambda_k1, lambda_q2, lambda_k2, subln, q_norm_b, k_norm_b, w_out, norm_ffn, w_gate_up, w_down):
    batch, seq, d = x_prompt.shape
    dec_batch, dec_seq, _ = x_sample.shape
    depth = w_mod.shape[0]
    past = cache_diff_k.shape[2]

    cond = jnp.concatenate([c_ctx[None], c, jnp.zeros((MOD_ROWS - 1 - dec_batch, d), F32)], axis=0)
    mod = _modulation(cond, w_mod, b_mod)
    mod4 = mod.reshape(depth, MOD_ROWS, 1, 6 * d)

    w_in_b = w_in.astype(BF16)
    w_out_b = w_out.astype(BF16)
    w_gu_b = w_gate_up.astype(BF16)
    w_down_b = w_down.astype(BF16)
    seg_ones = jnp.kron(jnp.eye(256 // HEAD_DIM, dtype=F32), jnp.ones((HEAD_DIM, HEAD_DIM), F32)).astype(BF16)
    rope = _rope_tables(dec_seq)

    cache = (jnp.transpose(cache_diff_k, (0, 1, 3, 4, 5, 2)).reshape(dec_batch, depth, DIFF_HEADS, 2 * HEAD_DIM, past),
             cache_diff_v.reshape(dec_batch, depth, past * DIFF_HEADS, 2 * HEAD_DIM),
             jnp.transpose(cache_gqa_k, (0, 1, 3, 4, 2)),
             jnp.transpose(cache_gqa_v, (0, 1, 3, 4, 2)))

    row = lambda a: a.reshape(depth, 1, a.shape[-1])
    gains = (row(q_norm_a), row(k_norm_a), row(q_norm_b), row(k_norm_b))
    lam_vecs = (row(lambda_q1), row(lambda_k1), row(lambda_q2), row(lambda_k2))
    na, nf, sl = row(norm_attn), row(norm_ffn), subln.reshape(depth, 2 * HEAD_DIM, 1)

    yp = x_prompt.reshape(batch * seq, d)
    ys = x_sample.reshape(dec_batch * dec_seq, d)
    ctx_tiles = seq // TOKEN_TILE
    lat_tiles = dec_seq // TOKEN_TILE
    new_caches = None
    for l in range(depth):
        qa, ka, va, qb, kb, vb, *new_caches = _qkv(
            yp, mod4, l, False, ctx_tiles, na, w_in_b, seg_ones, gains, None, new_caches)
        yp = _attn_ffn(l, yp, qa, qb, (ka, va, kb, vb), seq, lam_vecs, sl, mod4, w_out_b, nf, w_gu_b, w_down_b)

        qa, ka, va, qb, kb, vb = _qkv(
            ys, mod4, l, True, lat_tiles, na, w_in_b, seg_ones, gains, rope, None)
        ys = _attn_ffn_lagged(l, ys, qa, qb, (ka, va, kb, vb), cache, dec_seq, lam_vecs, sl,
                              mod4, w_out_b, nf, w_gu_b, w_down_b)

    ka_t, va_n, kb_t, vb_t = new_caches
    new_diff_k = jnp.transpose(ka_t.reshape(batch, depth, DIFF_HEADS, 2, HEAD_DIM, seq), (0, 1, 5, 2, 3, 4))
    new_gqa_k = jnp.transpose(kb_t.reshape(batch, depth, GQA_KV_HEADS, HEAD_DIM, seq), (0, 1, 4, 2, 3))
    new_gqa_v = jnp.transpose(vb_t.reshape(batch, depth, GQA_KV_HEADS, HEAD_DIM, seq), (0, 1, 4, 2, 3))
    return (yp.reshape(batch, seq, d), ys.reshape(dec_batch, dec_seq, d),
            new_diff_k, va_n.reshape(batch, depth, seq, DIFF_HEADS, 2 * HEAD_DIM), new_gqa_k, new_gqa_v)
```

```python
import functools
import math

import jax
import jax.numpy as jnp
from jax import lax
from jax.experimental import pallas as pl
from jax.experimental.pallas import tpu as pltpu

F32 = jnp.float32
BF16 = jnp.bfloat16

D_MODEL = 1024
HEAD_DIM = 64
GRID_W = 64
DIFF_HEADS = 4
GQA_KV_HEADS = 2
GQA_REP = 4
FFN_HIDDEN = 2816
IN_COLS = 2304
ROPE_THETA = 10000.0
EPS = 1e-6
LANES = 128
MOD_ROWS = 8
TOKEN_TILE = 256
Q_TILE = 256
VMEM_LIMIT = 56 * 1024 * 1024


def _dot(a, b):
    return jnp.dot(a, b, preferred_element_type=F32)


def _dot_nt(a, b):
    return lax.dot_general(a, b, (((1,), (1,)), ((), ())), preferred_element_type=F32)


def _rms(x, gain):
    ms = jnp.mean(x * x, axis=-1, keepdims=True)
    return x * lax.rsqrt(ms + EPS) * gain


def _layer_spec(shape, layer):
    n = len(shape)
    return pl.BlockSpec((None,) + tuple(shape), lambda *_: (layer,) + (0,) * n,
                        pipeline_mode=pl.Buffered(1))


def _const_spec(shape):
    n = len(shape)
    return pl.BlockSpec(shape, lambda *_: (0,) * n, pipeline_mode=pl.Buffered(1))


def _mod_spec(layer, latent, tiles_per_request):
    if latent:
        return pl.BlockSpec((None, None, 1, 6 * D_MODEL),
                            lambda i: (layer, 1 + i // tiles_per_request, 0, 0))
    return pl.BlockSpec((None, None, 1, 6 * D_MODEL), lambda i: (layer, 0, 0, 0))


def _mod_kernel(cond_ref, w_ref, b_ref, o_ref):
    c = cond_ref[...]
    s = c * jax.nn.sigmoid(c)
    o_ref[...] = _dot(s.astype(BF16), w_ref[...].astype(BF16)) + b_ref[...]


def _modulation(cond, w_mod, b_mod):
    depth = w_mod.shape[0]
    tn = 1536
    return pl.pallas_call(
        _mod_kernel,
        grid=(depth, 6 * D_MODEL // tn),
        in_specs=[
            pl.BlockSpec((MOD_ROWS, D_MODEL), lambda l, j: (0, 0)),
            pl.BlockSpec((None, D_MODEL, tn), lambda l, j: (l, 0, j)),
            pl.BlockSpec((None, 1, tn), lambda l, j: (l, 0, j)),
        ],
        out_specs=pl.BlockSpec((None, MOD_ROWS, tn), lambda l, j: (l, 0, j)),
        out_shape=jax.ShapeDtypeStruct((depth, MOD_ROWS, 6 * D_MODEL), F32),
        compiler_params=pltpu.CompilerParams(
            dimension_semantics=("arbitrary", "arbitrary"), vmem_limit_bytes=VMEM_LIMIT),
        name="modulation",
    )(cond, w_mod, b_mod.reshape(depth, 1, 6 * D_MODEL))


def _head_norm(x, seg_ones, gain):
    ss = _dot((x * x).astype(BF16), seg_ones)
    return x * lax.rsqrt(ss * (1.0 / HEAD_DIM) + EPS) * gain


def _swap_halves(x, first_half):
    return jnp.where(first_half, pltpu.roll(x, 96, 1), pltpu.roll(x, 32, 1))


def _rope(x, cos, sin, first_half):
    out = []
    for j in range(x.shape[1] // LANES):
        xj = x[:, j * LANES:(j + 1) * LANES]
        out.append(xj * cos + _swap_halves(xj, first_half) * sin)
    return out[0] if len(out) == 1 else jnp.concatenate(out, axis=1)


def _tile_gain(g):
    return jnp.concatenate([g] * (256 // HEAD_DIM), axis=1)


def _qkv_kernel(latent, n_cast, *refs):
    if latent:
        (x_ref, mod_ref, na_ref, w_ref, seg_ref, gqa_ref, gka_ref, gqb_ref, gkb_ref,
         cos_ref, sin_ref, qa_o, ka_o, va_o, qb_o, kb_o, vb_o) = refs
    else:
        (x_ref, mod_ref, na_ref, w_ref, seg_ref, gqa_ref, gka_ref, gqb_ref, gkb_ref) = refs[:9]
        outs = refs[len(refs) - 10 - n_cast:]
        qa_o, ka_o, va_o, qb_o, kb_o, vb_o, kaf_o, vaf_o, kbf_o, vbf_o = outs[:10]
        for src, dst in zip(refs[9:9 + n_cast], outs[10:]):
            dst[...] = src[...].astype(BF16)
    tm = TOKEN_TILE
    sh = mod_ref[:, 0:D_MODEL]
    sc = mod_ref[:, D_MODEL:2 * D_MODEL]
    seg = seg_ref[...]
    lane = lax.broadcasted_iota(jnp.int32, (1, LANES), 1)
    low = lane < HEAD_DIM
    first_half = (lane % HEAD_DIM) < (HEAD_DIM // 2)
    scale = HEAD_DIM ** -0.5 * math.log2(math.e)

    def project(r):
        h = _rms(x_ref[r, :], na_ref[...]) * (1.0 + sc) + sh
        return _dot(h.astype(BF16), w_ref[...])

    def finish(j, r, proj):
        def qk(lo_col, width, gain_ref):
            gain = _tile_gain(gain_ref[...])
            out = []
            for c0 in range(0, width, 256):
                w = min(256, width - c0)
                out.append(_head_norm(proj[:, lo_col + c0:lo_col + c0 + w], seg[:w, :w], gain[:, :w]))
            y = out[0] if len(out) == 1 else jnp.concatenate(out, axis=1)
            if latent:
                y = _rope(y, cos_ref[r, :], sin_ref[r, :], first_half)
            return y

        qa = qk(0, 512, gqa_ref)
        ka = qk(512, 512, gka_ref)
        va = proj[:, 1024:1536]
        qb = qk(1536, 512, gqb_ref)
        kb = qk(2048, 128, gkb_ref)
        vb = proj[:, 2176:2304]
        qa_o[r, :] = (qa * scale).astype(BF16)
        ka_o[r, :] = ka.astype(BF16)
        va_o[r, :] = va.astype(BF16)
        qb_o[r, :] = (qb * scale).astype(BF16)
        kb_o[r, :] = kb.astype(BF16)
        vb_o[r, :] = vb.astype(BF16)
        if not latent:
            kaf_o[j] = ka.T
            kbf_o[j] = kb.T
            vbf_o[j] = vb.T
            for h in range(DIFF_HEADS):
                vaf_o[j, pl.ds(h, tm, stride=DIFF_HEADS), :] = va[:, h * LANES:(h + 1) * LANES]

    rows = [slice(j * tm, (j + 1) * tm) for j in range(x_ref.shape[0] // tm)]
    pending = [project(rows[0])]
    for j, r in enumerate(rows):
        if j + 1 < len(rows):
            pending.append(project(rows[j + 1]))
        finish(j, r, pending.pop(0))


def _qkv(x, mod4, layer, latent, tiles_per_request, norm_attn, w_in, seg_ones, gains, rope,
         prev_caches=None, cast_weights=()):
    n = x.shape[0]
    sub = 4
    tm = sub * TOKEN_TILE
    depth = w_in.shape[0]
    tok = lambda w: pl.BlockSpec((tm, w), lambda i: (i, 0))
    in_specs = [
        tok(D_MODEL),
        _mod_spec(layer, latent, max(tiles_per_request // sub, 1)),
        _layer_spec((1, D_MODEL), layer),
        _layer_spec((D_MODEL, IN_COLS), layer),
        _const_spec((256, 256)),
    ] + [_layer_spec((1, HEAD_DIM), layer)] * 4
    args = [x, mod4, norm_attn, w_in, seg_ones, *gains]
    aliases = {}
    out_specs = [tok(512), tok(512), tok(512), tok(512), tok(LANES), tok(LANES)]
    out_shape = [jax.ShapeDtypeStruct((n, w), BF16) for w in (512, 512, 512, 512, LANES, LANES)]
    if latent:
        rope_spec = pl.BlockSpec((tm, LANES), lambda i: (i % (tiles_per_request // sub), 0))
        in_specs += [rope_spec, rope_spec]
        args += list(rope)
    else:
        assert tiles_per_request == 1, "context requests must be one token tile long"
        seq = TOKEN_TILE
        req = n // seq
        slab = lambda *dims: pl.BlockSpec((sub, None) + dims, lambda i: (i, layer) + (0,) * len(dims))
        out_specs += [slab(512, seq), slab(seq * DIFF_HEADS, LANES), slab(LANES, seq), slab(LANES, seq)]
        out_shape += [jax.ShapeDtypeStruct((req, depth, 512, seq), F32),
                      jax.ShapeDtypeStruct((req, depth, seq * DIFF_HEADS, LANES), F32),
                      jax.ShapeDtypeStruct((req, depth, LANES, seq), F32),
                      jax.ShapeDtypeStruct((req, depth, LANES, seq), F32)]
        steps = n // tm
        for w in cast_weights:
            rows, cols = w.shape[1] // steps, w.shape[2]
            assert rows * steps == w.shape[1] and rows % 16 == 0
            in_specs.append(pl.BlockSpec((None, rows, cols), lambda i: (layer, i, 0)))
            args.append(w)
            out_specs.append(pl.BlockSpec((rows, cols), lambda i: (i, 0)))
            out_shape.append(jax.ShapeDtypeStruct(w.shape[1:], BF16))
        if prev_caches is not None:
            aliases = {len(args) + k: 6 + k for k in range(4)}
            in_specs += [pl.BlockSpec(memory_space=pl.ANY)] * 4
            args += list(prev_caches)
    return pl.pallas_call(
        functools.partial(_qkv_kernel, latent, len(cast_weights)),
        grid=(n // tm,),
        in_specs=in_specs,
        out_specs=out_specs,
        out_shape=out_shape,
        input_output_aliases=aliases,
        compiler_params=pltpu.CompilerParams(
            dimension_semantics=("arbitrary",), vmem_limit_bytes=VMEM_LIMIT),
        name="qkv_latent" if latent else "qkv_context",
    )(*args)


def _pad_rows(x, block):
    z = jnp.zeros_like(x)
    return jnp.concatenate([x, z] if block == 0 else [z, x], axis=0)


ONES_ROWS = 16


def _softmax_pv(vt, st):
    d = vt.shape[0]
    e = jnp.exp2(st - st.max(axis=0, keepdims=True)).astype(BF16)
    ext = _dot(jnp.concatenate([vt, jnp.ones((ONES_ROWS, vt.shape[1]), BF16)], axis=0), e)
    return ext[:d], 1.0 / ext[d:d + 1]


KA_W = DIFF_HEADS * LANES


def _stage_kv(k_all, vt_all, ka, va, kb, vb, cache_refs):
    seq = ka.shape[0]
    k_all[0:seq, 0:KA_W] = ka
    k_all[0:seq, KA_W:] = kb
    vt_all[0:KA_W, 0:seq] = va.astype(F32).T.astype(BF16)
    vt_all[KA_W:, 0:seq] = vb.astype(F32).T.astype(BF16)
    if cache_refs is not None:
        cka_ref, cva_ref, ckb_ref, cvb_ref = cache_refs
        past = cka_ref.shape[2]
        for h in range(DIFF_HEADS):
            cs = slice(h * LANES, (h + 1) * LANES)
            k_all[seq:, cs] = cka_ref[h].T.astype(BF16)
            vt_all[cs, seq:] = cva_ref[pl.ds(h, past, stride=DIFF_HEADS), :].T.astype(BF16)
        k_all[seq:, KA_W:] = jnp.concatenate([ckb_ref[0], ckb_ref[1]], axis=0).T.astype(BF16)
        for g in range(GQA_KV_HEADS):
            vt_all[KA_W + g * HEAD_DIM:KA_W + (g + 1) * HEAD_DIM, seq:] = cvb_ref[g].astype(BF16)


def _lambda(lq1, lk1, lq2, lk2, lam_init):
    return (jnp.exp(jnp.sum(lq1[...] * lk1[...], axis=-1, keepdims=True))
            - jnp.exp(jnp.sum(lq2[...] * lk2[...], axis=-1, keepdims=True)) + lam_init)


def _attention_tasks(qa, qb, k_all, vt_all, lam, subln, lam_init, store):
    tq = qa.shape[0]
    qat = qa.astype(F32).T.astype(BF16)
    qbt = qb.astype(F32).T.astype(BF16)
    tasks = []

    def diff_finish(st, h, cs):
        num, r = _softmax_pv(vt_all[cs, :], st)
        ot = num[:, :tq] * r[:, :tq] - num[:, tq:] * (r[:, tq:] * lam)
        ms = jnp.mean(ot * ot, axis=0, keepdims=True)
        ot = ot * lax.rsqrt(ms + EPS) * subln * (1.0 - lam_init)
        store(cs, ot.T)

    def gqa_finish(st, g, os):
        num, r = _softmax_pv(vt_all[KA_W + g * HEAD_DIM:KA_W + (g + 1) * HEAD_DIM, :], st)
        ot = num * r
        store(os, jnp.concatenate([ot[:, :tq], ot[:, tq:]], axis=0).T)

    for h in range(DIFF_HEADS):
        cs = slice(h * LANES, (h + 1) * LANES)

        def logits(h=h, cs=cs):
            c0 = qat[h * LANES:h * LANES + HEAD_DIM]
            c1 = qat[h * LANES + HEAD_DIM:(h + 1) * LANES]
            return _dot(k_all[:, cs], jnp.concatenate([_pad_rows(c0, 0), _pad_rows(c1, 1)], axis=1))

        tasks.append((logits, lambda st, h=h, cs=cs: diff_finish(st, h, cs)))
    for g in range(GQA_KV_HEADS):
        for rp in range(2):
            r0 = (GQA_REP * g + 2 * rp) * HEAD_DIM
            os = slice(KA_W + r0, KA_W + r0 + LANES)

            def logits(g=g, r0=r0):
                a = qbt[r0:r0 + HEAD_DIM]
                b = qbt[r0 + HEAD_DIM:r0 + LANES]
                return _dot(k_all[:, KA_W:], jnp.concatenate([_pad_rows(a, g), _pad_rows(b, g)], axis=1))

            tasks.append((logits, lambda st, g=g, os=os: gqa_finish(st, g, os)))
    return tasks


LOGITS_AHEAD = 2
N_TASKS = DIFF_HEADS + 2 * GQA_KV_HEADS


def _attention_units(tasks):
    pending = []

    def prefill(t):
        pending.append(t[0]())

    def step(i):
        if i + LOGITS_AHEAD < len(tasks):
            pending.append(tasks[i + LOGITS_AHEAD][0]())
        tasks[i][1](pending.pop(0))

    units = [functools.partial(prefill, t) for t in tasks[:LOGITS_AHEAD]]
    return units + [functools.partial(step, i) for i in range(len(tasks))]


def _ffn_units(r, x_ref, mix_ref, mod_ref, wo_ref, nf_ref, wgu_ref, wd_ref, o_ref):
    g_a = mod_ref[:, 2 * D_MODEL:3 * D_MODEL]
    sh_f = mod_ref[:, 3 * D_MODEL:4 * D_MODEL]
    sc_f = mod_ref[:, 4 * D_MODEL:5 * D_MODEL]
    g_f = mod_ref[:, 5 * D_MODEL:6 * D_MODEL]
    state = {}

    def attn_residual():
        state["x"] = x_ref[r, :] + g_a * _dot(mix_ref[r, :], wo_ref[...])

    def gate():
        state["h"] = (_rms(state["x"], nf_ref[...]) * (1.0 + sc_f) + sh_f).astype(BF16)
        state["gate"] = _dot(state["h"], wgu_ref[:, :FFN_HIDDEN])

    def up():
        g = state.pop("gate")
        state["act"] = (g * jax.nn.sigmoid(g) * _dot(state.pop("h"), wgu_ref[:, FFN_HIDDEN:])).astype(BF16)

    def ffn_residual():
        o_ref[r, :] = state.pop("x") + g_f * _dot(state.pop("act"), wd_ref[...])

    return [attn_residual, gate, up, ffn_residual]


def _interleave(a, b):
    out, ia, ib = [], 0, 0
    while ia < len(a) or ib < len(b):
        if ib >= len(b) or (ia < len(a) and ia * len(b) <= ib * len(a)):
            out.append(a[ia])
            ia += 1
        else:
            out.append(b[ib])
            ib += 1
    return out


def _attn_ffn_kernel(lam_init, sub, *refs):
    (qa_ref, qb_ref, ka_ref, va_ref, kb_ref, vb_ref, lq1, lk1, lq2, lk2, subln_ref,
     x_ref, mod_ref, wo_ref, nf_ref, wgu_ref, wd_ref, o_ref, k_all, vt_all, mix) = refs
    seq = ka_ref.shape[0] // sub
    lam = _lambda(lq1, lk1, lq2, lk2, lam_init)
    subln = subln_ref[...]
    attn, ffn = [], []
    for j in range(sub):
        r = slice(j * seq, (j + 1) * seq)

        later = []

        def store(cols, tile, r=r):
            mix[r, cols] = tile.astype(mix.dtype)

        def stage(j=j, r=r, later=later, store=store):
            _stage_kv(k_all.at[j], vt_all.at[j], ka_ref[r, :], va_ref[r, :], kb_ref[r, :], vb_ref[r, :], None)
            later.extend(_attention_units(_attention_tasks(
                qa_ref[r, :], qb_ref[r, :], k_all.at[j], vt_all.at[j], lam, subln, lam_init, store)))

        attn.append([stage] + [lambda k=k, later=later: later[k]() for k in range(LOGITS_AHEAD + N_TASKS)])
        ffn.append(_ffn_units(r, x_ref, mix, mod_ref, wo_ref, nf_ref, wgu_ref, wd_ref, o_ref))
    units = attn[0]
    for j in range(1, sub):
        units = units + _interleave(attn[j], ffn[j - 1])
    for unit in units + ffn[sub - 1]:
        unit()


def _attn_ffn(layer, x, qa, qb, own, seq, lam_vecs, subln, mod4, w_out, norm_ffn, w_gu, w_down):
    n = x.shape[0]
    sub = 4
    tm = sub * seq
    lam_init = 0.8 - 0.6 * math.exp(-0.3 * layer)
    tok = lambda w: pl.BlockSpec((tm, w), lambda i: (i, 0))
    kv_w = own[0].shape[1] + own[2].shape[1]
    in_specs = ([tok(512), tok(512)] + [tok(a.shape[1]) for a in own]
                + [_layer_spec((1, HEAD_DIM), layer)] * 4 + [_layer_spec((2 * HEAD_DIM, 1), layer)]
                + [tok(D_MODEL), _mod_spec(layer, False, 1),
                   _const_spec((D_MODEL, D_MODEL)), _layer_spec((1, D_MODEL), layer),
                   _const_spec((D_MODEL, 2 * FFN_HIDDEN)), _const_spec((FFN_HIDDEN, D_MODEL))])
    return pl.pallas_call(
        functools.partial(_attn_ffn_kernel, lam_init, sub),
        grid=(n // tm,),
        in_specs=in_specs,
        out_specs=tok(D_MODEL),
        out_shape=jax.ShapeDtypeStruct((n, D_MODEL), F32),
        scratch_shapes=[pltpu.VMEM((sub, seq, kv_w), BF16), pltpu.VMEM((sub, kv_w, seq), BF16),
                        pltpu.VMEM((tm, D_MODEL), BF16)],
        compiler_params=pltpu.CompilerParams(
            dimension_semantics=("arbitrary",), vmem_limit_bytes=VMEM_LIMIT),
        name="attn_ffn_context",
    )(qa, qb, *own, *lam_vecs, subln, x, mod4, w_out, norm_ffn, w_gu, w_down)


def _attn_ffn_lagged_kernel(lam_init, tiles_per_request, n_tiles, *refs):
    (qa_ref, qb_ref, ka_ref, va_ref, kb_ref, vb_ref, cka_ref, cva_ref, ckb_ref, cvb_ref,
     lq1, lk1, lq2, lk2, subln_ref, x_ref, mod_ref, wo_ref, nf_ref, wgu_ref, wd_ref,
     o_ref, k_all, vt_all, mix) = refs
    s = pl.program_id(0)
    tq = qa_ref.shape[0]
    lam = _lambda(lq1, lk1, lq2, lk2, lam_init)
    subln = subln_ref[...]

    def attention_units(slot):
        def store(cols, tile):
            mix[slot, :, cols] = tile.astype(mix.dtype)

        return _attention_units(_attention_tasks(
            qa_ref[...], qb_ref[...], k_all, vt_all, lam, subln, lam_init, store))

    def ffn_units(slot):
        return _ffn_units(slice(0, tq), x_ref, mix.at[slot], mod_ref, wo_ref, nf_ref, wgu_ref, wd_ref, o_ref)

    @pl.when(jnp.logical_and(s % tiles_per_request == 0, s < n_tiles))
    def _():
        _stage_kv(k_all, vt_all, ka_ref[...], va_ref[...], kb_ref[...], vb_ref[...],
                  (cka_ref, cva_ref, ckb_ref, cvb_ref))

    @pl.when(s == 0)
    def _():
        for unit in attention_units(0):
            unit()

    @pl.when(jnp.logical_and(s > 0, s < n_tiles))
    def _():
        for unit in _interleave(attention_units(s % 2), ffn_units((s - 1) % 2)):
            unit()

    @pl.when(s == n_tiles)
    def _():
        for unit in ffn_units((n_tiles - 1) % 2):
            unit()


def _attn_ffn_lagged(layer, x, qa, qb, own, cache, seq, lam_vecs, subln, mod4, w_out, norm_ffn, w_gu, w_down):
    n = x.shape[0]
    tq = Q_TILE
    tpr = seq // tq
    n_tiles = n // tq
    lam_init = 0.8 - 0.6 * math.exp(-0.3 * layer)
    att = lambda s: jnp.minimum(s, n_tiles - 1)
    ffn = lambda s: jnp.maximum(s - 1, 0)
    once = pl.Buffered(1)
    in_specs = [pl.BlockSpec((tq, 512), lambda s: (att(s), 0))] * 2
    in_specs += [pl.BlockSpec((seq, a.shape[1]), lambda s: (att(s) // tpr, 0), pipeline_mode=once) for a in own]
    for a in cache:
        nd = a.ndim - 2
        in_specs.append(pl.BlockSpec((None, None) + a.shape[2:],
                                     lambda s, nd=nd: (att(s) // tpr, layer) + (0,) * nd, pipeline_mode=once))
    in_specs += [_layer_spec((1, HEAD_DIM), layer)] * 4 + [_layer_spec((2 * HEAD_DIM, 1), layer)]
    in_specs += [pl.BlockSpec((tq, D_MODEL), lambda s: (ffn(s), 0)),
                 pl.BlockSpec((None, None, 1, 6 * D_MODEL), lambda s: (layer, 1 + ffn(s) // tpr, 0, 0)),
                 _const_spec((D_MODEL, D_MODEL)), _layer_spec((1, D_MODEL), layer),
                 _const_spec((D_MODEL, 2 * FFN_HIDDEN)), _const_spec((FFN_HIDDEN, D_MODEL))]
    keys = seq + cache[0].shape[-1]
    kv_w = own[0].shape[1] + own[2].shape[1]
    return pl.pallas_call(
        functools.partial(_attn_ffn_lagged_kernel, lam_init, tpr, n_tiles),
        grid=(n_tiles + 1,),
        in_specs=in_specs,
        out_specs=pl.BlockSpec((tq, D_MODEL), lambda s: (ffn(s), 0)),
        out_shape=jax.ShapeDtypeStruct((n, D_MODEL), F32),
        scratch_shapes=[pltpu.VMEM((keys, kv_w), BF16), pltpu.VMEM((kv_w, keys), BF16),
                        pltpu.VMEM((2, tq, D_MODEL), BF16)],
        compiler_params=pltpu.CompilerParams(
            dimension_semantics=("arbitrary",), vmem_limit_bytes=VMEM_LIMIT),
        name="attn_ffn_latent",
    )(qa, qb, *own, *cache, *lam_vecs, subln, x, mod4, w_out, norm_ffn, w_gu, w_down)


def _rope_tables(n_tokens):
    t = jnp.arange(n_tokens, dtype=jnp.int32)
    row = (t // GRID_W).astype(F32)
    col = (t % GRID_W).astype(F32)
    axis_dim = HEAD_DIM // 2
    freqs = ROPE_THETA ** (-jnp.arange(0, axis_dim, 2, dtype=F32) / axis_dim)
    ang = jnp.concatenate([row[:, None] * freqs, col[:, None] * freqs], axis=-1)
    c, s = jnp.cos(ang), jnp.sin(ang)
    return jnp.concatenate([c, c, c, c], axis=-1), jnp.concatenate([-s, s, -s, s], axis=-1)


def kernel(x_prompt, x_sample, cache_diff_k, cache_diff_v, cache_gqa_k, cache_gqa_v, c, c_ctx, w_mod, b_mod, norm_attn, w_in, q_norm_a, k_norm_a, lambda_q1, lambda_k1, lambda_q2, lambda_k2, subln, q_norm_b, k_norm_b, w_out, norm_ffn, w_gate_up, w_down):
    batch, seq, d = x_prompt.shape
    dec_batch, dec_seq, _ = x_sample.shape
    depth = w_mod.shape[0]
    past = cache_diff_k.shape[2]

    cond = jnp.concatenate([c_ctx[None], c, jnp.zeros((MOD_ROWS - 1 - dec_batch, d), F32)], axis=0)
    mod = _modulation(cond, w_mod, b_mod)
    mod4 = mod.reshape(depth, MOD_ROWS, 1, 6 * d)

    w_in_b = w_in.astype(BF16)
    seg_ones = jnp.kron(jnp.eye(256 // HEAD_DIM, dtype=F32), jnp.ones((HEAD_DIM, HEAD_DIM), F32)).astype(BF16)
    rope = _rope_tables(dec_seq)

    cache = (jnp.transpose(cache_diff_k, (0, 1, 3, 4, 5, 2)).reshape(dec_batch, depth, DIFF_HEADS, 2 * HEAD_DIM, past),
             cache_diff_v.reshape(dec_batch, depth, past * DIFF_HEADS, 2 * HEAD_DIM),
             jnp.transpose(cache_gqa_k, (0, 1, 3, 4, 2)),
             jnp.transpose(cache_gqa_v, (0, 1, 3, 4, 2)))

    row = lambda a: a.reshape(depth, 1, a.shape[-1])
    gains = (row(q_norm_a), row(k_norm_a), row(q_norm_b), row(k_norm_b))
    lam_vecs = (row(lambda_q1), row(lambda_k1), row(lambda_q2), row(lambda_k2))
    na, nf, sl = row(norm_attn), row(norm_ffn), subln.reshape(depth, 2 * HEAD_DIM, 1)

    yp = x_prompt.reshape(batch * seq, d)
    ys = x_sample.reshape(dec_batch * dec_seq, d)
    ctx_tiles = seq // TOKEN_TILE
    lat_tiles = dec_seq // TOKEN_TILE
    new_caches = None
    for l in range(depth):
        qa, ka, va, qb, kb, vb, *rest = _qkv(
            yp, mod4, l, False, ctx_tiles, na, w_in_b, seg_ones, gains, None, new_caches,
            (w_out, w_gate_up, w_down))
        new_caches, (w_out_b, w_gu_b, w_down_b) = rest[:4], rest[4:]
        yp = _attn_ffn(l, yp, qa, qb, (ka, va, kb, vb), seq, lam_vecs, sl, mod4, w_out_b, nf, w_gu_b, w_down_b)

        qa, ka, va, qb, kb, vb = _qkv(
            ys, mod4, l, True, lat_tiles, na, w_in_b, seg_ones, gains, rope)
        ys = _attn_ffn_lagged(l, ys, qa, qb, (ka, va, kb, vb), cache, dec_seq, lam_vecs, sl,
                              mod4, w_out_b, nf, w_gu_b, w_down_b)

    ka_t, va_n, kb_t, vb_t = new_caches
    new_diff_k = jnp.transpose(ka_t.reshape(batch, depth, DIFF_HEADS, 2, HEAD_DIM, seq), (0, 1, 5, 2, 3, 4))
    new_gqa_k = jnp.transpose(kb_t.reshape(batch, depth, GQA_KV_HEADS, HEAD_DIM, seq), (0, 1, 4, 2, 3))
    new_gqa_v = jnp.transpose(vb_t.reshape(batch, depth, GQA_KV_HEADS, HEAD_DIM, seq), (0, 1, 4, 2, 3))
    return (yp.reshape(batch, seq, d), ys.reshape(dec_batch, dec_seq, d),
            new_diff_k, va_n.reshape(batch, depth, seq, DIFF_HEADS, 2 * HEAD_DIM), new_gqa_k, new_gqa_v)
```

```python
import functools
import math

import jax
import jax.numpy as jnp
from jax import lax
from jax.experimental import pallas as pl
from jax.experimental.pallas import tpu as pltpu

F32 = jnp.float32
BF16 = jnp.bfloat16

D_MODEL = 1024
HEAD_DIM = 64
GRID_W = 64
DIFF_HEADS = 4
GQA_KV_HEADS = 2
GQA_REP = 4
FFN_HIDDEN = 2816
IN_COLS = 2304
ROPE_THETA = 10000.0
EPS = 1e-6
LANES = 128
MOD_ROWS = 8
TOKEN_TILE = 256
Q_TILE = 256
VMEM_LIMIT = 56 * 1024 * 1024


def _dot(a, b):
    return jnp.dot(a, b, preferred_element_type=F32)


def _dot_nt(a, b):
    return lax.dot_general(a, b, (((1,), (1,)), ((), ())), preferred_element_type=F32)


def _rms(x, gain):
    ms = jnp.mean(x * x, axis=-1, keepdims=True)
    return x * lax.rsqrt(ms + EPS) * gain


def _layer_spec(shape, layer):
    n = len(shape)
    return pl.BlockSpec((None,) + tuple(shape), lambda *_: (layer,) + (0,) * n,
                        pipeline_mode=pl.Buffered(1))


def _const_spec(shape):
    n = len(shape)
    return pl.BlockSpec(shape, lambda *_: (0,) * n, pipeline_mode=pl.Buffered(1))


def _mod_spec(layer, latent, tiles_per_request):
    if latent:
        return pl.BlockSpec((None, None, 1, 6 * D_MODEL),
                            lambda i: (layer, 1 + i // tiles_per_request, 0, 0))
    return pl.BlockSpec((None, None, 1, 6 * D_MODEL), lambda i: (layer, 0, 0, 0))


def _mod_kernel(cond_ref, w_ref, b_ref, o_ref):
    c = cond_ref[...]
    s = c * jax.nn.sigmoid(c)
    o_ref[...] = _dot(s.astype(BF16), w_ref[...].astype(BF16)) + b_ref[...]


def _modulation(cond, w_mod, b_mod):
    depth = w_mod.shape[0]
    tn = 1536
    return pl.pallas_call(
        _mod_kernel,
        grid=(depth, 6 * D_MODEL // tn),
        in_specs=[
            pl.BlockSpec((MOD_ROWS, D_MODEL), lambda l, j: (0, 0)),
            pl.BlockSpec((None, D_MODEL, tn), lambda l, j: (l, 0, j)),
            pl.BlockSpec((None, 1, tn), lambda l, j: (l, 0, j)),
        ],
        out_specs=pl.BlockSpec((None, MOD_ROWS, tn), lambda l, j: (l, 0, j)),
        out_shape=jax.ShapeDtypeStruct((depth, MOD_ROWS, 6 * D_MODEL), F32),
        compiler_params=pltpu.CompilerParams(
            dimension_semantics=("arbitrary", "arbitrary"), vmem_limit_bytes=VMEM_LIMIT),
        name="modulation",
    )(cond, w_mod, b_mod.reshape(depth, 1, 6 * D_MODEL))


def _head_norm(x, seg_ones, gain):
    ss = _dot((x * x).astype(BF16), seg_ones)
    return x * lax.rsqrt(ss * (1.0 / HEAD_DIM) + EPS) * gain


def _swap_halves(x, first_half):
    return jnp.where(first_half, pltpu.roll(x, 96, 1), pltpu.roll(x, 32, 1))


def _rope(x, cos, sin, first_half):
    out = []
    for j in range(x.shape[1] // LANES):
        xj = x[:, j * LANES:(j + 1) * LANES]
        out.append(xj * cos + _swap_halves(xj, first_half) * sin)
    return out[0] if len(out) == 1 else jnp.concatenate(out, axis=1)


def _tile_gain(g):
    return jnp.concatenate([g] * (256 // HEAD_DIM), axis=1)


def _qkv_kernel(latent, n_cast, *refs):
    if latent:
        (x_ref, mod_ref, na_ref, w_ref, seg_ref, gqa_ref, gka_ref, gqb_ref, gkb_ref,
         cos_ref, sin_ref, qa_o, ka_o, va_o, qb_o, kb_o, vb_o) = refs
    else:
        (x_ref, mod_ref, na_ref, w_ref, seg_ref, gqa_ref, gka_ref, gqb_ref, gkb_ref) = refs[:9]
        outs = refs[len(refs) - 10 - n_cast:]
        qa_o, ka_o, va_o, qb_o, kb_o, vb_o, kaf_o, vaf_o, kbf_o, vbf_o = outs[:10]
        for src, dst in zip(refs[9:9 + n_cast], outs[10:]):
            dst[...] = src[...].astype(BF16)
    tm = TOKEN_TILE
    sh = mod_ref[:, 0:D_MODEL]
    sc = mod_ref[:, D_MODEL:2 * D_MODEL]
    seg = seg_ref[...]
    lane = lax.broadcasted_iota(jnp.int32, (1, LANES), 1)
    low = lane < HEAD_DIM
    first_half = (lane % HEAD_DIM) < (HEAD_DIM // 2)
    scale = HEAD_DIM ** -0.5 * math.log2(math.e)

    def project(r):
        h = _rms(x_ref[r, :], na_ref[...]) * (1.0 + sc) + sh
        return _dot(h.astype(BF16), w_ref[...])

    def finish(j, r, proj):
        def qk(lo_col, width, gain_ref):
            gain = _tile_gain(gain_ref[...])
            out = []
            for c0 in range(0, width, 256):
                w = min(256, width - c0)
                out.append(_head_norm(proj[:, lo_col + c0:lo_col + c0 + w], seg[:w, :w], gain[:, :w]))
            y = out[0] if len(out) == 1 else jnp.concatenate(out, axis=1)
            if latent:
                y = _rope(y, cos_ref[r, :], sin_ref[r, :], first_half)
            return y

        qa = qk(0, 512, gqa_ref)
        ka = qk(512, 512, gka_ref)
        va = proj[:, 1024:1536]
        qb = qk(1536, 512, gqb_ref)
        kb = qk(2048, 128, gkb_ref)
        vb = proj[:, 2176:2304]
        qa_o[r, :] = (qa * scale).astype(BF16)
        ka_o[r, :] = ka.astype(BF16)
        va_o[r, :] = va.astype(BF16)
        qb_o[r, :] = (qb * scale).astype(BF16)
        kb_o[r, :] = kb.astype(BF16)
        vb_o[r, :] = vb.astype(BF16)
        if not latent:
            kaf_o[j] = ka.T
            kbf_o[j] = kb.T
            vbf_o[j] = vb.T
            for h in range(DIFF_HEADS):
                vaf_o[j, pl.ds(h, tm, stride=DIFF_HEADS), :] = va[:, h * LANES:(h + 1) * LANES]

    rows = [slice(j * tm, (j + 1) * tm) for j in range(x_ref.shape[0] // tm)]
    pending = [project(rows[0])]
    for j, r in enumerate(rows):
        if j + 1 < len(rows):
            pending.append(project(rows[j + 1]))
        finish(j, r, pending.pop(0))


def _qkv(x, mod4, layer, latent, tiles_per_request, norm_attn, w_in, seg_ones, gains, rope,
         prev_caches=None, cast_weights=()):
    n = x.shape[0]
    sub = 4
    tm = sub * TOKEN_TILE
    depth = w_in.shape[0]
    tok = lambda w: pl.BlockSpec((tm, w), lambda i: (i, 0))
    in_specs = [
        tok(D_MODEL),
        _mod_spec(layer, latent, max(tiles_per_request // sub, 1)),
        _layer_spec((1, D_MODEL), layer),
        _layer_spec((D_MODEL, IN_COLS), layer),
        _const_spec((256, 256)),
    ] + [_layer_spec((1, HEAD_DIM), layer)] * 4
    args = [x, mod4, norm_attn, w_in, seg_ones, *gains]
    aliases = {}
    out_specs = [tok(512), tok(512), tok(512), tok(512), tok(LANES), tok(LANES)]
    out_shape = [jax.ShapeDtypeStruct((n, w), BF16) for w in (512, 512, 512, 512, LANES, LANES)]
    if latent:
        rope_spec = pl.BlockSpec((tm, LANES), lambda i: (i % (tiles_per_request // sub), 0))
        in_specs += [rope_spec, rope_spec]
        args += list(rope)
    else:
        assert tiles_per_request == 1, "context requests must be one token tile long"
        seq = TOKEN_TILE
        req = n // seq
        slab = lambda *dims: pl.BlockSpec((sub, None) + dims, lambda i: (i, layer) + (0,) * len(dims))
        out_specs += [slab(512, seq), slab(seq * DIFF_HEADS, LANES), slab(LANES, seq), slab(LANES, seq)]
        out_shape += [jax.ShapeDtypeStruct((req, depth, 512, seq), F32),
                      jax.ShapeDtypeStruct((req, depth, seq * DIFF_HEADS, LANES), F32),
                      jax.ShapeDtypeStruct((req, depth, LANES, seq), F32),
                      jax.ShapeDtypeStruct((req, depth, LANES, seq), F32)]
        steps = n // tm
        for w in cast_weights:
            rows, cols = w.shape[1] // steps, w.shape[2]
            assert rows * steps == w.shape[1] and rows % 16 == 0
            in_specs.append(pl.BlockSpec((None, rows, cols), lambda i: (layer, i, 0)))
            args.append(w)
            out_specs.append(pl.BlockSpec((rows, cols), lambda i: (i, 0)))
            out_shape.append(jax.ShapeDtypeStruct(w.shape[1:], BF16))
        if prev_caches is not None:
            aliases = {len(args) + k: 6 + k for k in range(4)}
            in_specs += [pl.BlockSpec(memory_space=pl.ANY)] * 4
            args += list(prev_caches)
    return pl.pallas_call(
        functools.partial(_qkv_kernel, latent, len(cast_weights)),
        grid=(n // tm,),
        in_specs=in_specs,
        out_specs=out_specs,
        out_shape=out_shape,
        input_output_aliases=aliases,
        compiler_params=pltpu.CompilerParams(
            dimension_semantics=("arbitrary",), vmem_limit_bytes=VMEM_LIMIT),
        name="qkv_latent" if latent else "qkv_context",
    )(*args)


def _pad_rows(x, block):
    z = jnp.zeros_like(x)
    return jnp.concatenate([x, z] if block == 0 else [z, x], axis=0)


ONES_ROWS = 16


SHIFT_KEYS = 32
DENOM_RANGE = (2.0 ** -100, 2.0 ** 100)


def _numerators(st, exact):
    shift = (st if exact else st[:SHIFT_KEYS]).max(axis=0, keepdims=True)
    return jnp.exp2(st - shift).astype(BF16)


def _pv(vt, e):
    d = vt.shape[0]
    ext = _dot(jnp.concatenate([vt, jnp.ones((ONES_ROWS, vt.shape[1]), BF16)], axis=0), e)
    return ext[:d], ext[d:d + 1]


def _denominators_bad(denoms):
    bad = None
    for l in denoms:
        b = jnp.where(jnp.logical_and(l > DENOM_RANGE[0], l < DENOM_RANGE[1]), 0.0, 1.0)
        bad = b if bad is None else jnp.maximum(bad, b)
    return (jnp.max(bad) > 0.0).astype(jnp.int32)


KA_W = DIFF_HEADS * LANES


def _stage_kv(k_all, vt_all, ka, va, kb, vb, cache_refs):
    seq = ka.shape[0]
    k_all[0:seq, 0:KA_W] = ka
    k_all[0:seq, KA_W:] = kb
    vt_all[0:KA_W, 0:seq] = va.astype(F32).T.astype(BF16)
    vt_all[KA_W:, 0:seq] = vb.astype(F32).T.astype(BF16)
    if cache_refs is not None:
        cka_ref, cva_ref, ckb_ref, cvb_ref = cache_refs
        past = cka_ref.shape[2]
        for h in range(DIFF_HEADS):
            cs = slice(h * LANES, (h + 1) * LANES)
            k_all[seq:, cs] = cka_ref[h].T.astype(BF16)
            vt_all[cs, seq:] = cva_ref[pl.ds(h, past, stride=DIFF_HEADS), :].T.astype(BF16)
        k_all[seq:, KA_W:] = jnp.concatenate([ckb_ref[0], ckb_ref[1]], axis=0).T.astype(BF16)
        for g in range(GQA_KV_HEADS):
            vt_all[KA_W + g * HEAD_DIM:KA_W + (g + 1) * HEAD_DIM, seq:] = cvb_ref[g].astype(BF16)


def _lambda(lq1, lk1, lq2, lk2, lam_init):
    return (jnp.exp(jnp.sum(lq1[...] * lk1[...], axis=-1, keepdims=True))
            - jnp.exp(jnp.sum(lq2[...] * lk2[...], axis=-1, keepdims=True)) + lam_init)


def _attention_tasks(qa, qb, k_all, vt_all, lam, subln, lam_init, store, denoms=None):
    tq = qa.shape[0]
    qat = qa.astype(F32).T.astype(BF16)
    qbt = qb.astype(F32).T.astype(BF16)
    exact = denoms is None
    tasks = []

    def first(st):
        return st if exact else _numerators(st, False)

    def pv(vt, s):
        num, l = _pv(vt, _numerators(s, True) if exact else s)
        if not exact:
            denoms.append(l)
        return num, 1.0 / l

    def diff_finish(s, h, cs):
        num, r = pv(vt_all[cs, :], s)
        ot = num[:, :tq] * r[:, :tq] - num[:, tq:] * (r[:, tq:] * lam)
        ms = jnp.mean(ot * ot, axis=0, keepdims=True)
        ot = ot * lax.rsqrt(ms + EPS) * subln * (1.0 - lam_init)
        store(cs, ot.T)

    def gqa_finish(s, g, os):
        num, r = pv(vt_all[KA_W + g * HEAD_DIM:KA_W + (g + 1) * HEAD_DIM, :], s)
        ot = num * r
        store(os, jnp.concatenate([ot[:, :tq], ot[:, tq:]], axis=0).T)

    for h in range(DIFF_HEADS):
        cs = slice(h * LANES, (h + 1) * LANES)

        def logits(h=h, cs=cs):
            c0 = qat[h * LANES:h * LANES + HEAD_DIM]
            c1 = qat[h * LANES + HEAD_DIM:(h + 1) * LANES]
            return first(_dot(k_all[:, cs], jnp.concatenate([_pad_rows(c0, 0), _pad_rows(c1, 1)], axis=1)))

        tasks.append((logits, lambda s, h=h, cs=cs: diff_finish(s, h, cs)))
    for g in range(GQA_KV_HEADS):
        for rp in range(2):
            r0 = (GQA_REP * g + 2 * rp) * HEAD_DIM
            os = slice(KA_W + r0, KA_W + r0 + LANES)

            def logits(g=g, r0=r0):
                a = qbt[r0:r0 + HEAD_DIM]
                b = qbt[r0 + HEAD_DIM:r0 + LANES]
                return first(_dot(k_all[:, KA_W:], jnp.concatenate([_pad_rows(a, g), _pad_rows(b, g)], axis=1)))

            tasks.append((logits, lambda s, g=g, os=os: gqa_finish(s, g, os)))
    return tasks


LOGITS_AHEAD = 2
N_TASKS = DIFF_HEADS + 2 * GQA_KV_HEADS


def _attention_units(tasks):
    pending = []

    def prefill(t):
        pending.append(t[0]())

    def step(i):
        if i + LOGITS_AHEAD < len(tasks):
            pending.append(tasks[i + LOGITS_AHEAD][0]())
        tasks[i][1](pending.pop(0))

    units = [functools.partial(prefill, t) for t in tasks[:LOGITS_AHEAD]]
    return units + [functools.partial(step, i) for i in range(len(tasks))]


def _ffn_units(r, x_ref, mix_ref, mod_ref, wo_ref, nf_ref, wgu_ref, wd_ref, o_ref):
    g_a = mod_ref[:, 2 * D_MODEL:3 * D_MODEL]
    sh_f = mod_ref[:, 3 * D_MODEL:4 * D_MODEL]
    sc_f = mod_ref[:, 4 * D_MODEL:5 * D_MODEL]
    g_f = mod_ref[:, 5 * D_MODEL:6 * D_MODEL]
    state = {}

    def attn_residual():
        state["x"] = x_ref[r, :] + g_a * _dot(mix_ref[r, :], wo_ref[...])

    def gate():
        state["h"] = (_rms(state["x"], nf_ref[...]) * (1.0 + sc_f) + sh_f).astype(BF16)
        state["gate"] = _dot(state["h"], wgu_ref[:, :FFN_HIDDEN])

    def up():
        g = state.pop("gate")
        state["act"] = (g * jax.nn.sigmoid(g) * _dot(state.pop("h"), wgu_ref[:, FFN_HIDDEN:])).astype(BF16)

    def ffn_residual():
        o_ref[r, :] = state.pop("x") + g_f * _dot(state.pop("act"), wd_ref[...])

    return [attn_residual, gate, up, ffn_residual]


def _interleave(a, b):
    out, ia, ib = [], 0, 0
    while ia < len(a) or ib < len(b):
        if ib >= len(b) or (ia < len(a) and ia * len(b) <= ib * len(a)):
            out.append(a[ia])
            ia += 1
        else:
            out.append(b[ib])
            ib += 1
    return out


def _attn_ffn_kernel(lam_init, sub, *refs):
    (qa_ref, qb_ref, ka_ref, va_ref, kb_ref, vb_ref, lq1, lk1, lq2, lk2, subln_ref,
     x_ref, mod_ref, wo_ref, nf_ref, wgu_ref, wd_ref, o_ref, k_all, vt_all, mix) = refs
    seq = ka_ref.shape[0] // sub
    lam = _lambda(lq1, lk1, lq2, lk2, lam_init)
    subln = subln_ref[...]
    attn, ffn = [], []
    for j in range(sub):
        r = slice(j * seq, (j + 1) * seq)

        later = []

        def store(cols, tile, r=r):
            mix[r, cols] = tile.astype(mix.dtype)

        def stage(j=j, r=r, later=later, store=store):
            _stage_kv(k_all.at[j], vt_all.at[j], ka_ref[r, :], va_ref[r, :], kb_ref[r, :], vb_ref[r, :], None)
            later.extend(_attention_units(_attention_tasks(
                qa_ref[r, :], qb_ref[r, :], k_all.at[j], vt_all.at[j], lam, subln, lam_init, store)))

        attn.append([stage] + [lambda k=k, later=later: later[k]() for k in range(LOGITS_AHEAD + N_TASKS)])
        ffn.append(_ffn_units(r, x_ref, mix, mod_ref, wo_ref, nf_ref, wgu_ref, wd_ref, o_ref))
    units = attn[0]
    for j in range(1, sub):
        units = units + _interleave(attn[j], ffn[j - 1])
    for unit in units + ffn[sub - 1]:
        unit()


def _attn_ffn(layer, x, qa, qb, own, seq, lam_vecs, subln, mod4, w_out, norm_ffn, w_gu, w_down):
    n = x.shape[0]
    sub = 4
    tm = sub * seq
    lam_init = 0.8 - 0.6 * math.exp(-0.3 * layer)
    tok = lambda w: pl.BlockSpec((tm, w), lambda i: (i, 0))
    kv_w = own[0].shape[1] + own[2].shape[1]
    in_specs = ([tok(512), tok(512)] + [tok(a.shape[1]) for a in own]
                + [_layer_spec((1, HEAD_DIM), layer)] * 4 + [_layer_spec((2 * HEAD_DIM, 1), layer)]
                + [tok(D_MODEL), _mod_spec(layer, False, 1),
                   _const_spec((D_MODEL, D_MODEL)), _layer_spec((1, D_MODEL), layer),
                   _const_spec((D_MODEL, 2 * FFN_HIDDEN)), _const_spec((FFN_HIDDEN, D_MODEL))])
    return pl.pallas_call(
        functools.partial(_attn_ffn_kernel, lam_init, sub),
        grid=(n // tm,),
        in_specs=in_specs,
        out_specs=tok(D_MODEL),
        out_shape=jax.ShapeDtypeStruct((n, D_MODEL), F32),
        scratch_shapes=[pltpu.VMEM((sub, seq, kv_w), BF16), pltpu.VMEM((sub, kv_w, seq), BF16),
                        pltpu.VMEM((tm, D_MODEL), BF16)],
        compiler_params=pltpu.CompilerParams(
            dimension_semantics=("arbitrary",), vmem_limit_bytes=VMEM_LIMIT),
        name="attn_ffn_context",
    )(qa, qb, *own, *lam_vecs, subln, x, mod4, w_out, norm_ffn, w_gu, w_down)


def _attn_ffn_lagged_kernel(lam_init, tiles_per_request, n_tiles, *refs):
    (qa_ref, qb_ref, ka_ref, va_ref, kb_ref, vb_ref, cka_ref, cva_ref, ckb_ref, cvb_ref,
     lq1, lk1, lq2, lk2, subln_ref, x_ref, mod_ref, wo_ref, nf_ref, wgu_ref, wd_ref,
     o_ref, k_all, vt_all, mix, redo) = refs
    s = pl.program_id(0)
    tq = qa_ref.shape[0]
    lam = _lambda(lq1, lk1, lq2, lk2, lam_init)
    subln = subln_ref[...]

    def attention_units(slot, exact=False):
        def store(cols, tile):
            mix[slot, :, cols] = tile.astype(mix.dtype)

        denoms = None if exact else []
        units = _attention_units(_attention_tasks(
            qa_ref[...], qb_ref[...], k_all, vt_all, lam, subln, lam_init, store, denoms))
        if exact:
            return units

        def check():
            redo[0] = _denominators_bad(denoms)

        return units + [check]

    def ffn_units(slot):
        return _ffn_units(slice(0, tq), x_ref, mix.at[slot], mod_ref, wo_ref, nf_ref, wgu_ref, wd_ref, o_ref)

    @pl.when(jnp.logical_and(s % tiles_per_request == 0, s < n_tiles))
    def _():
        _stage_kv(k_all, vt_all, ka_ref[...], va_ref[...], kb_ref[...], vb_ref[...],
                  (cka_ref, cva_ref, ckb_ref, cvb_ref))

    @pl.when(s == 0)
    def _():
        for unit in attention_units(0):
            unit()

    @pl.when(jnp.logical_and(s > 0, s < n_tiles))
    def _():
        for unit in _interleave(attention_units(s % 2), ffn_units((s - 1) % 2)):
            unit()

    @pl.when(jnp.logical_and(s < n_tiles, redo[0] != 0))
    def _():
        for unit in attention_units(s % 2, exact=True):
            unit()

    @pl.when(s == n_tiles)
    def _():
        for unit in ffn_units((n_tiles - 1) % 2):
            unit()


def _attn_ffn_lagged(layer, x, qa, qb, own, cache, seq, lam_vecs, subln, mod4, w_out, norm_ffn, w_gu, w_down):
    n = x.shape[0]
    tq = Q_TILE
    tpr = seq // tq
    n_tiles = n // tq
    lam_init = 0.8 - 0.6 * math.exp(-0.3 * layer)
    att = lambda s: jnp.minimum(s, n_tiles - 1)
    ffn = lambda s: jnp.maximum(s - 1, 0)
    once = pl.Buffered(1)
    in_specs = [pl.BlockSpec((tq, 512), lambda s: (att(s), 0))] * 2
    in_specs += [pl.BlockSpec((seq, a.shape[1]), lambda s: (att(s) // tpr, 0), pipeline_mode=once) for a in own]
    for a in cache:
        nd = a.ndim - 2
        in_specs.append(pl.BlockSpec((None, None) + a.shape[2:],
                                     lambda s, nd=nd: (att(s) // tpr, layer) + (0,) * nd, pipeline_mode=once))
    in_specs += [_layer_spec((1, HEAD_DIM), layer)] * 4 + [_layer_spec((2 * HEAD_DIM, 1), layer)]
    in_specs += [pl.BlockSpec((tq, D_MODEL), lambda s: (ffn(s), 0)),
                 pl.BlockSpec((None, None, 1, 6 * D_MODEL), lambda s: (layer, 1 + ffn(s) // tpr, 0, 0)),
                 _const_spec((D_MODEL, D_MODEL)), _layer_spec((1, D_MODEL), layer),
                 _const_spec((D_MODEL, 2 * FFN_HIDDEN)), _const_spec((FFN_HIDDEN, D_MODEL))]
    keys = seq + cache[0].shape[-1]
    kv_w = own[0].shape[1] + own[2].shape[1]
    return pl.pallas_call(
        functools.partial(_attn_ffn_lagged_kernel, lam_init, tpr, n_tiles),
        grid=(n_tiles + 1,),
        in_specs=in_specs,
        out_specs=pl.BlockSpec((tq, D_MODEL), lambda s: (ffn(s), 0)),
        out_shape=jax.ShapeDtypeStruct((n, D_MODEL), F32),
        scratch_shapes=[pltpu.VMEM((keys, kv_w), BF16), pltpu.VMEM((kv_w, keys), BF16),
                        pltpu.VMEM((2, tq, D_MODEL), BF16), pltpu.SMEM((1,), jnp.int32)],
        compiler_params=pltpu.CompilerParams(
            dimension_semantics=("arbitrary",), vmem_limit_bytes=VMEM_LIMIT),
        name="attn_ffn_latent",
    )(qa, qb, *own, *cache, *lam_vecs, subln, x, mod4, w_out, norm_ffn, w_gu, w_down)


def _rope_tables(n_tokens):
    t = jnp.arange(n_tokens, dtype=jnp.int32)
    row = (t // GRID_W).astype(F32)
    col = (t % GRID_W).astype(F32)
    axis_dim = HEAD_DIM // 2
    freqs = ROPE_THETA ** (-jnp.arange(0, axis_dim, 2, dtype=F32) / axis_dim)
    ang = jnp.concatenate([row[:, None] * freqs, col[:, None] * freqs], axis=-1)
    c, s = jnp.cos(ang), jnp.sin(ang)
    return jnp.concatenate([c, c, c, c], axis=-1), jnp.concatenate([-s, s, -s, s], axis=-1)


def kernel(x_prompt, x_sample, cache_diff_k, cache_diff_v, cache_gqa_k, cache_gqa_v, c, c_ctx, w_mod, b_mod, norm_attn, w_in, q_norm_a, k_norm_a, lambda_q1, lambda_k1, lambda_q2, lambda_k2, subln, q_norm_b, k_norm_b, w_out, norm_ffn, w_gate_up, w_down):
    batch, seq, d = x_prompt.shape
    dec_batch, dec_seq, _ = x_sample.shape
    depth = w_mod.shape[0]
    past = cache_diff_k.shape[2]

    cond = jnp.concatenate([c_ctx[None], c, jnp.zeros((MOD_ROWS - 1 - dec_batch, d), F32)], axis=0)
    mod = _modulation(cond, w_mod, b_mod)
    mod4 = mod.reshape(depth, MOD_ROWS, 1, 6 * d)

    w_in_b = w_in.astype(BF16)
    seg_ones = jnp.kron(jnp.eye(256 // HEAD_DIM, dtype=F32), jnp.ones((HEAD_DIM, HEAD_DIM), F32)).astype(BF16)
    rope = _rope_tables(dec_seq)

    cache = (jnp.transpose(cache_diff_k, (0, 1, 3, 4, 5, 2)).reshape(dec_batch, depth, DIFF_HEADS, 2 * HEAD_DIM, past),
             cache_diff_v.reshape(dec_batch, depth, past * DIFF_HEADS, 2 * HEAD_DIM),
             jnp.transpose(cache_gqa_k, (0, 1, 3, 4, 2)),
             jnp.transpose(cache_gqa_v, (0, 1, 3, 4, 2)))

    row = lambda a: a.reshape(depth, 1, a.shape[-1])
    gains = (row(q_norm_a), row(k_norm_a), row(q_norm_b), row(k_norm_b))
    lam_vecs = (row(lambda_q1), row(lambda_k1), row(lambda_q2), row(lambda_k2))
    na, nf, sl = row(norm_attn), row(norm_ffn), subln.reshape(depth, 2 * HEAD_DIM, 1)

    yp = x_prompt.reshape(batch * seq, d)
    ys = x_sample.reshape(dec_batch * dec_seq, d)
    ctx_tiles = seq // TOKEN_TILE
    lat_tiles = dec_seq // TOKEN_TILE
    new_caches = None
    for l in range(depth):
        qa, ka, va, qb, kb, vb, *rest = _qkv(
            yp, mod4, l, False, ctx_tiles, na, w_in_b, seg_ones, gains, None, new_caches,
            (w_out, w_gate_up, w_down))
        new_caches, (w_out_b, w_gu_b, w_down_b) = rest[:4], rest[4:]
        yp = _attn_ffn(l, yp, qa, qb, (ka, va, kb, vb), seq, lam_vecs, sl, mod4, w_out_b, nf, w_gu_b, w_down_b)

        qa, ka, va, qb, kb, vb = _qkv(
            ys, mod4, l, True, lat_tiles, na, w_in_b, seg_ones, gains, rope)
        ys = _attn_ffn_lagged(l, ys, qa, qb, (ka, va, kb, vb), cache, dec_seq, lam_vecs, sl,
                              mod4, w_out_b, nf, w_gu_b, w_down_b)

    ka_t, va_n, kb_t, vb_t = new_caches
    new_diff_k = jnp.transpose(ka_t.reshape(batch, depth, DIFF_HEADS, 2, HEAD_DIM, seq), (0, 1, 5, 2, 3, 4))
    new_gqa_k = jnp.transpose(kb_t.reshape(batch, depth, GQA_KV_HEADS, HEAD_DIM, seq), (0, 1, 4, 2, 3))
    new_gqa_v = jnp.transpose(vb_t.reshape(batch, depth, GQA_KV_HEADS, HEAD_DIM, seq), (0, 1, 4, 2, 3))
    return (yp.reshape(batch, seq, d), ys.reshape(dec_batch, dec_seq, d),
            new_diff_k, va_n.reshape(batch, depth, seq, DIFF_HEADS, 2 * HEAD_DIM), new_gqa_k, new_gqa_v)
```

```python
import functools
import math

import jax
import jax.numpy as jnp
from jax import lax
from jax.experimental import pallas as pl
from jax.experimental.pallas import tpu as pltpu

F32 = jnp.float32
BF16 = jnp.bfloat16

D_MODEL = 1024
HEAD_DIM = 64
GRID_W = 64
DIFF_HEADS = 4
GQA_KV_HEADS = 2
GQA_REP = 4
FFN_HIDDEN = 2816
IN_COLS = 2304
ROPE_THETA = 10000.0
EPS = 1e-6
LANES = 128
MOD_ROWS = 8
TOKEN_TILE = 256
Q_TILE = 256
VMEM_LIMIT = 56 * 1024 * 1024


def _dot(a, b):
    return jnp.dot(a, b, preferred_element_type=F32)


def _dot_nt(a, b):
    return lax.dot_general(a, b, (((1,), (1,)), ((), ())), preferred_element_type=F32)


def _rms(x, gain):
    ms = jnp.mean(x * x, axis=-1, keepdims=True)
    return x * lax.rsqrt(ms + EPS) * gain


def _layer_spec(shape, layer):
    n = len(shape)
    return pl.BlockSpec((None,) + tuple(shape), lambda *_: (layer,) + (0,) * n,
                        pipeline_mode=pl.Buffered(1))


def _const_spec(shape):
    n = len(shape)
    return pl.BlockSpec(shape, lambda *_: (0,) * n, pipeline_mode=pl.Buffered(1))


def _mod_spec(layer, latent, tiles_per_request):
    if latent:
        return pl.BlockSpec((None, None, 1, 6 * D_MODEL),
                            lambda i: (layer, 1 + i // tiles_per_request, 0, 0))
    return pl.BlockSpec((None, None, 1, 6 * D_MODEL), lambda i: (layer, 0, 0, 0))


def _mod_kernel(cond_ref, w_ref, b_ref, o_ref):
    c = cond_ref[...]
    s = c * jax.nn.sigmoid(c)
    o_ref[...] = _dot(s.astype(BF16), w_ref[...].astype(BF16)) + b_ref[...]


def _modulation(cond, w_mod, b_mod):
    depth = w_mod.shape[0]
    tn = 1536
    return pl.pallas_call(
        _mod_kernel,
        grid=(depth, 6 * D_MODEL // tn),
        in_specs=[
            pl.BlockSpec((MOD_ROWS, D_MODEL), lambda l, j: (0, 0)),
            pl.BlockSpec((None, D_MODEL, tn), lambda l, j: (l, 0, j)),
            pl.BlockSpec((None, 1, tn), lambda l, j: (l, 0, j)),
        ],
        out_specs=pl.BlockSpec((None, MOD_ROWS, tn), lambda l, j: (l, 0, j)),
        out_shape=jax.ShapeDtypeStruct((depth, MOD_ROWS, 6 * D_MODEL), F32),
        compiler_params=pltpu.CompilerParams(
            dimension_semantics=("arbitrary", "arbitrary"), vmem_limit_bytes=VMEM_LIMIT),
        name="modulation",
    )(cond, w_mod, b_mod.reshape(depth, 1, 6 * D_MODEL))


def _head_norm(x, seg_ones, gain):
    ss = _dot((x * x).astype(BF16), seg_ones)
    return x * lax.rsqrt(ss * (1.0 / HEAD_DIM) + EPS) * gain


def _swap_halves(x, first_half):
    return jnp.where(first_half, pltpu.roll(x, 96, 1), pltpu.roll(x, 32, 1))


def _rope(x, cos, sin, first_half):
    out = []
    for j in range(x.shape[1] // LANES):
        xj = x[:, j * LANES:(j + 1) * LANES]
        out.append(xj * cos + _swap_halves(xj, first_half) * sin)
    return out[0] if len(out) == 1 else jnp.concatenate(out, axis=1)


def _tile_gain(g):
    return jnp.concatenate([g] * (256 // HEAD_DIM), axis=1)


def _qkv_kernel(latent, n_cast, *refs):
    if latent:
        (x_ref, mod_ref, na_ref, w_ref, seg_ref, gqa_ref, gka_ref, gqb_ref, gkb_ref,
         cos_ref, sin_ref, qa_o, ka_o, va_o, qb_o, kb_o, vb_o) = refs
    else:
        (x_ref, mod_ref, na_ref, w_ref, seg_ref, gqa_ref, gka_ref, gqb_ref, gkb_ref) = refs[:9]
        outs = refs[len(refs) - 10 - n_cast:]
        qa_o, ka_o, va_o, qb_o, kb_o, vb_o, kaf_o, vaf_o, kbf_o, vbf_o = outs[:10]
        for src, dst in zip(refs[9:9 + n_cast], outs[10:]):
            dst[...] = src[...].astype(BF16)
    tm = TOKEN_TILE
    sh = mod_ref[:, 0:D_MODEL]
    sc = mod_ref[:, D_MODEL:2 * D_MODEL]
    seg = seg_ref[...]
    lane = lax.broadcasted_iota(jnp.int32, (1, LANES), 1)
    low = lane < HEAD_DIM
    first_half = (lane % HEAD_DIM) < (HEAD_DIM // 2)
    scale = HEAD_DIM ** -0.5 * math.log2(math.e)

    def project(r):
        h = _rms(x_ref[r, :], na_ref[...]) * (1.0 + sc) + sh
        return _dot(h.astype(BF16), w_ref[...])

    def finish(j, r, proj):
        def qk(lo_col, width, gain_ref):
            gain = _tile_gain(gain_ref[...])
            out = []
            for c0 in range(0, width, 256):
                w = min(256, width - c0)
                out.append(_head_norm(proj[:, lo_col + c0:lo_col + c0 + w], seg[:w, :w], gain[:, :w]))
            y = out[0] if len(out) == 1 else jnp.concatenate(out, axis=1)
            if latent:
                y = _rope(y, cos_ref[r, :], sin_ref[r, :], first_half)
            return y

        qa = qk(0, 512, gqa_ref)
        ka = qk(512, 512, gka_ref)
        va = proj[:, 1024:1536]
        qb = qk(1536, 512, gqb_ref)
        kb = qk(2048, 128, gkb_ref)
        vb = proj[:, 2176:2304]
        qa_o[r, :] = (qa * scale).astype(BF16)
        ka_o[r, :] = ka.astype(BF16)
        va_o[r, :] = va.astype(BF16)
        qb_o[r, :] = (qb * scale).astype(BF16)
        kb_o[r, :] = kb.astype(BF16)
        vb_o[r, :] = vb.astype(BF16)
        if not latent:
            kaf_o[j] = ka.T
            kbf_o[j] = kb.T
            vbf_o[j] = vb.T
            for h in range(DIFF_HEADS):
                vaf_o[j, pl.ds(h, tm, stride=DIFF_HEADS), :] = va[:, h * LANES:(h + 1) * LANES]

    rows = [slice(j * tm, (j + 1) * tm) for j in range(x_ref.shape[0] // tm)]
    pending = [project(rows[0])]
    for j, r in enumerate(rows):
        if j + 1 < len(rows):
            pending.append(project(rows[j + 1]))
        finish(j, r, pending.pop(0))


def _qkv(x, mod4, layer, latent, tiles_per_request, norm_attn, w_in, seg_ones, gains, rope,
         prev_caches=None, cast_weights=()):
    n = x.shape[0]
    sub = 4
    tm = sub * TOKEN_TILE
    depth = w_in.shape[0]
    tok = lambda w: pl.BlockSpec((tm, w), lambda i: (i, 0))
    in_specs = [
        tok(D_MODEL),
        _mod_spec(layer, latent, max(tiles_per_request // sub, 1)),
        _layer_spec((1, D_MODEL), layer),
        _layer_spec((D_MODEL, IN_COLS), layer),
        _const_spec((256, 256)),
    ] + [_layer_spec((1, HEAD_DIM), layer)] * 4
    args = [x, mod4, norm_attn, w_in, seg_ones, *gains]
    aliases = {}
    out_specs = [tok(512), tok(512), tok(512), tok(512), tok(LANES), tok(LANES)]
    out_shape = [jax.ShapeDtypeStruct((n, w), BF16) for w in (512, 512, 512, 512, LANES, LANES)]
    if latent:
        rope_spec = pl.BlockSpec((tm, LANES), lambda i: (i % (tiles_per_request // sub), 0))
        in_specs += [rope_spec, rope_spec]
        args += list(rope)
    else:
        assert tiles_per_request == 1, "context requests must be one token tile long"
        seq = TOKEN_TILE
        req = n // seq
        slab = lambda *dims: pl.BlockSpec((sub, None) + dims, lambda i: (i, layer) + (0,) * len(dims))
        out_specs += [slab(512, seq), slab(seq * DIFF_HEADS, LANES), slab(LANES, seq), slab(LANES, seq)]
        out_shape += [jax.ShapeDtypeStruct((req, depth, 512, seq), F32),
                      jax.ShapeDtypeStruct((req, depth, seq * DIFF_HEADS, LANES), F32),
                      jax.ShapeDtypeStruct((req, depth, LANES, seq), F32),
                      jax.ShapeDtypeStruct((req, depth, LANES, seq), F32)]
        steps = n // tm
        for w in cast_weights:
            rows, cols = w.shape[1] // steps, w.shape[2]
            assert rows * steps == w.shape[1] and rows % 16 == 0
            in_specs.append(pl.BlockSpec((None, rows, cols), lambda i: (layer, i, 0)))
            args.append(w)
            out_specs.append(pl.BlockSpec((rows, cols), lambda i: (i, 0)))
            out_shape.append(jax.ShapeDtypeStruct(w.shape[1:], BF16))
        if prev_caches is not None:
            aliases = {len(args) + k: 6 + k for k in range(4)}
            in_specs += [pl.BlockSpec(memory_space=pl.ANY)] * 4
            args += list(prev_caches)
    return pl.pallas_call(
        functools.partial(_qkv_kernel, latent, len(cast_weights)),
        grid=(n // tm,),
        in_specs=in_specs,
        out_specs=out_specs,
        out_shape=out_shape,
        input_output_aliases=aliases,
        compiler_params=pltpu.CompilerParams(
            dimension_semantics=("arbitrary",), vmem_limit_bytes=VMEM_LIMIT),
        name="qkv_latent" if latent else "qkv_context",
    )(*args)


def _pad_rows(x, block):
    z = jnp.zeros_like(x)
    return jnp.concatenate([x, z] if block == 0 else [z, x], axis=0)


ONES_ROWS = 16


SHIFT_KEYS = 32
DENOM_RANGE = (2.0 ** -100, 2.0 ** 100)


def _numerators(st, exact):
    shift = (st if exact else st[:SHIFT_KEYS]).max(axis=0, keepdims=True)
    return jnp.exp2(st - shift).astype(BF16)


def _pv(vt, e):
    d = vt.shape[0]
    ext = _dot(jnp.concatenate([vt, jnp.ones((ONES_ROWS, vt.shape[1]), BF16)], axis=0), e)
    return ext[:d], ext[d:d + 1]


def _denominators_bad(denoms):
    bad = None
    for l in denoms:
        b = jnp.where(jnp.logical_and(l > DENOM_RANGE[0], l < DENOM_RANGE[1]), 0.0, 1.0)
        bad = b if bad is None else jnp.maximum(bad, b)
    return (jnp.max(bad) > 0.0).astype(jnp.int32)


KA_W = DIFF_HEADS * LANES


def _stage_kv(k_all, vt_all, ka, va, kb, vb, cache_refs):
    seq = ka.shape[0]
    k_all[0:seq, 0:KA_W] = ka
    k_all[0:seq, KA_W:] = kb
    vt_all[0:KA_W, 0:seq] = va.astype(F32).T.astype(BF16)
    vt_all[KA_W:, 0:seq] = vb.astype(F32).T.astype(BF16)
    if cache_refs is not None:
        cka_ref, cva_ref, ckb_ref, cvb_ref = cache_refs
        past = cka_ref.shape[2]
        for h in range(DIFF_HEADS):
            cs = slice(h * LANES, (h + 1) * LANES)
            k_all[seq:, cs] = cka_ref[h].T.astype(BF16)
            vt_all[cs, seq:] = cva_ref[pl.ds(h, past, stride=DIFF_HEADS), :].T.astype(BF16)
        k_all[seq:, KA_W:] = jnp.concatenate([ckb_ref[0], ckb_ref[1]], axis=0).T.astype(BF16)
        for g in range(GQA_KV_HEADS):
            vt_all[KA_W + g * HEAD_DIM:KA_W + (g + 1) * HEAD_DIM, seq:] = cvb_ref[g].astype(BF16)


def _lambda(lq1, lk1, lq2, lk2, lam_init):
    return (jnp.exp(jnp.sum(lq1[...] * lk1[...], axis=-1, keepdims=True))
            - jnp.exp(jnp.sum(lq2[...] * lk2[...], axis=-1, keepdims=True)) + lam_init)


def _attention_tasks(qa, qb, k_all, vt_all, lam, subln, lam_init, store, denoms=None):
    tq = qa.shape[0]
    qat = qa.astype(F32).T.astype(BF16)
    qbt = qb.astype(F32).T.astype(BF16)
    exact = denoms is None
    tasks = []

    def first(st):
        return st if exact else _numerators(st, False)

    def pv(vt, s):
        num, l = _pv(vt, _numerators(s, True) if exact else s)
        if not exact:
            denoms.append(l)
        return num, 1.0 / l

    def diff_finish(s, h, cs):
        num, r = pv(vt_all[cs, :], s)
        ot = num[:, :tq] * r[:, :tq] - num[:, tq:] * (r[:, tq:] * lam)
        ms = jnp.mean(ot * ot, axis=0, keepdims=True)
        ot = ot * lax.rsqrt(ms + EPS) * subln * (1.0 - lam_init)
        store(cs, ot.T)

    def gqa_finish(s, g, os):
        num, r = pv(vt_all[KA_W + g * HEAD_DIM:KA_W + (g + 1) * HEAD_DIM, :], s)
        ot = num * r
        store(os, jnp.concatenate([ot[:, :tq], ot[:, tq:]], axis=0).T)

    for h in range(DIFF_HEADS):
        cs = slice(h * LANES, (h + 1) * LANES)

        def logits(h=h, cs=cs):
            c0 = qat[h * LANES:h * LANES + HEAD_DIM]
            c1 = qat[h * LANES + HEAD_DIM:(h + 1) * LANES]
            return first(_dot(k_all[:, cs], jnp.concatenate([_pad_rows(c0, 0), _pad_rows(c1, 1)], axis=1)))

        tasks.append((logits, lambda s, h=h, cs=cs: diff_finish(s, h, cs)))
    for g in range(GQA_KV_HEADS):
        for rp in range(2):
            r0 = (GQA_REP * g + 2 * rp) * HEAD_DIM
            os = slice(KA_W + r0, KA_W + r0 + LANES)

            def logits(g=g, r0=r0):
                a = qbt[r0:r0 + HEAD_DIM]
                b = qbt[r0 + HEAD_DIM:r0 + LANES]
                return first(_dot(k_all[:, KA_W:], jnp.concatenate([_pad_rows(a, g), _pad_rows(b, g)], axis=1)))

            tasks.append((logits, lambda s, g=g, os=os: gqa_finish(s, g, os)))
    return tasks


LOGITS_AHEAD = 2
GUARDED_AHEAD = 3
N_TASKS = DIFF_HEADS + 2 * GQA_KV_HEADS


def _attention_units(tasks, ahead=LOGITS_AHEAD):
    pending = []

    def prefill(t):
        pending.append(t[0]())

    def step(i):
        if i + ahead < len(tasks):
            pending.append(tasks[i + ahead][0]())
        tasks[i][1](pending.pop(0))

    units = [functools.partial(prefill, t) for t in tasks[:ahead]]
    return units + [functools.partial(step, i) for i in range(len(tasks))]


def _ffn_units(r, x_ref, mix_ref, mod_ref, wo_ref, nf_ref, wgu_ref, wd_ref, o_ref):
    g_a = mod_ref[:, 2 * D_MODEL:3 * D_MODEL]
    sh_f = mod_ref[:, 3 * D_MODEL:4 * D_MODEL]
    sc_f = mod_ref[:, 4 * D_MODEL:5 * D_MODEL]
    g_f = mod_ref[:, 5 * D_MODEL:6 * D_MODEL]
    state = {}

    def attn_residual():
        state["x"] = x_ref[r, :] + g_a * _dot(mix_ref[r, :], wo_ref[...])

    def gate():
        state["h"] = (_rms(state["x"], nf_ref[...]) * (1.0 + sc_f) + sh_f).astype(BF16)
        state["gate"] = _dot(state["h"], wgu_ref[:, :FFN_HIDDEN])

    def up():
        g = state.pop("gate")
        state["act"] = (g * jax.nn.sigmoid(g) * _dot(state.pop("h"), wgu_ref[:, FFN_HIDDEN:])).astype(BF16)

    def ffn_residual():
        o_ref[r, :] = state.pop("x") + g_f * _dot(state.pop("act"), wd_ref[...])

    return [attn_residual, gate, up, ffn_residual]


def _interleave(a, b):
    out, ia, ib = [], 0, 0
    while ia < len(a) or ib < len(b):
        if ib >= len(b) or (ia < len(a) and ia * len(b) <= ib * len(a)):
            out.append(a[ia])
            ia += 1
        else:
            out.append(b[ib])
            ib += 1
    return out


def _attn_ffn_kernel(lam_init, sub, *refs):
    (qa_ref, qb_ref, ka_ref, va_ref, kb_ref, vb_ref, lq1, lk1, lq2, lk2, subln_ref,
     x_ref, mod_ref, wo_ref, nf_ref, wgu_ref, wd_ref, o_ref, k_all, vt_all, mix) = refs
    seq = ka_ref.shape[0] // sub
    lam = _lambda(lq1, lk1, lq2, lk2, lam_init)
    subln = subln_ref[...]
    attn, ffn = [], []
    for j in range(sub):
        r = slice(j * seq, (j + 1) * seq)

        later = []

        def store(cols, tile, r=r):
            mix[r, cols] = tile.astype(mix.dtype)

        def stage(j=j, r=r, later=later, store=store):
            _stage_kv(k_all.at[j], vt_all.at[j], ka_ref[r, :], va_ref[r, :], kb_ref[r, :], vb_ref[r, :], None)
            later.extend(_attention_units(_attention_tasks(
                qa_ref[r, :], qb_ref[r, :], k_all.at[j], vt_all.at[j], lam, subln, lam_init, store)))

        attn.append([stage] + [lambda k=k, later=later: later[k]() for k in range(LOGITS_AHEAD + N_TASKS)])
        ffn.append(_ffn_units(r, x_ref, mix, mod_ref, wo_ref, nf_ref, wgu_ref, wd_ref, o_ref))
    units = attn[0]
    for j in range(1, sub):
        units = units + _interleave(attn[j], ffn[j - 1])
    for unit in units + ffn[sub - 1]:
        unit()


def _attn_ffn(layer, x, qa, qb, own, seq, lam_vecs, subln, mod4, w_out, norm_ffn, w_gu, w_down):
    n = x.shape[0]
    sub = 4
    tm = sub * seq
    lam_init = 0.8 - 0.6 * math.exp(-0.3 * layer)
    tok = lambda w: pl.BlockSpec((tm, w), lambda i: (i, 0))
    kv_w = own[0].shape[1] + own[2].shape[1]
    in_specs = ([tok(512), tok(512)] + [tok(a.shape[1]) for a in own]
                + [_layer_spec((1, HEAD_DIM), layer)] * 4 + [_layer_spec((2 * HEAD_DIM, 1), layer)]
                + [tok(D_MODEL), _mod_spec(layer, False, 1),
                   _const_spec((D_MODEL, D_MODEL)), _layer_spec((1, D_MODEL), layer),
                   _const_spec((D_MODEL, 2 * FFN_HIDDEN)), _const_spec((FFN_HIDDEN, D_MODEL))])
    return pl.pallas_call(
        functools.partial(_attn_ffn_kernel, lam_init, sub),
        grid=(n // tm,),
        in_specs=in_specs,
        out_specs=tok(D_MODEL),
        out_shape=jax.ShapeDtypeStruct((n, D_MODEL), F32),
        scratch_shapes=[pltpu.VMEM((sub, seq, kv_w), BF16), pltpu.VMEM((sub, kv_w, seq), BF16),
                        pltpu.VMEM((tm, D_MODEL), BF16)],
        compiler_params=pltpu.CompilerParams(
            dimension_semantics=("arbitrary",), vmem_limit_bytes=VMEM_LIMIT),
        name="attn_ffn_context",
    )(qa, qb, *own, *lam_vecs, subln, x, mod4, w_out, norm_ffn, w_gu, w_down)


def _attn_ffn_lagged_kernel(lam_init, tiles_per_request, n_tiles, *refs):
    (qa_ref, qb_ref, ka_ref, va_ref, kb_ref, vb_ref, cka_ref, cva_ref, ckb_ref, cvb_ref,
     lq1, lk1, lq2, lk2, subln_ref, x_ref, mod_ref, wo_ref, nf_ref, wgu_ref, wd_ref,
     o_ref, k_all, vt_all, mix, redo) = refs
    s = pl.program_id(0)
    tq = qa_ref.shape[0]
    lam = _lambda(lq1, lk1, lq2, lk2, lam_init)
    subln = subln_ref[...]

    def attention_units(slot, exact=False):
        def store(cols, tile):
            mix[slot, :, cols] = tile.astype(mix.dtype)

        denoms = None if exact else []
        units = _attention_units(_attention_tasks(
            qa_ref[...], qb_ref[...], k_all, vt_all, lam, subln, lam_init, store, denoms),
            LOGITS_AHEAD if exact else GUARDED_AHEAD)
        if exact:
            return units

        def check():
            redo[0] = _denominators_bad(denoms)

        return units + [check]

    def ffn_units(slot):
        return _ffn_units(slice(0, tq), x_ref, mix.at[slot], mod_ref, wo_ref, nf_ref, wgu_ref, wd_ref, o_ref)

    @pl.when(jnp.logical_and(s % tiles_per_request == 0, s < n_tiles))
    def _():
        _stage_kv(k_all, vt_all, ka_ref[...], va_ref[...], kb_ref[...], vb_ref[...],
                  (cka_ref, cva_ref, ckb_ref, cvb_ref))

    @pl.when(s == 0)
    def _():
        for unit in attention_units(0):
            unit()

    @pl.when(jnp.logical_and(s > 0, s < n_tiles))
    def _():
        for unit in _interleave(attention_units(s % 2), ffn_units((s - 1) % 2)):
            unit()

    @pl.when(jnp.logical_and(s < n_tiles, redo[0] != 0))
    def _():
        for unit in attention_units(s % 2, exact=True):
            unit()

    @pl.when(s == n_tiles)
    def _():
        for unit in ffn_units((n_tiles - 1) % 2):
            unit()


def _attn_ffn_lagged(layer, x, qa, qb, own, cache, seq, lam_vecs, subln, mod4, w_out, norm_ffn, w_gu, w_down):
    n = x.shape[0]
    tq = Q_TILE
    tpr = seq // tq
    n_tiles = n // tq
    lam_init = 0.8 - 0.6 * math.exp(-0.3 * layer)
    att = lambda s: jnp.minimum(s, n_tiles - 1)
    ffn = lambda s: jnp.maximum(s - 1, 0)
    once = pl.Buffered(1)
    in_specs = [pl.BlockSpec((tq, 512), lambda s: (att(s), 0))] * 2
    in_specs += [pl.BlockSpec((seq, a.shape[1]), lambda s: (att(s) // tpr, 0), pipeline_mode=once) for a in own]
    for a in cache:
        nd = a.ndim - 2
        in_specs.append(pl.BlockSpec((None, None) + a.shape[2:],
                                     lambda s, nd=nd: (att(s) // tpr, layer) + (0,) * nd, pipeline_mode=once))
    in_specs += [_layer_spec((1, HEAD_DIM), layer)] * 4 + [_layer_spec((2 * HEAD_DIM, 1), layer)]
    in_specs += [pl.BlockSpec((tq, D_MODEL), lambda s: (ffn(s), 0)),
                 pl.BlockSpec((None, None, 1, 6 * D_MODEL), lambda s: (layer, 1 + ffn(s) // tpr, 0, 0)),
                 _const_spec((D_MODEL, D_MODEL)), _layer_spec((1, D_MODEL), layer),
                 _const_spec((D_MODEL, 2 * FFN_HIDDEN)), _const_spec((FFN_HIDDEN, D_MODEL))]
    keys = seq + cache[0].shape[-1]
    kv_w = own[0].shape[1] + own[2].shape[1]
    return pl.pallas_call(
        functools.partial(_attn_ffn_lagged_kernel, lam_init, tpr, n_tiles),
        grid=(n_tiles + 1,),
        in_specs=in_specs,
        out_specs=pl.BlockSpec((tq, D_MODEL), lambda s: (ffn(s), 0)),
        out_shape=jax.ShapeDtypeStruct((n, D_MODEL), F32),
        scratch_shapes=[pltpu.VMEM((keys, kv_w), BF16), pltpu.VMEM((kv_w, keys), BF16),
                        pltpu.VMEM((2, tq, D_MODEL), BF16), pltpu.SMEM((1,), jnp.int32)],
        compiler_params=pltpu.CompilerParams(
            dimension_semantics=("arbitrary",), vmem_limit_bytes=VMEM_LIMIT),
        name="attn_ffn_latent",
    )(qa, qb, *own, *cache, *lam_vecs, subln, x, mod4, w_out, norm_ffn, w_gu, w_down)


def _rope_tables(n_tokens):
    t = jnp.arange(n_tokens, dtype=jnp.int32)
    row = (t // GRID_W).astype(F32)
    col = (t % GRID_W).astype(F32)
    axis_dim = HEAD_DIM // 2
    freqs = ROPE_THETA ** (-jnp.arange(0, axis_dim, 2, dtype=F32) / axis_dim)
    ang = jnp.concatenate([row[:, None] * freqs, col[:, None] * freqs], axis=-1)
    c, s = jnp.cos(ang), jnp.sin(ang)
    return jnp.concatenate([c, c, c, c], axis=-1), jnp.concatenate([-s, s, -s, s], axis=-1)


def kernel(x_prompt, x_sample, cache_diff_k, cache_diff_v, cache_gqa_k, cache_gqa_v, c, c_ctx, w_mod, b_mod, norm_attn, w_in, q_norm_a, k_norm_a, lambda_q1, lambda_k1, lambda_q2, lambda_k2, subln, q_norm_b, k_norm_b, w_out, norm_ffn, w_gate_up, w_down):
    batch, seq, d = x_prompt.shape
    dec_batch, dec_seq, _ = x_sample.shape
    depth = w_mod.shape[0]
    past = cache_diff_k.shape[2]

    cond = jnp.concatenate([c_ctx[None], c, jnp.zeros((MOD_ROWS - 1 - dec_batch, d), F32)], axis=0)
    mod = _modulation(cond, w_mod, b_mod)
    mod4 = mod.reshape(depth, MOD_ROWS, 1, 6 * d)

    w_in_b = w_in.astype(BF16)
    seg_ones = jnp.kron(jnp.eye(256 // HEAD_DIM, dtype=F32), jnp.ones((HEAD_DIM, HEAD_DIM), F32)).astype(BF16)
    rope = _rope_tables(dec_seq)

    cache = (jnp.transpose(cache_diff_k, (0, 1, 3, 4, 5, 2)).reshape(dec_batch, depth, DIFF_HEADS, 2 * HEAD_DIM, past),
             cache_diff_v.reshape(dec_batch, depth, past * DIFF_HEADS, 2 * HEAD_DIM),
             jnp.transpose(cache_gqa_k, (0, 1, 3, 4, 2)),
             jnp.transpose(cache_gqa_v, (0, 1, 3, 4, 2)))

    row = lambda a: a.reshape(depth, 1, a.shape[-1])
    gains = (row(q_norm_a), row(k_norm_a), row(q_norm_b), row(k_norm_b))
    lam_vecs = (row(lambda_q1), row(lambda_k1), row(lambda_q2), row(lambda_k2))
    na, nf, sl = row(norm_attn), row(norm_ffn), subln.reshape(depth, 2 * HEAD_DIM, 1)

    yp = x_prompt.reshape(batch * seq, d)
    ys = x_sample.reshape(dec_batch * dec_seq, d)
    ctx_tiles = seq // TOKEN_TILE
    lat_tiles = dec_seq // TOKEN_TILE
    new_caches = None
    for l in range(depth):
        qa, ka, va, qb, kb, vb, *rest = _qkv(
            yp, mod4, l, False, ctx_tiles, na, w_in_b, seg_ones, gains, None, new_caches,
            (w_out, w_gate_up, w_down))
        new_caches, (w_out_b, w_gu_b, w_down_b) = rest[:4], rest[4:]
        yp = _attn_ffn(l, yp, qa, qb, (ka, va, kb, vb), seq, lam_vecs, sl, mod4, w_out_b, nf, w_gu_b, w_down_b)

        qa, ka, va, qb, kb, vb = _qkv(
            ys, mod4, l, True, lat_tiles, na, w_in_b, seg_ones, gains, rope)
        ys = _attn_ffn_lagged(l, ys, qa, qb, (ka, va, kb, vb), cache, dec_seq, lam_vecs, sl,
                              mod4, w_out_b, nf, w_gu_b, w_down_b)

    ka_t, va_n, kb_t, vb_t = new_caches
    new_diff_k = jnp.transpose(ka_t.reshape(batch, depth, DIFF_HEADS, 2, HEAD_DIM, seq), (0, 1, 5, 2, 3, 4))
    new_gqa_k = jnp.transpose(kb_t.reshape(batch, depth, GQA_KV_HEADS, HEAD_DIM, seq), (0, 1, 4, 2, 3))
    new_gqa_v = jnp.transpose(vb_t.reshape(batch, depth, GQA_KV_HEADS, HEAD_DIM, seq), (0, 1, 4, 2, 3))
    return (yp.reshape(batch, seq, d), ys.reshape(dec_batch, dec_seq, d),
            new_diff_k, va_n.reshape(batch, depth, seq, DIFF_HEADS, 2 * HEAD_DIM), new_gqa_k, new_gqa_v)
```

```python
import functools
import math

import jax
import jax.numpy as jnp
from jax import lax
from jax.experimental import pallas as pl
from jax.experimental.pallas import tpu as pltpu

F32 = jnp.float32
BF16 = jnp.bfloat16

D_MODEL = 1024
HEAD_DIM = 64
GRID_W = 64
DIFF_HEADS = 4
GQA_KV_HEADS = 2
GQA_REP = 4
FFN_HIDDEN = 2816
IN_COLS = 2304
ROPE_THETA = 10000.0
EPS = 1e-6
LANES = 128
MOD_ROWS = 8
TOKEN_TILE = 256
Q_TILE = 256
VMEM_LIMIT = 56 * 1024 * 1024


def _dot(a, b):
    return jnp.dot(a, b, preferred_element_type=F32)


def _dot_nt(a, b):
    return lax.dot_general(a, b, (((1,), (1,)), ((), ())), preferred_element_type=F32)


def _rms(x, gain):
    ms = jnp.mean(x * x, axis=-1, keepdims=True)
    return x * lax.rsqrt(ms + EPS) * gain


def _layer_spec(shape, layer):
    n = len(shape)
    return pl.BlockSpec((None,) + tuple(shape), lambda *_: (layer,) + (0,) * n,
                        pipeline_mode=pl.Buffered(1))


def _const_spec(shape):
    n = len(shape)
    return pl.BlockSpec(shape, lambda *_: (0,) * n, pipeline_mode=pl.Buffered(1))


def _mod_spec(layer, latent, tiles_per_request):
    if latent:
        return pl.BlockSpec((None, None, 1, 6 * D_MODEL),
                            lambda i: (layer, 1 + i // tiles_per_request, 0, 0))
    return pl.BlockSpec((None, None, 1, 6 * D_MODEL), lambda i: (layer, 0, 0, 0))


def _mod_kernel(cond_ref, w_ref, b_ref, o_ref):
    c = cond_ref[...]
    s = c * jax.nn.sigmoid(c)
    o_ref[...] = _dot(s.astype(BF16), w_ref[...].astype(BF16)) + b_ref[...]


def _modulation(cond, w_mod, b_mod):
    depth = w_mod.shape[0]
    tn = 1536
    return pl.pallas_call(
        _mod_kernel,
        grid=(depth, 6 * D_MODEL // tn),
        in_specs=[
            pl.BlockSpec((MOD_ROWS, D_MODEL), lambda l, j: (0, 0)),
            pl.BlockSpec((None, D_MODEL, tn), lambda l, j: (l, 0, j)),
            pl.BlockSpec((None, 1, tn), lambda l, j: (l, 0, j)),
        ],
        out_specs=pl.BlockSpec((None, MOD_ROWS, tn), lambda l, j: (l, 0, j)),
        out_shape=jax.ShapeDtypeStruct((depth, MOD_ROWS, 6 * D_MODEL), F32),
        compiler_params=pltpu.CompilerParams(
            dimension_semantics=("arbitrary", "arbitrary"), vmem_limit_bytes=VMEM_LIMIT),
        name="modulation",
    )(cond, w_mod, b_mod.reshape(depth, 1, 6 * D_MODEL))


def _head_norm(x, seg_ones, gain):
    ss = _dot((x * x).astype(BF16), seg_ones)
    return x * lax.rsqrt(ss * (1.0 / HEAD_DIM) + EPS) * gain


def _swap_halves(x, first_half):
    return jnp.where(first_half, pltpu.roll(x, 96, 1), pltpu.roll(x, 32, 1))


def _rope(x, cos, sin, first_half):
    out = []
    for j in range(x.shape[1] // LANES):
        xj = x[:, j * LANES:(j + 1) * LANES]
        out.append(xj * cos + _swap_halves(xj, first_half) * sin)
    return out[0] if len(out) == 1 else jnp.concatenate(out, axis=1)


def _tile_gain(g):
    return jnp.concatenate([g] * (256 // HEAD_DIM), axis=1)


def _qkv_stages(x_ref, mod_ref, na_ref, w_ref, seg_ref, gain_refs, rope_refs):
    sh = mod_ref[:, 0:D_MODEL]
    sc = mod_ref[:, D_MODEL:2 * D_MODEL]
    seg = seg_ref[...]
    lane = lax.broadcasted_iota(jnp.int32, (1, LANES), 1)
    first_half = (lane % HEAD_DIM) < (HEAD_DIM // 2)
    scale = HEAD_DIM ** -0.5 * math.log2(math.e)

    def project(r):
        h = _rms(x_ref[r, :], na_ref[...]) * (1.0 + sc) + sh
        return _dot(h.astype(BF16), w_ref[...])

    def heads(r, proj):
        def qk(lo_col, width, gain_ref):
            gain = _tile_gain(gain_ref[...])
            out = []
            for c0 in range(0, width, 256):
                w = min(256, width - c0)
                out.append(_head_norm(proj[:, lo_col + c0:lo_col + c0 + w], seg[:w, :w], gain[:, :w]))
            y = out[0] if len(out) == 1 else jnp.concatenate(out, axis=1)
            if rope_refs is not None:
                y = _rope(y, rope_refs[0][r, :], rope_refs[1][r, :], first_half)
            return y

        gqa_ref, gka_ref, gqb_ref, gkb_ref = gain_refs
        return (qk(0, 512, gqa_ref) * scale, qk(512, 512, gka_ref), proj[:, 1024:1536],
                qk(1536, 512, gqb_ref) * scale, qk(2048, 128, gkb_ref), proj[:, 2176:2304])

    return project, heads


def _store_caches(j, ka, va, kb, vb, kaf_o, vaf_o, kbf_o, vbf_o):
    kaf_o[j] = ka.T
    kbf_o[j] = kb.T
    vbf_o[j] = vb.T
    for h in range(DIFF_HEADS):
        vaf_o[j, pl.ds(h, va.shape[0], stride=DIFF_HEADS), :] = va[:, h * LANES:(h + 1) * LANES]


def _qkv_kernel(x_ref, mod_ref, na_ref, w_ref, seg_ref, gqa_ref, gka_ref, gqb_ref, gkb_ref,
                cos_ref, sin_ref, *outs):
    tm = TOKEN_TILE
    project, heads = _qkv_stages(x_ref, mod_ref, na_ref, w_ref, seg_ref,
                                 (gqa_ref, gka_ref, gqb_ref, gkb_ref), (cos_ref, sin_ref))

    def finish(r, proj):
        for o, val in zip(outs, heads(r, proj)):
            o[r, :] = val.astype(BF16)

    rows = [slice(j * tm, (j + 1) * tm) for j in range(x_ref.shape[0] // tm)]
    pending = [project(rows[0])]
    for j, r in enumerate(rows):
        if j + 1 < len(rows):
            pending.append(project(rows[j + 1]))
        finish(r, pending.pop(0))


def _qkv_in_specs(layer):
    return ([_layer_spec((1, D_MODEL), layer), _layer_spec((D_MODEL, IN_COLS), layer), _const_spec((256, 256))]
            + [_layer_spec((1, HEAD_DIM), layer)] * 4)


def _qkv(x, mod4, layer, tiles_per_request, norm_attn, w_in, seg_ones, gains, rope):
    n = x.shape[0]
    sub = 4
    tm = sub * TOKEN_TILE
    tok = lambda w: pl.BlockSpec((tm, w), lambda i: (i, 0))
    rope_spec = pl.BlockSpec((tm, LANES), lambda i: (i % (tiles_per_request // sub), 0))
    return pl.pallas_call(
        _qkv_kernel,
        grid=(n // tm,),
        in_specs=([tok(D_MODEL), _mod_spec(layer, True, tiles_per_request // sub)] + _qkv_in_specs(layer)
                  + [rope_spec, rope_spec]),
        out_specs=[tok(512), tok(512), tok(512), tok(512), tok(LANES), tok(LANES)],
        out_shape=[jax.ShapeDtypeStruct((n, w), BF16) for w in (512, 512, 512, 512, LANES, LANES)],
        compiler_params=pltpu.CompilerParams(
            dimension_semantics=("arbitrary",), vmem_limit_bytes=VMEM_LIMIT),
        name="qkv_latent",
    )(x, mod4, norm_attn, w_in, seg_ones, *gains, *rope)


def _pad_rows(x, block):
    z = jnp.zeros_like(x)
    return jnp.concatenate([x, z] if block == 0 else [z, x], axis=0)


ONES_ROWS = 16


SHIFT_KEYS = 32
DENOM_RANGE = (2.0 ** -100, 2.0 ** 100)


def _numerators(st, exact):
    shift = (st if exact else st[:SHIFT_KEYS]).max(axis=0, keepdims=True)
    return jnp.exp2(st - shift).astype(BF16)


def _pv(vt, e):
    d = vt.shape[0]
    ext = _dot(jnp.concatenate([vt, jnp.ones((ONES_ROWS, vt.shape[1]), BF16)], axis=0), e)
    return ext[:d], ext[d:d + 1]


def _denominators_bad(denoms):
    bad = None
    for l in denoms:
        b = jnp.where(jnp.logical_and(l > DENOM_RANGE[0], l < DENOM_RANGE[1]), 0.0, 1.0)
        bad = b if bad is None else jnp.maximum(bad, b)
    return (jnp.max(bad) > 0.0).astype(jnp.int32)


KA_W = DIFF_HEADS * LANES


def _stage_kv(k_all, vt_all, ka, va, kb, vb, cache_refs):
    seq = ka.shape[0]
    k_all[0:seq, 0:KA_W] = ka
    k_all[0:seq, KA_W:] = kb
    vt_all[0:KA_W, 0:seq] = va.astype(F32).T.astype(BF16)
    vt_all[KA_W:, 0:seq] = vb.astype(F32).T.astype(BF16)
    if cache_refs is not None:
        cka_ref, cva_ref, ckb_ref, cvb_ref = cache_refs
        past = cka_ref.shape[2]
        for h in range(DIFF_HEADS):
            cs = slice(h * LANES, (h + 1) * LANES)
            k_all[seq:, cs] = cka_ref[h].T.astype(BF16)
            vt_all[cs, seq:] = cva_ref[pl.ds(h, past, stride=DIFF_HEADS), :].T.astype(BF16)
        k_all[seq:, KA_W:] = jnp.concatenate([ckb_ref[0], ckb_ref[1]], axis=0).T.astype(BF16)
        for g in range(GQA_KV_HEADS):
            vt_all[KA_W + g * HEAD_DIM:KA_W + (g + 1) * HEAD_DIM, seq:] = cvb_ref[g].astype(BF16)


def _lambda(lq1, lk1, lq2, lk2, lam_init):
    return (jnp.exp(jnp.sum(lq1[...] * lk1[...], axis=-1, keepdims=True))
            - jnp.exp(jnp.sum(lq2[...] * lk2[...], axis=-1, keepdims=True)) + lam_init)


def _attention_tasks(qa, qb, k_all, vt_all, lam, subln, lam_init, store, denoms=None):
    tq = qa.shape[0]
    qat = qa.astype(F32).T.astype(BF16)
    qbt = qb.astype(F32).T.astype(BF16)
    exact = denoms is None
    tasks = []

    def first(st):
        return st if exact else _numerators(st, False)

    def pv(vt, s):
        num, l = _pv(vt, _numerators(s, True) if exact else s)
        if not exact:
            denoms.append(l)
        return num, 1.0 / l

    def diff_finish(s, h, cs):
        num, r = pv(vt_all[cs, :], s)
        ot = num[:, :tq] * r[:, :tq] - num[:, tq:] * (r[:, tq:] * lam)
        ms = jnp.mean(ot * ot, axis=0, keepdims=True)
        ot = ot * lax.rsqrt(ms + EPS) * subln * (1.0 - lam_init)
        store(cs, ot.T)

    def gqa_finish(s, g, os):
        num, r = pv(vt_all[KA_W + g * HEAD_DIM:KA_W + (g + 1) * HEAD_DIM, :], s)
        ot = num * r
        store(os, jnp.concatenate([ot[:, :tq], ot[:, tq:]], axis=0).T)

    for h in range(DIFF_HEADS):
        cs = slice(h * LANES, (h + 1) * LANES)

        def logits(h=h, cs=cs):
            c0 = qat[h * LANES:h * LANES + HEAD_DIM]
            c1 = qat[h * LANES + HEAD_DIM:(h + 1) * LANES]
            return first(_dot(k_all[:, cs], jnp.concatenate([_pad_rows(c0, 0), _pad_rows(c1, 1)], axis=1)))

        tasks.append((logits, lambda s, h=h, cs=cs: diff_finish(s, h, cs)))
    for g in range(GQA_KV_HEADS):
        for rp in range(2):
            r0 = (GQA_REP * g + 2 * rp) * HEAD_DIM
            os = slice(KA_W + r0, KA_W + r0 + LANES)

            def logits(g=g, r0=r0):
                a = qbt[r0:r0 + HEAD_DIM]
                b = qbt[r0 + HEAD_DIM:r0 + LANES]
                return first(_dot(k_all[:, KA_W:], jnp.concatenate([_pad_rows(a, g), _pad_rows(b, g)], axis=1)))

            tasks.append((logits, lambda s, g=g, os=os: gqa_finish(s, g, os)))
    return tasks


LOGITS_AHEAD = 2
GUARDED_AHEAD = 3
N_TASKS = DIFF_HEADS + 2 * GQA_KV_HEADS


def _attention_units(tasks, ahead=LOGITS_AHEAD):
    pending = []

    def prefill(t):
        pending.append(t[0]())

    def step(i):
        if i + ahead < len(tasks):
            pending.append(tasks[i + ahead][0]())
        tasks[i][1](pending.pop(0))

    units = [functools.partial(prefill, t) for t in tasks[:ahead]]
    return units + [functools.partial(step, i) for i in range(len(tasks))]


def _ffn_units(r, x_ref, mix_ref, mod_ref, wo_ref, nf_ref, wgu_ref, wd_ref, o_ref):
    g_a = mod_ref[:, 2 * D_MODEL:3 * D_MODEL]
    sh_f = mod_ref[:, 3 * D_MODEL:4 * D_MODEL]
    sc_f = mod_ref[:, 4 * D_MODEL:5 * D_MODEL]
    g_f = mod_ref[:, 5 * D_MODEL:6 * D_MODEL]
    state = {}

    def attn_residual():
        state["x"] = x_ref[r, :] + g_a * _dot(mix_ref[r, :], wo_ref[...])

    def gate():
        state["h"] = (_rms(state["x"], nf_ref[...]) * (1.0 + sc_f) + sh_f).astype(BF16)
        state["gate"] = _dot(state["h"], wgu_ref[:, :FFN_HIDDEN])

    def up():
        g = state.pop("gate")
        state["act"] = (g * jax.nn.sigmoid(g) * _dot(state.pop("h"), wgu_ref[:, FFN_HIDDEN:])).astype(BF16)

    def ffn_residual():
        o_ref[r, :] = state.pop("x") + g_f * _dot(state.pop("act"), wd_ref[...])

    return [attn_residual, gate, up, ffn_residual]


def _interleave(a, b):
    out, ia, ib = [], 0, 0
    while ia < len(a) or ib < len(b):
        if ib >= len(b) or (ia < len(a) and ia * len(b) <= ib * len(a)):
            out.append(a[ia])
            ia += 1
        else:
            out.append(b[ib])
            ib += 1
    return out


def _context_layer_kernel(lam_init, sub, *refs):
    (x_ref, mod_ref, na_ref, w_ref, seg_ref, gqa_ref, gka_ref, gqb_ref, gkb_ref,
     lq1, lk1, lq2, lk2, subln_ref, wo_ref, nf_ref, wgu_ref, wd_ref) = refs[:18]
    o_ref, kaf_o, vaf_o, kbf_o, vbf_o, k_all, vt_all, mix = refs[-8:]
    seq = x_ref.shape[0] // sub
    lam = _lambda(lq1, lk1, lq2, lk2, lam_init)
    subln = subln_ref[...]
    project, heads = _qkv_stages(x_ref, mod_ref, na_ref, w_ref, seg_ref,
                                 (gqa_ref, gka_ref, gqb_ref, gkb_ref), None)
    rows = [slice(j * seq, (j + 1) * seq) for j in range(sub)]
    proj = {}
    later = [[] for _ in range(sub)]

    def project_stage(j):
        proj[j] = project(rows[j])

    def qk_stage(j):
        qa, ka, va, qb, kb, vb = heads(rows[j], proj.pop(j))
        _store_caches(j, ka, va, kb, vb, kaf_o, vaf_o, kbf_o, vbf_o)
        _stage_kv(k_all.at[j], vt_all.at[j], ka.astype(BF16), va, kb.astype(BF16), vb, None)

        def store(cols, tile):
            mix[rows[j], cols] = tile.astype(mix.dtype)

        later[j].extend(_attention_units(_attention_tasks(
            qa, qb, k_all.at[j], vt_all.at[j], lam, subln, lam_init, store)))

    P = lambda j: functools.partial(project_stage, j)
    Q = lambda j: functools.partial(qk_stage, j)
    A = lambda j: [lambda k=k: later[j][k]() for k in range(LOGITS_AHEAD + N_TASKS)]
    F = lambda j: _ffn_units(rows[j], x_ref, mix, mod_ref, wo_ref, nf_ref, wgu_ref, wd_ref, o_ref)

    units = [P(0)] + ([P(1)] if sub > 1 else []) + [Q(0)]
    for j in range(sub):
        side = ([P(j + 2)] if j + 2 < sub else []) + ([Q(j + 1)] if j + 1 < sub else [])
        units += _interleave(A(j), side + (F(j - 1) if j >= 1 else []))
    for unit in units + F(sub - 1):
        unit()


def _context_layer(layer, x, seq, mod4, norm_attn, w_in, seg_ones, gains, lam_vecs, subln,
                   w_out, norm_ffn, w_gu, w_down, prev_caches):
    n = x.shape[0]
    assert seq == TOKEN_TILE, "context requests must be one token tile long"
    sub = 2
    tm = sub * seq
    req = n // seq
    depth = w_in.shape[0]
    lam_init = 0.8 - 0.6 * math.exp(-0.3 * layer)
    tok = lambda w: pl.BlockSpec((tm, w), lambda i: (i, 0))
    slab = lambda *dims: pl.BlockSpec((sub, None) + dims, lambda i: (i, layer) + (0,) * len(dims))
    in_specs = ([tok(D_MODEL), _mod_spec(layer, False, 1)] + _qkv_in_specs(layer)
                + [_layer_spec((1, HEAD_DIM), layer)] * 4 + [_layer_spec((2 * HEAD_DIM, 1), layer)]
                + [_layer_spec((D_MODEL, D_MODEL), layer), _layer_spec((1, D_MODEL), layer),
                   _layer_spec((D_MODEL, 2 * FFN_HIDDEN), layer), _layer_spec((FFN_HIDDEN, D_MODEL), layer)])
    args = [x, mod4, norm_attn, w_in, seg_ones, *gains, *lam_vecs, subln, w_out, norm_ffn, w_gu, w_down]
    aliases = {}
    if prev_caches is not None:
        aliases = {len(args) + k: 1 + k for k in range(4)}
        in_specs += [pl.BlockSpec(memory_space=pl.ANY)] * 4
        args += list(prev_caches)
    kv_w = KA_W + LANES
    return pl.pallas_call(
        functools.partial(_context_layer_kernel, lam_init, sub),
        grid=(n // tm,),
        in_specs=in_specs,
        out_specs=[tok(D_MODEL), slab(512, seq), slab(seq * DIFF_HEADS, LANES), slab(LANES, seq), slab(LANES, seq)],
        out_shape=[jax.ShapeDtypeStruct((n, D_MODEL), F32),
                   jax.ShapeDtypeStruct((req, depth, 512, seq), F32),
                   jax.ShapeDtypeStruct((req, depth, seq * DIFF_HEADS, LANES), F32),
                   jax.ShapeDtypeStruct((req, depth, LANES, seq), F32),
                   jax.ShapeDtypeStruct((req, depth, LANES, seq), F32)],
        input_output_aliases=aliases,
        scratch_shapes=[pltpu.VMEM((sub, seq, kv_w), BF16), pltpu.VMEM((sub, kv_w, seq), BF16),
                        pltpu.VMEM((tm, D_MODEL), BF16)],
        compiler_params=pltpu.CompilerParams(
            dimension_semantics=("arbitrary",), vmem_limit_bytes=VMEM_LIMIT),
        name="context_layer",
    )(*args)


def _attn_ffn_lagged_kernel(lam_init, tiles_per_request, n_tiles, *refs):
    (qa_ref, qb_ref, ka_ref, va_ref, kb_ref, vb_ref, cka_ref, cva_ref, ckb_ref, cvb_ref,
     lq1, lk1, lq2, lk2, subln_ref, x_ref, mod_ref, wo_ref, nf_ref, wgu_ref, wd_ref,
     o_ref, k_all, vt_all, mix, redo) = refs
    s = pl.program_id(0)
    tq = qa_ref.shape[0]
    lam = _lambda(lq1, lk1, lq2, lk2, lam_init)
    subln = subln_ref[...]

    def attention_units(slot, exact=False):
        def store(cols, tile):
            mix[slot, :, cols] = tile.astype(mix.dtype)

        denoms = None if exact else []
        units = _attention_units(_attention_tasks(
            qa_ref[...], qb_ref[...], k_all, vt_all, lam, subln, lam_init, store, denoms),
            LOGITS_AHEAD if exact else GUARDED_AHEAD)
        if exact:
            return units

        def check():
            redo[0] = _denominators_bad(denoms)

        return units + [check]

    def ffn_units(slot):
        return _ffn_units(slice(0, tq), x_ref, mix.at[slot], mod_ref, wo_ref, nf_ref, wgu_ref, wd_ref, o_ref)

    @pl.when(jnp.logical_and(s % tiles_per_request == 0, s < n_tiles))
    def _():
        _stage_kv(k_all, vt_all, ka_ref[...], va_ref[...], kb_ref[...], vb_ref[...],
                  (cka_ref, cva_ref, ckb_ref, cvb_ref))

    @pl.when(s == 0)
    def _():
        for unit in attention_units(0):
            unit()

    @pl.when(jnp.logical_and(s > 0, s < n_tiles))
    def _():
        for unit in _interleave(attention_units(s % 2), ffn_units((s - 1) % 2)):
            unit()

    @pl.when(jnp.logical_and(s < n_tiles, redo[0] != 0))
    def _():
        for unit in attention_units(s % 2, exact=True):
            unit()

    @pl.when(s == n_tiles)
    def _():
        for unit in ffn_units((n_tiles - 1) % 2):
            unit()


def _attn_ffn_lagged(layer, x, qa, qb, own, cache, seq, lam_vecs, subln, mod4, w_out, norm_ffn, w_gu, w_down):
    n = x.shape[0]
    tq = Q_TILE
    tpr = seq // tq
    n_tiles = n // tq
    lam_init = 0.8 - 0.6 * math.exp(-0.3 * layer)
    att = lambda s: jnp.minimum(s, n_tiles - 1)
    ffn = lambda s: jnp.maximum(s - 1, 0)
    once = pl.Buffered(1)
    in_specs = [pl.BlockSpec((tq, 512), lambda s: (att(s), 0))] * 2
    in_specs += [pl.BlockSpec((seq, a.shape[1]), lambda s: (att(s) // tpr, 0), pipeline_mode=once) for a in own]
    for a in cache:
        nd = a.ndim - 2
        in_specs.append(pl.BlockSpec((None, None) + a.shape[2:],
                                     lambda s, nd=nd: (att(s) // tpr, layer) + (0,) * nd, pipeline_mode=once))
    in_specs += [_layer_spec((1, HEAD_DIM), layer)] * 4 + [_layer_spec((2 * HEAD_DIM, 1), layer)]
    in_specs += [pl.BlockSpec((tq, D_MODEL), lambda s: (ffn(s), 0)),
                 pl.BlockSpec((None, None, 1, 6 * D_MODEL), lambda s: (layer, 1 + ffn(s) // tpr, 0, 0)),
                 _layer_spec((D_MODEL, D_MODEL), layer), _layer_spec((1, D_MODEL), layer),
                 _layer_spec((D_MODEL, 2 * FFN_HIDDEN), layer), _layer_spec((FFN_HIDDEN, D_MODEL), layer)]
    keys = seq + cache[0].shape[-1]
    kv_w = own[0].shape[1] + own[2].shape[1]
    return pl.pallas_call(
        functools.partial(_attn_ffn_lagged_kernel, lam_init, tpr, n_tiles),
        grid=(n_tiles + 1,),
        in_specs=in_specs,
        out_specs=pl.BlockSpec((tq, D_MODEL), lambda s: (ffn(s), 0)),
        out_shape=jax.ShapeDtypeStruct((n, D_MODEL), F32),
        scratch_shapes=[pltpu.VMEM((keys, kv_w), BF16), pltpu.VMEM((kv_w, keys), BF16),
                        pltpu.VMEM((2, tq, D_MODEL), BF16), pltpu.SMEM((1,), jnp.int32)],
        compiler_params=pltpu.CompilerParams(
            dimension_semantics=("arbitrary",), vmem_limit_bytes=VMEM_LIMIT),
        name="attn_ffn_latent",
    )(qa, qb, *own, *cache, *lam_vecs, subln, x, mod4, w_out, norm_ffn, w_gu, w_down)


def _rope_tables(n_tokens):
    t = jnp.arange(n_tokens, dtype=jnp.int32)
    row = (t // GRID_W).astype(F32)
    col = (t % GRID_W).astype(F32)
    axis_dim = HEAD_DIM // 2
    freqs = ROPE_THETA ** (-jnp.arange(0, axis_dim, 2, dtype=F32) / axis_dim)
    ang = jnp.concatenate([row[:, None] * freqs, col[:, None] * freqs], axis=-1)
    c, s = jnp.cos(ang), jnp.sin(ang)
    return jnp.concatenate([c, c, c, c], axis=-1), jnp.concatenate([-s, s, -s, s], axis=-1)


def kernel(x_prompt, x_sample, cache_diff_k, cache_diff_v, cache_gqa_k, cache_gqa_v, c, c_ctx, w_mod, b_mod, norm_attn, w_in, q_norm_a, k_norm_a, lambda_q1, lambda_k1, lambda_q2, lambda_k2, subln, q_norm_b, k_norm_b, w_out, norm_ffn, w_gate_up, w_down):
    batch, seq, d = x_prompt.shape
    dec_batch, dec_seq, _ = x_sample.shape
    depth = w_mod.shape[0]
    past = cache_diff_k.shape[2]

    cond = jnp.concatenate([c_ctx[None], c, jnp.zeros((MOD_ROWS - 1 - dec_batch, d), F32)], axis=0)
    mod = _modulation(cond, w_mod, b_mod)
    mod4 = mod.reshape(depth, MOD_ROWS, 1, 6 * d)

    w_in_b = w_in.astype(BF16)
    w_out_b = w_out.astype(BF16)
    w_gu_b = w_gate_up.astype(BF16)
    w_down_b = w_down.astype(BF16)
    seg_ones = jnp.kron(jnp.eye(256 // HEAD_DIM, dtype=F32), jnp.ones((HEAD_DIM, HEAD_DIM), F32)).astype(BF16)
    rope = _rope_tables(dec_seq)

    cache = (jnp.transpose(cache_diff_k, (0, 1, 3, 4, 5, 2)).reshape(dec_batch, depth, DIFF_HEADS, 2 * HEAD_DIM, past),
             cache_diff_v.reshape(dec_batch, depth, past * DIFF_HEADS, 2 * HEAD_DIM),
             jnp.transpose(cache_gqa_k, (0, 1, 3, 4, 2)),
             jnp.transpose(cache_gqa_v, (0, 1, 3, 4, 2)))

    row = lambda a: a.reshape(depth, 1, a.shape[-1])
    gains = (row(q_norm_a), row(k_norm_a), row(q_norm_b), row(k_norm_b))
    lam_vecs = (row(lambda_q1), row(lambda_k1), row(lambda_q2), row(lambda_k2))
    na, nf, sl = row(norm_attn), row(norm_ffn), subln.reshape(depth, 2 * HEAD_DIM, 1)

    yp = x_prompt.reshape(batch * seq, d)
    ys = x_sample.reshape(dec_batch * dec_seq, d)
    ctx_tiles = seq // TOKEN_TILE
    lat_tiles = dec_seq // TOKEN_TILE
    new_caches = None
    for l in range(depth):
        yp, *new_caches = _context_layer(l, yp, seq, mod4, na, w_in_b, seg_ones, gains, lam_vecs, sl,
                                         w_out_b, nf, w_gu_b, w_down_b, new_caches)

        qa, ka, va, qb, kb, vb = _qkv(ys, mod4, l, lat_tiles, na, w_in_b, seg_ones, gains, rope)
        ys = _attn_ffn_lagged(l, ys, qa, qb, (ka, va, kb, vb), cache, dec_seq, lam_vecs, sl,
                              mod4, w_out_b, nf, w_gu_b, w_down_b)

    ka_t, va_n, kb_t, vb_t = new_caches
    new_diff_k = jnp.transpose(ka_t.reshape(batch, depth, DIFF_HEADS, 2, HEAD_DIM, seq), (0, 1, 5, 2, 3, 4))
    new_gqa_k = jnp.transpose(kb_t.reshape(batch, depth, GQA_KV_HEADS, HEAD_DIM, seq), (0, 1, 4, 2, 3))
    new_gqa_v = jnp.transpose(vb_t.reshape(batch, depth, GQA_KV_HEADS, HEAD_DIM, seq), (0, 1, 4, 2, 3))
    return (yp.reshape(batch, seq, d), ys.reshape(dec_batch, dec_seq, d),
            new_diff_k, va_n.reshape(batch, depth, seq, DIFF_HEADS, 2 * HEAD_DIM), new_gqa_k, new_gqa_v)
```

```python
import functools
import math

import jax
import jax.numpy as jnp
from jax import lax
from jax.experimental import pallas as pl
from jax.experimental.pallas import tpu as pltpu

F32 = jnp.float32
BF16 = jnp.bfloat16

D_MODEL = 1024
HEAD_DIM = 64
GRID_W = 64
DIFF_HEADS = 4
GQA_KV_HEADS = 2
GQA_REP = 4
FFN_HIDDEN = 2816
IN_COLS = 2304
ROPE_THETA = 10000.0
EPS = 1e-6
LANES = 128
MOD_ROWS = 8
TOKEN_TILE = 256
Q_TILE = 256
VMEM_LIMIT = 56 * 1024 * 1024


def _dot(a, b):
    return jnp.dot(a, b, preferred_element_type=F32)


def _rms(x, gain):
    ms = jnp.mean(x * x, axis=-1, keepdims=True)
    return x * lax.rsqrt(ms + EPS) * gain


def _layer_spec(shape, layer):
    n = len(shape)
    return pl.BlockSpec((None,) + tuple(shape), lambda *_: (layer,) + (0,) * n,
                        pipeline_mode=pl.Buffered(1))


def _const_spec(shape):
    n = len(shape)
    return pl.BlockSpec(shape, lambda *_: (0,) * n, pipeline_mode=pl.Buffered(1))


def _mod_spec(layer, latent, tiles_per_request):
    if latent:
        return pl.BlockSpec((None, None, 1, 6 * D_MODEL),
                            lambda i: (layer, 1 + i // tiles_per_request, 0, 0))
    return pl.BlockSpec((None, None, 1, 6 * D_MODEL), lambda i: (layer, 0, 0, 0))


def _mod_kernel(cond_ref, w_ref, b_ref, o_ref):
    c = cond_ref[...]
    s = c * jax.nn.sigmoid(c)
    o_ref[...] = _dot(s.astype(BF16), w_ref[...].astype(BF16)) + b_ref[...]


def _modulation(cond, w_mod, b_mod):
    depth = w_mod.shape[0]
    tn = 1536
    return pl.pallas_call(
        _mod_kernel,
        grid=(depth, 6 * D_MODEL // tn),
        in_specs=[
            pl.BlockSpec((MOD_ROWS, D_MODEL), lambda l, j: (0, 0)),
            pl.BlockSpec((None, D_MODEL, tn), lambda l, j: (l, 0, j)),
            pl.BlockSpec((None, 1, tn), lambda l, j: (l, 0, j)),
        ],
        out_specs=pl.BlockSpec((None, MOD_ROWS, tn), lambda l, j: (l, 0, j)),
        out_shape=jax.ShapeDtypeStruct((depth, MOD_ROWS, 6 * D_MODEL), F32),
        compiler_params=pltpu.CompilerParams(
            dimension_semantics=("arbitrary", "arbitrary"), vmem_limit_bytes=VMEM_LIMIT),
        name="modulation",
    )(cond, w_mod, b_mod.reshape(depth, 1, 6 * D_MODEL))


def _head_norm(x, seg_ones, gain):
    ss = _dot((x * x).astype(BF16), seg_ones)
    return x * lax.rsqrt(ss * (1.0 / HEAD_DIM) + EPS) * gain


def _swap_halves(x, first_half):
    return jnp.where(first_half, pltpu.roll(x, 96, 1), pltpu.roll(x, 32, 1))


def _rope(x, cos, sin, first_half):
    out = []
    for j in range(x.shape[1] // LANES):
        xj = x[:, j * LANES:(j + 1) * LANES]
        out.append(xj * cos + _swap_halves(xj, first_half) * sin)
    return out[0] if len(out) == 1 else jnp.concatenate(out, axis=1)


def _tile_gain(g):
    return jnp.concatenate([g] * (256 // HEAD_DIM), axis=1)


def _qkv_kernel(latent, n_cast, *refs):
    if latent:
        (x_ref, mod_ref, na_ref, w_ref, seg_ref, gqa_ref, gka_ref, gqb_ref, gkb_ref,
         cos_ref, sin_ref, qa_o, ka_o, va_o, qb_o, kb_o, vb_o) = refs
    else:
        (x_ref, mod_ref, na_ref, w_ref, seg_ref, gqa_ref, gka_ref, gqb_ref, gkb_ref) = refs[:9]
        outs = refs[len(refs) - 10 - n_cast:]
        qa_o, ka_o, va_o, qb_o, kb_o, vb_o, kaf_o, vaf_o, kbf_o, vbf_o = outs[:10]
        for src, dst in zip(refs[9:9 + n_cast], outs[10:]):
            dst[...] = src[...].astype(BF16)
    tm = TOKEN_TILE
    sh = mod_ref[:, 0:D_MODEL]
    sc = mod_ref[:, D_MODEL:2 * D_MODEL]
    seg = seg_ref[...]
    lane = lax.broadcasted_iota(jnp.int32, (1, LANES), 1)
    first_half = (lane % HEAD_DIM) < (HEAD_DIM // 2)
    scale = HEAD_DIM ** -0.5 * math.log2(math.e)

    def project(r):
        h = _rms(x_ref[r, :], na_ref[...]) * (1.0 + sc) + sh
        return _dot(h.astype(BF16), w_ref[...])

    def finish(j, r, proj):
        def qk(lo_col, width, gain_ref):
            gain = _tile_gain(gain_ref[...])
            out = []
            for c0 in range(0, width, 256):
                w = min(256, width - c0)
                out.append(_head_norm(proj[:, lo_col + c0:lo_col + c0 + w], seg[:w, :w], gain[:, :w]))
            y = out[0] if len(out) == 1 else jnp.concatenate(out, axis=1)
            if latent:
                y = _rope(y, cos_ref[r, :], sin_ref[r, :], first_half)
            return y

        qa = qk(0, 512, gqa_ref)
        ka = qk(512, 512, gka_ref)
        va = proj[:, 1024:1536]
        qb = qk(1536, 512, gqb_ref)
        kb = qk(2048, 128, gkb_ref)
        vb = proj[:, 2176:2304]
        qa_o[r, :] = (qa * scale).astype(BF16)
        ka_o[r, :] = ka.astype(BF16)
        va_o[r, :] = va.astype(BF16)
        qb_o[r, :] = (qb * scale).astype(BF16)
        kb_o[r, :] = kb.astype(BF16)
        vb_o[r, :] = vb.astype(BF16)
        if not latent:
            kaf_o[j] = ka.T
            kbf_o[j] = kb.T
            vbf_o[j] = vb.T
            for h in range(DIFF_HEADS):
                vaf_o[j, pl.ds(h, tm, stride=DIFF_HEADS), :] = va[:, h * LANES:(h + 1) * LANES]

    rows = [slice(j * tm, (j + 1) * tm) for j in range(x_ref.shape[0] // tm)]
    pending = [project(rows[0])]
    for j, r in enumerate(rows):
        if j + 1 < len(rows):
            pending.append(project(rows[j + 1]))
        finish(j, r, pending.pop(0))


def _qkv(x, mod4, layer, latent, tiles_per_request, norm_attn, w_in, seg_ones, gains, rope,
         prev_caches=None, cast_weights=()):
    n = x.shape[0]
    sub = 4
    tm = sub * TOKEN_TILE
    depth = w_in.shape[0]
    tok = lambda w: pl.BlockSpec((tm, w), lambda i: (i, 0))
    in_specs = [
        tok(D_MODEL),
        _mod_spec(layer, latent, max(tiles_per_request // sub, 1)),
        _layer_spec((1, D_MODEL), layer),
        _layer_spec((D_MODEL, IN_COLS), layer),
        _const_spec((256, 256)),
    ] + [_layer_spec((1, HEAD_DIM), layer)] * 4
    args = [x, mod4, norm_attn, w_in, seg_ones, *gains]
    aliases = {}
    out_specs = [tok(512), tok(512), tok(512), tok(512), tok(LANES), tok(LANES)]
    out_shape = [jax.ShapeDtypeStruct((n, w), BF16) for w in (512, 512, 512, 512, LANES, LANES)]
    if latent:
        rope_spec = pl.BlockSpec((tm, LANES), lambda i: (i % (tiles_per_request // sub), 0))
        in_specs += [rope_spec, rope_spec]
        args += list(rope)
    else:
        assert tiles_per_request == 1, "context requests must be one token tile long"
        seq = TOKEN_TILE
        req = n // seq
        slab = lambda *dims: pl.BlockSpec((sub, None) + dims, lambda i: (i, layer) + (0,) * len(dims))
        out_specs += [slab(512, seq), slab(seq * DIFF_HEADS, LANES), slab(LANES, seq), slab(LANES, seq)]
        out_shape += [jax.ShapeDtypeStruct((req, depth, 512, seq), F32),
                      jax.ShapeDtypeStruct((req, depth, seq * DIFF_HEADS, LANES), F32),
                      jax.ShapeDtypeStruct((req, depth, LANES, seq), F32),
                      jax.ShapeDtypeStruct((req, depth, LANES, seq), F32)]
        steps = n // tm
        for w in cast_weights:
            rows, cols = w.shape[1] // steps, w.shape[2]
            assert rows * steps == w.shape[1] and rows % 16 == 0
            in_specs.append(pl.BlockSpec((None, rows, cols), lambda i: (layer, i, 0)))
            args.append(w)
            out_specs.append(pl.BlockSpec((rows, cols), lambda i: (i, 0)))
            out_shape.append(jax.ShapeDtypeStruct(w.shape[1:], BF16))
        if prev_caches is not None:
            aliases = {len(args) + k: 6 + k for k in range(4)}
            in_specs += [pl.BlockSpec(memory_space=pl.ANY)] * 4
            args += list(prev_caches)
    return pl.pallas_call(
        functools.partial(_qkv_kernel, latent, len(cast_weights)),
        grid=(n // tm,),
        in_specs=in_specs,
        out_specs=out_specs,
        out_shape=out_shape,
        input_output_aliases=aliases,
        compiler_params=pltpu.CompilerParams(
            dimension_semantics=("arbitrary",), vmem_limit_bytes=VMEM_LIMIT),
        name="qkv_latent" if latent else "qkv_context",
    )(*args)


def _pad_rows(x, block):
    z = jnp.zeros_like(x)
    return jnp.concatenate([x, z] if block == 0 else [z, x], axis=0)


ONES_ROWS = 16
SHIFT_KEYS = 8
DENOM_RANGE = (2.0 ** -100, 2.0 ** 100)


def _numerators(st, exact):
    shift = (st if exact else st[:SHIFT_KEYS]).max(axis=0, keepdims=True)
    return jnp.exp2(st - shift).astype(BF16)


def _pv(vt, e):
    d = vt.shape[0]
    ext = _dot(jnp.concatenate([vt, jnp.ones((ONES_ROWS, vt.shape[1]), BF16)], axis=0), e)
    return ext[:d], ext[d:d + 1]


def _denominators_bad(denoms):
    bad = None
    for l in denoms:
        b = jnp.where(jnp.logical_and(l > DENOM_RANGE[0], l < DENOM_RANGE[1]), 0.0, 1.0)
        bad = b if bad is None else jnp.maximum(bad, b)
    return (jnp.max(bad) > 0.0).astype(jnp.int32)


KA_W = DIFF_HEADS * LANES


def _stage_kv(k_all, vt_all, ka, va, kb, vb, cache_refs):
    seq = ka.shape[0]
    k_all[0:seq, 0:KA_W] = ka
    k_all[0:seq, KA_W:] = kb
    vt_all[0:KA_W, 0:seq] = va.astype(F32).T.astype(BF16)
    vt_all[KA_W:, 0:seq] = vb.astype(F32).T.astype(BF16)
    if cache_refs is not None:
        cka_ref, cva_ref, ckb_ref, cvb_ref = cache_refs
        past = cka_ref.shape[2]
        for h in range(DIFF_HEADS):
            cs = slice(h * LANES, (h + 1) * LANES)
            k_all[seq:, cs] = cka_ref[h].T.astype(BF16)
            vt_all[cs, seq:] = cva_ref[pl.ds(h, past, stride=DIFF_HEADS), :].T.astype(BF16)
        k_all[seq:, KA_W:] = jnp.concatenate([ckb_ref[0], ckb_ref[1]], axis=0).T.astype(BF16)
        for g in range(GQA_KV_HEADS):
            vt_all[KA_W + g * HEAD_DIM:KA_W + (g + 1) * HEAD_DIM, seq:] = cvb_ref[g].astype(BF16)


def _lambda(lq1, lk1, lq2, lk2, lam_init):
    return (jnp.exp(jnp.sum(lq1[...] * lk1[...], axis=-1, keepdims=True))
            - jnp.exp(jnp.sum(lq2[...] * lk2[...], axis=-1, keepdims=True)) + lam_init)


def _attention_tasks(qa, qb, k_all, vt_all, lam, subln, lam_init, store, denoms=None):
    tq = qa.shape[0]
    qat = qa.astype(F32).T.astype(BF16)
    qbt = qb.astype(F32).T.astype(BF16)
    exact = denoms is None
    tasks = []

    def first(st):
        return _numerators(st, exact)

    def pv(vt, e):
        num, l = _pv(vt, e)
        if not exact:
            denoms.append(l)
        return num, 1.0 / l

    def diff_finish(e, h, cs):
        num, r = pv(vt_all[cs, :], e)
        ot = num[:, :tq] * r[:, :tq] - num[:, tq:] * (r[:, tq:] * lam)
        ms = jnp.mean(ot * ot, axis=0, keepdims=True)
        ot = ot * lax.rsqrt(ms + EPS) * subln * (1.0 - lam_init)
        store(cs, ot.T)

    def gqa_finish(e, g, os):
        num, r = pv(vt_all[KA_W + g * HEAD_DIM:KA_W + (g + 1) * HEAD_DIM, :], e)
        ot = num * r
        store(os, jnp.concatenate([ot[:, :tq], ot[:, tq:]], axis=0).T)

    for h in range(DIFF_HEADS):
        cs = slice(h * LANES, (h + 1) * LANES)

        def logits(h=h, cs=cs):
            c0 = qat[h * LANES:h * LANES + HEAD_DIM]
            c1 = qat[h * LANES + HEAD_DIM:(h + 1) * LANES]
            return first(_dot(k_all[:, cs], jnp.concatenate([_pad_rows(c0, 0), _pad_rows(c1, 1)], axis=1)))

        tasks.append((logits, lambda e, h=h, cs=cs: diff_finish(e, h, cs)))
    for g in range(GQA_KV_HEADS):
        for rp in range(2):
            r0 = (GQA_REP * g + 2 * rp) * HEAD_DIM
            os = slice(KA_W + r0, KA_W + r0 + LANES)

            def logits(g=g, r0=r0):
                a = qbt[r0:r0 + HEAD_DIM]
                b = qbt[r0 + HEAD_DIM:r0 + LANES]
                return first(_dot(k_all[:, KA_W:], jnp.concatenate([_pad_rows(a, g), _pad_rows(b, g)], axis=1)))

            tasks.append((logits, lambda e, g=g, os=os: gqa_finish(e, g, os)))
    return tasks


LOGITS_AHEAD = 2
GUARDED_AHEAD = 3
N_TASKS = DIFF_HEADS + 2 * GQA_KV_HEADS


def _attention_units(tasks, ahead=LOGITS_AHEAD):
    pending = []

    def prefill(t):
        pending.append(t[0]())

    def step(i):
        if i + ahead < len(tasks):
            pending.append(tasks[i + ahead][0]())
        tasks[i][1](pending.pop(0))

    units = [functools.partial(prefill, t) for t in tasks[:ahead]]
    return units + [functools.partial(step, i) for i in range(len(tasks))]


def _ffn_units(r, x_ref, mix_ref, mod_ref, wo_ref, nf_ref, wgu_ref, wd_ref, o_ref):
    g_a = mod_ref[:, 2 * D_MODEL:3 * D_MODEL]
    sh_f = mod_ref[:, 3 * D_MODEL:4 * D_MODEL]
    sc_f = mod_ref[:, 4 * D_MODEL:5 * D_MODEL]
    g_f = mod_ref[:, 5 * D_MODEL:6 * D_MODEL]
    state = {}

    def attn_residual():
        state["x"] = x_ref[r, :] + g_a * _dot(mix_ref[r, :], wo_ref[...])

    def gate():
        state["h"] = (_rms(state["x"], nf_ref[...]) * (1.0 + sc_f) + sh_f).astype(BF16)
        state["gate"] = _dot(state["h"], wgu_ref[:, :FFN_HIDDEN])

    def up():
        g = state.pop("gate")
        state["act"] = (g * jax.nn.sigmoid(g) * _dot(state.pop("h"), wgu_ref[:, FFN_HIDDEN:])).astype(BF16)

    def ffn_residual():
        o_ref[r, :] = state.pop("x") + g_f * _dot(state.pop("act"), wd_ref[...])

    return [attn_residual, gate, up, ffn_residual]


def _interleave(a, b):
    out, ia, ib = [], 0, 0
    while ia < len(a) or ib < len(b):
        if ib >= len(b) or (ia < len(a) and ia * len(b) <= ib * len(a)):
            out.append(a[ia])
            ia += 1
        else:
            out.append(b[ib])
            ib += 1
    return out


def _attn_ffn_kernel(lam_init, sub, *refs):
    (qa_ref, qb_ref, ka_ref, va_ref, kb_ref, vb_ref, lq1, lk1, lq2, lk2, subln_ref,
     x_ref, mod_ref, wo_ref, nf_ref, wgu_ref, wd_ref, o_ref, k_all, vt_all, mix) = refs
    seq = ka_ref.shape[0] // sub
    lam = _lambda(lq1, lk1, lq2, lk2, lam_init)
    subln = subln_ref[...]
    attn, ffn = [], []
    for j in range(sub):
        r = slice(j * seq, (j + 1) * seq)

        later = []

        def store(cols, tile, r=r):
            mix[r, cols] = tile.astype(mix.dtype)

        def stage(j=j, r=r, later=later, store=store):
            _stage_kv(k_all.at[j], vt_all.at[j], ka_ref[r, :], va_ref[r, :], kb_ref[r, :], vb_ref[r, :], None)
            later.extend(_attention_units(_attention_tasks(
                qa_ref[r, :], qb_ref[r, :], k_all.at[j], vt_all.at[j], lam, subln, lam_init, store)))

        attn.append([stage] + [lambda k=k, later=later: later[k]() for k in range(LOGITS_AHEAD + N_TASKS)])
        ffn.append(_ffn_units(r, x_ref, mix, mod_ref, wo_ref, nf_ref, wgu_ref, wd_ref, o_ref))
    units = attn[0]
    for j in range(1, sub):
        units = units + _interleave(attn[j], ffn[j - 1])
    for unit in units + ffn[sub - 1]:
        unit()


def _attn_ffn(layer, x, qa, qb, own, seq, lam_vecs, subln, mod4, w_out, norm_ffn, w_gu, w_down):
    n = x.shape[0]
    sub = 4
    tm = sub * seq
    lam_init = 0.8 - 0.6 * math.exp(-0.3 * layer)
    tok = lambda w: pl.BlockSpec((tm, w), lambda i: (i, 0))
    kv_w = own[0].shape[1] + own[2].shape[1]
    in_specs = ([tok(512), tok(512)] + [tok(a.shape[1]) for a in own]
                + [_layer_spec((1, HEAD_DIM), layer)] * 4 + [_layer_spec((2 * HEAD_DIM, 1), layer)]
                + [tok(D_MODEL), _mod_spec(layer, False, 1),
                   _const_spec((D_MODEL, D_MODEL)), _layer_spec((1, D_MODEL), layer),
                   _const_spec((D_MODEL, 2 * FFN_HIDDEN)), _const_spec((FFN_HIDDEN, D_MODEL))])
    return pl.pallas_call(
        functools.partial(_attn_ffn_kernel, lam_init, sub),
        grid=(n // tm,),
        in_specs=in_specs,
        out_specs=tok(D_MODEL),
        out_shape=jax.ShapeDtypeStruct((n, D_MODEL), F32),
        scratch_shapes=[pltpu.VMEM((sub, seq, kv_w), BF16), pltpu.VMEM((sub, kv_w, seq), BF16),
                        pltpu.VMEM((tm, D_MODEL), BF16)],
        compiler_params=pltpu.CompilerParams(
            dimension_semantics=("arbitrary",), vmem_limit_bytes=VMEM_LIMIT),
        name="attn_ffn_context",
    )(qa, qb, *own, *lam_vecs, subln, x, mod4, w_out, norm_ffn, w_gu, w_down)


def _attn_ffn_lagged_kernel(lam_init, tiles_per_request, n_tiles, *refs):
    (qa_ref, qb_ref, ka_ref, va_ref, kb_ref, vb_ref, cka_ref, cva_ref, ckb_ref, cvb_ref,
     lq1, lk1, lq2, lk2, subln_ref, x_ref, mod_ref, wo_ref, nf_ref, wgu_ref, wd_ref,
     o_ref, k_all, vt_all, mix, redo) = refs
    s = pl.program_id(0)
    tq = qa_ref.shape[0]
    lam = _lambda(lq1, lk1, lq2, lk2, lam_init)
    subln = subln_ref[...]

    def attention_units(slot, exact=False):
        def store(cols, tile):
            mix[slot, :, cols] = tile.astype(mix.dtype)

        denoms = None if exact else []
        units = _attention_units(_attention_tasks(
            qa_ref[...], qb_ref[...], k_all, vt_all, lam, subln, lam_init, store, denoms),
            LOGITS_AHEAD if exact else GUARDED_AHEAD)
        if exact:
            return units

        def check():
            redo[0] = _denominators_bad(denoms)

        return units + [check]

    def ffn_units(slot):
        return _ffn_units(slice(0, tq), x_ref, mix.at[slot], mod_ref, wo_ref, nf_ref, wgu_ref, wd_ref, o_ref)

    @pl.when(jnp.logical_and(s % tiles_per_request == 0, s < n_tiles))
    def _():
        _stage_kv(k_all, vt_all, ka_ref[...], va_ref[...], kb_ref[...], vb_ref[...],
                  (cka_ref, cva_ref, ckb_ref, cvb_ref))

    @pl.when(s == 0)
    def _():
        for unit in attention_units(0):
            unit()

    @pl.when(jnp.logical_and(s > 0, s < n_tiles))
    def _():
        for unit in _interleave(attention_units(s % 2), ffn_units((s - 1) % 2)):
            unit()

    @pl.when(jnp.logical_and(s < n_tiles, redo[0] != 0))
    def _():
        for unit in attention_units(s % 2, exact=True):
            unit()

    @pl.when(s == n_tiles)
    def _():
        for unit in ffn_units((n_tiles - 1) % 2):
            unit()


def _attn_ffn_lagged(layer, x, qa, qb, own, cache, seq, lam_vecs, subln, mod4, w_out, norm_ffn, w_gu, w_down):
    n = x.shape[0]
    tq = Q_TILE
    tpr = seq // tq
    n_tiles = n // tq
    lam_init = 0.8 - 0.6 * math.exp(-0.3 * layer)
    att = lambda s: jnp.minimum(s, n_tiles - 1)
    ffn = lambda s: jnp.maximum(s - 1, 0)
    once = pl.Buffered(1)
    in_specs = [pl.BlockSpec((tq, 512), lambda s: (att(s), 0))] * 2
    in_specs += [pl.BlockSpec((seq, a.shape[1]), lambda s: (att(s) // tpr, 0), pipeline_mode=once) for a in own]
    for a in cache:
        nd = a.ndim - 2
        in_specs.append(pl.BlockSpec((None, None) + a.shape[2:],
                                     lambda s, nd=nd: (att(s) // tpr, layer) + (0,) * nd, pipeline_mode=once))
    in_specs += [_layer_spec((1, HEAD_DIM), layer)] * 4 + [_layer_spec((2 * HEAD_DIM, 1), layer)]
    in_specs += [pl.BlockSpec((tq, D_MODEL), lambda s: (ffn(s), 0)),
                 pl.BlockSpec((None, None, 1, 6 * D_MODEL), lambda s: (layer, 1 + ffn(s) // tpr, 0, 0)),
                 _const_spec((D_MODEL, D_MODEL)), _layer_spec((1, D_MODEL), layer),
                 _const_spec((D_MODEL, 2 * FFN_HIDDEN)), _const_spec((FFN_HIDDEN, D_MODEL))]
    keys = seq + cache[0].shape[-1]
    kv_w = own[0].shape[1] + own[2].shape[1]
    return pl.pallas_call(
        functools.partial(_attn_ffn_lagged_kernel, lam_init, tpr, n_tiles),
        grid=(n_tiles + 1,),
        in_specs=in_specs,
        out_specs=pl.BlockSpec((tq, D_MODEL), lambda s: (ffn(s), 0)),
        out_shape=jax.ShapeDtypeStruct((n, D_MODEL), F32),
        scratch_shapes=[pltpu.VMEM((keys, kv_w), BF16), pltpu.VMEM((kv_w, keys), BF16),
                        pltpu.VMEM((2, tq, D_MODEL), BF16), pltpu.SMEM((1,), jnp.int32)],
        compiler_params=pltpu.CompilerParams(
            dimension_semantics=("arbitrary",), vmem_limit_bytes=VMEM_LIMIT),
        name="attn_ffn_latent",
    )(qa, qb, *own, *cache, *lam_vecs, subln, x, mod4, w_out, norm_ffn, w_gu, w_down)


def _rope_tables(n_tokens):
    t = jnp.arange(n_tokens, dtype=jnp.int32)
    row = (t // GRID_W).astype(F32)
    col = (t % GRID_W).astype(F32)
    axis_dim = HEAD_DIM // 2
    freqs = ROPE_THETA ** (-jnp.arange(0, axis_dim, 2, dtype=F32) / axis_dim)
    ang = jnp.concatenate([row[:, None] * freqs, col[:, None] * freqs], axis=-1)
    c, s = jnp.cos(ang), jnp.sin(ang)
    return jnp.concatenate([c, c, c, c], axis=-1), jnp.concatenate([-s, s, -s, s], axis=-1)


def kernel(x_prompt, x_sample, cache_diff_k, cache_diff_v, cache_gqa_k, cache_gqa_v, c, c_ctx, w_mod, b_mod, norm_attn, w_in, q_norm_a, k_norm_a, lambda_q1, lambda_k1, lambda_q2, lambda_k2, subln, q_norm_b, k_norm_b, w_out, norm_ffn, w_gate_up, w_down):
    batch, seq, d = x_prompt.shape
    dec_batch, dec_seq, _ = x_sample.shape
    depth = w_mod.shape[0]
    past = cache_diff_k.shape[2]

    cond = jnp.concatenate([c_ctx[None], c, jnp.zeros((MOD_ROWS - 1 - dec_batch, d), F32)], axis=0)
    mod = _modulation(cond, w_mod, b_mod)
    mod4 = mod.reshape(depth, MOD_ROWS, 1, 6 * d)

    w_in_b = w_in.astype(BF16)
    seg_ones = jnp.kron(jnp.eye(256 // HEAD_DIM, dtype=F32), jnp.ones((HEAD_DIM, HEAD_DIM), F32)).astype(BF16)
    rope = _rope_tables(dec_seq)

    cache = (jnp.transpose(cache_diff_k, (0, 1, 3, 4, 5, 2)).reshape(dec_batch, depth, DIFF_HEADS, 2 * HEAD_DIM, past),
             cache_diff_v.reshape(dec_batch, depth, past * DIFF_HEADS, 2 * HEAD_DIM),
             jnp.transpose(cache_gqa_k, (0, 1, 3, 4, 2)),
             jnp.transpose(cache_gqa_v, (0, 1, 3, 4, 2)))

    row = lambda a: a.reshape(depth, 1, a.shape[-1])
    gains = (row(q_norm_a), row(k_norm_a), row(q_norm_b), row(k_norm_b))
    lam_vecs = (row(lambda_q1), row(lambda_k1), row(lambda_q2), row(lambda_k2))
    na, nf, sl = row(norm_attn), row(norm_ffn), subln.reshape(depth, 2 * HEAD_DIM, 1)

    yp = x_prompt.reshape(batch * seq, d)
    ys = x_sample.reshape(dec_batch * dec_seq, d)
    lat_tiles = dec_seq // TOKEN_TILE
    new_caches = None
    for l in range(depth):
        qa, ka, va, qb, kb, vb, *rest = _qkv(
            yp, mod4, l, False, seq // TOKEN_TILE, na, w_in_b, seg_ones, gains, None, new_caches,
            (w_out, w_gate_up, w_down))
        new_caches, (w_out_b, w_gu_b, w_down_b) = rest[:4], rest[4:]
        yp = _attn_ffn(l, yp, qa, qb, (ka, va, kb, vb), seq, lam_vecs, sl, mod4, w_out_b, nf, w_gu_b, w_down_b)

        qa, ka, va, qb, kb, vb = _qkv(
            ys, mod4, l, True, lat_tiles, na, w_in_b, seg_ones, gains, rope)
        ys = _attn_ffn_lagged(l, ys, qa, qb, (ka, va, kb, vb), cache, dec_seq, lam_vecs, sl,
                              mod4, w_out_b, nf, w_gu_b, w_down_b)

    ka_t, va_n, kb_t, vb_t = new_caches
    new_diff_k = jnp.transpose(ka_t.reshape(batch, depth, DIFF_HEADS, 2, HEAD_DIM, seq), (0, 1, 5, 2, 3, 4))
    new_gqa_k = jnp.transpose(kb_t.reshape(batch, depth, GQA_KV_HEADS, HEAD_DIM, seq), (0, 1, 4, 2, 3))
    new_gqa_v = jnp.transpose(vb_t.reshape(batch, depth, GQA_KV_HEADS, HEAD_DIM, seq), (0, 1, 4, 2, 3))
    return (yp.reshape(batch, seq, d), ys.reshape(dec_batch, dec_seq, d),
            new_diff_k, va_n.reshape(batch, depth, seq, DIFF_HEADS, 2 * HEAD_DIM), new_gqa_k, new_gqa_v)
```

```python
import functools
import math

import jax
import jax.numpy as jnp
from jax import lax
from jax.experimental import pallas as pl
from jax.experimental.pallas import tpu as pltpu

F32 = jnp.float32
BF16 = jnp.bfloat16

D_MODEL = 1024
HEAD_DIM = 64
GRID_W = 64
DIFF_HEADS = 4
GQA_KV_HEADS = 2
GQA_REP = 4
FFN_HIDDEN = 2816
IN_COLS = 2304
ROPE_THETA = 10000.0
EPS = 1e-6
LANES = 128
MOD_ROWS = 8
TOKEN_TILE = 256
Q_TILE = 256
VMEM_LIMIT = 56 * 1024 * 1024


def _dot(a, b):
    return jnp.dot(a, b, preferred_element_type=F32)


def _rms(x, gain):
    ms = jnp.mean(x * x, axis=-1, keepdims=True)
    return x * lax.rsqrt(ms + EPS) * gain


def _layer_spec(shape, layer):
    n = len(shape)
    return pl.BlockSpec((None,) + tuple(shape), lambda *_: (layer,) + (0,) * n,
                        pipeline_mode=pl.Buffered(1))


def _const_spec(shape):
    n = len(shape)
    return pl.BlockSpec(shape, lambda *_: (0,) * n, pipeline_mode=pl.Buffered(1))


def _mod_spec(layer, latent, tiles_per_request):
    if latent:
        return pl.BlockSpec((None, None, 1, 6 * D_MODEL),
                            lambda i: (layer, 1 + i // tiles_per_request, 0, 0))
    return pl.BlockSpec((None, None, 1, 6 * D_MODEL), lambda i: (layer, 0, 0, 0))


def _mod_kernel(cond_ref, w_ref, b_ref, o_ref):
    c = cond_ref[...]
    s = c * jax.nn.sigmoid(c)
    o_ref[...] = _dot(s.astype(BF16), w_ref[...].astype(BF16)) + b_ref[...]


def _modulation(cond, w_mod, b_mod):
    depth = w_mod.shape[0]
    tn = 1536
    return pl.pallas_call(
        _mod_kernel,
        grid=(depth, 6 * D_MODEL // tn),
        in_specs=[
            pl.BlockSpec((MOD_ROWS, D_MODEL), lambda l, j: (0, 0)),
            pl.BlockSpec((None, D_MODEL, tn), lambda l, j: (l, 0, j)),
            pl.BlockSpec((None, 1, tn), lambda l, j: (l, 0, j)),
        ],
        out_specs=pl.BlockSpec((None, MOD_ROWS, tn), lambda l, j: (l, 0, j)),
        out_shape=jax.ShapeDtypeStruct((depth, MOD_ROWS, 6 * D_MODEL), F32),
        compiler_params=pltpu.CompilerParams(
            dimension_semantics=("arbitrary", "arbitrary"), vmem_limit_bytes=VMEM_LIMIT),
        name="modulation",
    )(cond, w_mod, b_mod.reshape(depth, 1, 6 * D_MODEL))


def _head_norm(x, seg_ones, gain):
    ss = _dot((x * x).astype(BF16), seg_ones)
    return x * lax.rsqrt(ss * (1.0 / HEAD_DIM) + EPS) * gain


def _swap_halves(x, first_half):
    return jnp.where(first_half, pltpu.roll(x, 96, 1), pltpu.roll(x, 32, 1))


def _rope(x, cos, sin, first_half):
    out = []
    for j in range(x.shape[1] // LANES):
        xj = x[:, j * LANES:(j + 1) * LANES]
        out.append(xj * cos + _swap_halves(xj, first_half) * sin)
    return out[0] if len(out) == 1 else jnp.concatenate(out, axis=1)


def _tile_gain(g):
    return jnp.concatenate([g] * (256 // HEAD_DIM), axis=1)


def _qkv_kernel(latent, n_cast, *refs):
    if latent:
        (x_ref, mod_ref, na_ref, w_ref, seg_ref, gqa_ref, gka_ref, gqb_ref, gkb_ref,
         cos_ref, sin_ref, qa_o, ka_o, va_o, qb_o, kb_o, vb_o) = refs
    else:
        (x_ref, mod_ref, na_ref, w_ref, seg_ref, gqa_ref, gka_ref, gqb_ref, gkb_ref) = refs[:9]
        outs = refs[len(refs) - 10 - n_cast:]
        qa_o, ka_o, va_o, qb_o, kb_o, vb_o, kaf_o, vaf_o, kbf_o, vbf_o = outs[:10]
        for src, dst in zip(refs[9:9 + n_cast], outs[10:]):
            dst[...] = src[...].astype(BF16)
    tm = TOKEN_TILE
    sh = mod_ref[:, 0:D_MODEL]
    sc = mod_ref[:, D_MODEL:2 * D_MODEL]
    seg = seg_ref[...]
    lane = lax.broadcasted_iota(jnp.int32, (1, LANES), 1)
    first_half = (lane % HEAD_DIM) < (HEAD_DIM // 2)
    scale = HEAD_DIM ** -0.5 * math.log2(math.e)

    def project(r):
        h = _rms(x_ref[r, :], na_ref[...]) * (1.0 + sc) + sh
        return _dot(h.astype(BF16), w_ref[...])

    def finish(j, r, proj):
        def qk(lo_col, width, gain_ref):
            gain = _tile_gain(gain_ref[...])
            out = []
            for c0 in range(0, width, 256):
                w = min(256, width - c0)
                out.append(_head_norm(proj[:, lo_col + c0:lo_col + c0 + w], seg[:w, :w], gain[:, :w]))
            y = out[0] if len(out) == 1 else jnp.concatenate(out, axis=1)
            if latent:
                y = _rope(y, cos_ref[r, :], sin_ref[r, :], first_half)
            return y

        qa = qk(0, 512, gqa_ref)
        ka = qk(512, 512, gka_ref)
        va = proj[:, 1024:1536]
        qb = qk(1536, 512, gqb_ref)
        kb = qk(2048, 128, gkb_ref)
        vb = proj[:, 2176:2304]
        qa_o[r, :] = (qa * scale).astype(BF16)
        ka_o[r, :] = ka.astype(BF16)
        va_o[r, :] = va.astype(BF16)
        qb_o[r, :] = (qb * scale).astype(BF16)
        kb_o[r, :] = kb.astype(BF16)
        vb_o[r, :] = vb.astype(BF16)
        if not latent:
            kaf_o[j] = ka.T
            kbf_o[j] = kb.T
            vbf_o[j] = vb.T
            for h in range(DIFF_HEADS):
                vaf_o[j, pl.ds(h, tm, stride=DIFF_HEADS), :] = va[:, h * LANES:(h + 1) * LANES]

    rows = [slice(j * tm, (j + 1) * tm) for j in range(x_ref.shape[0] // tm)]
    pending = [project(rows[0])]
    for j, r in enumerate(rows):
        if j + 1 < len(rows):
            pending.append(project(rows[j + 1]))
        finish(j, r, pending.pop(0))


def _qkv(x, mod4, layer, latent, tiles_per_request, norm_attn, w_in, seg_ones, gains, rope,
         prev_caches=None, cast_weights=()):
    n = x.shape[0]
    sub = 4
    tm = sub * TOKEN_TILE
    depth = w_in.shape[0]
    tok = lambda w: pl.BlockSpec((tm, w), lambda i: (i, 0))
    in_specs = [
        tok(D_MODEL),
        _mod_spec(layer, latent, max(tiles_per_request // sub, 1)),
        _layer_spec((1, D_MODEL), layer),
        _layer_spec((D_MODEL, IN_COLS), layer),
        _const_spec((256, 256)),
    ] + [_layer_spec((1, HEAD_DIM), layer)] * 4
    args = [x, mod4, norm_attn, w_in, seg_ones, *gains]
    aliases = {}
    out_specs = [tok(512), tok(512), tok(512), tok(512), tok(LANES), tok(LANES)]
    out_shape = [jax.ShapeDtypeStruct((n, w), BF16) for w in (512, 512, 512, 512, LANES, LANES)]
    if latent:
        rope_spec = pl.BlockSpec((tm, LANES), lambda i: (i % (tiles_per_request // sub), 0))
        in_specs += [rope_spec, rope_spec]
        args += list(rope)
    else:
        assert tiles_per_request == 1, "context requests must be one token tile long"
        seq = TOKEN_TILE
        req = n // seq
        slab = lambda *dims: pl.BlockSpec((sub, None) + dims, lambda i: (i, layer) + (0,) * len(dims))
        out_specs += [slab(512, seq), slab(seq * DIFF_HEADS, LANES), slab(LANES, seq), slab(LANES, seq)]
        out_shape += [jax.ShapeDtypeStruct((req, depth, 512, seq), F32),
                      jax.ShapeDtypeStruct((req, depth, seq * DIFF_HEADS, LANES), F32),
                      jax.ShapeDtypeStruct((req, depth, LANES, seq), F32),
                      jax.ShapeDtypeStruct((req, depth, LANES, seq), F32)]
        steps = n // tm
        for w in cast_weights:
            rows, cols = w.shape[1] // steps, w.shape[2]
            assert rows * steps == w.shape[1] and rows % 16 == 0
            in_specs.append(pl.BlockSpec((None, rows, cols), lambda i: (layer, i, 0)))
            args.append(w)
            out_specs.append(pl.BlockSpec((rows, cols), lambda i: (i, 0)))
            out_shape.append(jax.ShapeDtypeStruct(w.shape[1:], BF16))
        if prev_caches is not None:
            aliases = {len(args) + k: 6 + k for k in range(4)}
            in_specs += [pl.BlockSpec(memory_space=pl.ANY)] * 4
            args += list(prev_caches)
    return pl.pallas_call(
        functools.partial(_qkv_kernel, latent, len(cast_weights)),
        grid=(n // tm,),
        in_specs=in_specs,
        out_specs=out_specs,
        out_shape=out_shape,
        input_output_aliases=aliases,
        compiler_params=pltpu.CompilerParams(
            dimension_semantics=("arbitrary",), vmem_limit_bytes=VMEM_LIMIT),
        name="qkv_latent" if latent else "qkv_context",
    )(*args)


def _pad_rows(x, block):
    z = jnp.zeros_like(x)
    return jnp.concatenate([x, z] if block == 0 else [z, x], axis=0)


ONES_ROWS = 16
SHIFT_KEYS = 8
DENOM_RANGE = (2.0 ** -100, 2.0 ** 100)


def _numerators(st, exact):
    shift = (st if exact else st[:SHIFT_KEYS]).max(axis=0, keepdims=True)
    return jnp.exp2(st - shift).astype(BF16)


def _pv(vt, e):
    d = vt.shape[0]
    ext = _dot(jnp.concatenate([vt, jnp.ones((ONES_ROWS, vt.shape[1]), BF16)], axis=0), e)
    return ext[:d], ext[d:d + 1]


def _denominators_bad(denoms):
    bad = None
    for l in denoms:
        b = jnp.where(jnp.logical_and(l > DENOM_RANGE[0], l < DENOM_RANGE[1]), 0.0, 1.0)
        bad = b if bad is None else jnp.maximum(bad, b)
    return (jnp.max(bad) > 0.0).astype(jnp.int32)


KA_W = DIFF_HEADS * LANES


def _stage_kv(k_all, vt_all, ka, va, kb, vb, cache_refs):
    seq = ka.shape[0]
    k_all[0:seq, 0:KA_W] = ka
    k_all[0:seq, KA_W:] = kb
    vt_all[0:KA_W, 0:seq] = va.astype(F32).T.astype(BF16)
    vt_all[KA_W:, 0:seq] = vb.astype(F32).T.astype(BF16)
    if cache_refs is not None:
        cka_ref, cva_ref, ckb_ref, cvb_ref = cache_refs
        past = cka_ref.shape[2]
        for h in range(DIFF_HEADS):
            cs = slice(h * LANES, (h + 1) * LANES)
            k_all[seq:, cs] = cka_ref[h].T.astype(BF16)
            vt_all[cs, seq:] = cva_ref[pl.ds(h, past, stride=DIFF_HEADS), :].T.astype(BF16)
        k_all[seq:, KA_W:] = jnp.concatenate([ckb_ref[0], ckb_ref[1]], axis=0).T.astype(BF16)
        for g in range(GQA_KV_HEADS):
            vt_all[KA_W + g * HEAD_DIM:KA_W + (g + 1) * HEAD_DIM, seq:] = cvb_ref[g].astype(BF16)


def _lambda(lq1, lk1, lq2, lk2, lam_init):
    return (jnp.exp(jnp.sum(lq1[...] * lk1[...], axis=-1, keepdims=True))
            - jnp.exp(jnp.sum(lq2[...] * lk2[...], axis=-1, keepdims=True)) + lam_init)


def _attention_tasks(qa, qb, k_all, vt_all, lam, subln, lam_init, store, denoms=None):
    tq = qa.shape[0]
    qat = qa.astype(F32).T.astype(BF16)
    qbt = qb.astype(F32).T.astype(BF16)
    exact = denoms is None
    tasks = []

    def first(st):
        return _numerators(st, exact)

    def pv(vt, e):
        num, l = _pv(vt, e)
        if not exact:
            denoms.append(l)
        return num, 1.0 / l

    def diff_finish(e, h, cs):
        num, r = pv(vt_all[cs, :], e)
        ot = num[:, :tq] * r[:, :tq] - num[:, tq:] * (r[:, tq:] * lam)
        ms = jnp.mean(ot * ot, axis=0, keepdims=True)
        ot = ot * lax.rsqrt(ms + EPS) * subln * (1.0 - lam_init)
        store(cs, ot.T)

    def gqa_finish(e, g, os):
        num, r = pv(vt_all[KA_W + g * HEAD_DIM:KA_W + (g + 1) * HEAD_DIM, :], e)
        ot = num * r
        store(os, jnp.concatenate([ot[:, :tq], ot[:, tq:]], axis=0).T)

    for h in range(DIFF_HEADS):
        cs = slice(h * LANES, (h + 1) * LANES)

        def logits(h=h, cs=cs):
            c0 = qat[h * LANES:h * LANES + HEAD_DIM]
            c1 = qat[h * LANES + HEAD_DIM:(h + 1) * LANES]
            return first(_dot(k_all[:, cs], jnp.concatenate([_pad_rows(c0, 0), _pad_rows(c1, 1)], axis=1)))

        tasks.append((logits, lambda e, h=h, cs=cs: diff_finish(e, h, cs)))
    for g in range(GQA_KV_HEADS):
        for rp in range(2):
            r0 = (GQA_REP * g + 2 * rp) * HEAD_DIM
            os = slice(KA_W + r0, KA_W + r0 + LANES)

            def logits(g=g, r0=r0):
                a = qbt[r0:r0 + HEAD_DIM]
                b = qbt[r0 + HEAD_DIM:r0 + LANES]
                return first(_dot(k_all[:, KA_W:], jnp.concatenate([_pad_rows(a, g), _pad_rows(b, g)], axis=1)))

            tasks.append((logits, lambda e, g=g, os=os: gqa_finish(e, g, os)))
    return tasks


LOGITS_AHEAD = 2
GUARDED_AHEAD = 3
N_TASKS = DIFF_HEADS + 2 * GQA_KV_HEADS


def _attention_units(tasks, ahead=LOGITS_AHEAD):
    pending = []

    def prefill(t):
        pending.append(t[0]())

    def step(i):
        if i + ahead < len(tasks):
            pending.append(tasks[i + ahead][0]())
        tasks[i][1](pending.pop(0))

    units = [functools.partial(prefill, t) for t in tasks[:ahead]]
    return units + [functools.partial(step, i) for i in range(len(tasks))]


def _ffn_units(r, x_ref, mix_ref, mod_ref, wo_ref, nf_ref, wgu_ref, wd_ref, o_ref):
    g_a = mod_ref[:, 2 * D_MODEL:3 * D_MODEL]
    sh_f = mod_ref[:, 3 * D_MODEL:4 * D_MODEL]
    sc_f = mod_ref[:, 4 * D_MODEL:5 * D_MODEL]
    g_f = mod_ref[:, 5 * D_MODEL:6 * D_MODEL]
    state = {}

    def attn_residual():
        state["x"] = x_ref[r, :] + g_a * _dot(mix_ref[r, :], wo_ref[...])

    cut = (0, 6 * 256, FFN_HIDDEN)

    def gate(c):
        if c == 0:
            state["h"] = (_rms(state["x"], nf_ref[...]) * (1.0 + sc_f) + sh_f).astype(BF16)
        state["gate", c] = _dot(state["h"], wgu_ref[:, cut[c]:cut[c + 1]])

    def up(c):
        g = state.pop(("gate", c))
        u = _dot(state["h"], wgu_ref[:, FFN_HIDDEN + cut[c]:FFN_HIDDEN + cut[c + 1]])
        state["act", c] = (g * jax.nn.sigmoid(g) * u).astype(BF16)

    def ffn_residual():
        act = jnp.concatenate([state.pop(("act", 0)), state.pop(("act", 1))], axis=1)
        o_ref[r, :] = state.pop("x") + g_f * _dot(act, wd_ref[...])

    return [attn_residual, functools.partial(gate, 0), functools.partial(up, 0),
            functools.partial(gate, 1), functools.partial(up, 1), ffn_residual]


def _interleave(a, b):
    out, ia, ib = [], 0, 0
    while ia < len(a) or ib < len(b):
        if ib >= len(b) or (ia < len(a) and ia * len(b) <= ib * len(a)):
            out.append(a[ia])
            ia += 1
        else:
            out.append(b[ib])
            ib += 1
    return out


def _attn_ffn_kernel(lam_init, sub, *refs):
    (qa_ref, qb_ref, ka_ref, va_ref, kb_ref, vb_ref, lq1, lk1, lq2, lk2, subln_ref,
     x_ref, mod_ref, wo_ref, nf_ref, wgu_ref, wd_ref, o_ref, k_all, vt_all, mix) = refs
    seq = ka_ref.shape[0] // sub
    lam = _lambda(lq1, lk1, lq2, lk2, lam_init)
    subln = subln_ref[...]
    attn, ffn = [], []
    for j in range(sub):
        r = slice(j * seq, (j + 1) * seq)

        later = []

        def store(cols, tile, r=r):
            mix[r, cols] = tile.astype(mix.dtype)

        def stage(j=j, r=r, later=later, store=store):
            _stage_kv(k_all.at[j], vt_all.at[j], ka_ref[r, :], va_ref[r, :], kb_ref[r, :], vb_ref[r, :], None)
            later.extend(_attention_units(_attention_tasks(
                qa_ref[r, :], qb_ref[r, :], k_all.at[j], vt_all.at[j], lam, subln, lam_init, store)))

        attn.append([stage] + [lambda k=k, later=later: later[k]() for k in range(LOGITS_AHEAD + N_TASKS)])
        ffn.append(_ffn_units(r, x_ref, mix, mod_ref, wo_ref, nf_ref, wgu_ref, wd_ref, o_ref))
    units = attn[0]
    for j in range(1, sub):
        units = units + _interleave(attn[j], ffn[j - 1])
    for unit in units + ffn[sub - 1]:
        unit()


def _attn_ffn(layer, x, qa, qb, own, seq, lam_vecs, subln, mod4, w_out, norm_ffn, w_gu, w_down):
    n = x.shape[0]
    sub = 4
    tm = sub * seq
    lam_init = 0.8 - 0.6 * math.exp(-0.3 * layer)
    tok = lambda w: pl.BlockSpec((tm, w), lambda i: (i, 0))
    kv_w = own[0].shape[1] + own[2].shape[1]
    in_specs = ([tok(512), tok(512)] + [tok(a.shape[1]) for a in own]
                + [_layer_spec((1, HEAD_DIM), layer)] * 4 + [_layer_spec((2 * HEAD_DIM, 1), layer)]
                + [tok(D_MODEL), _mod_spec(layer, False, 1),
                   _const_spec((D_MODEL, D_MODEL)), _layer_spec((1, D_MODEL), layer),
                   _const_spec((D_MODEL, 2 * FFN_HIDDEN)), _const_spec((FFN_HIDDEN, D_MODEL))])
    return pl.pallas_call(
        functools.partial(_attn_ffn_kernel, lam_init, sub),
        grid=(n // tm,),
        in_specs=in_specs,
        out_specs=tok(D_MODEL),
        out_shape=jax.ShapeDtypeStruct((n, D_MODEL), F32),
        scratch_shapes=[pltpu.VMEM((sub, seq, kv_w), BF16), pltpu.VMEM((sub, kv_w, seq), BF16),
                        pltpu.VMEM((tm, D_MODEL), BF16)],
        compiler_params=pltpu.CompilerParams(
            dimension_semantics=("arbitrary",), vmem_limit_bytes=VMEM_LIMIT),
        name="attn_ffn_context",
    )(qa, qb, *own, *lam_vecs, subln, x, mod4, w_out, norm_ffn, w_gu, w_down)


def _attn_ffn_lagged_kernel(lam_init, tiles_per_request, n_tiles, *refs):
    (qa_ref, qb_ref, ka_ref, va_ref, kb_ref, vb_ref, cka_ref, cva_ref, ckb_ref, cvb_ref,
     lq1, lk1, lq2, lk2, subln_ref, x_ref, mod_ref, wo_ref, nf_ref, wgu_ref, wd_ref,
     o_ref, k_all, vt_all, mix, redo) = refs
    s = pl.program_id(0)
    tq = qa_ref.shape[0]
    lam = _lambda(lq1, lk1, lq2, lk2, lam_init)
    subln = subln_ref[...]

    def attention_units(slot, exact=False):
        def store(cols, tile):
            mix[slot, :, cols] = tile.astype(mix.dtype)

        denoms = None if exact else []
        units = _attention_units(_attention_tasks(
            qa_ref[...], qb_ref[...], k_all, vt_all, lam, subln, lam_init, store, denoms),
            LOGITS_AHEAD if exact else GUARDED_AHEAD)
        if exact:
            return units

        def check():
            redo[0] = _denominators_bad(denoms)

        return units + [check]

    def ffn_units(slot):
        return _ffn_units(slice(0, tq), x_ref, mix.at[slot], mod_ref, wo_ref, nf_ref, wgu_ref, wd_ref, o_ref)

    @pl.when(jnp.logical_and(s % tiles_per_request == 0, s < n_tiles))
    def _():
        _stage_kv(k_all, vt_all, ka_ref[...], va_ref[...], kb_ref[...], vb_ref[...],
                  (cka_ref, cva_ref, ckb_ref, cvb_ref))

    @pl.when(s == 0)
    def _():
        for unit in attention_units(0):
            unit()

    @pl.when(jnp.logical_and(s > 0, s < n_tiles))
    def _():
        for unit in _interleave(attention_units(s % 2), ffn_units((s - 1) % 2)):
            unit()

    @pl.when(jnp.logical_and(s < n_tiles, redo[0] != 0))
    def _():
        for unit in attention_units(s % 2, exact=True):
            unit()

    @pl.when(s == n_tiles)
    def _():
        for unit in ffn_units((n_tiles - 1) % 2):
            unit()


def _attn_ffn_lagged(layer, x, qa, qb, own, cache, seq, lam_vecs, subln, mod4, w_out, norm_ffn, w_gu, w_down):
    n = x.shape[0]
    tq = Q_TILE
    tpr = seq // tq
    n_tiles = n // tq
    lam_init = 0.8 - 0.6 * math.exp(-0.3 * layer)
    att = lambda s: jnp.minimum(s, n_tiles - 1)
    ffn = lambda s: jnp.maximum(s - 1, 0)
    once = pl.Buffered(1)
    in_specs = [pl.BlockSpec((tq, 512), lambda s: (att(s), 0))] * 2
    in_specs += [pl.BlockSpec((seq, a.shape[1]), lambda s: (att(s) // tpr, 0), pipeline_mode=once) for a in own]
    for a in cache:
        nd = a.ndim - 2
        in_specs.append(pl.BlockSpec((None, None) + a.shape[2:],
                                     lambda s, nd=nd: (att(s) // tpr, layer) + (0,) * nd, pipeline_mode=once))
    in_specs += [_layer_spec((1, HEAD_DIM), layer)] * 4 + [_layer_spec((2 * HEAD_DIM, 1), layer)]
    in_specs += [pl.BlockSpec((tq, D_MODEL), lambda s: (ffn(s), 0)),
                 pl.BlockSpec((None, None, 1, 6 * D_MODEL), lambda s: (layer, 1 + ffn(s) // tpr, 0, 0)),
                 _const_spec((D_MODEL, D_MODEL)), _layer_spec((1, D_MODEL), layer),
                 _const_spec((D_MODEL, 2 * FFN_HIDDEN)), _const_spec((FFN_HIDDEN, D_MODEL))]
    keys = seq + cache[0].shape[-1]
    kv_w = own[0].shape[1] + own[2].shape[1]
    return pl.pallas_call(
        functools.partial(_attn_ffn_lagged_kernel, lam_init, tpr, n_tiles),
        grid=(n_tiles + 1,),
        in_specs=in_specs,
        out_specs=pl.BlockSpec((tq, D_MODEL), lambda s: (ffn(s), 0)),
        out_shape=jax.ShapeDtypeStruct((n, D_MODEL), F32),
        scratch_shapes=[pltpu.VMEM((keys, kv_w), BF16), pltpu.VMEM((kv_w, keys), BF16),
                        pltpu.VMEM((2, tq, D_MODEL), BF16), pltpu.SMEM((1,), jnp.int32)],
        compiler_params=pltpu.CompilerParams(
            dimension_semantics=("arbitrary",), vmem_limit_bytes=VMEM_LIMIT),
        name="attn_ffn_latent",
    )(qa, qb, *own, *cache, *lam_vecs, subln, x, mod4, w_out, norm_ffn, w_gu, w_down)


def _rope_tables(n_tokens):
    t = jnp.arange(n_tokens, dtype=jnp.int32)
    row = (t // GRID_W).astype(F32)
    col = (t % GRID_W).astype(F32)
    axis_dim = HEAD_DIM // 2
    freqs = ROPE_THETA ** (-jnp.arange(0, axis_dim, 2, dtype=F32) / axis_dim)
    ang = jnp.concatenate([row[:, None] * freqs, col[:, None] * freqs], axis=-1)
    c, s = jnp.cos(ang), jnp.sin(ang)
    return jnp.concatenate([c, c, c, c], axis=-1), jnp.concatenate([-s, s, -s, s], axis=-1)


def kernel(x_prompt, x_sample, cache_diff_k, cache_diff_v, cache_gqa_k, cache_gqa_v, c, c_ctx, w_mod, b_mod, norm_attn, w_in, q_norm_a, k_norm_a, lambda_q1, lambda_k1, lambda_q2, lambda_k2, subln, q_norm_b, k_norm_b, w_out, norm_ffn, w_gate_up, w_down):
    batch, seq, d = x_prompt.shape
    dec_batch, dec_seq, _ = x_sample.shape
    depth = w_mod.shape[0]
    past = cache_diff_k.shape[2]

    cond = jnp.concatenate([c_ctx[None], c, jnp.zeros((MOD_ROWS - 1 - dec_batch, d), F32)], axis=0)
    mod = _modulation(cond, w_mod, b_mod)
    mod4 = mod.reshape(depth, MOD_ROWS, 1, 6 * d)

    w_in_b = w_in.astype(BF16)
    seg_ones = jnp.kron(jnp.eye(256 // HEAD_DIM, dtype=F32), jnp.ones((HEAD_DIM, HEAD_DIM), F32)).astype(BF16)
    rope = _rope_tables(dec_seq)

    cache = (jnp.transpose(cache_diff_k, (0, 1, 3, 4, 5, 2)).reshape(dec_batch, depth, DIFF_HEADS, 2 * HEAD_DIM, past),
             cache_diff_v.reshape(dec_batch, depth, past * DIFF_HEADS, 2 * HEAD_DIM),
             jnp.transpose(cache_gqa_k, (0, 1, 3, 4, 2)),
             jnp.transpose(cache_gqa_v, (0, 1, 3, 4, 2)))

    row = lambda a: a.reshape(depth, 1, a.shape[-1])
    gains = (row(q_norm_a), row(k_norm_a), row(q_norm_b), row(k_norm_b))
    lam_vecs = (row(lambda_q1), row(lambda_k1), row(lambda_q2), row(lambda_k2))
    na, nf, sl = row(norm_attn), row(norm_ffn), subln.reshape(depth, 2 * HEAD_DIM, 1)

    yp = x_prompt.reshape(batch * seq, d)
    ys = x_sample.reshape(dec_batch * dec_seq, d)
    lat_tiles = dec_seq // TOKEN_TILE
    new_caches = None
    for l in range(depth):
        qa, ka, va, qb, kb, vb, *rest = _qkv(
            yp, mod4, l, False, seq // TOKEN_TILE, na, w_in_b, seg_ones, gains, None, new_caches,
            (w_out, w_gate_up, w_down))
        new_caches, (w_out_b, w_gu_b, w_down_b) = rest[:4], rest[4:]
        yp = _attn_ffn(l, yp, qa, qb, (ka, va, kb, vb), seq, lam_vecs, sl, mod4, w_out_b, nf, w_gu_b, w_down_b)

        qa, ka, va, qb, kb, vb = _qkv(
            ys, mod4, l, True, lat_tiles, na, w_in_b, seg_ones, gains, rope)
        ys = _attn_ffn_lagged(l, ys, qa, qb, (ka, va, kb, vb), cache, dec_seq, lam_vecs, sl,
                              mod4, w_out_b, nf, w_gu_b, w_down_b)

    ka_t, va_n, kb_t, vb_t = new_caches
    new_diff_k = jnp.transpose(ka_t.reshape(batch, depth, DIFF_HEADS, 2, HEAD_DIM, seq), (0, 1, 5, 2, 3, 4))
    new_gqa_k = jnp.transpose(kb_t.reshape(batch, depth, GQA_KV_HEADS, HEAD_DIM, seq), (0, 1, 4, 2, 3))
    new_gqa_v = jnp.transpose(vb_t.reshape(batch, depth, GQA_KV_HEADS, HEAD_DIM, seq), (0, 1, 4, 2, 3))
    return (yp.reshape(batch, seq, d), ys.reshape(dec_batch, dec_seq, d),
            new_diff_k, va_n.reshape(batch, depth, seq, DIFF_HEADS, 2 * HEAD_DIM), new_gqa_k, new_gqa_v)
```

```python
import functools
import math

import jax
import jax.numpy as jnp
from jax import lax
from jax.experimental import pallas as pl
from jax.experimental.pallas import tpu as pltpu

F32 = jnp.float32
BF16 = jnp.bfloat16

D_MODEL = 1024
HEAD_DIM = 64
GRID_W = 64
DIFF_HEADS = 4
GQA_KV_HEADS = 2
GQA_REP = 4
FFN_HIDDEN = 2816
IN_COLS = 2304
ROPE_THETA = 10000.0
EPS = 1e-6
LANES = 128
MOD_ROWS = 8
TOKEN_TILE = 256
Q_TILE = 256
VMEM_LIMIT = 56 * 1024 * 1024


def _dot(a, b):
    return jnp.dot(a, b, preferred_element_type=F32)


def _rms(x, gain):
    ms = jnp.mean(x * x, axis=-1, keepdims=True)
    return x * lax.rsqrt(ms + EPS) * gain


def _layer_spec(shape, layer):
    n = len(shape)
    return pl.BlockSpec((None,) + tuple(shape), lambda *_: (layer,) + (0,) * n,
                        pipeline_mode=pl.Buffered(1))


def _const_spec(shape):
    n = len(shape)
    return pl.BlockSpec(shape, lambda *_: (0,) * n, pipeline_mode=pl.Buffered(1))


def _mod_spec(layer, latent, tiles_per_request):
    if latent:
        return pl.BlockSpec((None, None, 1, 6 * D_MODEL),
                            lambda i: (layer, 1 + i // tiles_per_request, 0, 0))
    return pl.BlockSpec((None, None, 1, 6 * D_MODEL), lambda i: (layer, 0, 0, 0))


def _mod_kernel(cond_ref, w_ref, b_ref, o_ref):
    c = cond_ref[...]
    s = c * jax.nn.sigmoid(c)
    o_ref[...] = _dot(s.astype(BF16), w_ref[...].astype(BF16)) + b_ref[...]


def _modulation(cond, w_mod, b_mod):
    depth = w_mod.shape[0]
    tn = 1536
    return pl.pallas_call(
        _mod_kernel,
        grid=(depth, 6 * D_MODEL // tn),
        in_specs=[
            pl.BlockSpec((MOD_ROWS, D_MODEL), lambda l, j: (0, 0)),
            pl.BlockSpec((None, D_MODEL, tn), lambda l, j: (l, 0, j)),
            pl.BlockSpec((None, 1, tn), lambda l, j: (l, 0, j)),
        ],
        out_specs=pl.BlockSpec((None, MOD_ROWS, tn), lambda l, j: (l, 0, j)),
        out_shape=jax.ShapeDtypeStruct((depth, MOD_ROWS, 6 * D_MODEL), F32),
        compiler_params=pltpu.CompilerParams(
            dimension_semantics=("arbitrary", "arbitrary"), vmem_limit_bytes=VMEM_LIMIT),
        name="modulation",
    )(cond, w_mod, b_mod.reshape(depth, 1, 6 * D_MODEL))


def _head_norm(x, seg_ones, gain):
    ss = _dot((x * x).astype(BF16), seg_ones)
    return x * lax.rsqrt(ss * (1.0 / HEAD_DIM) + EPS) * gain


def _swap_halves(x, first_half):
    return jnp.where(first_half, pltpu.roll(x, 96, 1), pltpu.roll(x, 32, 1))


def _rope(x, cos, sin, first_half):
    out = []
    for j in range(x.shape[1] // LANES):
        xj = x[:, j * LANES:(j + 1) * LANES]
        out.append(xj * cos + _swap_halves(xj, first_half) * sin)
    return out[0] if len(out) == 1 else jnp.concatenate(out, axis=1)


def _tile_gain(g):
    return jnp.concatenate([g] * (256 // HEAD_DIM), axis=1)


def _qkv_kernel(latent, n_cast, *refs):
    if latent:
        (x_ref, mod_ref, na_ref, w_ref, seg_ref, gqa_ref, gka_ref, gqb_ref, gkb_ref,
         cos_ref, sin_ref, qa_o, ka_o, va_o, qb_o, kb_o, vb_o) = refs
    else:
        (x_ref, mod_ref, na_ref, w_ref, seg_ref, gqa_ref, gka_ref, gqb_ref, gkb_ref) = refs[:9]
        outs = refs[len(refs) - 10 - n_cast:]
        qa_o, ka_o, va_o, qb_o, kb_o, vb_o, kaf_o, vaf_o, kbf_o, vbf_o = outs[:10]
        for src, dst in zip(refs[9:9 + n_cast], outs[10:]):
            dst[...] = src[...].astype(BF16)
    tm = TOKEN_TILE
    sh = mod_ref[:, 0:D_MODEL]
    sc = mod_ref[:, D_MODEL:2 * D_MODEL]
    seg = seg_ref[...]
    lane = lax.broadcasted_iota(jnp.int32, (1, LANES), 1)
    first_half = (lane % HEAD_DIM) < (HEAD_DIM // 2)
    scale = HEAD_DIM ** -0.5 * math.log2(math.e)

    def project(r):
        h = _rms(x_ref[r, :], na_ref[...]) * (1.0 + sc) + sh
        return _dot(h.astype(BF16), w_ref[...])

    def finish(j, r, proj):
        def qk(lo_col, width, gain_ref):
            gain = _tile_gain(gain_ref[...])
            out = []
            for c0 in range(0, width, 256):
                w = min(256, width - c0)
                out.append(_head_norm(proj[:, lo_col + c0:lo_col + c0 + w], seg[:w, :w], gain[:, :w]))
            y = out[0] if len(out) == 1 else jnp.concatenate(out, axis=1)
            if latent:
                y = _rope(y, cos_ref[r, :], sin_ref[r, :], first_half)
            return y

        qa = qk(0, 512, gqa_ref)
        ka = qk(512, 512, gka_ref)
        va = proj[:, 1024:1536]
        qb = qk(1536, 512, gqb_ref)
        kb = qk(2048, 128, gkb_ref)
        vb = proj[:, 2176:2304]
        qa_o[r, :] = (qa * scale).astype(BF16)
        ka_o[r, :] = ka.astype(BF16)
        va_o[r, :] = va.astype(BF16)
        qb_o[r, :] = (qb * scale).astype(BF16)
        kb_o[r, :] = kb.astype(BF16)
        vb_o[r, :] = vb.astype(BF16)
        if not latent:
            kaf_o[j] = ka.T
            kbf_o[j] = kb.T
            vbf_o[j] = vb.T
            for h in range(DIFF_HEADS):
                vaf_o[j, pl.ds(h, tm, stride=DIFF_HEADS), :] = va[:, h * LANES:(h + 1) * LANES]

    rows = [slice(j * tm, (j + 1) * tm) for j in range(x_ref.shape[0] // tm)]
    pending = [project(rows[0])]
    for j, r in enumerate(rows):
        if j + 1 < len(rows):
            pending.append(project(rows[j + 1]))
        finish(j, r, pending.pop(0))


def _qkv(x, mod4, layer, latent, tiles_per_request, norm_attn, w_in, seg_ones, gains, rope,
         prev_caches=None, cast_weights=()):
    n = x.shape[0]
    sub = 4
    tm = sub * TOKEN_TILE
    depth = w_in.shape[0]
    tok = lambda w: pl.BlockSpec((tm, w), lambda i: (i, 0))
    in_specs = [
        tok(D_MODEL),
        _mod_spec(layer, latent, max(tiles_per_request // sub, 1)),
        _layer_spec((1, D_MODEL), layer),
        _layer_spec((D_MODEL, IN_COLS), layer),
        _const_spec((256, 256)),
    ] + [_layer_spec((1, HEAD_DIM), layer)] * 4
    args = [x, mod4, norm_attn, w_in, seg_ones, *gains]
    aliases = {}
    out_specs = [tok(512), tok(512), tok(512), tok(512), tok(LANES), tok(LANES)]
    out_shape = [jax.ShapeDtypeStruct((n, w), BF16) for w in (512, 512, 512, 512, LANES, LANES)]
    if latent:
        rope_spec = pl.BlockSpec((tm, LANES), lambda i: (i % (tiles_per_request // sub), 0))
        in_specs += [rope_spec, rope_spec]
        args += list(rope)
    else:
        assert tiles_per_request == 1, "context requests must be one token tile long"
        seq = TOKEN_TILE
        req = n // seq
        slab = lambda *dims: pl.BlockSpec((sub, None) + dims, lambda i: (i, layer) + (0,) * len(dims))
        out_specs += [slab(512, seq), slab(seq * DIFF_HEADS, LANES), slab(LANES, seq), slab(LANES, seq)]
        out_shape += [jax.ShapeDtypeStruct((req, depth, 512, seq), F32),
                      jax.ShapeDtypeStruct((req, depth, seq * DIFF_HEADS, LANES), F32),
                      jax.ShapeDtypeStruct((req, depth, LANES, seq), F32),
                      jax.ShapeDtypeStruct((req, depth, LANES, seq), F32)]
        steps = n // tm
        for w in cast_weights:
            rows, cols = w.shape[1] // steps, w.shape[2]
            assert rows * steps == w.shape[1] and rows % 16 == 0
            in_specs.append(pl.BlockSpec((None, rows, cols), lambda i: (layer, i, 0)))
            args.append(w)
            out_specs.append(pl.BlockSpec((rows, cols), lambda i: (i, 0)))
            out_shape.append(jax.ShapeDtypeStruct(w.shape[1:], BF16))
        if prev_caches is not None:
            aliases = {len(args) + k: 6 + k for k in range(4)}
            in_specs += [pl.BlockSpec(memory_space=pl.ANY)] * 4
            args += list(prev_caches)
    return pl.pallas_call(
        functools.partial(_qkv_kernel, latent, len(cast_weights)),
        grid=(n // tm,),
        in_specs=in_specs,
        out_specs=out_specs,
        out_shape=out_shape,
        input_output_aliases=aliases,
        compiler_params=pltpu.CompilerParams(
            dimension_semantics=("arbitrary",), vmem_limit_bytes=VMEM_LIMIT),
        name="qkv_latent" if latent else "qkv_context",
    )(*args)


def _pad_rows(x, block):
    z = jnp.zeros_like(x)
    return jnp.concatenate([x, z] if block == 0 else [z, x], axis=0)


ONES_ROWS = 16
SHIFT_KEYS = 8
DENOM_RANGE = (2.0 ** -100, 2.0 ** 100)


def _numerators(st, exact):
    shift = (st if exact else st[:SHIFT_KEYS]).max(axis=0, keepdims=True)
    if exact:
        return jnp.exp2(st - shift).astype(BF16)
    return jnp.exp2((st - shift).astype(BF16))


def _pv(vt, e):
    d = vt.shape[0]
    ext = _dot(jnp.concatenate([vt, jnp.ones((ONES_ROWS, vt.shape[1]), BF16)], axis=0), e)
    return ext[:d], ext[d:d + 1]


def _denominators_bad(denoms):
    bad = None
    for l in denoms:
        b = jnp.where(jnp.logical_and(l > DENOM_RANGE[0], l < DENOM_RANGE[1]), 0.0, 1.0)
        bad = b if bad is None else jnp.maximum(bad, b)
    return (jnp.max(bad) > 0.0).astype(jnp.int32)


KA_W = DIFF_HEADS * LANES


def _stage_kv(k_all, vt_all, ka, va, kb, vb, cache_refs):
    seq = ka.shape[0]
    k_all[0:seq, 0:KA_W] = ka
    k_all[0:seq, KA_W:] = kb
    vt_all[0:KA_W, 0:seq] = va.astype(F32).T.astype(BF16)
    vt_all[KA_W:, 0:seq] = vb.astype(F32).T.astype(BF16)
    if cache_refs is not None:
        cka_ref, cva_ref, ckb_ref, cvb_ref = cache_refs
        past = cka_ref.shape[2]
        for h in range(DIFF_HEADS):
            cs = slice(h * LANES, (h + 1) * LANES)
            k_all[seq:, cs] = cka_ref[h].T.astype(BF16)
            vt_all[cs, seq:] = cva_ref[pl.ds(h, past, stride=DIFF_HEADS), :].T.astype(BF16)
        k_all[seq:, KA_W:] = jnp.concatenate([ckb_ref[0], ckb_ref[1]], axis=0).T.astype(BF16)
        for g in range(GQA_KV_HEADS):
            vt_all[KA_W + g * HEAD_DIM:KA_W + (g + 1) * HEAD_DIM, seq:] = cvb_ref[g].astype(BF16)


def _lambda(lq1, lk1, lq2, lk2, lam_init):
    return (jnp.exp(jnp.sum(lq1[...] * lk1[...], axis=-1, keepdims=True))
            - jnp.exp(jnp.sum(lq2[...] * lk2[...], axis=-1, keepdims=True)) + lam_init)


def _attention_tasks(qa, qb, k_all, vt_all, lam, subln, lam_init, store, denoms=None):
    tq = qa.shape[0]
    qat = qa.astype(F32).T.astype(BF16)
    qbt = qb.astype(F32).T.astype(BF16)
    exact = denoms is None
    tasks = []

    def first(st):
        return _numerators(st, exact)

    def pv(vt, e):
        num, l = _pv(vt, e)
        if not exact:
            denoms.append(l)
        return num, 1.0 / l

    def diff_finish(e, h, cs):
        num, r = pv(vt_all[cs, :], e)
        ot = num[:, :tq] * r[:, :tq] - num[:, tq:] * (r[:, tq:] * lam)
        ms = jnp.mean(ot * ot, axis=0, keepdims=True)
        ot = ot * lax.rsqrt(ms + EPS) * subln * (1.0 - lam_init)
        store(cs, ot.T)

    def gqa_finish(e, g, os):
        num, r = pv(vt_all[KA_W + g * HEAD_DIM:KA_W + (g + 1) * HEAD_DIM, :], e)
        ot = num * r
        store(os, jnp.concatenate([ot[:, :tq], ot[:, tq:]], axis=0).T)

    for h in range(DIFF_HEADS):
        cs = slice(h * LANES, (h + 1) * LANES)

        def logits(h=h, cs=cs):
            c0 = qat[h * LANES:h * LANES + HEAD_DIM]
            c1 = qat[h * LANES + HEAD_DIM:(h + 1) * LANES]
            return first(_dot(k_all[:, cs], jnp.concatenate([_pad_rows(c0, 0), _pad_rows(c1, 1)], axis=1)))

        tasks.append((logits, lambda e, h=h, cs=cs: diff_finish(e, h, cs)))
    for g in range(GQA_KV_HEADS):
        for rp in range(2):
            r0 = (GQA_REP * g + 2 * rp) * HEAD_DIM
            os = slice(KA_W + r0, KA_W + r0 + LANES)

            def logits(g=g, r0=r0):
                a = qbt[r0:r0 + HEAD_DIM]
                b = qbt[r0 + HEAD_DIM:r0 + LANES]
                return first(_dot(k_all[:, KA_W:], jnp.concatenate([_pad_rows(a, g), _pad_rows(b, g)], axis=1)))

            tasks.append((logits, lambda e, g=g, os=os: gqa_finish(e, g, os)))
    return tasks


LOGITS_AHEAD = 2
GUARDED_AHEAD = 3
N_TASKS = DIFF_HEADS + 2 * GQA_KV_HEADS


def _attention_units(tasks, ahead=LOGITS_AHEAD):
    pending = []

    def prefill(t):
        pending.append(t[0]())

    def step(i):
        if i + ahead < len(tasks):
            pending.append(tasks[i + ahead][0]())
        tasks[i][1](pending.pop(0))

    units = [functools.partial(prefill, t) for t in tasks[:ahead]]
    return units + [functools.partial(step, i) for i in range(len(tasks))]


def _ffn_units(r, x_ref, mix_ref, mod_ref, wo_ref, nf_ref, wgu_ref, wd_ref, o_ref):
    g_a = mod_ref[:, 2 * D_MODEL:3 * D_MODEL]
    sh_f = mod_ref[:, 3 * D_MODEL:4 * D_MODEL]
    sc_f = mod_ref[:, 4 * D_MODEL:5 * D_MODEL]
    g_f = mod_ref[:, 5 * D_MODEL:6 * D_MODEL]
    state = {}

    def attn_residual():
        state["x"] = x_ref[r, :] + g_a * _dot(mix_ref[r, :], wo_ref[...])

    def gate():
        state["h"] = (_rms(state["x"], nf_ref[...]) * (1.0 + sc_f) + sh_f).astype(BF16)
        state["gate"] = _dot(state["h"], wgu_ref[:, :FFN_HIDDEN])

    def up():
        g = state.pop("gate")
        state["act"] = (g * jax.nn.sigmoid(g) * _dot(state.pop("h"), wgu_ref[:, FFN_HIDDEN:])).astype(BF16)

    def ffn_residual():
        o_ref[r, :] = state.pop("x") + g_f * _dot(state.pop("act"), wd_ref[...])

    return [attn_residual, gate, up, ffn_residual]


def _interleave(a, b):
    out, ia, ib = [], 0, 0
    while ia < len(a) or ib < len(b):
        if ib >= len(b) or (ia < len(a) and ia * len(b) <= ib * len(a)):
            out.append(a[ia])
            ia += 1
        else:
            out.append(b[ib])
            ib += 1
    return out


def _attn_ffn_kernel(lam_init, sub, *refs):
    (qa_ref, qb_ref, ka_ref, va_ref, kb_ref, vb_ref, lq1, lk1, lq2, lk2, subln_ref,
     x_ref, mod_ref, wo_ref, nf_ref, wgu_ref, wd_ref, o_ref, k_all, vt_all, mix) = refs
    seq = ka_ref.shape[0] // sub
    lam = _lambda(lq1, lk1, lq2, lk2, lam_init)
    subln = subln_ref[...]
    attn, ffn = [], []
    for j in range(sub):
        r = slice(j * seq, (j + 1) * seq)

        later = []

        def store(cols, tile, r=r):
            mix[r, cols] = tile.astype(mix.dtype)

        def stage(j=j, r=r, later=later, store=store):
            _stage_kv(k_all.at[j], vt_all.at[j], ka_ref[r, :], va_ref[r, :], kb_ref[r, :], vb_ref[r, :], None)
            later.extend(_attention_units(_attention_tasks(
                qa_ref[r, :], qb_ref[r, :], k_all.at[j], vt_all.at[j], lam, subln, lam_init, store)))

        attn.append([stage] + [lambda k=k, later=later: later[k]() for k in range(LOGITS_AHEAD + N_TASKS)])
        ffn.append(_ffn_units(r, x_ref, mix, mod_ref, wo_ref, nf_ref, wgu_ref, wd_ref, o_ref))
    units = attn[0]
    for j in range(1, sub):
        units = units + _interleave(attn[j], ffn[j - 1])
    for unit in units + ffn[sub - 1]:
        unit()


def _attn_ffn(layer, x, qa, qb, own, seq, lam_vecs, subln, mod4, w_out, norm_ffn, w_gu, w_down):
    n = x.shape[0]
    sub = 4
    tm = sub * seq
    lam_init = 0.8 - 0.6 * math.exp(-0.3 * layer)
    tok = lambda w: pl.BlockSpec((tm, w), lambda i: (i, 0))
    kv_w = own[0].shape[1] + own[2].shape[1]
    in_specs = ([tok(512), tok(512)] + [tok(a.shape[1]) for a in own]
                + [_layer_spec((1, HEAD_DIM), layer)] * 4 + [_layer_spec((2 * HEAD_DIM, 1), layer)]
                + [tok(D_MODEL), _mod_spec(layer, False, 1),
                   _const_spec((D_MODEL, D_MODEL)), _layer_spec((1, D_MODEL), layer),
                   _const_spec((D_MODEL, 2 * FFN_HIDDEN)), _const_spec((FFN_HIDDEN, D_MODEL))])
    return pl.pallas_call(
        functools.partial(_attn_ffn_kernel, lam_init, sub),
        grid=(n // tm,),
        in_specs=in_specs,
        out_specs=tok(D_MODEL),
        out_shape=jax.ShapeDtypeStruct((n, D_MODEL), F32),
        scratch_shapes=[pltpu.VMEM((sub, seq, kv_w), BF16), pltpu.VMEM((sub, kv_w, seq), BF16),
                        pltpu.VMEM((tm, D_MODEL), BF16)],
        compiler_params=pltpu.CompilerParams(
            dimension_semantics=("arbitrary",), vmem_limit_bytes=VMEM_LIMIT),
        name="attn_ffn_context",
    )(qa, qb, *own, *lam_vecs, subln, x, mod4, w_out, norm_ffn, w_gu, w_down)


def _attn_ffn_lagged_kernel(lam_init, tiles_per_request, n_tiles, *refs):
    (qa_ref, qb_ref, ka_ref, va_ref, kb_ref, vb_ref, cka_ref, cva_ref, ckb_ref, cvb_ref,
     lq1, lk1, lq2, lk2, subln_ref, x_ref, mod_ref, wo_ref, nf_ref, wgu_ref, wd_ref,
     o_ref, k_all, vt_all, mix, redo) = refs
    s = pl.program_id(0)
    tq = qa_ref.shape[0]
    lam = _lambda(lq1, lk1, lq2, lk2, lam_init)
    subln = subln_ref[...]

    def attention_units(slot, exact=False):
        def store(cols, tile):
            mix[slot, :, cols] = tile.astype(mix.dtype)

        denoms = None if exact else []
        units = _attention_units(_attention_tasks(
            qa_ref[...], qb_ref[...], k_all, vt_all, lam, subln, lam_init, store, denoms),
            LOGITS_AHEAD if exact else GUARDED_AHEAD)
        if exact:
            return units

        def check():
            redo[0] = _denominators_bad(denoms)

        return units + [check]

    def ffn_units(slot):
        return _ffn_units(slice(0, tq), x_ref, mix.at[slot], mod_ref, wo_ref, nf_ref, wgu_ref, wd_ref, o_ref)

    @pl.when(jnp.logical_and(s % tiles_per_request == 0, s < n_tiles))
    def _():
        _stage_kv(k_all, vt_all, ka_ref[...], va_ref[...], kb_ref[...], vb_ref[...],
                  (cka_ref, cva_ref, ckb_ref, cvb_ref))

    @pl.when(s == 0)
    def _():
        for unit in attention_units(0):
            unit()

    @pl.when(jnp.logical_and(s > 0, s < n_tiles))
    def _():
        for unit in _interleave(attention_units(s % 2), ffn_units((s - 1) % 2)):
            unit()

    @pl.when(jnp.logical_and(s < n_tiles, redo[0] != 0))
    def _():
        for unit in attention_units(s % 2, exact=True):
            unit()

    @pl.when(s == n_tiles)
    def _():
        for unit in ffn_units((n_tiles - 1) % 2):
            unit()


def _attn_ffn_lagged(layer, x, qa, qb, own, cache, seq, lam_vecs, subln, mod4, w_out, norm_ffn, w_gu, w_down):
    n = x.shape[0]
    tq = Q_TILE
    tpr = seq // tq
    n_tiles = n // tq
    lam_init = 0.8 - 0.6 * math.exp(-0.3 * layer)
    att = lambda s: jnp.minimum(s, n_tiles - 1)
    ffn = lambda s: jnp.maximum(s - 1, 0)
    once = pl.Buffered(1)
    in_specs = [pl.BlockSpec((tq, 512), lambda s: (att(s), 0))] * 2
    in_specs += [pl.BlockSpec((seq, a.shape[1]), lambda s: (att(s) // tpr, 0), pipeline_mode=once) for a in own]
    for a in cache:
        nd = a.ndim - 2
        in_specs.append(pl.BlockSpec((None, None) + a.shape[2:],
                                     lambda s, nd=nd: (att(s) // tpr, layer) + (0,) * nd, pipeline_mode=once))
    in_specs += [_layer_spec((1, HEAD_DIM), layer)] * 4 + [_layer_spec((2 * HEAD_DIM, 1), layer)]
    in_specs += [pl.BlockSpec((tq, D_MODEL), lambda s: (ffn(s), 0)),
                 pl.BlockSpec((None, None, 1, 6 * D_MODEL), lambda s: (layer, 1 + ffn(s) // tpr, 0, 0)),
                 _const_spec((D_MODEL, D_MODEL)), _layer_spec((1, D_MODEL), layer),
                 _const_spec((D_MODEL, 2 * FFN_HIDDEN)), _const_spec((FFN_HIDDEN, D_MODEL))]
    keys = seq + cache[0].shape[-1]
    kv_w = own[0].shape[1] + own[2].shape[1]
    return pl.pallas_call(
        functools.partial(_attn_ffn_lagged_kernel, lam_init, tpr, n_tiles),
        grid=(n_tiles + 1,),
        in_specs=in_specs,
        out_specs=pl.BlockSpec((tq, D_MODEL), lambda s: (ffn(s), 0)),
        out_shape=jax.ShapeDtypeStruct((n, D_MODEL), F32),
        scratch_shapes=[pltpu.VMEM((keys, kv_w), BF16), pltpu.VMEM((kv_w, keys), BF16),
                        pltpu.VMEM((2, tq, D_MODEL), BF16), pltpu.SMEM((1,), jnp.int32)],
        compiler_params=pltpu.CompilerParams(
            dimension_semantics=("arbitrary",), vmem_limit_bytes=VMEM_LIMIT),
        name="attn_ffn_latent",
    )(qa, qb, *own, *cache, *lam_vecs, subln, x, mod4, w_out, norm_ffn, w_gu, w_down)


def _rope_tables(n_tokens):
    t = jnp.arange(n_tokens, dtype=jnp.int32)
    row = (t // GRID_W).astype(F32)
    col = (t % GRID_W).astype(F32)
    axis_dim = HEAD_DIM // 2
    freqs = ROPE_THETA ** (-jnp.arange(0, axis_dim, 2, dtype=F32) / axis_dim)
    ang = jnp.concatenate([row[:, None] * freqs, col[:, None] * freqs], axis=-1)
    c, s = jnp.cos(ang), jnp.sin(ang)
    return jnp.concatenate([c, c, c, c], axis=-1), jnp.concatenate([-s, s, -s, s], axis=-1)


def kernel(x_prompt, x_sample, cache_diff_k, cache_diff_v, cache_gqa_k, cache_gqa_v, c, c_ctx, w_mod, b_mod, norm_attn, w_in, q_norm_a, k_norm_a, lambda_q1, lambda_k1, lambda_q2, lambda_k2, subln, q_norm_b, k_norm_b, w_out, norm_ffn, w_gate_up, w_down):
    batch, seq, d = x_prompt.shape
    dec_batch, dec_seq, _ = x_sample.shape
    depth = w_mod.shape[0]
    past = cache_diff_k.shape[2]

    cond = jnp.concatenate([c_ctx[None], c, jnp.zeros((MOD_ROWS - 1 - dec_batch, d), F32)], axis=0)
    mod = _modulation(cond, w_mod, b_mod)
    mod4 = mod.reshape(depth, MOD_ROWS, 1, 6 * d)

    w_in_b = w_in.astype(BF16)
    seg_ones = jnp.kron(jnp.eye(256 // HEAD_DIM, dtype=F32), jnp.ones((HEAD_DIM, HEAD_DIM), F32)).astype(BF16)
    rope = _rope_tables(dec_seq)

    cache = (jnp.transpose(cache_diff_k, (0, 1, 3, 4, 5, 2)).reshape(dec_batch, depth, DIFF_HEADS, 2 * HEAD_DIM, past),
             cache_diff_v.reshape(dec_batch, depth, past * DIFF_HEADS, 2 * HEAD_DIM),
             jnp.transpose(cache_gqa_k, (0, 1, 3, 4, 2)),
             jnp.transpose(cache_gqa_v, (0, 1, 3, 4, 2)))

    row = lambda a: a.reshape(depth, 1, a.shape[-1])
    gains = (row(q_norm_a), row(k_norm_a), row(q_norm_b), row(k_norm_b))
    lam_vecs = (row(lambda_q1), row(lambda_k1), row(lambda_q2), row(lambda_k2))
    na, nf, sl = row(norm_attn), row(norm_ffn), subln.reshape(depth, 2 * HEAD_DIM, 1)

    yp = x_prompt.reshape(batch * seq, d)
    ys = x_sample.reshape(dec_batch * dec_seq, d)
    lat_tiles = dec_seq // TOKEN_TILE
    new_caches = None
    for l in range(depth):
        qa, ka, va, qb, kb, vb, *rest = _qkv(
            yp, mod4, l, False, seq // TOKEN_TILE, na, w_in_b, seg_ones, gains, None, new_caches,
            (w_out, w_gate_up, w_down))
        new_caches, (w_out_b, w_gu_b, w_down_b) = rest[:4], rest[4:]
        yp = _attn_ffn(l, yp, qa, qb, (ka, va, kb, vb), seq, lam_vecs, sl, mod4, w_out_b, nf, w_gu_b, w_down_b)

        qa, ka, va, qb, kb, vb = _qkv(
            ys, mod4, l, True, lat_tiles, na, w_in_b, seg_ones, gains, rope)
        ys = _attn_ffn_lagged(l, ys, qa, qb, (ka, va, kb, vb), cache, dec_seq, lam_vecs, sl,
                              mod4, w_out_b, nf, w_gu_b, w_down_b)

    ka_t, va_n, kb_t, vb_t = new_caches
    new_diff_k = jnp.transpose(ka_t.reshape(batch, depth, DIFF_HEADS, 2, HEAD_DIM, seq), (0, 1, 5, 2, 3, 4))
    new_gqa_k = jnp.transpose(kb_t.reshape(batch, depth, GQA_KV_HEADS, HEAD_DIM, seq), (0, 1, 4, 2, 3))
    new_gqa_v = jnp.transpose(vb_t.reshape(batch, depth, GQA_KV_HEADS, HEAD_DIM, seq), (0, 1, 4, 2, 3))
    return (yp.reshape(batch, seq, d), ys.reshape(dec_batch, dec_seq, d),
            new_diff_k, va_n.reshape(batch, depth, seq, DIFF_HEADS, 2 * HEAD_DIM), new_gqa_k, new_gqa_v)
```

```python
import functools
import math

import jax
import jax.numpy as jnp
from jax import lax
from jax.experimental import pallas as pl
from jax.experimental.pallas import tpu as pltpu

F32 = jnp.float32
BF16 = jnp.bfloat16

D_MODEL = 1024
HEAD_DIM = 64
GRID_W = 64
DIFF_HEADS = 4
GQA_KV_HEADS = 2
GQA_REP = 4
FFN_HIDDEN = 2816
IN_COLS = 2304
ROPE_THETA = 10000.0
EPS = 1e-6
LANES = 128
MOD_ROWS = 8
TOKEN_TILE = 256
Q_TILE = 256
VMEM_LIMIT = 56 * 1024 * 1024


def _dot(a, b):
    return jnp.dot(a, b, preferred_element_type=F32)


def _rms(x, gain):
    ms = jnp.mean(x * x, axis=-1, keepdims=True)
    return x * lax.rsqrt(ms + EPS) * gain


def _layer_spec(shape, layer):
    n = len(shape)
    return pl.BlockSpec((None,) + tuple(shape), lambda *_: (layer,) + (0,) * n,
                        pipeline_mode=pl.Buffered(1))


def _const_spec(shape):
    n = len(shape)
    return pl.BlockSpec(shape, lambda *_: (0,) * n, pipeline_mode=pl.Buffered(1))


def _mod_spec(layer, latent, tiles_per_request):
    if latent:
        return pl.BlockSpec((None, None, 1, 6 * D_MODEL),
                            lambda i: (layer, 1 + i // tiles_per_request, 0, 0))
    return pl.BlockSpec((None, None, 1, 6 * D_MODEL), lambda i: (layer, 0, 0, 0))


def _mod_kernel(cond_ref, w_ref, b_ref, o_ref):
    c = cond_ref[...]
    s = c * jax.nn.sigmoid(c)
    o_ref[...] = _dot(s.astype(BF16), w_ref[...].astype(BF16)) + b_ref[...]


def _modulation(cond, w_mod, b_mod):
    depth = w_mod.shape[0]
    tn = 1536
    return pl.pallas_call(
        _mod_kernel,
        grid=(depth, 6 * D_MODEL // tn),
        in_specs=[
            pl.BlockSpec((MOD_ROWS, D_MODEL), lambda l, j: (0, 0)),
            pl.BlockSpec((None, D_MODEL, tn), lambda l, j: (l, 0, j)),
            pl.BlockSpec((None, 1, tn), lambda l, j: (l, 0, j)),
        ],
        out_specs=pl.BlockSpec((None, MOD_ROWS, tn), lambda l, j: (l, 0, j)),
        out_shape=jax.ShapeDtypeStruct((depth, MOD_ROWS, 6 * D_MODEL), F32),
        compiler_params=pltpu.CompilerParams(
            dimension_semantics=("arbitrary", "arbitrary"), vmem_limit_bytes=VMEM_LIMIT),
        name="modulation",
    )(cond, w_mod, b_mod.reshape(depth, 1, 6 * D_MODEL))


def _head_norm(x, seg_ones, gain):
    ss = _dot((x * x).astype(BF16), seg_ones)
    return x * lax.rsqrt(ss * (1.0 / HEAD_DIM) + EPS) * gain


def _swap_halves(x, first_half):
    return jnp.where(first_half, pltpu.roll(x, 96, 1), pltpu.roll(x, 32, 1))


def _rope(x, cos, sin, first_half):
    out = []
    for j in range(x.shape[1] // LANES):
        xj = x[:, j * LANES:(j + 1) * LANES]
        out.append(xj * cos + _swap_halves(xj, first_half) * sin)
    return out[0] if len(out) == 1 else jnp.concatenate(out, axis=1)


def _tile_gain(g):
    return jnp.concatenate([g] * (256 // HEAD_DIM), axis=1)


def _qkv_kernel(latent, n_cast, *refs):
    if latent:
        (x_ref, mod_ref, na_ref, w_ref, seg_ref, gqa_ref, gka_ref, gqb_ref, gkb_ref,
         cos_ref, sin_ref, qa_o, ka_o, va_o, qb_o, kb_o, vb_o) = refs
    else:
        (x_ref, mod_ref, na_ref, w_ref, seg_ref, gqa_ref, gka_ref, gqb_ref, gkb_ref) = refs[:9]
        outs = refs[len(refs) - 10 - n_cast:]
        qa_o, ka_o, va_o, qb_o, kb_o, vb_o, kaf_o, vaf_o, kbf_o, vbf_o = outs[:10]
        for src, dst in zip(refs[9:9 + n_cast], outs[10:]):
            dst[...] = src[...].astype(BF16)
    tm = TOKEN_TILE
    sh = mod_ref[:, 0:D_MODEL]
    sc = mod_ref[:, D_MODEL:2 * D_MODEL]
    seg = seg_ref[...]
    lane = lax.broadcasted_iota(jnp.int32, (1, LANES), 1)
    first_half = (lane % HEAD_DIM) < (HEAD_DIM // 2)
    scale = HEAD_DIM ** -0.5 * math.log2(math.e)

    def project(r):
        h = _rms(x_ref[r, :], na_ref[...]) * (1.0 + sc) + sh
        return _dot(h.astype(BF16), w_ref[...])

    def finish(j, r, proj):
        def qk(lo_col, width, gain_ref):
            gain = _tile_gain(gain_ref[...])
            out = []
            for c0 in range(0, width, 256):
                w = min(256, width - c0)
                out.append(_head_norm(proj[:, lo_col + c0:lo_col + c0 + w], seg[:w, :w], gain[:, :w]))
            y = out[0] if len(out) == 1 else jnp.concatenate(out, axis=1)
            if latent:
                y = _rope(y, cos_ref[r, :], sin_ref[r, :], first_half)
            return y

        qa = qk(0, 512, gqa_ref)
        ka = qk(512, 512, gka_ref)
        va = proj[:, 1024:1536]
        qb = qk(1536, 512, gqb_ref)
        kb = qk(2048, 128, gkb_ref)
        vb = proj[:, 2176:2304]
        qa_o[r, :] = (qa * scale).astype(BF16)
        ka_o[r, :] = ka.astype(BF16)
        va_o[r, :] = va.astype(BF16)
        qb_o[r, :] = (qb * scale).astype(BF16)
        kb_o[r, :] = kb.astype(BF16)
        vb_o[r, :] = vb.astype(BF16)
        if not latent:
            kaf_o[j] = ka.T
            kbf_o[j] = kb.T
            vbf_o[j] = vb.T
            for h in range(DIFF_HEADS):
                vaf_o[j, pl.ds(h, tm, stride=DIFF_HEADS), :] = va[:, h * LANES:(h + 1) * LANES]

    rows = [slice(j * tm, (j + 1) * tm) for j in range(x_ref.shape[0] // tm)]
    pending = [project(rows[0])]
    for j, r in enumerate(rows):
        if j + 1 < len(rows):
            pending.append(project(rows[j + 1]))
        finish(j, r, pending.pop(0))


def _qkv(x, mod4, layer, latent, tiles_per_request, norm_attn, w_in, seg_ones, gains, rope,
         prev_caches=None, cast_weights=()):
    n = x.shape[0]
    sub = 4
    tm = sub * TOKEN_TILE
    depth = w_in.shape[0]
    tok = lambda w: pl.BlockSpec((tm, w), lambda i: (i, 0))
    in_specs = [
        tok(D_MODEL),
        _mod_spec(layer, latent, max(tiles_per_request // sub, 1)),
        _layer_spec((1, D_MODEL), layer),
        _layer_spec((D_MODEL, IN_COLS), layer),
        _const_spec((256, 256)),
    ] + [_layer_spec((1, HEAD_DIM), layer)] * 4
    args = [x, mod4, norm_attn, w_in, seg_ones, *gains]
    aliases = {}
    out_specs = [tok(512), tok(512), tok(512), tok(512), tok(LANES), tok(LANES)]
    out_shape = [jax.ShapeDtypeStruct((n, w), BF16) for w in (512, 512, 512, 512, LANES, LANES)]
    if latent:
        rope_spec = pl.BlockSpec((tm, LANES), lambda i: (i % (tiles_per_request // sub), 0))
        in_specs += [rope_spec, rope_spec]
        args += list(rope)
    else:
        assert tiles_per_request == 1, "context requests must be one token tile long"
        seq = TOKEN_TILE
        req = n // seq
        slab = lambda *dims: pl.BlockSpec((sub, None) + dims, lambda i: (i, layer) + (0,) * len(dims))
        out_specs += [slab(512, seq), slab(seq * DIFF_HEADS, LANES), slab(LANES, seq), slab(LANES, seq)]
        out_shape += [jax.ShapeDtypeStruct((req, depth, 512, seq), F32),
                      jax.ShapeDtypeStruct((req, depth, seq * DIFF_HEADS, LANES), F32),
                      jax.ShapeDtypeStruct((req, depth, LANES, seq), F32),
                      jax.ShapeDtypeStruct((req, depth, LANES, seq), F32)]
        steps = n // tm
        for w in cast_weights:
            rows, cols = w.shape[1] // steps, w.shape[2]
            assert rows * steps == w.shape[1] and rows % 16 == 0
            in_specs.append(pl.BlockSpec((None, rows, cols), lambda i: (layer, i, 0)))
            args.append(w)
            out_specs.append(pl.BlockSpec((rows, cols), lambda i: (i, 0)))
            out_shape.append(jax.ShapeDtypeStruct(w.shape[1:], BF16))
        if prev_caches is not None:
            aliases = {len(args) + k: 6 + k for k in range(4)}
            in_specs += [pl.BlockSpec(memory_space=pl.ANY)] * 4
            args += list(prev_caches)
    return pl.pallas_call(
        functools.partial(_qkv_kernel, latent, len(cast_weights)),
        grid=(n // tm,),
        in_specs=in_specs,
        out_specs=out_specs,
        out_shape=out_shape,
        input_output_aliases=aliases,
        compiler_params=pltpu.CompilerParams(
            dimension_semantics=("arbitrary",), vmem_limit_bytes=VMEM_LIMIT),
        name="qkv_latent" if latent else "qkv_context",
    )(*args)


def _pad_rows(x, block):
    z = jnp.zeros_like(x)
    return jnp.concatenate([x, z] if block == 0 else [z, x], axis=0)


ONES_ROWS = 16
SHIFT_KEYS = 8
DENOM_RANGE = (2.0 ** -100, 2.0 ** 100)


def _numerators(st, exact):
    shift = (st if exact else st[:SHIFT_KEYS]).max(axis=0, keepdims=True)
    return jnp.exp2(st - shift).astype(BF16)


def _pv(vt, e):
    d = vt.shape[0]
    ext = _dot(jnp.concatenate([vt, jnp.ones((ONES_ROWS, vt.shape[1]), BF16)], axis=0), e)
    return ext[:d], ext[d:d + 1]


def _denominators_bad(denoms):
    bad = None
    for l in denoms:
        b = jnp.where(jnp.logical_and(l > DENOM_RANGE[0], l < DENOM_RANGE[1]), 0.0, 1.0)
        bad = b if bad is None else jnp.maximum(bad, b)
    return (jnp.max(bad) > 0.0).astype(jnp.int32)


KA_W = DIFF_HEADS * LANES


def _stage_kv(k_all, vt_all, ka, va, kb, vb, cache_refs):
    seq = ka.shape[0]
    k_all[0:seq, 0:KA_W] = ka
    k_all[0:seq, KA_W:] = kb
    vt_all[0:KA_W, 0:seq] = va.astype(F32).T.astype(BF16)
    vt_all[KA_W:, 0:seq] = vb.astype(F32).T.astype(BF16)
    if cache_refs is not None:
        cka_ref, cva_ref, ckb_ref, cvb_ref = cache_refs
        past = cka_ref.shape[2]
        for h in range(DIFF_HEADS):
            cs = slice(h * LANES, (h + 1) * LANES)
            k_all[seq:, cs] = cka_ref[h].T.astype(BF16)
            vt_all[cs, seq:] = cva_ref[pl.ds(h, past, stride=DIFF_HEADS), :].T.astype(BF16)
        k_all[seq:, KA_W:] = jnp.concatenate([ckb_ref[0], ckb_ref[1]], axis=0).T.astype(BF16)
        for g in range(GQA_KV_HEADS):
            vt_all[KA_W + g * HEAD_DIM:KA_W + (g + 1) * HEAD_DIM, seq:] = cvb_ref[g].astype(BF16)


def _lambda(lq1, lk1, lq2, lk2, lam_init):
    return (jnp.exp(jnp.sum(lq1[...] * lk1[...], axis=-1, keepdims=True))
            - jnp.exp(jnp.sum(lq2[...] * lk2[...], axis=-1, keepdims=True)) + lam_init)


def _attention_tasks(qa, qb, k_all, vt_all, lam, subln, lam_init, store, denoms=None):
    tq = qa.shape[0]
    qat = qa.astype(F32).T.astype(BF16)
    qbt = qb.astype(F32).T.astype(BF16)
    exact = denoms is None
    tasks = []

    def first(st):
        return _numerators(st, exact)

    def pv(vt, e):
        num, l = _pv(vt, e)
        if not exact:
            denoms.append(l)
        return num, 1.0 / l

    def diff_finish(e, h, cs):
        num, r = pv(vt_all[cs, :], e)
        ot = num[:, :tq] * r[:, :tq] - num[:, tq:] * (r[:, tq:] * lam)
        ms = jnp.mean(ot * ot, axis=0, keepdims=True)
        ot = ot * lax.rsqrt(ms + EPS) * subln * (1.0 - lam_init)
        store(cs, ot.T)

    def gqa_finish(e, g, os):
        num, r = pv(vt_all[KA_W + g * HEAD_DIM:KA_W + (g + 1) * HEAD_DIM, :], e)
        ot = num * r
        store(os, jnp.concatenate([ot[:, :tq], ot[:, tq:]], axis=0).T)

    for h in range(DIFF_HEADS):
        cs = slice(h * LANES, (h + 1) * LANES)

        def logits(h=h, cs=cs):
            c0 = qat[h * LANES:h * LANES + HEAD_DIM]
            c1 = qat[h * LANES + HEAD_DIM:(h + 1) * LANES]
            return first(_dot(k_all[:, cs], jnp.concatenate([_pad_rows(c0, 0), _pad_rows(c1, 1)], axis=1)))

        tasks.append((logits, lambda e, h=h, cs=cs: diff_finish(e, h, cs)))
    for g in range(GQA_KV_HEADS):
        for rp in range(2):
            r0 = (GQA_REP * g + 2 * rp) * HEAD_DIM
            os = slice(KA_W + r0, KA_W + r0 + LANES)

            def logits(g=g, r0=r0):
                a = qbt[r0:r0 + HEAD_DIM]
                b = qbt[r0 + HEAD_DIM:r0 + LANES]
                return first(_dot(k_all[:, KA_W:], jnp.concatenate([_pad_rows(a, g), _pad_rows(b, g)], axis=1)))

            tasks.append((logits, lambda e, g=g, os=os: gqa_finish(e, g, os)))
    return tasks


LOGITS_AHEAD = 2
GUARDED_AHEAD = 4
N_TASKS = DIFF_HEADS + 2 * GQA_KV_HEADS


def _attention_units(tasks, ahead=LOGITS_AHEAD):
    pending = []

    def prefill(t):
        pending.append(t[0]())

    def step(i):
        if i + ahead < len(tasks):
            pending.append(tasks[i + ahead][0]())
        tasks[i][1](pending.pop(0))

    units = [functools.partial(prefill, t) for t in tasks[:ahead]]
    return units + [functools.partial(step, i) for i in range(len(tasks))]


def _ffn_units(r, x_ref, mix_ref, mod_ref, wo_ref, nf_ref, wgu_ref, wd_ref, o_ref):
    g_a = mod_ref[:, 2 * D_MODEL:3 * D_MODEL]
    sh_f = mod_ref[:, 3 * D_MODEL:4 * D_MODEL]
    sc_f = mod_ref[:, 4 * D_MODEL:5 * D_MODEL]
    g_f = mod_ref[:, 5 * D_MODEL:6 * D_MODEL]
    state = {}

    def attn_residual():
        state["x"] = x_ref[r, :] + g_a * _dot(mix_ref[r, :], wo_ref[...])

    def gate():
        state["h"] = (_rms(state["x"], nf_ref[...]) * (1.0 + sc_f) + sh_f).astype(BF16)
        state["gate"] = _dot(state["h"], wgu_ref[:, :FFN_HIDDEN])

    def up():
        g = state.pop("gate")
        state["act"] = (g * jax.nn.sigmoid(g) * _dot(state.pop("h"), wgu_ref[:, FFN_HIDDEN:])).astype(BF16)

    def ffn_residual():
        o_ref[r, :] = state.pop("x") + g_f * _dot(state.pop("act"), wd_ref[...])

    return [attn_residual, gate, up, ffn_residual]


def _interleave(a, b):
    out, ia, ib = [], 0, 0
    while ia < len(a) or ib < len(b):
        if ib >= len(b) or (ia < len(a) and ia * len(b) <= ib * len(a)):
            out.append(a[ia])
            ia += 1
        else:
            out.append(b[ib])
            ib += 1
    return out


def _attn_ffn_kernel(lam_init, sub, *refs):
    (qa_ref, qb_ref, ka_ref, va_ref, kb_ref, vb_ref, lq1, lk1, lq2, lk2, subln_ref,
     x_ref, mod_ref, wo_ref, nf_ref, wgu_ref, wd_ref, o_ref, k_all, vt_all, mix) = refs
    seq = ka_ref.shape[0] // sub
    lam = _lambda(lq1, lk1, lq2, lk2, lam_init)
    subln = subln_ref[...]
    attn, ffn = [], []
    for j in range(sub):
        r = slice(j * seq, (j + 1) * seq)

        later = []

        def store(cols, tile, r=r):
            mix[r, cols] = tile.astype(mix.dtype)

        def stage(j=j, r=r, later=later, store=store):
            _stage_kv(k_all.at[j], vt_all.at[j], ka_ref[r, :], va_ref[r, :], kb_ref[r, :], vb_ref[r, :], None)
            later.extend(_attention_units(_attention_tasks(
                qa_ref[r, :], qb_ref[r, :], k_all.at[j], vt_all.at[j], lam, subln, lam_init, store)))

        attn.append([stage] + [lambda k=k, later=later: later[k]() for k in range(LOGITS_AHEAD + N_TASKS)])
        ffn.append(_ffn_units(r, x_ref, mix, mod_ref, wo_ref, nf_ref, wgu_ref, wd_ref, o_ref))
    units = attn[0]
    for j in range(1, sub):
        units = units + _interleave(attn[j], ffn[j - 1])
    for unit in units + ffn[sub - 1]:
        unit()


def _attn_ffn(layer, x, qa, qb, own, seq, lam_vecs, subln, mod4, w_out, norm_ffn, w_gu, w_down):
    n = x.shape[0]
    sub = 4
    tm = sub * seq
    lam_init = 0.8 - 0.6 * math.exp(-0.3 * layer)
    tok = lambda w: pl.BlockSpec((tm, w), lambda i: (i, 0))
    kv_w = own[0].shape[1] + own[2].shape[1]
    in_specs = ([tok(512), tok(512)] + [tok(a.shape[1]) for a in own]
                + [_layer_spec((1, HEAD_DIM), layer)] * 4 + [_layer_spec((2 * HEAD_DIM, 1), layer)]
                + [tok(D_MODEL), _mod_spec(layer, False, 1),
                   _const_spec((D_MODEL, D_MODEL)), _layer_spec((1, D_MODEL), layer),
                   _const_spec((D_MODEL, 2 * FFN_HIDDEN)), _const_spec((FFN_HIDDEN, D_MODEL))])
    return pl.pallas_call(
        functools.partial(_attn_ffn_kernel, lam_init, sub),
        grid=(n // tm,),
        in_specs=in_specs,
        out_specs=tok(D_MODEL),
        out_shape=jax.ShapeDtypeStruct((n, D_MODEL), F32),
        scratch_shapes=[pltpu.VMEM((sub, seq, kv_w), BF16), pltpu.VMEM((sub, kv_w, seq), BF16),
                        pltpu.VMEM((tm, D_MODEL), BF16)],
        compiler_params=pltpu.CompilerParams(
            dimension_semantics=("arbitrary",), vmem_limit_bytes=VMEM_LIMIT),
        name="attn_ffn_context",
    )(qa, qb, *own, *lam_vecs, subln, x, mod4, w_out, norm_ffn, w_gu, w_down)


def _attn_ffn_lagged_kernel(lam_init, tiles_per_request, n_tiles, *refs):
    (qa_ref, qb_ref, ka_ref, va_ref, kb_ref, vb_ref, cka_ref, cva_ref, ckb_ref, cvb_ref,
     lq1, lk1, lq2, lk2, subln_ref, x_ref, mod_ref, wo_ref, nf_ref, wgu_ref, wd_ref,
     o_ref, k_all, vt_all, mix, redo) = refs
    s = pl.program_id(0)
    tq = qa_ref.shape[0]
    lam = _lambda(lq1, lk1, lq2, lk2, lam_init)
    subln = subln_ref[...]

    def attention_units(slot, exact=False):
        def store(cols, tile):
            mix[slot, :, cols] = tile.astype(mix.dtype)

        denoms = None if exact else []
        units = _attention_units(_attention_tasks(
            qa_ref[...], qb_ref[...], k_all, vt_all, lam, subln, lam_init, store, denoms),
            LOGITS_AHEAD if exact else GUARDED_AHEAD)
        if exact:
            return units

        def check():
            redo[0] = _denominators_bad(denoms)

        return units + [check]

    def ffn_units(slot):
        return _ffn_units(slice(0, tq), x_ref, mix.at[slot], mod_ref, wo_ref, nf_ref, wgu_ref, wd_ref, o_ref)

    @pl.when(jnp.logical_and(s % tiles_per_request == 0, s < n_tiles))
    def _():
        _stage_kv(k_all, vt_all, ka_ref[...], va_ref[...], kb_ref[...], vb_ref[...],
                  (cka_ref, cva_ref, ckb_ref, cvb_ref))

    @pl.when(s == 0)
    def _():
        for unit in attention_units(0):
            unit()

    @pl.when(jnp.logical_and(s > 0, s < n_tiles))
    def _():
        for unit in _interleave(attention_units(s % 2), ffn_units((s - 1) % 2)):
            unit()

    @pl.when(jnp.logical_and(s < n_tiles, redo[0] != 0))
    def _():
        for unit in attention_units(s % 2, exact=True):
            unit()

    @pl.when(s == n_tiles)
    def _():
        for unit in ffn_units((n_tiles - 1) % 2):
            unit()


def _attn_ffn_lagged(layer, x, qa, qb, own, cache, seq, lam_vecs, subln, mod4, w_out, norm_ffn, w_gu, w_down):
    n = x.shape[0]
    tq = Q_TILE
    tpr = seq // tq
    n_tiles = n // tq
    lam_init = 0.8 - 0.6 * math.exp(-0.3 * layer)
    att = lambda s: jnp.minimum(s, n_tiles - 1)
    ffn = lambda s: jnp.maximum(s - 1, 0)
    once = pl.Buffered(1)
    in_specs = [pl.BlockSpec((tq, 512), lambda s: (att(s), 0))] * 2
    in_specs += [pl.BlockSpec((seq, a.shape[1]), lambda s: (att(s) // tpr, 0), pipeline_mode=once) for a in own]
    for a in cache:
        nd = a.ndim - 2
        in_specs.append(pl.BlockSpec((None, None) + a.shape[2:],
                                     lambda s, nd=nd: (att(s) // tpr, layer) + (0,) * nd, pipeline_mode=once))
    in_specs += [_layer_spec((1, HEAD_DIM), layer)] * 4 + [_layer_spec((2 * HEAD_DIM, 1), layer)]
    in_specs += [pl.BlockSpec((tq, D_MODEL), lambda s: (ffn(s), 0)),
                 pl.BlockSpec((None, None, 1, 6 * D_MODEL), lambda s: (layer, 1 + ffn(s) // tpr, 0, 0)),
                 _const_spec((D_MODEL, D_MODEL)), _layer_spec((1, D_MODEL), layer),
                 _const_spec((D_MODEL, 2 * FFN_HIDDEN)), _const_spec((FFN_HIDDEN, D_MODEL))]
    keys = seq + cache[0].shape[-1]
    kv_w = own[0].shape[1] + own[2].shape[1]
    return pl.pallas_call(
        functools.partial(_attn_ffn_lagged_kernel, lam_init, tpr, n_tiles),
        grid=(n_tiles + 1,),
        in_specs=in_specs,
        out_specs=pl.BlockSpec((tq, D_MODEL), lambda s: (ffn(s), 0)),
        out_shape=jax.ShapeDtypeStruct((n, D_MODEL), F32),
        scratch_shapes=[pltpu.VMEM((keys, kv_w), BF16), pltpu.VMEM((kv_w, keys), BF16),
                        pltpu.VMEM((2, tq, D_MODEL), BF16), pltpu.SMEM((1,), jnp.int32)],
        compiler_params=pltpu.CompilerParams(
            dimension_semantics=("arbitrary",), vmem_limit_bytes=VMEM_LIMIT),
        name="attn_ffn_latent",
    )(qa, qb, *own, *cache, *lam_vecs, subln, x, mod4, w_out, norm_ffn, w_gu, w_down)


def _rope_tables(n_tokens):
    t = jnp.arange(n_tokens, dtype=jnp.int32)
    row = (t // GRID_W).astype(F32)
    col = (t % GRID_W).astype(F32)
    axis_dim = HEAD_DIM // 2
    freqs = ROPE_THETA ** (-jnp.arange(0, axis_dim, 2, dtype=F32) / axis_dim)
    ang = jnp.concatenate([row[:, None] * freqs, col[:, None] * freqs], axis=-1)
    c, s = jnp.cos(ang), jnp.sin(ang)
    return jnp.concatenate([c, c, c, c], axis=-1), jnp.concatenate([-s, s, -s, s], axis=-1)


def kernel(x_prompt, x_sample, cache_diff_k, cache_diff_v, cache_gqa_k, cache_gqa_v, c, c_ctx, w_mod, b_mod, norm_attn, w_in, q_norm_a, k_norm_a, lambda_q1, lambda_k1, lambda_q2, lambda_k2, subln, q_norm_b, k_norm_b, w_out, norm_ffn, w_gate_up, w_down):
    batch, seq, d = x_prompt.shape
    dec_batch, dec_seq, _ = x_sample.shape
    depth = w_mod.shape[0]
    past = cache_diff_k.shape[2]

    cond = jnp.concatenate([c_ctx[None], c, jnp.zeros((MOD_ROWS - 1 - dec_batch, d), F32)], axis=0)
    mod = _modulation(cond, w_mod, b_mod)
    mod4 = mod.reshape(depth, MOD_ROWS, 1, 6 * d)

    w_in_b = w_in.astype(BF16)
    seg_ones = jnp.kron(jnp.eye(256 // HEAD_DIM, dtype=F32), jnp.ones((HEAD_DIM, HEAD_DIM), F32)).astype(BF16)
    rope = _rope_tables(dec_seq)

    cache = (jnp.transpose(cache_diff_k, (0, 1, 3, 4, 5, 2)).reshape(dec_batch, depth, DIFF_HEADS, 2 * HEAD_DIM, past),
             cache_diff_v.reshape(dec_batch, depth, past * DIFF_HEADS, 2 * HEAD_DIM),
             jnp.transpose(cache_gqa_k, (0, 1, 3, 4, 2)),
             jnp.transpose(cache_gqa_v, (0, 1, 3, 4, 2)))

    row = lambda a: a.reshape(depth, 1, a.shape[-1])
    gains = (row(q_norm_a), row(k_norm_a), row(q_norm_b), row(k_norm_b))
    lam_vecs = (row(lambda_q1), row(lambda_k1), row(lambda_q2), row(lambda_k2))
    na, nf, sl = row(norm_attn), row(norm_ffn), subln.reshape(depth, 2 * HEAD_DIM, 1)

    yp = x_prompt.reshape(batch * seq, d)
    ys = x_sample.reshape(dec_batch * dec_seq, d)
    lat_tiles = dec_seq // TOKEN_TILE
    new_caches = None
    for l in range(depth):
        qa, ka, va, qb, kb, vb, *rest = _qkv(
            yp, mod4, l, False, seq // TOKEN_TILE, na, w_in_b, seg_ones, gains, None, new_caches,
            (w_out, w_gate_up, w_down))
        new_caches, (w_out_b, w_gu_b, w_down_b) = rest[:4], rest[4:]
        yp = _attn_ffn(l, yp, qa, qb, (ka, va, kb, vb), seq, lam_vecs, sl, mod4, w_out_b, nf, w_gu_b, w_down_b)

        qa, ka, va, qb, kb, vb = _qkv(
            ys, mod4, l, True, lat_tiles, na, w_in_b, seg_ones, gains, rope)
        ys = _attn_ffn_lagged(l, ys, qa, qb, (ka, va, kb, vb), cache, dec_seq, lam_vecs, sl,
                              mod4, w_out_b, nf, w_gu_b, w_down_b)

    ka_t, va_n, kb_t, vb_t = new_caches
    new_diff_k = jnp.transpose(ka_t.reshape(batch, depth, DIFF_HEADS, 2, HEAD_DIM, seq), (0, 1, 5, 2, 3, 4))
    new_gqa_k = jnp.transpose(kb_t.reshape(batch, depth, GQA_KV_HEADS, HEAD_DIM, seq), (0, 1, 4, 2, 3))
    new_gqa_v = jnp.transpose(vb_t.reshape(batch, depth, GQA_KV_HEADS, HEAD_DIM, seq), (0, 1, 4, 2, 3))
    return (yp.reshape(batch, seq, d), ys.reshape(dec_batch, dec_seq, d),
            new_diff_k, va_n.reshape(batch, depth, seq, DIFF_HEADS, 2 * HEAD_DIM), new_gqa_k, new_gqa_v)
```

```python
import functools
import math

import jax
import jax.numpy as jnp
from jax import lax
from jax.experimental import pallas as pl
from jax.experimental.pallas import tpu as pltpu

F32 = jnp.float32
BF16 = jnp.bfloat16

D_MODEL = 1024
HEAD_DIM = 64
GRID_W = 64
DIFF_HEADS = 4
GQA_KV_HEADS = 2
GQA_REP = 4
FFN_HIDDEN = 2816
IN_COLS = 2304
ROPE_THETA = 10000.0
EPS = 1e-6
LANES = 128
MOD_ROWS = 8
TOKEN_TILE = 256
Q_TILE = 256
VMEM_LIMIT = 56 * 1024 * 1024


def _dot(a, b):
    return jnp.dot(a, b, preferred_element_type=F32)


def _rms(x, gain):
    ms = jnp.mean(x * x, axis=-1, keepdims=True)
    return x * lax.rsqrt(ms + EPS) * gain


def _layer_spec(shape, layer):
    n = len(shape)
    return pl.BlockSpec((None,) + tuple(shape), lambda *_: (layer,) + (0,) * n,
                        pipeline_mode=pl.Buffered(1))


def _const_spec(shape):
    n = len(shape)
    return pl.BlockSpec(shape, lambda *_: (0,) * n, pipeline_mode=pl.Buffered(1))


def _mod_spec(layer, latent, tiles_per_request):
    if latent:
        return pl.BlockSpec((None, None, 1, 6 * D_MODEL),
                            lambda i: (layer, 1 + i // tiles_per_request, 0, 0))
    return pl.BlockSpec((None, None, 1, 6 * D_MODEL), lambda i: (layer, 0, 0, 0))


def _mod_kernel(cond_ref, w_ref, b_ref, o_ref):
    c = cond_ref[...]
    s = c * jax.nn.sigmoid(c)
    o_ref[...] = _dot(s.astype(BF16), w_ref[...].astype(BF16)) + b_ref[...]


def _modulation(cond, w_mod, b_mod):
    depth = w_mod.shape[0]
    tn = 1536
    return pl.pallas_call(
        _mod_kernel,
        grid=(depth, 6 * D_MODEL // tn),
        in_specs=[
            pl.BlockSpec((MOD_ROWS, D_MODEL), lambda l, j: (0, 0)),
            pl.BlockSpec((None, D_MODEL, tn), lambda l, j: (l, 0, j)),
            pl.BlockSpec((None, 1, tn), lambda l, j: (l, 0, j)),
        ],
        out_specs=pl.BlockSpec((None, MOD_ROWS, tn), lambda l, j: (l, 0, j)),
        out_shape=jax.ShapeDtypeStruct((depth, MOD_ROWS, 6 * D_MODEL), F32),
        compiler_params=pltpu.CompilerParams(
            dimension_semantics=("arbitrary", "arbitrary"), vmem_limit_bytes=VMEM_LIMIT),
        name="modulation",
    )(cond, w_mod, b_mod.reshape(depth, 1, 6 * D_MODEL))


def _head_norm(x, seg_ones, gain):
    ss = _dot((x * x).astype(BF16), seg_ones)
    return x * lax.rsqrt(ss * (1.0 / HEAD_DIM) + EPS) * gain


def _swap_halves(x, first_half):
    return jnp.where(first_half, pltpu.roll(x, 96, 1), pltpu.roll(x, 32, 1))


def _rope(x, cos, sin, first_half):
    out = []
    for j in range(x.shape[1] // LANES):
        xj = x[:, j * LANES:(j + 1) * LANES]
        out.append(xj * cos + _swap_halves(xj, first_half) * sin)
    return out[0] if len(out) == 1 else jnp.concatenate(out, axis=1)


def _tile_gain(g):
    return jnp.concatenate([g] * (256 // HEAD_DIM), axis=1)


def _qkv_kernel(latent, n_cast, *refs):
    if latent:
        (x_ref, mod_ref, na_ref, w_ref, seg_ref, gqa_ref, gka_ref, gqb_ref, gkb_ref,
         cos_ref, sin_ref, qa_o, ka_o, va_o, qb_o, kb_o, vb_o) = refs
    else:
        (x_ref, mod_ref, na_ref, w_ref, seg_ref, gqa_ref, gka_ref, gqb_ref, gkb_ref) = refs[:9]
        outs = refs[len(refs) - 10 - n_cast:]
        qa_o, ka_o, va_o, qb_o, kb_o, vb_o, kaf_o, vaf_o, kbf_o, vbf_o = outs[:10]
        for src, dst in zip(refs[9:9 + n_cast], outs[10:]):
            dst[...] = src[...].astype(BF16)
    tm = TOKEN_TILE
    sh = mod_ref[:, 0:D_MODEL]
    sc = mod_ref[:, D_MODEL:2 * D_MODEL]
    seg = seg_ref[...]
    lane = lax.broadcasted_iota(jnp.int32, (1, LANES), 1)
    first_half = (lane % HEAD_DIM) < (HEAD_DIM // 2)
    scale = HEAD_DIM ** -0.5 * math.log2(math.e)

    def project(r):
        h = _rms(x_ref[r, :], na_ref[...]) * (1.0 + sc) + sh
        return _dot(h.astype(BF16), w_ref[...])

    def finish(j, r, proj):
        def qk(lo_col, width, gain_ref):
            gain = _tile_gain(gain_ref[...])
            out = []
            for c0 in range(0, width, 256):
                w = min(256, width - c0)
                out.append(_head_norm(proj[:, lo_col + c0:lo_col + c0 + w], seg[:w, :w], gain[:, :w]))
            y = out[0] if len(out) == 1 else jnp.concatenate(out, axis=1)
            if latent:
                y = _rope(y, cos_ref[r, :], sin_ref[r, :], first_half)
            return y

        qa = qk(0, 512, gqa_ref)
        ka = qk(512, 512, gka_ref)
        va = proj[:, 1024:1536]
        qb = qk(1536, 512, gqb_ref)
        kb = qk(2048, 128, gkb_ref)
        vb = proj[:, 2176:2304]
        qa_o[r, :] = (qa * scale).astype(BF16)
        ka_o[r, :] = ka.astype(BF16)
        va_o[r, :] = va.astype(BF16)
        qb_o[r, :] = (qb * scale).astype(BF16)
        kb_o[r, :] = kb.astype(BF16)
        vb_o[r, :] = vb.astype(BF16)
        if not latent:
            kaf_o[j] = ka.T
            kbf_o[j] = kb.T
            vbf_o[j] = vb.T
            for h in range(DIFF_HEADS):
                vaf_o[j, pl.ds(h, tm, stride=DIFF_HEADS), :] = va[:, h * LANES:(h + 1) * LANES]

    rows = [slice(j * tm, (j + 1) * tm) for j in range(x_ref.shape[0] // tm)]
    pending = [project(rows[0])]
    for j, r in enumerate(rows):
        if j + 1 < len(rows):
            pending.append(project(rows[j + 1]))
        finish(j, r, pending.pop(0))


def _qkv(x, mod4, layer, latent, tiles_per_request, norm_attn, w_in, seg_ones, gains, rope,
         prev_caches=None, cast_weights=()):
    n = x.shape[0]
    sub = 4
    tm = sub * TOKEN_TILE
    depth = w_in.shape[0]
    tok = lambda w: pl.BlockSpec((tm, w), lambda i: (i, 0))
    in_specs = [
        tok(D_MODEL),
        _mod_spec(layer, latent, max(tiles_per_request // sub, 1)),
        _layer_spec((1, D_MODEL), layer),
        _layer_spec((D_MODEL, IN_COLS), layer),
        _const_spec((256, 256)),
    ] + [_layer_spec((1, HEAD_DIM), layer)] * 4
    args = [x, mod4, norm_attn, w_in, seg_ones, *gains]
    aliases = {}
    out_specs = [tok(512), tok(512), tok(512), tok(512), tok(LANES), tok(LANES)]
    out_shape = [jax.ShapeDtypeStruct((n, w), BF16) for w in (512, 512, 512, 512, LANES, LANES)]
    if latent:
        rope_spec = pl.BlockSpec((tm, LANES), lambda i: (i % (tiles_per_request // sub), 0))
        in_specs += [rope_spec, rope_spec]
        args += list(rope)
    else:
        assert tiles_per_request == 1, "context requests must be one token tile long"
        seq = TOKEN_TILE
        req = n // seq
        slab = lambda *dims: pl.BlockSpec((sub, None) + dims, lambda i: (i, layer) + (0,) * len(dims))
        out_specs += [slab(512, seq), slab(seq * DIFF_HEADS, LANES), slab(LANES, seq), slab(LANES, seq)]
        out_shape += [jax.ShapeDtypeStruct((req, depth, 512, seq), F32),
                      jax.ShapeDtypeStruct((req, depth, seq * DIFF_HEADS, LANES), F32),
                      jax.ShapeDtypeStruct((req, depth, LANES, seq), F32),
                      jax.ShapeDtypeStruct((req, depth, LANES, seq), F32)]
        steps = n // tm
        for w in cast_weights:
            rows, cols = w.shape[1] // steps, w.shape[2]
            assert rows * steps == w.shape[1] and rows % 16 == 0
            in_specs.append(pl.BlockSpec((None, rows, cols), lambda i: (layer, i, 0)))
            args.append(w)
            out_specs.append(pl.BlockSpec((rows, cols), lambda i: (i, 0)))
            out_shape.append(jax.ShapeDtypeStruct(w.shape[1:], BF16))
        if prev_caches is not None:
            aliases = {len(args) + k: 6 + k for k in range(4)}
            in_specs += [pl.BlockSpec(memory_space=pl.ANY)] * 4
            args += list(prev_caches)
    return pl.pallas_call(
        functools.partial(_qkv_kernel, latent, len(cast_weights)),
        grid=(n // tm,),
        in_specs=in_specs,
        out_specs=out_specs,
        out_shape=out_shape,
        input_output_aliases=aliases,
        compiler_params=pltpu.CompilerParams(
            dimension_semantics=("arbitrary",), vmem_limit_bytes=VMEM_LIMIT),
        name="qkv_latent" if latent else "qkv_context",
    )(*args)


def _pad_rows(x, block):
    z = jnp.zeros_like(x)
    return jnp.concatenate([x, z] if block == 0 else [z, x], axis=0)


ONES_ROWS = 16
SHIFT_KEYS = 8
DENOM_RANGE = (2.0 ** -100, 2.0 ** 100)


def _numerators(st, exact):
    shift = (st if exact else st[:SHIFT_KEYS]).max(axis=0, keepdims=True)
    return jnp.exp2(st - shift).astype(BF16)


def _pv(vt, e):
    d = vt.shape[0]
    ext = _dot(jnp.concatenate([vt, jnp.ones((ONES_ROWS, vt.shape[1]), BF16)], axis=0), e)
    return ext[:d], ext[d:d + 1]


def _denominators_bad(denoms):
    bad = None
    for l in denoms:
        b = jnp.where(jnp.logical_and(l > DENOM_RANGE[0], l < DENOM_RANGE[1]), 0.0, 1.0)
        bad = b if bad is None else jnp.maximum(bad, b)
    return (jnp.max(bad) > 0.0).astype(jnp.int32)


KA_W = DIFF_HEADS * LANES


def _stage_kv(k_all, vt_all, ka, va, kb, vb, cache_refs):
    seq = ka.shape[0]
    k_all[0:seq, 0:KA_W] = ka
    k_all[0:seq, KA_W:] = kb
    vt_all[0:KA_W, 0:seq] = va.astype(F32).T.astype(BF16)
    vt_all[KA_W:, 0:seq] = vb.astype(F32).T.astype(BF16)
    if cache_refs is not None:
        cka_ref, cva_ref, ckb_ref, cvb_ref = cache_refs
        past = cka_ref.shape[2]
        for h in range(DIFF_HEADS):
            cs = slice(h * LANES, (h + 1) * LANES)
            k_all[seq:, cs] = cka_ref[h].T.astype(BF16)
            vt_all[cs, seq:] = cva_ref[pl.ds(h, past, stride=DIFF_HEADS), :].T.astype(BF16)
        k_all[seq:, KA_W:] = jnp.concatenate([ckb_ref[0], ckb_ref[1]], axis=0).T.astype(BF16)
        for g in range(GQA_KV_HEADS):
            vt_all[KA_W + g * HEAD_DIM:KA_W + (g + 1) * HEAD_DIM, seq:] = cvb_ref[g].astype(BF16)


def _lambda(lq1, lk1, lq2, lk2, lam_init):
    return (jnp.exp(jnp.sum(lq1[...] * lk1[...], axis=-1, keepdims=True))
            - jnp.exp(jnp.sum(lq2[...] * lk2[...], axis=-1, keepdims=True)) + lam_init)


def _attention_tasks(qa, qb, k_all, vt_all, lam, subln, lam_init, store, denoms=None):
    tq = qa.shape[0]
    qat = qa.astype(F32).T.astype(BF16)
    qbt = qb.astype(F32).T.astype(BF16)
    exact = denoms is None
    tasks = []

    def first(st):
        return _numerators(st, exact)

    def pv(vt, e):
        num, l = _pv(vt, e)
        if not exact:
            denoms.append(l)
        return num, 1.0 / l

    def diff_finish(e, h, cs):
        num, r = pv(vt_all[cs, :], e)
        ot = num[:, :tq] * r[:, :tq] - num[:, tq:] * (r[:, tq:] * lam)
        ms = jnp.mean(ot * ot, axis=0, keepdims=True)
        ot = ot * lax.rsqrt(ms + EPS) * subln * (1.0 - lam_init)
        store(cs, ot.T)

    def gqa_finish(e, g, os):
        num, r = pv(vt_all[KA_W + g * HEAD_DIM:KA_W + (g + 1) * HEAD_DIM, :], e)
        ot = num * r
        store(os, jnp.concatenate([ot[:, :tq], ot[:, tq:]], axis=0).T)

    for h in range(DIFF_HEADS):
        cs = slice(h * LANES, (h + 1) * LANES)

        def logits(h=h, cs=cs):
            c0 = qat[h * LANES:h * LANES + HEAD_DIM]
            c1 = qat[h * LANES + HEAD_DIM:(h + 1) * LANES]
            return first(_dot(k_all[:, cs], jnp.concatenate([_pad_rows(c0, 0), _pad_rows(c1, 1)], axis=1)))

        tasks.append((logits, lambda e, h=h, cs=cs: diff_finish(e, h, cs)))
    for g in range(GQA_KV_HEADS):
        for rp in range(2):
            r0 = (GQA_REP * g + 2 * rp) * HEAD_DIM
            os = slice(KA_W + r0, KA_W + r0 + LANES)

            def logits(g=g, r0=r0):
                a = qbt[r0:r0 + HEAD_DIM]
                b = qbt[r0 + HEAD_DIM:r0 + LANES]
                return first(_dot(k_all[:, KA_W:], jnp.concatenate([_pad_rows(a, g), _pad_rows(b, g)], axis=1)))

            tasks.append((logits, lambda e, g=g, os=os: gqa_finish(e, g, os)))
    return tasks


LOGITS_AHEAD = 2
GUARDED_AHEAD = 5
N_TASKS = DIFF_HEADS + 2 * GQA_KV_HEADS


def _attention_units(tasks, ahead=LOGITS_AHEAD):
    pending = []

    def prefill(t):
        pending.append(t[0]())

    def step(i):
        if i + ahead < len(tasks):
            pending.append(tasks[i + ahead][0]())
        tasks[i][1](pending.pop(0))

    units = [functools.partial(prefill, t) for t in tasks[:ahead]]
    return units + [functools.partial(step, i) for i in range(len(tasks))]


def _ffn_units(r, x_ref, mix_ref, mod_ref, wo_ref, nf_ref, wgu_ref, wd_ref, o_ref):
    g_a = mod_ref[:, 2 * D_MODEL:3 * D_MODEL]
    sh_f = mod_ref[:, 3 * D_MODEL:4 * D_MODEL]
    sc_f = mod_ref[:, 4 * D_MODEL:5 * D_MODEL]
    g_f = mod_ref[:, 5 * D_MODEL:6 * D_MODEL]
    state = {}

    def attn_residual():
        state["x"] = x_ref[r, :] + g_a * _dot(mix_ref[r, :], wo_ref[...])

    def gate():
        state["h"] = (_rms(state["x"], nf_ref[...]) * (1.0 + sc_f) + sh_f).astype(BF16)
        state["gate"] = _dot(state["h"], wgu_ref[:, :FFN_HIDDEN])

    def up():
        g = state.pop("gate")
        state["act"] = (g * jax.nn.sigmoid(g) * _dot(state.pop("h"), wgu_ref[:, FFN_HIDDEN:])).astype(BF16)

    def ffn_residual():
        o_ref[r, :] = state.pop("x") + g_f * _dot(state.pop("act"), wd_ref[...])

    return [attn_residual, gate, up, ffn_residual]


def _interleave(a, b):
    out, ia, ib = [], 0, 0
    while ia < len(a) or ib < len(b):
        if ib >= len(b) or (ia < len(a) and ia * len(b) <= ib * len(a)):
            out.append(a[ia])
            ia += 1
        else:
            out.append(b[ib])
            ib += 1
    return out


def _attn_ffn_kernel(lam_init, sub, *refs):
    (qa_ref, qb_ref, ka_ref, va_ref, kb_ref, vb_ref, lq1, lk1, lq2, lk2, subln_ref,
     x_ref, mod_ref, wo_ref, nf_ref, wgu_ref, wd_ref, o_ref, k_all, vt_all, mix) = refs
    seq = ka_ref.shape[0] // sub
    lam = _lambda(lq1, lk1, lq2, lk2, lam_init)
    subln = subln_ref[...]
    attn, ffn = [], []
    for j in range(sub):
        r = slice(j * seq, (j + 1) * seq)

        later = []

        def store(cols, tile, r=r):
            mix[r, cols] = tile.astype(mix.dtype)

        def stage(j=j, r=r, later=later, store=store):
            _stage_kv(k_all.at[j], vt_all.at[j], ka_ref[r, :], va_ref[r, :], kb_ref[r, :], vb_ref[r, :], None)
            later.extend(_attention_units(_attention_tasks(
                qa_ref[r, :], qb_ref[r, :], k_all.at[j], vt_all.at[j], lam, subln, lam_init, store)))

        attn.append([stage] + [lambda k=k, later=later: later[k]() for k in range(LOGITS_AHEAD + N_TASKS)])
        ffn.append(_ffn_units(r, x_ref, mix, mod_ref, wo_ref, nf_ref, wgu_ref, wd_ref, o_ref))
    units = attn[0]
    for j in range(1, sub):
        units = units + _interleave(attn[j], ffn[j - 1])
    for unit in units + ffn[sub - 1]:
        unit()


def _attn_ffn(layer, x, qa, qb, own, seq, lam_vecs, subln, mod4, w_out, norm_ffn, w_gu, w_down):
    n = x.shape[0]
    sub = 4
    tm = sub * seq
    lam_init = 0.8 - 0.6 * math.exp(-0.3 * layer)
    tok = lambda w: pl.BlockSpec((tm, w), lambda i: (i, 0))
    kv_w = own[0].shape[1] + own[2].shape[1]
    in_specs = ([tok(512), tok(512)] + [tok(a.shape[1]) for a in own]
                + [_layer_spec((1, HEAD_DIM), layer)] * 4 + [_layer_spec((2 * HEAD_DIM, 1), layer)]
                + [tok(D_MODEL), _mod_spec(layer, False, 1),
                   _const_spec((D_MODEL, D_MODEL)), _layer_spec((1, D_MODEL), layer),
                   _const_spec((D_MODEL, 2 * FFN_HIDDEN)), _const_spec((FFN_HIDDEN, D_MODEL))])
    return pl.pallas_call(
        functools.partial(_attn_ffn_kernel, lam_init, sub),
        grid=(n // tm,),
        in_specs=in_specs,
        out_specs=tok(D_MODEL),
        out_shape=jax.ShapeDtypeStruct((n, D_MODEL), F32),
        scratch_shapes=[pltpu.VMEM((sub, seq, kv_w), BF16), pltpu.VMEM((sub, kv_w, seq), BF16),
                        pltpu.VMEM((tm, D_MODEL), BF16)],
        compiler_params=pltpu.CompilerParams(
            dimension_semantics=("arbitrary",), vmem_limit_bytes=VMEM_LIMIT),
        name="attn_ffn_context",
    )(qa, qb, *own, *lam_vecs, subln, x, mod4, w_out, norm_ffn, w_gu, w_down)


def _attn_ffn_lagged_kernel(lam_init, tiles_per_request, n_tiles, *refs):
    (qa_ref, qb_ref, ka_ref, va_ref, kb_ref, vb_ref, cka_ref, cva_ref, ckb_ref, cvb_ref,
     lq1, lk1, lq2, lk2, subln_ref, x_ref, mod_ref, wo_ref, nf_ref, wgu_ref, wd_ref,
     o_ref, k_all, vt_all, mix, redo) = refs
    s = pl.program_id(0)
    tq = qa_ref.shape[0]
    lam = _lambda(lq1, lk1, lq2, lk2, lam_init)
    subln = subln_ref[...]

    def attention_units(slot, exact=False):
        def store(cols, tile):
            mix[slot, :, cols] = tile.astype(mix.dtype)

        denoms = None if exact else []
        units = _attention_units(_attention_tasks(
            qa_ref[...], qb_ref[...], k_all, vt_all, lam, subln, lam_init, store, denoms),
            LOGITS_AHEAD if exact else GUARDED_AHEAD)
        if exact:
            return units

        def check():
            redo[0] = _denominators_bad(denoms)

        return units + [check]

    def ffn_units(slot):
        return _ffn_units(slice(0, tq), x_ref, mix.at[slot], mod_ref, wo_ref, nf_ref, wgu_ref, wd_ref, o_ref)

    @pl.when(jnp.logical_and(s % tiles_per_request == 0, s < n_tiles))
    def _():
        _stage_kv(k_all, vt_all, ka_ref[...], va_ref[...], kb_ref[...], vb_ref[...],
                  (cka_ref, cva_ref, ckb_ref, cvb_ref))

    @pl.when(s == 0)
    def _():
        for unit in attention_units(0):
            unit()

    @pl.when(jnp.logical_and(s > 0, s < n_tiles))
    def _():
        for unit in _interleave(attention_units(s % 2), ffn_units((s - 1) % 2)):
            unit()

    @pl.when(jnp.logical_and(s < n_tiles, redo[0] != 0))
    def _():
        for unit in attention_units(s % 2, exact=True):
            unit()

    @pl.when(s == n_tiles)
    def _():
        for unit in ffn_units((n_tiles - 1) % 2):
            unit()


def _attn_ffn_lagged(layer, x, qa, qb, own, cache, seq, lam_vecs, subln, mod4, w_out, norm_ffn, w_gu, w_down):
    n = x.shape[0]
    tq = Q_TILE
    tpr = seq // tq
    n_tiles = n // tq
    lam_init = 0.8 - 0.6 * math.exp(-0.3 * layer)
    att = lambda s: jnp.minimum(s, n_tiles - 1)
    ffn = lambda s: jnp.maximum(s - 1, 0)
    once = pl.Buffered(1)
    in_specs = [pl.BlockSpec((tq, 512), lambda s: (att(s), 0))] * 2
    in_specs += [pl.BlockSpec((seq, a.shape[1]), lambda s: (att(s) // tpr, 0), pipeline_mode=once) for a in own]
    for a in cache:
        nd = a.ndim - 2
        in_specs.append(pl.BlockSpec((None, None) + a.shape[2:],
                                     lambda s, nd=nd: (att(s) // tpr, layer) + (0,) * nd, pipeline_mode=once))
    in_specs += [_layer_spec((1, HEAD_DIM), layer)] * 4 + [_layer_spec((2 * HEAD_DIM, 1), layer)]
    in_specs += [pl.BlockSpec((tq, D_MODEL), lambda s: (ffn(s), 0)),
                 pl.BlockSpec((None, None, 1, 6 * D_MODEL), lambda s: (layer, 1 + ffn(s) // tpr, 0, 0)),
                 _const_spec((D_MODEL, D_MODEL)), _layer_spec((1, D_MODEL), layer),
                 _const_spec((D_MODEL, 2 * FFN_HIDDEN)), _const_spec((FFN_HIDDEN, D_MODEL))]
    keys = seq + cache[0].shape[-1]
    kv_w = own[0].shape[1] + own[2].shape[1]
    return pl.pallas_call(
        functools.partial(_attn_ffn_lagged_kernel, lam_init, tpr, n_tiles),
        grid=(n_tiles + 1,),
        in_specs=in_specs,
        out_specs=pl.BlockSpec((tq, D_MODEL), lambda s: (ffn(s), 0)),
        out_shape=jax.ShapeDtypeStruct((n, D_MODEL), F32),
        scratch_shapes=[pltpu.VMEM((keys, kv_w), BF16), pltpu.VMEM((kv_w, keys), BF16),
                        pltpu.VMEM((2, tq, D_MODEL), BF16), pltpu.SMEM((1,), jnp.int32)],
        compiler_params=pltpu.CompilerParams(
            dimension_semantics=("arbitrary",), vmem_limit_bytes=VMEM_LIMIT),
        name="attn_ffn_latent",
    )(qa, qb, *own, *cache, *lam_vecs, subln, x, mod4, w_out, norm_ffn, w_gu, w_down)


def _rope_tables(n_tokens):
    t = jnp.arange(n_tokens, dtype=jnp.int32)
    row = (t // GRID_W).astype(F32)
    col = (t % GRID_W).astype(F32)
    axis_dim = HEAD_DIM // 2
    freqs = ROPE_THETA ** (-jnp.arange(0, axis_dim, 2, dtype=F32) / axis_dim)
    ang = jnp.concatenate([row[:, None] * freqs, col[:, None] * freqs], axis=-1)
    c, s = jnp.cos(ang), jnp.sin(ang)
    return jnp.concatenate([c, c, c, c], axis=-1), jnp.concatenate([-s, s, -s, s], axis=-1)


def kernel(x_prompt, x_sample, cache_diff_k, cache_diff_v, cache_gqa_k, cache_gqa_v, c, c_ctx, w_mod, b_mod, norm_attn, w_in, q_norm_a, k_norm_a, lambda_q1, lambda_k1, lambda_q2, lambda_k2, subln, q_norm_b, k_norm_b, w_out, norm_ffn, w_gate_up, w_down):
    batch, seq, d = x_prompt.shape
    dec_batch, dec_seq, _ = x_sample.shape
    depth = w_mod.shape[0]
    past = cache_diff_k.shape[2]

    cond = jnp.concatenate([c_ctx[None], c, jnp.zeros((MOD_ROWS - 1 - dec_batch, d), F32)], axis=0)
    mod = _modulation(cond, w_mod, b_mod)
    mod4 = mod.reshape(depth, MOD_ROWS, 1, 6 * d)

    w_in_b = w_in.astype(BF16)
    seg_ones = jnp.kron(jnp.eye(256 // HEAD_DIM, dtype=F32), jnp.ones((HEAD_DIM, HEAD_DIM), F32)).astype(BF16)
    rope = _rope_tables(dec_seq)

    cache = (jnp.transpose(cache_diff_k, (0, 1, 3, 4, 5, 2)).reshape(dec_batch, depth, DIFF_HEADS, 2 * HEAD_DIM, past),
             cache_diff_v.reshape(dec_batch, depth, past * DIFF_HEADS, 2 * HEAD_DIM),
             jnp.transpose(cache_gqa_k, (0, 1, 3, 4, 2)),
             jnp.transpose(cache_gqa_v, (0, 1, 3, 4, 2)))

    row = lambda a: a.reshape(depth, 1, a.shape[-1])
    gains = (row(q_norm_a), row(k_norm_a), row(q_norm_b), row(k_norm_b))
    lam_vecs = (row(lambda_q1), row(lambda_k1), row(lambda_q2), row(lambda_k2))
    na, nf, sl = row(norm_attn), row(norm_ffn), subln.reshape(depth, 2 * HEAD_DIM, 1)

    yp = x_prompt.reshape(batch * seq, d)
    ys = x_sample.reshape(dec_batch * dec_seq, d)
    lat_tiles = dec_seq // TOKEN_TILE
    new_caches = None
    for l in range(depth):
        qa, ka, va, qb, kb, vb, *rest = _qkv(
            yp, mod4, l, False, seq // TOKEN_TILE, na, w_in_b, seg_ones, gains, None, new_caches,
            (w_out, w_gate_up, w_down))
        new_caches, (w_out_b, w_gu_b, w_down_b) = rest[:4], rest[4:]
        yp = _attn_ffn(l, yp, qa, qb, (ka, va, kb, vb), seq, lam_vecs, sl, mod4, w_out_b, nf, w_gu_b, w_down_b)

        qa, ka, va, qb, kb, vb = _qkv(
            ys, mod4, l, True, lat_tiles, na, w_in_b, seg_ones, gains, rope)
        ys = _attn_ffn_lagged(l, ys, qa, qb, (ka, va, kb, vb), cache, dec_seq, lam_vecs, sl,
                              mod4, w_out_b, nf, w_gu_b, w_down_b)

    ka_t, va_n, kb_t, vb_t = new_caches
    new_diff_k = jnp.transpose(ka_t.reshape(batch, depth, DIFF_HEADS, 2, HEAD_DIM, seq), (0, 1, 5, 2, 3, 4))
    new_gqa_k = jnp.transpose(kb_t.reshape(batch, depth, GQA_KV_HEADS, HEAD_DIM, seq), (0, 1, 4, 2, 3))
    new_gqa_v = jnp.transpose(vb_t.reshape(batch, depth, GQA_KV_HEADS, HEAD_DIM, seq), (0, 1, 4, 2, 3))
    return (yp.reshape(batch, seq, d), ys.reshape(dec_batch, dec_seq, d),
            new_diff_k, va_n.reshape(batch, depth, seq, DIFF_HEADS, 2 * HEAD_DIM), new_gqa_k, new_gqa_v)
```

```python
import functools
import math

import jax
import jax.numpy as jnp
from jax import lax
from jax.experimental import pallas as pl
from jax.experimental.pallas import tpu as pltpu

F32 = jnp.float32
BF16 = jnp.bfloat16

D_MODEL = 1024
HEAD_DIM = 64
GRID_W = 64
DIFF_HEADS = 4
GQA_KV_HEADS = 2
GQA_REP = 4
FFN_HIDDEN = 2816
IN_COLS = 2304
ROPE_THETA = 10000.0
EPS = 1e-6
LANES = 128
MOD_ROWS = 8
TOKEN_TILE = 256
Q_TILE = 256
VMEM_LIMIT = 60 * 1024 * 1024


def _dot(a, b):
    return jnp.dot(a, b, preferred_element_type=F32)


def _rms(x, gain):
    ms = jnp.mean(x * x, axis=-1, keepdims=True)
    return x * lax.rsqrt(ms + EPS) * gain


def _layer_spec(shape, layer):
    n = len(shape)
    return pl.BlockSpec((None,) + tuple(shape), lambda *_: (layer,) + (0,) * n,
                        pipeline_mode=pl.Buffered(1))


def _const_spec(shape):
    n = len(shape)
    return pl.BlockSpec(shape, lambda *_: (0,) * n, pipeline_mode=pl.Buffered(1))


def _mod_spec(layer, latent, tiles_per_request):
    if latent:
        return pl.BlockSpec((None, None, 1, 6 * D_MODEL),
                            lambda i: (layer, 1 + i // tiles_per_request, 0, 0))
    return pl.BlockSpec((None, None, 1, 6 * D_MODEL), lambda i: (layer, 0, 0, 0))


def _mod_kernel(cond_ref, w_ref, b_ref, o_ref):
    c = cond_ref[...]
    s = c * jax.nn.sigmoid(c)
    o_ref[...] = _dot(s.astype(BF16), w_ref[...].astype(BF16)) + b_ref[...]


def _modulation(cond, w_mod, b_mod):
    depth = w_mod.shape[0]
    tn = 1536
    return pl.pallas_call(
        _mod_kernel,
        grid=(depth, 6 * D_MODEL // tn),
        in_specs=[
            pl.BlockSpec((MOD_ROWS, D_MODEL), lambda l, j: (0, 0)),
            pl.BlockSpec((None, D_MODEL, tn), lambda l, j: (l, 0, j)),
            pl.BlockSpec((None, 1, tn), lambda l, j: (l, 0, j)),
        ],
        out_specs=pl.BlockSpec((None, MOD_ROWS, tn), lambda l, j: (l, 0, j)),
        out_shape=jax.ShapeDtypeStruct((depth, MOD_ROWS, 6 * D_MODEL), F32),
        compiler_params=pltpu.CompilerParams(
            dimension_semantics=("arbitrary", "arbitrary"), vmem_limit_bytes=VMEM_LIMIT),
        name="modulation",
    )(cond, w_mod, b_mod.reshape(depth, 1, 6 * D_MODEL))


def _head_norm(x, seg_ones, gain):
    ss = _dot((x * x).astype(BF16), seg_ones)
    return x * lax.rsqrt(ss * (1.0 / HEAD_DIM) + EPS) * gain


def _swap_halves(x, first_half):
    return jnp.where(first_half, pltpu.roll(x, 96, 1), pltpu.roll(x, 32, 1))


def _rope(x, cos, sin, first_half):
    out = []
    for j in range(x.shape[1] // LANES):
        xj = x[:, j * LANES:(j + 1) * LANES]
        out.append(xj * cos + _swap_halves(xj, first_half) * sin)
    return out[0] if len(out) == 1 else jnp.concatenate(out, axis=1)


def _tile_gain(g):
    return jnp.concatenate([g] * (256 // HEAD_DIM), axis=1)


def _qkv_kernel(latent, n_cast, *refs):
    if latent:
        (x_ref, mod_ref, na_ref, w_ref, seg_ref, gqa_ref, gka_ref, gqb_ref, gkb_ref,
         cos_ref, sin_ref, qa_o, ka_o, va_o, qb_o, kb_o, vb_o) = refs
    else:
        (x_ref, mod_ref, na_ref, w_ref, seg_ref, gqa_ref, gka_ref, gqb_ref, gkb_ref) = refs[:9]
        outs = refs[len(refs) - 10 - n_cast:]
        qa_o, ka_o, va_o, qb_o, kb_o, vb_o, kaf_o, vaf_o, kbf_o, vbf_o = outs[:10]
        for src, dst in zip(refs[9:9 + n_cast], outs[10:]):
            dst[...] = src[...].astype(BF16)
    tm = TOKEN_TILE
    sh = mod_ref[:, 0:D_MODEL]
    sc = mod_ref[:, D_MODEL:2 * D_MODEL]
    seg = seg_ref[...]
    lane = lax.broadcasted_iota(jnp.int32, (1, LANES), 1)
    first_half = (lane % HEAD_DIM) < (HEAD_DIM // 2)
    scale = HEAD_DIM ** -0.5 * math.log2(math.e)

    def project(r):
        h = _rms(x_ref[r, :], na_ref[...]) * (1.0 + sc) + sh
        return _dot(h.astype(BF16), w_ref[...])

    def finish(j, r, proj):
        def qk(lo_col, width, gain_ref):
            gain = _tile_gain(gain_ref[...])
            out = []
            for c0 in range(0, width, 256):
                w = min(256, width - c0)
                out.append(_head_norm(proj[:, lo_col + c0:lo_col + c0 + w], seg[:w, :w], gain[:, :w]))
            y = out[0] if len(out) == 1 else jnp.concatenate(out, axis=1)
            if latent:
                y = _rope(y, cos_ref[r, :], sin_ref[r, :], first_half)
            return y

        qa = qk(0, 512, gqa_ref)
        ka = qk(512, 512, gka_ref)
        va = proj[:, 1024:1536]
        qb = qk(1536, 512, gqb_ref)
        kb = qk(2048, 128, gkb_ref)
        vb = proj[:, 2176:2304]
        qa_o[r, :] = (qa * scale).astype(BF16)
        ka_o[r, :] = ka.astype(BF16)
        va_o[r, :] = va.astype(BF16)
        qb_o[r, :] = (qb * scale).astype(BF16)
        kb_o[r, :] = kb.astype(BF16)
        vb_o[r, :] = vb.astype(BF16)
        if not latent:
            kaf_o[j] = ka.T
            kbf_o[j] = kb.T
            vbf_o[j] = vb.T
            for h in range(DIFF_HEADS):
                vaf_o[j, pl.ds(h, tm, stride=DIFF_HEADS), :] = va[:, h * LANES:(h + 1) * LANES]

    rows = [slice(j * tm, (j + 1) * tm) for j in range(x_ref.shape[0] // tm)]
    pending = [project(rows[0])]
    for j, r in enumerate(rows):
        if j + 1 < len(rows):
            pending.append(project(rows[j + 1]))
        finish(j, r, pending.pop(0))


def _qkv(x, mod4, layer, latent, tiles_per_request, norm_attn, w_in, seg_ones, gains, rope,
         prev_caches=None, cast_weights=()):
    n = x.shape[0]
    sub = 4
    tm = sub * TOKEN_TILE
    depth = w_in.shape[0]
    tok = lambda w: pl.BlockSpec((tm, w), lambda i: (i, 0))
    in_specs = [
        tok(D_MODEL),
        _mod_spec(layer, latent, max(tiles_per_request // sub, 1)),
        _layer_spec((1, D_MODEL), layer),
        _layer_spec((D_MODEL, IN_COLS), layer),
        _const_spec((256, 256)),
    ] + [_layer_spec((1, HEAD_DIM), layer)] * 4
    args = [x, mod4, norm_attn, w_in, seg_ones, *gains]
    aliases = {}
    out_specs = [tok(512), tok(512), tok(512), tok(512), tok(LANES), tok(LANES)]
    out_shape = [jax.ShapeDtypeStruct((n, w), BF16) for w in (512, 512, 512, 512, LANES, LANES)]
    if latent:
        rope_spec = pl.BlockSpec((tm, LANES), lambda i: (i % (tiles_per_request // sub), 0))
        in_specs += [rope_spec, rope_spec]
        args += list(rope)
    else:
        assert tiles_per_request == 1, "context requests must be one token tile long"
        seq = TOKEN_TILE
        req = n // seq
        slab = lambda *dims: pl.BlockSpec((sub, None) + dims, lambda i: (i, layer) + (0,) * len(dims))
        out_specs += [slab(512, seq), slab(seq * DIFF_HEADS, LANES), slab(LANES, seq), slab(LANES, seq)]
        out_shape += [jax.ShapeDtypeStruct((req, depth, 512, seq), F32),
                      jax.ShapeDtypeStruct((req, depth, seq * DIFF_HEADS, LANES), F32),
                      jax.ShapeDtypeStruct((req, depth, LANES, seq), F32),
                      jax.ShapeDtypeStruct((req, depth, LANES, seq), F32)]
        steps = n // tm
        for w in cast_weights:
            rows, cols = w.shape[1] // steps, w.shape[2]
            assert rows * steps == w.shape[1] and rows % 16 == 0
            in_specs.append(pl.BlockSpec((None, rows, cols), lambda i: (layer, i, 0)))
            args.append(w)
            out_specs.append(pl.BlockSpec((rows, cols), lambda i: (i, 0)))
            out_shape.append(jax.ShapeDtypeStruct(w.shape[1:], BF16))
        if prev_caches is not None:
            aliases = {len(args) + k: 6 + k for k in range(4)}
            in_specs += [pl.BlockSpec(memory_space=pl.ANY)] * 4
            args += list(prev_caches)
    return pl.pallas_call(
        functools.partial(_qkv_kernel, latent, len(cast_weights)),
        grid=(n // tm,),
        in_specs=in_specs,
        out_specs=out_specs,
        out_shape=out_shape,
        input_output_aliases=aliases,
        compiler_params=pltpu.CompilerParams(
            dimension_semantics=("arbitrary",), vmem_limit_bytes=VMEM_LIMIT),
        name="qkv_latent" if latent else "qkv_context",
    )(*args)


def _pad_rows(x, block):
    z = jnp.zeros_like(x)
    return jnp.concatenate([x, z] if block == 0 else [z, x], axis=0)


ONES_ROWS = 16
SHIFT_KEYS = 8
DENOM_RANGE = (2.0 ** -100, 2.0 ** 100)


def _numerators(st, exact):
    shift = (st if exact else st[:SHIFT_KEYS]).max(axis=0, keepdims=True)
    return jnp.exp2(st - shift).astype(BF16)


def _pv(vt, e):
    d = vt.shape[0]
    ext = _dot(jnp.concatenate([vt, jnp.ones((ONES_ROWS, vt.shape[1]), BF16)], axis=0), e)
    return ext[:d], ext[d:d + 1]


def _denominators_bad(denoms):
    bad = None
    for l in denoms:
        b = jnp.where(jnp.logical_and(l > DENOM_RANGE[0], l < DENOM_RANGE[1]), 0.0, 1.0)
        bad = b if bad is None else jnp.maximum(bad, b)
    return (jnp.max(bad) > 0.0).astype(jnp.int32)


KA_W = DIFF_HEADS * LANES


def _stage_kv(k_all, vt_all, ka, va, kb, vb, cache_refs):
    seq = ka.shape[0]
    k_all[0:seq, 0:KA_W] = ka
    k_all[0:seq, KA_W:] = kb
    vt_all[0:KA_W, 0:seq] = va.astype(F32).T.astype(BF16)
    vt_all[KA_W:, 0:seq] = vb.astype(F32).T.astype(BF16)
    if cache_refs is not None:
        cka_ref, cva_ref, ckb_ref, cvb_ref = cache_refs
        past = cka_ref.shape[2]
        for h in range(DIFF_HEADS):
            cs = slice(h * LANES, (h + 1) * LANES)
            k_all[seq:, cs] = cka_ref[h].T.astype(BF16)
            vt_all[cs, seq:] = cva_ref[pl.ds(h, past, stride=DIFF_HEADS), :].T.astype(BF16)
        k_all[seq:, KA_W:] = jnp.concatenate([ckb_ref[0], ckb_ref[1]], axis=0).T.astype(BF16)
        for g in range(GQA_KV_HEADS):
            vt_all[KA_W + g * HEAD_DIM:KA_W + (g + 1) * HEAD_DIM, seq:] = cvb_ref[g].astype(BF16)


def _lambda(lq1, lk1, lq2, lk2, lam_init):
    return (jnp.exp(jnp.sum(lq1[...] * lk1[...], axis=-1, keepdims=True))
            - jnp.exp(jnp.sum(lq2[...] * lk2[...], axis=-1, keepdims=True)) + lam_init)


def _attention_tasks(qa, qb, k_all, vt_all, lam, subln, lam_init, store, denoms=None):
    tq = qa.shape[0]
    qat = qa.astype(F32).T.astype(BF16)
    qbt = qb.astype(F32).T.astype(BF16)
    exact = denoms is None
    tasks = []

    def first(st):
        return _numerators(st, exact)

    def pv(vt, e):
        num, l = _pv(vt, e)
        if not exact:
            denoms.append(l)
        return num, 1.0 / l

    def diff_finish(e, h, cs):
        num, r = pv(vt_all[cs, :], e)
        ot = num[:, :tq] * r[:, :tq] - num[:, tq:] * (r[:, tq:] * lam)
        ms = jnp.mean(ot * ot, axis=0, keepdims=True)
        ot = ot * lax.rsqrt(ms + EPS) * subln * (1.0 - lam_init)
        store(cs, ot.T)

    def gqa_finish(e, g, os):
        num, r = pv(vt_all[KA_W + g * HEAD_DIM:KA_W + (g + 1) * HEAD_DIM, :], e)
        ot = num * r
        store(os, jnp.concatenate([ot[:, :tq], ot[:, tq:]], axis=0).T)

    for h in range(DIFF_HEADS):
        cs = slice(h * LANES, (h + 1) * LANES)

        def logits(h=h, cs=cs):
            c0 = qat[h * LANES:h * LANES + HEAD_DIM]
            c1 = qat[h * LANES + HEAD_DIM:(h + 1) * LANES]
            return first(_dot(k_all[:, cs], jnp.concatenate([_pad_rows(c0, 0), _pad_rows(c1, 1)], axis=1)))

        tasks.append((logits, lambda e, h=h, cs=cs: diff_finish(e, h, cs)))
    for g in range(GQA_KV_HEADS):
        for rp in range(2):
            r0 = (GQA_REP * g + 2 * rp) * HEAD_DIM
            os = slice(KA_W + r0, KA_W + r0 + LANES)

            def logits(g=g, r0=r0):
                a = qbt[r0:r0 + HEAD_DIM]
                b = qbt[r0 + HEAD_DIM:r0 + LANES]
                return first(_dot(k_all[:, KA_W:], jnp.concatenate([_pad_rows(a, g), _pad_rows(b, g)], axis=1)))

            tasks.append((logits, lambda e, g=g, os=os: gqa_finish(e, g, os)))
    return tasks


LOGITS_AHEAD = 2
GUARDED_AHEAD = 7
N_TASKS = DIFF_HEADS + 2 * GQA_KV_HEADS


def _attention_units(tasks, ahead=LOGITS_AHEAD):
    pending = []

    def prefill(t):
        pending.append(t[0]())

    def step(i):
        if i + ahead < len(tasks):
            pending.append(tasks[i + ahead][0]())
        tasks[i][1](pending.pop(0))

    units = [functools.partial(prefill, t) for t in tasks[:ahead]]
    return units + [functools.partial(step, i) for i in range(len(tasks))]


def _ffn_units(r, x_ref, mix_ref, mod_ref, wo_ref, nf_ref, wgu_ref, wd_ref, o_ref):
    g_a = mod_ref[:, 2 * D_MODEL:3 * D_MODEL]
    sh_f = mod_ref[:, 3 * D_MODEL:4 * D_MODEL]
    sc_f = mod_ref[:, 4 * D_MODEL:5 * D_MODEL]
    g_f = mod_ref[:, 5 * D_MODEL:6 * D_MODEL]
    state = {}

    def attn_residual():
        state["x"] = x_ref[r, :] + g_a * _dot(mix_ref[r, :], wo_ref[...])

    def gate():
        state["h"] = (_rms(state["x"], nf_ref[...]) * (1.0 + sc_f) + sh_f).astype(BF16)
        state["gate"] = _dot(state["h"], wgu_ref[:, :FFN_HIDDEN])

    def up():
        g = state.pop("gate")
        state["act"] = (g * jax.nn.sigmoid(g) * _dot(state.pop("h"), wgu_ref[:, FFN_HIDDEN:])).astype(BF16)

    def ffn_residual():
        o_ref[r, :] = state.pop("x") + g_f * _dot(state.pop("act"), wd_ref[...])

    return [attn_residual, gate, up, ffn_residual]


def _interleave(a, b):
    out, ia, ib = [], 0, 0
    while ia < len(a) or ib < len(b):
        if ib >= len(b) or (ia < len(a) and ia * len(b) <= ib * len(a)):
            out.append(a[ia])
            ia += 1
        else:
            out.append(b[ib])
            ib += 1
    return out


def _attn_ffn_kernel(lam_init, sub, *refs):
    (qa_ref, qb_ref, ka_ref, va_ref, kb_ref, vb_ref, lq1, lk1, lq2, lk2, subln_ref,
     x_ref, mod_ref, wo_ref, nf_ref, wgu_ref, wd_ref, o_ref, k_all, vt_all, mix) = refs
    seq = ka_ref.shape[0] // sub
    lam = _lambda(lq1, lk1, lq2, lk2, lam_init)
    subln = subln_ref[...]
    attn, ffn = [], []
    for j in range(sub):
        r = slice(j * seq, (j + 1) * seq)

        later = []

        def store(cols, tile, r=r):
            mix[r, cols] = tile.astype(mix.dtype)

        def stage(j=j, r=r, later=later, store=store):
            _stage_kv(k_all.at[j], vt_all.at[j], ka_ref[r, :], va_ref[r, :], kb_ref[r, :], vb_ref[r, :], None)
            later.extend(_attention_units(_attention_tasks(
                qa_ref[r, :], qb_ref[r, :], k_all.at[j], vt_all.at[j], lam, subln, lam_init, store)))

        attn.append([stage] + [lambda k=k, later=later: later[k]() for k in range(LOGITS_AHEAD + N_TASKS)])
        ffn.append(_ffn_units(r, x_ref, mix, mod_ref, wo_ref, nf_ref, wgu_ref, wd_ref, o_ref))
    units = attn[0]
    for j in range(1, sub):
        units = units + _interleave(attn[j], ffn[j - 1])
    for unit in units + ffn[sub - 1]:
        unit()


def _attn_ffn(layer, x, qa, qb, own, seq, lam_vecs, subln, mod4, w_out, norm_ffn, w_gu, w_down):
    n = x.shape[0]
    sub = 4
    tm = sub * seq
    lam_init = 0.8 - 0.6 * math.exp(-0.3 * layer)
    tok = lambda w: pl.BlockSpec((tm, w), lambda i: (i, 0))
    kv_w = own[0].shape[1] + own[2].shape[1]
    in_specs = ([tok(512), tok(512)] + [tok(a.shape[1]) for a in own]
                + [_layer_spec((1, HEAD_DIM), layer)] * 4 + [_layer_spec((2 * HEAD_DIM, 1), layer)]
                + [tok(D_MODEL), _mod_spec(layer, False, 1),
                   _const_spec((D_MODEL, D_MODEL)), _layer_spec((1, D_MODEL), layer),
                   _const_spec((D_MODEL, 2 * FFN_HIDDEN)), _const_spec((FFN_HIDDEN, D_MODEL))])
    return pl.pallas_call(
        functools.partial(_attn_ffn_kernel, lam_init, sub),
        grid=(n // tm,),
        in_specs=in_specs,
        out_specs=tok(D_MODEL),
        out_shape=jax.ShapeDtypeStruct((n, D_MODEL), F32),
        scratch_shapes=[pltpu.VMEM((sub, seq, kv_w), BF16), pltpu.VMEM((sub, kv_w, seq), BF16),
                        pltpu.VMEM((tm, D_MODEL), BF16)],
        compiler_params=pltpu.CompilerParams(
            dimension_semantics=("arbitrary",), vmem_limit_bytes=VMEM_LIMIT),
        name="attn_ffn_context",
    )(qa, qb, *own, *lam_vecs, subln, x, mod4, w_out, norm_ffn, w_gu, w_down)


def _attn_ffn_lagged_kernel(lam_init, tiles_per_request, n_tiles, *refs):
    (qa_ref, qb_ref, ka_ref, va_ref, kb_ref, vb_ref, cka_ref, cva_ref, ckb_ref, cvb_ref,
     lq1, lk1, lq2, lk2, subln_ref, x_ref, mod_ref, wo_ref, nf_ref, wgu_ref, wd_ref,
     o_ref, k_all, vt_all, mix, redo) = refs
    s = pl.program_id(0)
    tq = qa_ref.shape[0]
    lam = _lambda(lq1, lk1, lq2, lk2, lam_init)
    subln = subln_ref[...]

    def attention_units(slot, exact=False):
        def store(cols, tile):
            mix[slot, :, cols] = tile.astype(mix.dtype)

        denoms = None if exact else []
        units = _attention_units(_attention_tasks(
            qa_ref[...], qb_ref[...], k_all, vt_all, lam, subln, lam_init, store, denoms),
            LOGITS_AHEAD if exact else GUARDED_AHEAD)
        if exact:
            return units

        def check():
            redo[0] = _denominators_bad(denoms)

        return units + [check]

    def ffn_units(slot):
        return _ffn_units(slice(0, tq), x_ref, mix.at[slot], mod_ref, wo_ref, nf_ref, wgu_ref, wd_ref, o_ref)

    @pl.when(jnp.logical_and(s % tiles_per_request == 0, s < n_tiles))
    def _():
        _stage_kv(k_all, vt_all, ka_ref[...], va_ref[...], kb_ref[...], vb_ref[...],
                  (cka_ref, cva_ref, ckb_ref, cvb_ref))

    @pl.when(s == 0)
    def _():
        for unit in attention_units(0):
            unit()

    @pl.when(jnp.logical_and(s > 0, s < n_tiles))
    def _():
        for unit in _interleave(attention_units(s % 2), ffn_units((s - 1) % 2)):
            unit()

    @pl.when(jnp.logical_and(s < n_tiles, redo[0] != 0))
    def _():
        for unit in attention_units(s % 2, exact=True):
            unit()

    @pl.when(s == n_tiles)
    def _():
        for unit in ffn_units((n_tiles - 1) % 2):
            unit()


def _attn_ffn_lagged(layer, x, qa, qb, own, cache, seq, lam_vecs, subln, mod4, w_out, norm_ffn, w_gu, w_down):
    n = x.shape[0]
    tq = Q_TILE
    tpr = seq // tq
    n_tiles = n // tq
    lam_init = 0.8 - 0.6 * math.exp(-0.3 * layer)
    att = lambda s: jnp.minimum(s, n_tiles - 1)
    ffn = lambda s: jnp.maximum(s - 1, 0)
    once = pl.Buffered(1)
    in_specs = [pl.BlockSpec((tq, 512), lambda s: (att(s), 0))] * 2
    in_specs += [pl.BlockSpec((seq, a.shape[1]), lambda s: (att(s) // tpr, 0), pipeline_mode=once) for a in own]
    for a in cache:
        nd = a.ndim - 2
        in_specs.append(pl.BlockSpec((None, None) + a.shape[2:],
                                     lambda s, nd=nd: (att(s) // tpr, layer) + (0,) * nd, pipeline_mode=once))
    in_specs += [_layer_spec((1, HEAD_DIM), layer)] * 4 + [_layer_spec((2 * HEAD_DIM, 1), layer)]
    in_specs += [pl.BlockSpec((tq, D_MODEL), lambda s: (ffn(s), 0)),
                 pl.BlockSpec((None, None, 1, 6 * D_MODEL), lambda s: (layer, 1 + ffn(s) // tpr, 0, 0)),
                 _const_spec((D_MODEL, D_MODEL)), _layer_spec((1, D_MODEL), layer),
                 _const_spec((D_MODEL, 2 * FFN_HIDDEN)), _const_spec((FFN_HIDDEN, D_MODEL))]
    keys = seq + cache[0].shape[-1]
    kv_w = own[0].shape[1] + own[2].shape[1]
    return pl.pallas_call(
        functools.partial(_attn_ffn_lagged_kernel, lam_init, tpr, n_tiles),
        grid=(n_tiles + 1,),
        in_specs=in_specs,
        out_specs=pl.BlockSpec((tq, D_MODEL), lambda s: (ffn(s), 0)),
        out_shape=jax.ShapeDtypeStruct((n, D_MODEL), F32),
        scratch_shapes=[pltpu.VMEM((keys, kv_w), BF16), pltpu.VMEM((kv_w, keys), BF16),
                        pltpu.VMEM((2, tq, D_MODEL), BF16), pltpu.SMEM((1,), jnp.int32)],
        compiler_params=pltpu.CompilerParams(
            dimension_semantics=("arbitrary",), vmem_limit_bytes=VMEM_LIMIT),
        name="attn_ffn_latent",
    )(qa, qb, *own, *cache, *lam_vecs, subln, x, mod4, w_out, norm_ffn, w_gu, w_down)


def _rope_tables(n_tokens):
    t = jnp.arange(n_tokens, dtype=jnp.int32)
    row = (t // GRID_W).astype(F32)
    col = (t % GRID_W).astype(F32)
    axis_dim = HEAD_DIM // 2
    freqs = ROPE_THETA ** (-jnp.arange(0, axis_dim, 2, dtype=F32) / axis_dim)
    ang = jnp.concatenate([row[:, None] * freqs, col[:, None] * freqs], axis=-1)
    c, s = jnp.cos(ang), jnp.sin(ang)
    return jnp.concatenate([c, c, c, c], axis=-1), jnp.concatenate([-s, s, -s, s], axis=-1)


def kernel(x_prompt, x_sample, cache_diff_k, cache_diff_v, cache_gqa_k, cache_gqa_v, c, c_ctx, w_mod, b_mod, norm_attn, w_in, q_norm_a, k_norm_a, lambda_q1, lambda_k1, lambda_q2, lambda_k2, subln, q_norm_b, k_norm_b, w_out, norm_ffn, w_gate_up, w_down):
    batch, seq, d = x_prompt.shape
    dec_batch, dec_seq, _ = x_sample.shape
    depth = w_mod.shape[0]
    past = cache_diff_k.shape[2]

    cond = jnp.concatenate([c_ctx[None], c, jnp.zeros((MOD_ROWS - 1 - dec_batch, d), F32)], axis=0)
    mod = _modulation(cond, w_mod, b_mod)
    mod4 = mod.reshape(depth, MOD_ROWS, 1, 6 * d)

    w_in_b = w_in.astype(BF16)
    seg_ones = jnp.kron(jnp.eye(256 // HEAD_DIM, dtype=F32), jnp.ones((HEAD_DIM, HEAD_DIM), F32)).astype(BF16)
    rope = _rope_tables(dec_seq)

    cache = (jnp.transpose(cache_diff_k, (0, 1, 3, 4, 5, 2)).reshape(dec_batch, depth, DIFF_HEADS, 2 * HEAD_DIM, past),
             cache_diff_v.reshape(dec_batch, depth, past * DIFF_HEADS, 2 * HEAD_DIM),
             jnp.transpose(cache_gqa_k, (0, 1, 3, 4, 2)),
             jnp.transpose(cache_gqa_v, (0, 1, 3, 4, 2)))

    row = lambda a: a.reshape(depth, 1, a.shape[-1])
    gains = (row(q_norm_a), row(k_norm_a), row(q_norm_b), row(k_norm_b))
    lam_vecs = (row(lambda_q1), row(lambda_k1), row(lambda_q2), row(lambda_k2))
    na, nf, sl = row(norm_attn), row(norm_ffn), subln.reshape(depth, 2 * HEAD_DIM, 1)

    yp = x_prompt.reshape(batch * seq, d)
    ys = x_sample.reshape(dec_batch * dec_seq, d)
    lat_tiles = dec_seq // TOKEN_TILE
    new_caches = None
    for l in range(depth):
        qa, ka, va, qb, kb, vb, *rest = _qkv(
            yp, mod4, l, False, seq // TOKEN_TILE, na, w_in_b, seg_ones, gains, None, new_caches,
            (w_out, w_gate_up, w_down))
        new_caches, (w_out_b, w_gu_b, w_down_b) = rest[:4], rest[4:]
        yp = _attn_ffn(l, yp, qa, qb, (ka, va, kb, vb), seq, lam_vecs, sl, mod4, w_out_b, nf, w_gu_b, w_down_b)

        qa, ka, va, qb, kb, vb = _qkv(
            ys, mod4, l, True, lat_tiles, na, w_in_b, seg_ones, gains, rope)
        ys = _attn_ffn_lagged(l, ys, qa, qb, (ka, va, kb, vb), cache, dec_seq, lam_vecs, sl,
                              mod4, w_out_b, nf, w_gu_b, w_down_b)

    ka_t, va_n, kb_t, vb_t = new_caches
    new_diff_k = jnp.transpose(ka_t.reshape(batch, depth, DIFF_HEADS, 2, HEAD_DIM, seq), (0, 1, 5, 2, 3, 4))
    new_gqa_k = jnp.transpose(kb_t.reshape(batch, depth, GQA_KV_HEADS, HEAD_DIM, seq), (0, 1, 4, 2, 3))
    new_gqa_v = jnp.transpose(vb_t.reshape(batch, depth, GQA_KV_HEADS, HEAD_DIM, seq), (0, 1, 4, 2, 3))
    return (yp.reshape(batch, seq, d), ys.reshape(dec_batch, dec_seq, d),
            new_diff_k, va_n.reshape(batch, depth, seq, DIFF_HEADS, 2 * HEAD_DIM), new_gqa_k, new_gqa_v)
```

```python
import functools
import math

import jax
import jax.numpy as jnp
from jax import lax
from jax.experimental import pallas as pl
from jax.experimental.pallas import tpu as pltpu

F32 = jnp.float32
BF16 = jnp.bfloat16

D_MODEL = 1024
HEAD_DIM = 64
GRID_W = 64
DIFF_HEADS = 4
GQA_KV_HEADS = 2
GQA_REP = 4
FFN_HIDDEN = 2816
IN_COLS = 2304
ROPE_THETA = 10000.0
EPS = 1e-6
LANES = 128
MOD_ROWS = 8
TOKEN_TILE = 256
Q_TILE = 256
VMEM_LIMIT = 60 * 1024 * 1024


def _dot(a, b):
    return jnp.dot(a, b, preferred_element_type=F32)


def _rms(x, gain):
    ms = jnp.mean(x * x, axis=-1, keepdims=True)
    return x * lax.rsqrt(ms + EPS) * gain


def _layer_spec(shape, layer):
    n = len(shape)
    return pl.BlockSpec((None,) + tuple(shape), lambda *_: (layer,) + (0,) * n,
                        pipeline_mode=pl.Buffered(1))


def _const_spec(shape):
    n = len(shape)
    return pl.BlockSpec(shape, lambda *_: (0,) * n, pipeline_mode=pl.Buffered(1))


def _mod_spec(layer, latent, tiles_per_request):
    if latent:
        return pl.BlockSpec((None, None, 1, 6 * D_MODEL),
                            lambda i: (layer, 1 + i // tiles_per_request, 0, 0))
    return pl.BlockSpec((None, None, 1, 6 * D_MODEL), lambda i: (layer, 0, 0, 0))


def _mod_kernel(cond_ref, w_ref, b_ref, o_ref):
    c = cond_ref[...]
    s = c * jax.nn.sigmoid(c)
    o_ref[...] = _dot(s.astype(BF16), w_ref[...].astype(BF16)) + b_ref[...]


def _modulation(cond, w_mod, b_mod):
    depth = w_mod.shape[0]
    tn = 1536
    return pl.pallas_call(
        _mod_kernel,
        grid=(depth, 6 * D_MODEL // tn),
        in_specs=[
            pl.BlockSpec((MOD_ROWS, D_MODEL), lambda l, j: (0, 0)),
            pl.BlockSpec((None, D_MODEL, tn), lambda l, j: (l, 0, j)),
            pl.BlockSpec((None, 1, tn), lambda l, j: (l, 0, j)),
        ],
        out_specs=pl.BlockSpec((None, MOD_ROWS, tn), lambda l, j: (l, 0, j)),
        out_shape=jax.ShapeDtypeStruct((depth, MOD_ROWS, 6 * D_MODEL), F32),
        compiler_params=pltpu.CompilerParams(
            dimension_semantics=("arbitrary", "arbitrary"), vmem_limit_bytes=VMEM_LIMIT),
        name="modulation",
    )(cond, w_mod, b_mod.reshape(depth, 1, 6 * D_MODEL))


def _head_norm(x, seg_ones, gain):
    ss = _dot((x * x).astype(BF16), seg_ones)
    return x * lax.rsqrt(ss * (1.0 / HEAD_DIM) + EPS) * gain


def _swap_halves(x, first_half):
    return jnp.where(first_half, pltpu.roll(x, 96, 1), pltpu.roll(x, 32, 1))


def _rope(x, cos, sin, first_half):
    out = []
    for j in range(x.shape[1] // LANES):
        xj = x[:, j * LANES:(j + 1) * LANES]
        out.append(xj * cos + _swap_halves(xj, first_half) * sin)
    return out[0] if len(out) == 1 else jnp.concatenate(out, axis=1)


def _tile_gain(g):
    return jnp.concatenate([g] * (256 // HEAD_DIM), axis=1)


def _qkv_kernel(latent, n_cast, *refs):
    if latent:
        (x_ref, mod_ref, na_ref, w_ref, seg_ref, gqa_ref, gka_ref, gqb_ref, gkb_ref,
         cos_ref, sin_ref, qa_o, ka_o, va_o, qb_o, kb_o, vb_o) = refs
    else:
        (x_ref, mod_ref, na_ref, w_ref, seg_ref, gqa_ref, gka_ref, gqb_ref, gkb_ref) = refs[:9]
        outs = refs[len(refs) - 10 - n_cast:]
        qa_o, ka_o, va_o, qb_o, kb_o, vb_o, kaf_o, vaf_o, kbf_o, vbf_o = outs[:10]
        for src, dst in zip(refs[9:9 + n_cast], outs[10:]):
            dst[...] = src[...].astype(BF16)
    tm = TOKEN_TILE
    sh = mod_ref[:, 0:D_MODEL]
    sc = mod_ref[:, D_MODEL:2 * D_MODEL]
    seg = seg_ref[...]
    lane = lax.broadcasted_iota(jnp.int32, (1, LANES), 1)
    first_half = (lane % HEAD_DIM) < (HEAD_DIM // 2)
    scale = HEAD_DIM ** -0.5 * math.log2(math.e)

    def project(r):
        h = _rms(x_ref[r, :], na_ref[...]) * (1.0 + sc) + sh
        return _dot(h.astype(BF16), w_ref[...])

    def finish(j, r, proj):
        def qk(lo_col, width, gain_ref):
            gain = _tile_gain(gain_ref[...])
            out = []
            for c0 in range(0, width, 256):
                w = min(256, width - c0)
                out.append(_head_norm(proj[:, lo_col + c0:lo_col + c0 + w], seg[:w, :w], gain[:, :w]))
            y = out[0] if len(out) == 1 else jnp.concatenate(out, axis=1)
            if latent:
                y = _rope(y, cos_ref[r, :], sin_ref[r, :], first_half)
            return y

        qa = qk(0, 512, gqa_ref)
        ka = qk(512, 512, gka_ref)
        va = proj[:, 1024:1536]
        qb = qk(1536, 512, gqb_ref)
        kb = qk(2048, 128, gkb_ref)
        vb = proj[:, 2176:2304]
        qa_o[r, :] = (qa * scale).astype(BF16)
        ka_o[r, :] = ka.astype(BF16)
        va_o[r, :] = va.astype(BF16)
        qb_o[r, :] = (qb * scale).astype(BF16)
        kb_o[r, :] = kb.astype(BF16)
        vb_o[r, :] = vb.astype(BF16)
        if not latent:
            kaf_o[j] = ka.T
            kbf_o[j] = kb.T
            vbf_o[j] = vb.T
            for h in range(DIFF_HEADS):
                vaf_o[j, pl.ds(h, tm, stride=DIFF_HEADS), :] = va[:, h * LANES:(h + 1) * LANES]

    rows = [slice(j * tm, (j + 1) * tm) for j in range(x_ref.shape[0] // tm)]
    pending = [project(rows[0])]
    for j, r in enumerate(rows):
        if j + 1 < len(rows):
            pending.append(project(rows[j + 1]))
        finish(j, r, pending.pop(0))


def _qkv(x, mod4, layer, latent, tiles_per_request, norm_attn, w_in, seg_ones, gains, rope,
         prev_caches=None, cast_weights=()):
    n = x.shape[0]
    sub = 4
    tm = sub * TOKEN_TILE
    depth = w_in.shape[0]
    tok = lambda w: pl.BlockSpec((tm, w), lambda i: (i, 0))
    in_specs = [
        tok(D_MODEL),
        _mod_spec(layer, latent, max(tiles_per_request // sub, 1)),
        _layer_spec((1, D_MODEL), layer),
        _layer_spec((D_MODEL, IN_COLS), layer),
        _const_spec((256, 256)),
    ] + [_layer_spec((1, HEAD_DIM), layer)] * 4
    args = [x, mod4, norm_attn, w_in, seg_ones, *gains]
    aliases = {}
    out_specs = [tok(512), tok(512), tok(512), tok(512), tok(LANES), tok(LANES)]
    out_shape = [jax.ShapeDtypeStruct((n, w), BF16) for w in (512, 512, 512, 512, LANES, LANES)]
    if latent:
        rope_spec = pl.BlockSpec((tm, LANES), lambda i: (i % (tiles_per_request // sub), 0))
        in_specs += [rope_spec, rope_spec]
        args += list(rope)
    else:
        assert tiles_per_request == 1, "context requests must be one token tile long"
        seq = TOKEN_TILE
        req = n // seq
        slab = lambda *dims: pl.BlockSpec((sub, None) + dims, lambda i: (i, layer) + (0,) * len(dims))
        out_specs += [slab(512, seq), slab(seq * DIFF_HEADS, LANES), slab(LANES, seq), slab(LANES, seq)]
        out_shape += [jax.ShapeDtypeStruct((req, depth, 512, seq), F32),
                      jax.ShapeDtypeStruct((req, depth, seq * DIFF_HEADS, LANES), F32),
                      jax.ShapeDtypeStruct((req, depth, LANES, seq), F32),
                      jax.ShapeDtypeStruct((req, depth, LANES, seq), F32)]
        steps = n // tm
        for w in cast_weights:
            rows, cols = w.shape[1] // steps, w.shape[2]
            assert rows * steps == w.shape[1] and rows % 16 == 0
            in_specs.append(pl.BlockSpec((None, rows, cols), lambda i: (layer, i, 0)))
            args.append(w)
            out_specs.append(pl.BlockSpec((rows, cols), lambda i: (i, 0)))
            out_shape.append(jax.ShapeDtypeStruct(w.shape[1:], BF16))
        if prev_caches is not None:
            aliases = {len(args) + k: 6 + k for k in range(4)}
            in_specs += [pl.BlockSpec(memory_space=pl.ANY)] * 4
            args += list(prev_caches)
    return pl.pallas_call(
        functools.partial(_qkv_kernel, latent, len(cast_weights)),
        grid=(n // tm,),
        in_specs=in_specs,
        out_specs=out_specs,
        out_shape=out_shape,
        input_output_aliases=aliases,
        compiler_params=pltpu.CompilerParams(
            dimension_semantics=("arbitrary",), vmem_limit_bytes=VMEM_LIMIT),
        name="qkv_latent" if latent else "qkv_context",
    )(*args)


def _pad_rows(x, block):
    z = jnp.zeros_like(x)
    return jnp.concatenate([x, z] if block == 0 else [z, x], axis=0)


ONES_ROWS = 16
SHIFT_KEYS = 8
DENOM_RANGE = (2.0 ** -100, 2.0 ** 100)


def _numerators(st, exact):
    shift = (st if exact else st[:SHIFT_KEYS]).max(axis=0, keepdims=True)
    return jnp.exp2(st - shift).astype(BF16)


def _pv(vt, e):
    d = vt.shape[0]
    ext = _dot(jnp.concatenate([vt, jnp.ones((ONES_ROWS, vt.shape[1]), BF16)], axis=0), e)
    return ext[:d], ext[d:d + 1]


def _denominators_bad(denoms):
    bad = None
    for l in denoms:
        b = jnp.where(jnp.logical_and(l > DENOM_RANGE[0], l < DENOM_RANGE[1]), 0.0, 1.0)
        bad = b if bad is None else jnp.maximum(bad, b)
    return (jnp.max(bad) > 0.0).astype(jnp.int32)


KA_W = DIFF_HEADS * LANES


def _stage_kv(k_all, vt_all, ka, va, kb, vb, cache_refs):
    seq = ka.shape[0]
    k_all[0:seq, 0:KA_W] = ka
    k_all[0:seq, KA_W:] = kb
    vt_all[0:KA_W, 0:seq] = va.astype(F32).T.astype(BF16)
    vt_all[KA_W:, 0:seq] = vb.astype(F32).T.astype(BF16)
    if cache_refs is not None:
        cka_ref, cva_ref, ckb_ref, cvb_ref = cache_refs
        past = cka_ref.shape[2]
        for h in range(DIFF_HEADS):
            cs = slice(h * LANES, (h + 1) * LANES)
            k_all[seq:, cs] = cka_ref[h].T.astype(BF16)
            vt_all[cs, seq:] = cva_ref[pl.ds(h, past, stride=DIFF_HEADS), :].T.astype(BF16)
        k_all[seq:, KA_W:] = jnp.concatenate([ckb_ref[0], ckb_ref[1]], axis=0).T.astype(BF16)
        for g in range(GQA_KV_HEADS):
            vt_all[KA_W + g * HEAD_DIM:KA_W + (g + 1) * HEAD_DIM, seq:] = cvb_ref[g].astype(BF16)


def _lambda(lq1, lk1, lq2, lk2, lam_init):
    return (jnp.exp(jnp.sum(lq1[...] * lk1[...], axis=-1, keepdims=True))
            - jnp.exp(jnp.sum(lq2[...] * lk2[...], axis=-1, keepdims=True)) + lam_init)


def _attention_tasks(qa, qb, k_all, vt_all, lam, subln, lam_init, store, denoms=None):
    tq = qa.shape[0]
    qat = qa.astype(F32).T.astype(BF16)
    qbt = qb.astype(F32).T.astype(BF16)
    exact = denoms is None
    tasks = []

    def first(st):
        return _numerators(st, exact)

    def pv(vt, e):
        num, l = _pv(vt, e)
        if not exact:
            denoms.append(l)
        return num, 1.0 / l

    def diff_finish(e, h, cs):
        num, r = pv(vt_all[cs, :], e)
        ot = num[:, :tq] * r[:, :tq] - num[:, tq:] * (r[:, tq:] * lam)
        ms = jnp.mean(ot * ot, axis=0, keepdims=True)
        ot = ot * lax.rsqrt(ms + EPS) * subln * (1.0 - lam_init)
        store(cs, ot.T)

    def gqa_finish(e, g, os):
        num, r = pv(vt_all[KA_W + g * HEAD_DIM:KA_W + (g + 1) * HEAD_DIM, :], e)
        ot = num * r
        store(os, jnp.concatenate([ot[:, :tq], ot[:, tq:]], axis=0).T)

    for h in range(DIFF_HEADS):
        cs = slice(h * LANES, (h + 1) * LANES)

        def logits(h=h, cs=cs):
            c0 = qat[h * LANES:h * LANES + HEAD_DIM]
            c1 = qat[h * LANES + HEAD_DIM:(h + 1) * LANES]
            return first(_dot(k_all[:, cs], jnp.concatenate([_pad_rows(c0, 0), _pad_rows(c1, 1)], axis=1)))

        tasks.append((logits, lambda e, h=h, cs=cs: diff_finish(e, h, cs)))
    for g in range(GQA_KV_HEADS):
        for rp in range(2):
            r0 = (GQA_REP * g + 2 * rp) * HEAD_DIM
            os = slice(KA_W + r0, KA_W + r0 + LANES)

            def logits(g=g, r0=r0):
                a = qbt[r0:r0 + HEAD_DIM]
                b = qbt[r0 + HEAD_DIM:r0 + LANES]
                return first(_dot(k_all[:, KA_W:], jnp.concatenate([_pad_rows(a, g), _pad_rows(b, g)], axis=1)))

            tasks.append((logits, lambda e, g=g, os=os: gqa_finish(e, g, os)))
    return tasks


LOGITS_AHEAD = 2
GUARDED_AHEAD = 5
N_TASKS = DIFF_HEADS + 2 * GQA_KV_HEADS


def _attention_units(tasks, ahead=LOGITS_AHEAD):
    pending = []

    def prefill(t):
        pending.append(t[0]())

    def step(i):
        if i + ahead < len(tasks):
            pending.append(tasks[i + ahead][0]())
        tasks[i][1](pending.pop(0))

    units = [functools.partial(prefill, t) for t in tasks[:ahead]]
    return units + [functools.partial(step, i) for i in range(len(tasks))]


def _ffn_units(r, x_ref, mix_ref, mod_ref, wo_ref, nf_ref, wgu_ref, wd_ref, o_ref):
    g_a = mod_ref[:, 2 * D_MODEL:3 * D_MODEL]
    sh_f = mod_ref[:, 3 * D_MODEL:4 * D_MODEL]
    sc_f = mod_ref[:, 4 * D_MODEL:5 * D_MODEL]
    g_f = mod_ref[:, 5 * D_MODEL:6 * D_MODEL]
    state = {}

    def attn_residual():
        state["x"] = x_ref[r, :] + g_a * _dot(mix_ref[r, :], wo_ref[...])

    def gate():
        state["h"] = (_rms(state["x"], nf_ref[...]) * (1.0 + sc_f) + sh_f).astype(BF16)
        state["gate"] = _dot(state["h"], wgu_ref[:, :FFN_HIDDEN])

    def up():
        g = state.pop("gate")
        state["act"] = (g * jax.nn.sigmoid(g) * _dot(state.pop("h"), wgu_ref[:, FFN_HIDDEN:])).astype(BF16)

    def ffn_residual():
        o_ref[r, :] = state.pop("x") + g_f * _dot(state.pop("act"), wd_ref[...])

    return [attn_residual, gate, up, ffn_residual]


def _interleave(a, b):
    out, ia, ib = [], 0, 0
    while ia < len(a) or ib < len(b):
        if ib >= len(b) or (ia < len(a) and ia * len(b) <= ib * len(a)):
            out.append(a[ia])
            ia += 1
        else:
            out.append(b[ib])
            ib += 1
    return out


def _attn_ffn_kernel(lam_init, sub, *refs):
    (qa_ref, qb_ref, ka_ref, va_ref, kb_ref, vb_ref, lq1, lk1, lq2, lk2, subln_ref,
     x_ref, mod_ref, wo_ref, nf_ref, wgu_ref, wd_ref, o_ref, k_all, vt_all, mix) = refs
    seq = ka_ref.shape[0] // sub
    lam = _lambda(lq1, lk1, lq2, lk2, lam_init)
    subln = subln_ref[...]
    attn, ffn = [], []
    for j in range(sub):
        r = slice(j * seq, (j + 1) * seq)

        later = []

        def store(cols, tile, r=r):
            mix[r, cols] = tile.astype(mix.dtype)

        def stage(j=j, r=r, later=later, store=store):
            _stage_kv(k_all.at[j], vt_all.at[j], ka_ref[r, :], va_ref[r, :], kb_ref[r, :], vb_ref[r, :], None)
            later.extend(_attention_units(_attention_tasks(
                qa_ref[r, :], qb_ref[r, :], k_all.at[j], vt_all.at[j], lam, subln, lam_init, store)))

        attn.append([stage] + [lambda k=k, later=later: later[k]() for k in range(LOGITS_AHEAD + N_TASKS)])
    for p in range(sub // 2):
        r2 = slice(2 * p * seq, (2 * p + 2) * seq)
        ffn.append(_ffn_units(r2, x_ref, mix, mod_ref, wo_ref, nf_ref, wgu_ref, wd_ref, o_ref))
    units = attn[0] + attn[1]
    for p in range(1, sub // 2):
        prev = ffn[p - 1]
        units = units + _interleave(attn[2 * p], prev[:2]) + _interleave(attn[2 * p + 1], prev[2:])
    for unit in units + ffn[sub // 2 - 1]:
        unit()


def _attn_ffn(layer, x, qa, qb, own, seq, lam_vecs, subln, mod4, w_out, norm_ffn, w_gu, w_down):
    n = x.shape[0]
    sub = 4
    tm = sub * seq
    lam_init = 0.8 - 0.6 * math.exp(-0.3 * layer)
    tok = lambda w: pl.BlockSpec((tm, w), lambda i: (i, 0))
    kv_w = own[0].shape[1] + own[2].shape[1]
    in_specs = ([tok(512), tok(512)] + [tok(a.shape[1]) for a in own]
                + [_layer_spec((1, HEAD_DIM), layer)] * 4 + [_layer_spec((2 * HEAD_DIM, 1), layer)]
                + [tok(D_MODEL), _mod_spec(layer, False, 1),
                   _const_spec((D_MODEL, D_MODEL)), _layer_spec((1, D_MODEL), layer),
                   _const_spec((D_MODEL, 2 * FFN_HIDDEN)), _const_spec((FFN_HIDDEN, D_MODEL))])
    return pl.pallas_call(
        functools.partial(_attn_ffn_kernel, lam_init, sub),
        grid=(n // tm,),
        in_specs=in_specs,
        out_specs=tok(D_MODEL),
        out_shape=jax.ShapeDtypeStruct((n, D_MODEL), F32),
        scratch_shapes=[pltpu.VMEM((sub, seq, kv_w), BF16), pltpu.VMEM((sub, kv_w, seq), BF16),
                        pltpu.VMEM((tm, D_MODEL), BF16)],
        compiler_params=pltpu.CompilerParams(
            dimension_semantics=("arbitrary",), vmem_limit_bytes=VMEM_LIMIT),
        name="attn_ffn_context",
    )(qa, qb, *own, *lam_vecs, subln, x, mod4, w_out, norm_ffn, w_gu, w_down)


def _attn_ffn_lagged_kernel(lam_init, tiles_per_request, n_tiles, *refs):
    (qa_ref, qb_ref, ka_ref, va_ref, kb_ref, vb_ref, cka_ref, cva_ref, ckb_ref, cvb_ref,
     lq1, lk1, lq2, lk2, subln_ref, x_ref, mod_ref, wo_ref, nf_ref, wgu_ref, wd_ref,
     o_ref, k_all, vt_all, mix, redo) = refs
    s = pl.program_id(0)
    tq = qa_ref.shape[0]
    lam = _lambda(lq1, lk1, lq2, lk2, lam_init)
    subln = subln_ref[...]

    def attention_units(slot, exact=False):
        def store(cols, tile):
            mix[slot, :, cols] = tile.astype(mix.dtype)

        denoms = None if exact else []
        units = _attention_units(_attention_tasks(
            qa_ref[...], qb_ref[...], k_all, vt_all, lam, subln, lam_init, store, denoms),
            LOGITS_AHEAD if exact else GUARDED_AHEAD)
        if exact:
            return units

        def check():
            redo[0] = _denominators_bad(denoms)

        return units + [check]

    def ffn_units(slot):
        return _ffn_units(slice(0, tq), x_ref, mix.at[slot], mod_ref, wo_ref, nf_ref, wgu_ref, wd_ref, o_ref)

    @pl.when(jnp.logical_and(s % tiles_per_request == 0, s < n_tiles))
    def _():
        _stage_kv(k_all, vt_all, ka_ref[...], va_ref[...], kb_ref[...], vb_ref[...],
                  (cka_ref, cva_ref, ckb_ref, cvb_ref))

    @pl.when(s == 0)
    def _():
        for unit in attention_units(0):
            unit()

    @pl.when(jnp.logical_and(s > 0, s < n_tiles))
    def _():
        for unit in _interleave(attention_units(s % 2), ffn_units((s - 1) % 2)):
            unit()

    @pl.when(jnp.logical_and(s < n_tiles, redo[0] != 0))
    def _():
        for unit in attention_units(s % 2, exact=True):
            unit()

    @pl.when(s == n_tiles)
    def _():
        for unit in ffn_units((n_tiles - 1) % 2):
            unit()


def _attn_ffn_lagged(layer, x, qa, qb, own, cache, seq, lam_vecs, subln, mod4, w_out, norm_ffn, w_gu, w_down):
    n = x.shape[0]
    tq = Q_TILE
    tpr = seq // tq
    n_tiles = n // tq
    lam_init = 0.8 - 0.6 * math.exp(-0.3 * layer)
    att = lambda s: jnp.minimum(s, n_tiles - 1)
    ffn = lambda s: jnp.maximum(s - 1, 0)
    once = pl.Buffered(1)
    in_specs = [pl.BlockSpec((tq, 512), lambda s: (att(s), 0))] * 2
    in_specs += [pl.BlockSpec((seq, a.shape[1]), lambda s: (att(s) // tpr, 0), pipeline_mode=once) for a in own]
    for a in cache:
        nd = a.ndim - 2
        in_specs.append(pl.BlockSpec((None, None) + a.shape[2:],
                                     lambda s, nd=nd: (att(s) // tpr, layer) + (0,) * nd, pipeline_mode=once))
    in_specs += [_layer_spec((1, HEAD_DIM), layer)] * 4 + [_layer_spec((2 * HEAD_DIM, 1), layer)]
    in_specs += [pl.BlockSpec((tq, D_MODEL), lambda s: (ffn(s), 0)),
                 pl.BlockSpec((None, None, 1, 6 * D_MODEL), lambda s: (layer, 1 + ffn(s) // tpr, 0, 0)),
                 _const_spec((D_MODEL, D_MODEL)), _layer_spec((1, D_MODEL), layer),
                 _const_spec((D_MODEL, 2 * FFN_HIDDEN)), _const_spec((FFN_HIDDEN, D_MODEL))]
    keys = seq + cache[0].shape[-1]
    kv_w = own[0].shape[1] + own[2].shape[1]
    return pl.pallas_call(
        functools.partial(_attn_ffn_lagged_kernel, lam_init, tpr, n_tiles),
        grid=(n_tiles + 1,),
        in_specs=in_specs,
        out_specs=pl.BlockSpec((tq, D_MODEL), lambda s: (ffn(s), 0)),
        out_shape=jax.ShapeDtypeStruct((n, D_MODEL), F32),
        scratch_shapes=[pltpu.VMEM((keys, kv_w), BF16), pltpu.VMEM((kv_w, keys), BF16),
                        pltpu.VMEM((2, tq, D_MODEL), BF16), pltpu.SMEM((1,), jnp.int32)],
        compiler_params=pltpu.CompilerParams(
            dimension_semantics=("arbitrary",), vmem_limit_bytes=VMEM_LIMIT),
        name="attn_ffn_latent",
    )(qa, qb, *own, *cache, *lam_vecs, subln, x, mod4, w_out, norm_ffn, w_gu, w_down)


def _rope_tables(n_tokens):
    t = jnp.arange(n_tokens, dtype=jnp.int32)
    row = (t // GRID_W).astype(F32)
    col = (t % GRID_W).astype(F32)
    axis_dim = HEAD_DIM // 2
    freqs = ROPE_THETA ** (-jnp.arange(0, axis_dim, 2, dtype=F32) / axis_dim)
    ang = jnp.concatenate([row[:, None] * freqs, col[:, None] * freqs], axis=-1)
    c, s = jnp.cos(ang), jnp.sin(ang)
    return jnp.concatenate([c, c, c, c], axis=-1), jnp.concatenate([-s, s, -s, s], axis=-1)


def kernel(x_prompt, x_sample, cache_diff_k, cache_diff_v, cache_gqa_k, cache_gqa_v, c, c_ctx, w_mod, b_mod, norm_attn, w_in, q_norm_a, k_norm_a, lambda_q1, lambda_k1, lambda_q2, lambda_k2, subln, q_norm_b, k_norm_b, w_out, norm_ffn, w_gate_up, w_down):
    batch, seq, d = x_prompt.shape
    dec_batch, dec_seq, _ = x_sample.shape
    depth = w_mod.shape[0]
    past = cache_diff_k.shape[2]

    cond = jnp.concatenate([c_ctx[None], c, jnp.zeros((MOD_ROWS - 1 - dec_batch, d), F32)], axis=0)
    mod = _modulation(cond, w_mod, b_mod)
    mod4 = mod.reshape(depth, MOD_ROWS, 1, 6 * d)

    w_in_b = w_in.astype(BF16)
    seg_ones = jnp.kron(jnp.eye(256 // HEAD_DIM, dtype=F32), jnp.ones((HEAD_DIM, HEAD_DIM), F32)).astype(BF16)
    rope = _rope_tables(dec_seq)

    cache = (jnp.transpose(cache_diff_k, (0, 1, 3, 4, 5, 2)).reshape(dec_batch, depth, DIFF_HEADS, 2 * HEAD_DIM, past),
             cache_diff_v.reshape(dec_batch, depth, past * DIFF_HEADS, 2 * HEAD_DIM),
             jnp.transpose(cache_gqa_k, (0, 1, 3, 4, 2)),
             jnp.transpose(cache_gqa_v, (0, 1, 3, 4, 2)))

    row = lambda a: a.reshape(depth, 1, a.shape[-1])
    gains = (row(q_norm_a), row(k_norm_a), row(q_norm_b), row(k_norm_b))
    lam_vecs = (row(lambda_q1), row(lambda_k1), row(lambda_q2), row(lambda_k2))
    na, nf, sl = row(norm_attn), row(norm_ffn), subln.reshape(depth, 2 * HEAD_DIM, 1)

    yp = x_prompt.reshape(batch * seq, d)
    ys = x_sample.reshape(dec_batch * dec_seq, d)
    lat_tiles = dec_seq // TOKEN_TILE
    new_caches = None
    for l in range(depth):
        qa, ka, va, qb, kb, vb, *rest = _qkv(
            yp, mod4, l, False, seq // TOKEN_TILE, na, w_in_b, seg_ones, gains, None, new_caches,
            (w_out, w_gate_up, w_down))
        new_caches, (w_out_b, w_gu_b, w_down_b) = rest[:4], rest[4:]
        yp = _attn_ffn(l, yp, qa, qb, (ka, va, kb, vb), seq, lam_vecs, sl, mod4, w_out_b, nf, w_gu_b, w_down_b)

        qa, ka, va, qb, kb, vb = _qkv(
            ys, mod4, l, True, lat_tiles, na, w_in_b, seg_ones, gains, rope)
        ys = _attn_ffn_lagged(l, ys, qa, qb, (ka, va, kb, vb), cache, dec_seq, lam_vecs, sl,
                              mod4, w_out_b, nf, w_gu_b, w_down_b)

    ka_t, va_n, kb_t, vb_t = new_caches
    new_diff_k = jnp.transpose(ka_t.reshape(batch, depth, DIFF_HEADS, 2, HEAD_DIM, seq), (0, 1, 5, 2, 3, 4))
    new_gqa_k = jnp.transpose(kb_t.reshape(batch, depth, GQA_KV_HEADS, HEAD_DIM, seq), (0, 1, 4, 2, 3))
    new_gqa_v = jnp.transpose(vb_t.reshape(batch, depth, GQA_KV_HEADS, HEAD_DIM, seq), (0, 1, 4, 2, 3))
    return (yp.reshape(batch, seq, d), ys.reshape(dec_batch, dec_seq, d),
            new_diff_k, va_n.reshape(batch, depth, seq, DIFF_HEADS, 2 * HEAD_DIM), new_gqa_k, new_gqa_v)
```

```python
import functools
import math

import jax
import jax.numpy as jnp
from jax import lax
from jax.experimental import pallas as pl
from jax.experimental.pallas import tpu as pltpu

F32 = jnp.float32
BF16 = jnp.bfloat16

D_MODEL = 1024
HEAD_DIM = 64
GRID_W = 64
DIFF_HEADS = 4
GQA_KV_HEADS = 2
GQA_REP = 4
FFN_HIDDEN = 2816
IN_COLS = 2304
ROPE_THETA = 10000.0
EPS = 1e-6
LANES = 128
MOD_ROWS = 8
TOKEN_TILE = 256
Q_TILE = 256
VMEM_LIMIT = 60 * 1024 * 1024


def _dot(a, b):
    return jnp.dot(a, b, preferred_element_type=F32)


def _rms(x, gain):
    ms = jnp.mean(x * x, axis=-1, keepdims=True)
    return x * lax.rsqrt(ms + EPS) * gain


def _layer_spec(shape, layer):
    n = len(shape)
    return pl.BlockSpec((None,) + tuple(shape), lambda *_: (layer,) + (0,) * n,
                        pipeline_mode=pl.Buffered(1))


def _const_spec(shape):
    n = len(shape)
    return pl.BlockSpec(shape, lambda *_: (0,) * n, pipeline_mode=pl.Buffered(1))


def _mod_spec(latent, tiles_per_request):
    if latent:
        return pl.BlockSpec((None, 1, 6 * D_MODEL), lambda i: (1 + i // tiles_per_request, 0, 0))
    return pl.BlockSpec((None, 1, 6 * D_MODEL), lambda i: (0, 0, 0))


def _mod_kernel(cond_ref, w_ref, b_ref, o_ref):
    c = cond_ref[...]
    s = c * jax.nn.sigmoid(c)
    o_ref[...] = _dot(s.astype(BF16), w_ref[...].astype(BF16)) + b_ref[...]


def _modulation(cond, w_mod, b_mod, layer):
    tn = 1536
    return pl.pallas_call(
        _mod_kernel,
        grid=(6 * D_MODEL // tn,),
        in_specs=[
            pl.BlockSpec((MOD_ROWS, D_MODEL), lambda j: (0, 0)),
            pl.BlockSpec((None, D_MODEL, tn), lambda j: (layer, 0, j)),
            pl.BlockSpec((None, 1, tn), lambda j: (layer, 0, j)),
        ],
        out_specs=pl.BlockSpec((MOD_ROWS, tn), lambda j: (0, j)),
        out_shape=jax.ShapeDtypeStruct((MOD_ROWS, 6 * D_MODEL), F32),
        compiler_params=pltpu.CompilerParams(
            dimension_semantics=("arbitrary",), vmem_limit_bytes=VMEM_LIMIT),
        name="modulation",
    )(cond, w_mod, b_mod)


def _head_norm(x, seg_ones, gain):
    ss = _dot((x * x).astype(BF16), seg_ones)
    return x * lax.rsqrt(ss * (1.0 / HEAD_DIM) + EPS) * gain


def _swap_halves(x, first_half):
    return jnp.where(first_half, pltpu.roll(x, 96, 1), pltpu.roll(x, 32, 1))


def _rope(x, cos, sin, first_half):
    out = []
    for j in range(x.shape[1] // LANES):
        xj = x[:, j * LANES:(j + 1) * LANES]
        out.append(xj * cos + _swap_halves(xj, first_half) * sin)
    return out[0] if len(out) == 1 else jnp.concatenate(out, axis=1)


def _tile_gain(g):
    return jnp.concatenate([g] * (256 // HEAD_DIM), axis=1)


def _qkv_kernel(latent, n_cast, *refs):
    if latent:
        (x_ref, mod_ref, na_ref, w_ref, seg_ref, gqa_ref, gka_ref, gqb_ref, gkb_ref,
         cos_ref, sin_ref, qa_o, ka_o, va_o, qb_o, kb_o, vb_o) = refs
    else:
        (x_ref, mod_ref, na_ref, w_ref, seg_ref, gqa_ref, gka_ref, gqb_ref, gkb_ref) = refs[:9]
        outs = refs[len(refs) - 10 - n_cast:]
        qa_o, ka_o, va_o, qb_o, kb_o, vb_o, kaf_o, vaf_o, kbf_o, vbf_o = outs[:10]
        for src, dst in zip(refs[9:9 + n_cast], outs[10:]):
            dst[...] = src[...].astype(BF16)
    tm = TOKEN_TILE
    sh = mod_ref[:, 0:D_MODEL]
    sc = mod_ref[:, D_MODEL:2 * D_MODEL]
    seg = seg_ref[...]
    lane = lax.broadcasted_iota(jnp.int32, (1, LANES), 1)
    first_half = (lane % HEAD_DIM) < (HEAD_DIM // 2)
    scale = HEAD_DIM ** -0.5 * math.log2(math.e)

    def project(r):
        h = _rms(x_ref[r, :], na_ref[...]) * (1.0 + sc) + sh
        return _dot(h.astype(BF16), w_ref[...])

    def finish(j, r, proj):
        def qk(lo_col, width, gain_ref):
            gain = _tile_gain(gain_ref[...])
            out = []
            for c0 in range(0, width, 256):
                w = min(256, width - c0)
                out.append(_head_norm(proj[:, lo_col + c0:lo_col + c0 + w], seg[:w, :w], gain[:, :w]))
            y = out[0] if len(out) == 1 else jnp.concatenate(out, axis=1)
            if latent:
                y = _rope(y, cos_ref[r, :], sin_ref[r, :], first_half)
            return y

        qa = qk(0, 512, gqa_ref)
        ka = qk(512, 512, gka_ref)
        va = proj[:, 1024:1536]
        qb = qk(1536, 512, gqb_ref)
        kb = qk(2048, 128, gkb_ref)
        vb = proj[:, 2176:2304]
        qa_o[r, :] = (qa * scale).astype(BF16)
        ka_o[r, :] = ka.astype(BF16)
        va_o[r, :] = va.astype(BF16)
        qb_o[r, :] = (qb * scale).astype(BF16)
        kb_o[r, :] = kb.astype(BF16)
        vb_o[r, :] = vb.astype(BF16)
        if not latent:
            kaf_o[j] = ka.T
            kbf_o[j] = kb.T
            vbf_o[j] = vb.T
            for h in range(DIFF_HEADS):
                vaf_o[j, pl.ds(h, tm, stride=DIFF_HEADS), :] = va[:, h * LANES:(h + 1) * LANES]

    rows = [slice(j * tm, (j + 1) * tm) for j in range(x_ref.shape[0] // tm)]
    pending = [project(rows[0])]
    for j, r in enumerate(rows):
        if j + 1 < len(rows):
            pending.append(project(rows[j + 1]))
        finish(j, r, pending.pop(0))


def _qkv(x, mod4, layer, latent, tiles_per_request, norm_attn, w_in, seg_ones, gains, rope,
         prev_caches=None, cast_weights=()):
    n = x.shape[0]
    sub = 4
    tm = sub * TOKEN_TILE
    depth = w_in.shape[0]
    tok = lambda w: pl.BlockSpec((tm, w), lambda i: (i, 0))
    in_specs = [
        tok(D_MODEL),
        _mod_spec(latent, max(tiles_per_request // sub, 1)),
        _layer_spec((1, D_MODEL), layer),
        _layer_spec((D_MODEL, IN_COLS), layer),
        _const_spec((256, 256)),
    ] + [_layer_spec((1, HEAD_DIM), layer)] * 4
    args = [x, mod4, norm_attn, w_in, seg_ones, *gains]
    aliases = {}
    out_specs = [tok(512), tok(512), tok(512), tok(512), tok(LANES), tok(LANES)]
    out_shape = [jax.ShapeDtypeStruct((n, w), BF16) for w in (512, 512, 512, 512, LANES, LANES)]
    if latent:
        rope_spec = pl.BlockSpec((tm, LANES), lambda i: (i % (tiles_per_request // sub), 0))
        in_specs += [rope_spec, rope_spec]
        args += list(rope)
    else:
        assert tiles_per_request == 1, "context requests must be one token tile long"
        seq = TOKEN_TILE
        req = n // seq
        slab = lambda *dims: pl.BlockSpec((sub, None) + dims, lambda i: (i, layer) + (0,) * len(dims))
        out_specs += [slab(512, seq), slab(seq * DIFF_HEADS, LANES), slab(LANES, seq), slab(LANES, seq)]
        out_shape += [jax.ShapeDtypeStruct((req, depth, 512, seq), F32),
                      jax.ShapeDtypeStruct((req, depth, seq * DIFF_HEADS, LANES), F32),
                      jax.ShapeDtypeStruct((req, depth, LANES, seq), F32),
                      jax.ShapeDtypeStruct((req, depth, LANES, seq), F32)]
        steps = n // tm
        for w in cast_weights:
            rows, cols = w.shape[1] // steps, w.shape[2]
            assert rows * steps == w.shape[1] and rows % 16 == 0
            in_specs.append(pl.BlockSpec((None, rows, cols), lambda i: (layer, i, 0)))
            args.append(w)
            out_specs.append(pl.BlockSpec((rows, cols), lambda i: (i, 0)))
            out_shape.append(jax.ShapeDtypeStruct(w.shape[1:], BF16))
        if prev_caches is not None:
            aliases = {len(args) + k: 6 + k for k in range(4)}
            in_specs += [pl.BlockSpec(memory_space=pl.ANY)] * 4
            args += list(prev_caches)
    return pl.pallas_call(
        functools.partial(_qkv_kernel, latent, len(cast_weights)),
        grid=(n // tm,),
        in_specs=in_specs,
        out_specs=out_specs,
        out_shape=out_shape,
        input_output_aliases=aliases,
        compiler_params=pltpu.CompilerParams(
            dimension_semantics=("arbitrary",), vmem_limit_bytes=VMEM_LIMIT),
        name="qkv_latent" if latent else "qkv_context",
    )(*args)


def _pad_rows(x, block):
    z = jnp.zeros_like(x)
    return jnp.concatenate([x, z] if block == 0 else [z, x], axis=0)


ONES_ROWS = 16
SHIFT_KEYS = 8
DENOM_RANGE = (2.0 ** -100, 2.0 ** 100)


def _numerators(st, exact):
    shift = (st if exact else st[:SHIFT_KEYS]).max(axis=0, keepdims=True)
    return jnp.exp2(st - shift).astype(BF16)


def _pv(vt, e):
    d = vt.shape[0]
    ext = _dot(jnp.concatenate([vt, jnp.ones((ONES_ROWS, vt.shape[1]), BF16)], axis=0), e)
    return ext[:d], ext[d:d + 1]


def _denominators_bad(denoms):
    bad = None
    for l in denoms:
        b = jnp.where(jnp.logical_and(l > DENOM_RANGE[0], l < DENOM_RANGE[1]), 0.0, 1.0)
        bad = b if bad is None else jnp.maximum(bad, b)
    return (jnp.max(bad) > 0.0).astype(jnp.int32)


KA_W = DIFF_HEADS * LANES


def _stage_kv(k_all, vt_all, ka, va, kb, vb, cache_refs):
    seq = ka.shape[0]
    k_all[0:seq, 0:KA_W] = ka
    k_all[0:seq, KA_W:] = kb
    vt_all[0:KA_W, 0:seq] = va.astype(F32).T.astype(BF16)
    vt_all[KA_W:, 0:seq] = vb.astype(F32).T.astype(BF16)
    if cache_refs is not None:
        cka_ref, cva_ref, ckb_ref, cvb_ref = cache_refs
        past = cka_ref.shape[2]
        for h in range(DIFF_HEADS):
            cs = slice(h * LANES, (h + 1) * LANES)
            k_all[seq:, cs] = cka_ref[h].T.astype(BF16)
            vt_all[cs, seq:] = cva_ref[pl.ds(h, past, stride=DIFF_HEADS), :].T.astype(BF16)
        k_all[seq:, KA_W:] = jnp.concatenate([ckb_ref[0], ckb_ref[1]], axis=0).T.astype(BF16)
        for g in range(GQA_KV_HEADS):
            vt_all[KA_W + g * HEAD_DIM:KA_W + (g + 1) * HEAD_DIM, seq:] = cvb_ref[g].astype(BF16)


def _lambda(lq1, lk1, lq2, lk2, lam_init):
    return (jnp.exp(jnp.sum(lq1[...] * lk1[...], axis=-1, keepdims=True))
            - jnp.exp(jnp.sum(lq2[...] * lk2[...], axis=-1, keepdims=True)) + lam_init)


def _attention_tasks(qa, qb, k_all, vt_all, lam, subln, lam_init, store, denoms=None):
    tq = qa.shape[0]
    qat = qa.astype(F32).T.astype(BF16)
    qbt = qb.astype(F32).T.astype(BF16)
    exact = denoms is None
    tasks = []

    def first(st):
        return _numerators(st, exact)

    def pv(vt, e):
        num, l = _pv(vt, e)
        if not exact:
            denoms.append(l)
        return num, 1.0 / l

    def diff_finish(e, h, cs):
        num, r = pv(vt_all[cs, :], e)
        ot = num[:, :tq] * r[:, :tq] - num[:, tq:] * (r[:, tq:] * lam)
        ms = jnp.mean(ot * ot, axis=0, keepdims=True)
        ot = ot * lax.rsqrt(ms + EPS) * subln * (1.0 - lam_init)
        store(cs, ot.T)

    def gqa_finish(e, g, os):
        num, r = pv(vt_all[KA_W + g * HEAD_DIM:KA_W + (g + 1) * HEAD_DIM, :], e)
        ot = num * r
        store(os, jnp.concatenate([ot[:, :tq], ot[:, tq:]], axis=0).T)

    for h in range(DIFF_HEADS):
        cs = slice(h * LANES, (h + 1) * LANES)

        def logits(h=h, cs=cs):
            c0 = qat[h * LANES:h * LANES + HEAD_DIM]
            c1 = qat[h * LANES + HEAD_DIM:(h + 1) * LANES]
            return first(_dot(k_all[:, cs], jnp.concatenate([_pad_rows(c0, 0), _pad_rows(c1, 1)], axis=1)))

        tasks.append((logits, lambda e, h=h, cs=cs: diff_finish(e, h, cs)))
    for g in range(GQA_KV_HEADS):
        for rp in range(2):
            r0 = (GQA_REP * g + 2 * rp) * HEAD_DIM
            os = slice(KA_W + r0, KA_W + r0 + LANES)

            def logits(g=g, r0=r0):
                a = qbt[r0:r0 + HEAD_DIM]
                b = qbt[r0 + HEAD_DIM:r0 + LANES]
                return first(_dot(k_all[:, KA_W:], jnp.concatenate([_pad_rows(a, g), _pad_rows(b, g)], axis=1)))

            tasks.append((logits, lambda e, g=g, os=os: gqa_finish(e, g, os)))
    return tasks


LOGITS_AHEAD = 2
GUARDED_AHEAD = 5
N_TASKS = DIFF_HEADS + 2 * GQA_KV_HEADS


def _attention_units(tasks, ahead=LOGITS_AHEAD):
    pending = []

    def prefill(t):
        pending.append(t[0]())

    def step(i):
        if i + ahead < len(tasks):
            pending.append(tasks[i + ahead][0]())
        tasks[i][1](pending.pop(0))

    units = [functools.partial(prefill, t) for t in tasks[:ahead]]
    return units + [functools.partial(step, i) for i in range(len(tasks))]


def _ffn_units(r, x_ref, mix_ref, mod_ref, wo_ref, nf_ref, wgu_ref, wd_ref, o_ref):
    g_a = mod_ref[:, 2 * D_MODEL:3 * D_MODEL]
    sh_f = mod_ref[:, 3 * D_MODEL:4 * D_MODEL]
    sc_f = mod_ref[:, 4 * D_MODEL:5 * D_MODEL]
    g_f = mod_ref[:, 5 * D_MODEL:6 * D_MODEL]
    state = {}

    def attn_residual():
        state["x"] = x_ref[r, :] + g_a * _dot(mix_ref[r, :], wo_ref[...])

    def gate():
        state["h"] = (_rms(state["x"], nf_ref[...]) * (1.0 + sc_f) + sh_f).astype(BF16)
        state["gate"] = _dot(state["h"], wgu_ref[:, :FFN_HIDDEN])

    def up():
        g = state.pop("gate")
        state["act"] = (g * jax.nn.sigmoid(g) * _dot(state.pop("h"), wgu_ref[:, FFN_HIDDEN:])).astype(BF16)

    def ffn_residual():
        o_ref[r, :] = state.pop("x") + g_f * _dot(state.pop("act"), wd_ref[...])

    return [attn_residual, gate, up, ffn_residual]


def _interleave(a, b):
    out, ia, ib = [], 0, 0
    while ia < len(a) or ib < len(b):
        if ib >= len(b) or (ia < len(a) and ia * len(b) <= ib * len(a)):
            out.append(a[ia])
            ia += 1
        else:
            out.append(b[ib])
            ib += 1
    return out


def _attn_ffn_kernel(lam_init, sub, *refs):
    (qa_ref, qb_ref, ka_ref, va_ref, kb_ref, vb_ref, lq1, lk1, lq2, lk2, subln_ref,
     x_ref, mod_ref, wo_ref, nf_ref, wgu_ref, wd_ref, o_ref, k_all, vt_all, mix) = refs
    seq = ka_ref.shape[0] // sub
    lam = _lambda(lq1, lk1, lq2, lk2, lam_init)
    subln = subln_ref[...]
    attn, ffn = [], []
    for j in range(sub):
        r = slice(j * seq, (j + 1) * seq)

        later = []

        def store(cols, tile, r=r):
            mix[r, cols] = tile.astype(mix.dtype)

        def stage(j=j, r=r, later=later, store=store):
            _stage_kv(k_all.at[j], vt_all.at[j], ka_ref[r, :], va_ref[r, :], kb_ref[r, :], vb_ref[r, :], None)
            later.extend(_attention_units(_attention_tasks(
                qa_ref[r, :], qb_ref[r, :], k_all.at[j], vt_all.at[j], lam, subln, lam_init, store)))

        attn.append([stage] + [lambda k=k, later=later: later[k]() for k in range(LOGITS_AHEAD + N_TASKS)])
        ffn.append(_ffn_units(r, x_ref, mix, mod_ref, wo_ref, nf_ref, wgu_ref, wd_ref, o_ref))
    units = attn[0]
    for j in range(1, sub):
        units = units + _interleave(attn[j], ffn[j - 1])
    for unit in units + ffn[sub - 1]:
        unit()


def _attn_ffn(layer, x, qa, qb, own, seq, lam_vecs, subln, mod4, w_out, norm_ffn, w_gu, w_down):
    n = x.shape[0]
    sub = 4
    tm = sub * seq
    lam_init = 0.8 - 0.6 * math.exp(-0.3 * layer)
    tok = lambda w: pl.BlockSpec((tm, w), lambda i: (i, 0))
    kv_w = own[0].shape[1] + own[2].shape[1]
    in_specs = ([tok(512), tok(512)] + [tok(a.shape[1]) for a in own]
                + [_layer_spec((1, HEAD_DIM), layer)] * 4 + [_layer_spec((2 * HEAD_DIM, 1), layer)]
                + [tok(D_MODEL), _mod_spec(False, 1),
                   _const_spec((D_MODEL, D_MODEL)), _layer_spec((1, D_MODEL), layer),
                   _const_spec((D_MODEL, 2 * FFN_HIDDEN)), _const_spec((FFN_HIDDEN, D_MODEL))])
    return pl.pallas_call(
        functools.partial(_attn_ffn_kernel, lam_init, sub),
        grid=(n // tm,),
        in_specs=in_specs,
        out_specs=tok(D_MODEL),
        out_shape=jax.ShapeDtypeStruct((n, D_MODEL), F32),
        scratch_shapes=[pltpu.VMEM((sub, seq, kv_w), BF16), pltpu.VMEM((sub, kv_w, seq), BF16),
                        pltpu.VMEM((tm, D_MODEL), BF16)],
        compiler_params=pltpu.CompilerParams(
            dimension_semantics=("arbitrary",), vmem_limit_bytes=VMEM_LIMIT),
        name="attn_ffn_context",
    )(qa, qb, *own, *lam_vecs, subln, x, mod4, w_out, norm_ffn, w_gu, w_down)


def _attn_ffn_lagged_kernel(lam_init, tiles_per_request, n_tiles, next_mod, *refs):
    (qa_ref, qb_ref, ka_ref, va_ref, kb_ref, vb_ref, cka_ref, cva_ref, ckb_ref, cvb_ref,
     lq1, lk1, lq2, lk2, subln_ref, x_ref, mod_ref, wo_ref, nf_ref, wgu_ref, wd_ref) = refs[:21]
    o_ref = refs[24 if next_mod else 21]
    k_all, vt_all, mix, redo = refs[-4:]
    s = pl.program_id(0)
    if next_mod:
        @pl.when(s < n_tiles)
        def _():
            _mod_kernel(refs[21], refs[22], refs[23], refs[25])

    tq = qa_ref.shape[0]
    lam = _lambda(lq1, lk1, lq2, lk2, lam_init)
    subln = subln_ref[...]

    def attention_units(slot, exact=False):
        def store(cols, tile):
            mix[slot, :, cols] = tile.astype(mix.dtype)

        denoms = None if exact else []
        units = _attention_units(_attention_tasks(
            qa_ref[...], qb_ref[...], k_all, vt_all, lam, subln, lam_init, store, denoms),
            LOGITS_AHEAD if exact else GUARDED_AHEAD)
        if exact:
            return units

        def check():
            redo[0] = _denominators_bad(denoms)

        return units + [check]

    def ffn_units(slot):
        return _ffn_units(slice(0, tq), x_ref, mix.at[slot], mod_ref, wo_ref, nf_ref, wgu_ref, wd_ref, o_ref)

    @pl.when(jnp.logical_and(s % tiles_per_request == 0, s < n_tiles))
    def _():
        _stage_kv(k_all, vt_all, ka_ref[...], va_ref[...], kb_ref[...], vb_ref[...],
                  (cka_ref, cva_ref, ckb_ref, cvb_ref))

    @pl.when(s == 0)
    def _():
        for unit in attention_units(0):
            unit()

    @pl.when(jnp.logical_and(s > 0, s < n_tiles))
    def _():
        for unit in _interleave(attention_units(s % 2), ffn_units((s - 1) % 2)):
            unit()

    @pl.when(jnp.logical_and(s < n_tiles, redo[0] != 0))
    def _():
        for unit in attention_units(s % 2, exact=True):
            unit()

    @pl.when(s == n_tiles)
    def _():
        for unit in ffn_units((n_tiles - 1) % 2):
            unit()


def _attn_ffn_lagged(layer, x, qa, qb, own, cache, seq, lam_vecs, subln, mod, w_out, norm_ffn, w_gu, w_down,
                     next_mod=None):
    n = x.shape[0]
    tq = Q_TILE
    tpr = seq // tq
    n_tiles = n // tq
    lam_init = 0.8 - 0.6 * math.exp(-0.3 * layer)
    att = lambda s: jnp.minimum(s, n_tiles - 1)
    ffn = lambda s: jnp.maximum(s - 1, 0)
    once = pl.Buffered(1)
    in_specs = [pl.BlockSpec((tq, 512), lambda s: (att(s), 0))] * 2
    in_specs += [pl.BlockSpec((seq, a.shape[1]), lambda s: (att(s) // tpr, 0), pipeline_mode=once) for a in own]
    for a in cache:
        nd = a.ndim - 2
        in_specs.append(pl.BlockSpec((None, None) + a.shape[2:],
                                     lambda s, nd=nd: (att(s) // tpr, layer) + (0,) * nd, pipeline_mode=once))
    in_specs += [_layer_spec((1, HEAD_DIM), layer)] * 4 + [_layer_spec((2 * HEAD_DIM, 1), layer)]
    in_specs += [pl.BlockSpec((tq, D_MODEL), lambda s: (ffn(s), 0)),
                 pl.BlockSpec((None, 1, 6 * D_MODEL), lambda s: (1 + ffn(s) // tpr, 0, 0)),
                 _const_spec((D_MODEL, D_MODEL)), _layer_spec((1, D_MODEL), layer),
                 _const_spec((D_MODEL, 2 * FFN_HIDDEN)), _const_spec((FFN_HIDDEN, D_MODEL))]
    args = [qa, qb, *own, *cache, *lam_vecs, subln, x, mod, w_out, norm_ffn, w_gu, w_down]
    out_specs = [pl.BlockSpec((tq, D_MODEL), lambda s: (ffn(s), 0))]
    out_shape = [jax.ShapeDtypeStruct((n, D_MODEL), F32)]
    if next_mod is not None:
        tn = 6 * D_MODEL // n_tiles
        assert tn * n_tiles == 6 * D_MODEL and tn % LANES == 0
        in_specs += [pl.BlockSpec((MOD_ROWS, D_MODEL), lambda s: (0, 0)),
                     pl.BlockSpec((None, D_MODEL, tn), lambda s: (layer + 1, 0, att(s))),
                     pl.BlockSpec((None, 1, tn), lambda s: (layer + 1, 0, att(s)))]
        args += list(next_mod)
        out_specs.append(pl.BlockSpec((MOD_ROWS, tn), lambda s: (0, att(s))))
        out_shape.append(jax.ShapeDtypeStruct((MOD_ROWS, 6 * D_MODEL), F32))
    keys = seq + cache[0].shape[-1]
    kv_w = own[0].shape[1] + own[2].shape[1]
    return pl.pallas_call(
        functools.partial(_attn_ffn_lagged_kernel, lam_init, tpr, n_tiles, next_mod is not None),
        grid=(n_tiles + 1,),
        in_specs=in_specs,
        out_specs=out_specs,
        out_shape=out_shape,
        scratch_shapes=[pltpu.VMEM((keys, kv_w), BF16), pltpu.VMEM((kv_w, keys), BF16),
                        pltpu.VMEM((2, tq, D_MODEL), BF16), pltpu.SMEM((1,), jnp.int32)],
        compiler_params=pltpu.CompilerParams(
            dimension_semantics=("arbitrary",), vmem_limit_bytes=VMEM_LIMIT),
        name="attn_ffn_latent",
    )(*args)


def _rope_tables(n_tokens):
    t = jnp.arange(n_tokens, dtype=jnp.int32)
    row = (t // GRID_W).astype(F32)
    col = (t % GRID_W).astype(F32)
    axis_dim = HEAD_DIM // 2
    freqs = ROPE_THETA ** (-jnp.arange(0, axis_dim, 2, dtype=F32) / axis_dim)
    ang = jnp.concatenate([row[:, None] * freqs, col[:, None] * freqs], axis=-1)
    c, s = jnp.cos(ang), jnp.sin(ang)
    return jnp.concatenate([c, c, c, c], axis=-1), jnp.concatenate([-s, s, -s, s], axis=-1)


def kernel(x_prompt, x_sample, cache_diff_k, cache_diff_v, cache_gqa_k, cache_gqa_v, c, c_ctx, w_mod, b_mod, norm_attn, w_in, q_norm_a, k_norm_a, lambda_q1, lambda_k1, lambda_q2, lambda_k2, subln, q_norm_b, k_norm_b, w_out, norm_ffn, w_gate_up, w_down):
    batch, seq, d = x_prompt.shape
    dec_batch, dec_seq, _ = x_sample.shape
    depth = w_mod.shape[0]
    past = cache_diff_k.shape[2]

    cond = jnp.concatenate([c_ctx[None], c, jnp.zeros((MOD_ROWS - 1 - dec_batch, d), F32)], axis=0)
    b_mod3 = b_mod.reshape(depth, 1, 6 * d)
    mod = _modulation(cond, w_mod, b_mod3, 0).reshape(MOD_ROWS, 1, 6 * d)

    w_in_b = w_in.astype(BF16)
    seg_ones = jnp.kron(jnp.eye(256 // HEAD_DIM, dtype=F32), jnp.ones((HEAD_DIM, HEAD_DIM), F32)).astype(BF16)
    rope = _rope_tables(dec_seq)

    cache = (jnp.transpose(cache_diff_k, (0, 1, 3, 4, 5, 2)).reshape(dec_batch, depth, DIFF_HEADS, 2 * HEAD_DIM, past),
             cache_diff_v.reshape(dec_batch, depth, past * DIFF_HEADS, 2 * HEAD_DIM),
             jnp.transpose(cache_gqa_k, (0, 1, 3, 4, 2)),
             jnp.transpose(cache_gqa_v, (0, 1, 3, 4, 2)))

    row = lambda a: a.reshape(depth, 1, a.shape[-1])
    gains = (row(q_norm_a), row(k_norm_a), row(q_norm_b), row(k_norm_b))
    lam_vecs = (row(lambda_q1), row(lambda_k1), row(lambda_q2), row(lambda_k2))
    na, nf, sl = row(norm_attn), row(norm_ffn), subln.reshape(depth, 2 * HEAD_DIM, 1)

    yp = x_prompt.reshape(batch * seq, d)
    ys = x_sample.reshape(dec_batch * dec_seq, d)
    lat_tiles = dec_seq // TOKEN_TILE
    new_caches = None
    for l in range(depth):
        qa, ka, va, qb, kb, vb, *rest = _qkv(
            yp, mod, l, False, seq // TOKEN_TILE, na, w_in_b, seg_ones, gains, None, new_caches,
            (w_out, w_gate_up, w_down))
        new_caches, (w_out_b, w_gu_b, w_down_b) = rest[:4], rest[4:]
        yp = _attn_ffn(l, yp, qa, qb, (ka, va, kb, vb), seq, lam_vecs, sl, mod, w_out_b, nf, w_gu_b, w_down_b)

        qa, ka, va, qb, kb, vb = _qkv(
            ys, mod, l, True, lat_tiles, na, w_in_b, seg_ones, gains, rope)
        ys, *nxt = _attn_ffn_lagged(l, ys, qa, qb, (ka, va, kb, vb), cache, dec_seq, lam_vecs, sl,
                                    mod, w_out_b, nf, w_gu_b, w_down_b,
                                    (cond, w_mod, b_mod3) if l + 1 < depth else None)
        if nxt:
            mod = nxt[0].reshape(MOD_ROWS, 1, 6 * d)

    ka_t, va_n, kb_t, vb_t = new_caches
    new_diff_k = jnp.transpose(ka_t.reshape(batch, depth, DIFF_HEADS, 2, HEAD_DIM, seq), (0, 1, 5, 2, 3, 4))
    new_gqa_k = jnp.transpose(kb_t.reshape(batch, depth, GQA_KV_HEADS, HEAD_DIM, seq), (0, 1, 4, 2, 3))
    new_gqa_v = jnp.transpose(vb_t.reshape(batch, depth, GQA_KV_HEADS, HEAD_DIM, seq), (0, 1, 4, 2, 3))
    return (yp.reshape(batch, seq, d), ys.reshape(dec_batch, dec_seq, d),
            new_diff_k, va_n.reshape(batch, depth, seq, DIFF_HEADS, 2 * HEAD_DIM), new_gqa_k, new_gqa_v)
```

```python
import functools
import math

import jax
import jax.numpy as jnp
from jax import lax
from jax.experimental import pallas as pl
from jax.experimental.pallas import tpu as pltpu

F32 = jnp.float32
BF16 = jnp.bfloat16

D_MODEL = 1024
HEAD_DIM = 64
GRID_W = 64
DIFF_HEADS = 4
GQA_KV_HEADS = 2
GQA_REP = 4
FFN_HIDDEN = 2816
IN_COLS = 2304
ROPE_THETA = 10000.0
EPS = 1e-6
LANES = 128
MOD_ROWS = 8
TOKEN_TILE = 256
Q_TILE = 256
VMEM_LIMIT = 56 * 1024 * 1024


def _dot(a, b):
    return jnp.dot(a, b, preferred_element_type=F32)


def _rms(x, gain):
    ms = jnp.mean(x * x, axis=-1, keepdims=True)
    return x * lax.rsqrt(ms + EPS) * gain


def _layer_spec(shape, layer):
    n = len(shape)
    return pl.BlockSpec((None,) + tuple(shape), lambda *_: (layer,) + (0,) * n,
                        pipeline_mode=pl.Buffered(1))


def _const_spec(shape):
    n = len(shape)
    return pl.BlockSpec(shape, lambda *_: (0,) * n, pipeline_mode=pl.Buffered(1))


def _mod_spec(layer, latent, tiles_per_request):
    if latent:
        return pl.BlockSpec((None, None, 1, 6 * D_MODEL),
                            lambda i: (layer, 1 + i // tiles_per_request, 0, 0))
    return pl.BlockSpec((None, None, 1, 6 * D_MODEL), lambda i: (layer, 0, 0, 0))


def _mod_kernel(cond_ref, w_ref, b_ref, o_ref):
    c = cond_ref[...]
    s = c * jax.nn.sigmoid(c)
    o_ref[...] = _dot(s.astype(BF16), w_ref[...].astype(BF16)) + b_ref[...]


def _modulation(cond, w_mod, b_mod):
    depth = w_mod.shape[0]
    tn = 1536
    return pl.pallas_call(
        _mod_kernel,
        grid=(depth, 6 * D_MODEL // tn),
        in_specs=[
            pl.BlockSpec((MOD_ROWS, D_MODEL), lambda l, j: (0, 0)),
            pl.BlockSpec((None, D_MODEL, tn), lambda l, j: (l, 0, j)),
            pl.BlockSpec((None, 1, tn), lambda l, j: (l, 0, j)),
        ],
        out_specs=pl.BlockSpec((None, MOD_ROWS, tn), lambda l, j: (l, 0, j)),
        out_shape=jax.ShapeDtypeStruct((depth, MOD_ROWS, 6 * D_MODEL), F32),
        compiler_params=pltpu.CompilerParams(
            dimension_semantics=("arbitrary", "arbitrary"), vmem_limit_bytes=VMEM_LIMIT),
        name="modulation",
    )(cond, w_mod, b_mod.reshape(depth, 1, 6 * D_MODEL))


def _head_norm(x, seg_ones, gain):
    ss = _dot((x * x).astype(BF16), seg_ones)
    return x * lax.rsqrt(ss * (1.0 / HEAD_DIM) + EPS) * gain


def _swap_halves(x, first_half):
    return jnp.where(first_half, pltpu.roll(x, 96, 1), pltpu.roll(x, 32, 1))


def _rope(x, cos, sin, first_half):
    out = []
    for j in range(x.shape[1] // LANES):
        xj = x[:, j * LANES:(j + 1) * LANES]
        out.append(xj * cos + _swap_halves(xj, first_half) * sin)
    return out[0] if len(out) == 1 else jnp.concatenate(out, axis=1)


def _tile_gain(g):
    return jnp.concatenate([g] * (256 // HEAD_DIM), axis=1)


def _qkv_kernel(latent, n_cast, *refs):
    if latent:
        (x_ref, mod_ref, na_ref, w_ref, seg_ref, gqa_ref, gka_ref, gqb_ref, gkb_ref,
         cos_ref, sin_ref, qa_o, ka_o, va_o, qb_o, kb_o, vb_o) = refs
    else:
        (x_ref, mod_ref, na_ref, w_ref, seg_ref, gqa_ref, gka_ref, gqb_ref, gkb_ref) = refs[:9]
        outs = refs[len(refs) - 10 - n_cast:]
        qa_o, ka_o, va_o, qb_o, kb_o, vb_o, kaf_o, vaf_o, kbf_o, vbf_o = outs[:10]
        for src, dst in zip(refs[9:9 + n_cast], outs[10:]):
            dst[...] = src[...].astype(BF16)
    tm = TOKEN_TILE
    sh = mod_ref[:, 0:D_MODEL]
    sc = mod_ref[:, D_MODEL:2 * D_MODEL]
    seg = seg_ref[...]
    lane = lax.broadcasted_iota(jnp.int32, (1, LANES), 1)
    first_half = (lane % HEAD_DIM) < (HEAD_DIM // 2)
    scale = HEAD_DIM ** -0.5 * math.log2(math.e)

    def project(r):
        h = _rms(x_ref[r, :], na_ref[...]) * (1.0 + sc) + sh
        return _dot(h.astype(BF16), w_ref[...])

    def finish(j, r, proj):
        def qk(lo_col, width, gain_ref):
            gain = _tile_gain(gain_ref[...])
            out = []
            for c0 in range(0, width, 256):
                w = min(256, width - c0)
                out.append(_head_norm(proj[:, lo_col + c0:lo_col + c0 + w], seg[:w, :w], gain[:, :w]))
            y = out[0] if len(out) == 1 else jnp.concatenate(out, axis=1)
            if latent:
                y = _rope(y, cos_ref[r, :], sin_ref[r, :], first_half)
            return y

        qa = qk(0, 512, gqa_ref)
        ka = qk(512, 512, gka_ref)
        va = proj[:, 1024:1536]
        qb = qk(1536, 512, gqb_ref)
        kb = qk(2048, 128, gkb_ref)
        vb = proj[:, 2176:2304]
        qa_o[r, :] = (qa * scale).astype(BF16)
        ka_o[r, :] = ka.astype(BF16)
        va_o[r, :] = va.astype(BF16)
        qb_o[r, :] = (qb * scale).astype(BF16)
        kb_o[r, :] = kb.astype(BF16)
        vb_o[r, :] = vb.astype(BF16)
        if not latent:
            kaf_o[j] = ka.T
            kbf_o[j] = kb.T
            vbf_o[j] = vb.T
            for h in range(DIFF_HEADS):
                vaf_o[j, pl.ds(h, tm, stride=DIFF_HEADS), :] = va[:, h * LANES:(h + 1) * LANES]

    rows = [slice(j * tm, (j + 1) * tm) for j in range(x_ref.shape[0] // tm)]
    pending = [project(rows[0])]
    for j, r in enumerate(rows):
        if j + 1 < len(rows):
            pending.append(project(rows[j + 1]))
        finish(j, r, pending.pop(0))


def _qkv(x, mod4, layer, latent, tiles_per_request, norm_attn, w_in, seg_ones, gains, rope,
         prev_caches=None, cast_weights=()):
    n = x.shape[0]
    sub = 4
    tm = sub * TOKEN_TILE
    depth = w_in.shape[0]
    tok = lambda w: pl.BlockSpec((tm, w), lambda i: (i, 0))
    in_specs = [
        tok(D_MODEL),
        _mod_spec(layer, latent, max(tiles_per_request // sub, 1)),
        _layer_spec((1, D_MODEL), layer),
        _layer_spec((D_MODEL, IN_COLS), layer),
        _const_spec((256, 256)),
    ] + [_layer_spec((1, HEAD_DIM), layer)] * 4
    args = [x, mod4, norm_attn, w_in, seg_ones, *gains]
    aliases = {}
    out_specs = [tok(512), tok(512), tok(512), tok(512), tok(LANES), tok(LANES)]
    out_shape = [jax.ShapeDtypeStruct((n, w), BF16) for w in (512, 512, 512, 512, LANES, LANES)]
    if latent:
        rope_spec = pl.BlockSpec((tm, LANES), lambda i: (i % (tiles_per_request // sub), 0))
        in_specs += [rope_spec, rope_spec]
        args += list(rope)
    else:
        assert tiles_per_request == 1, "context requests must be one token tile long"
        seq = TOKEN_TILE
        req = n // seq
        slab = lambda *dims: pl.BlockSpec((sub, None) + dims, lambda i: (i, layer) + (0,) * len(dims))
        out_specs += [slab(512, seq), slab(seq * DIFF_HEADS, LANES), slab(LANES, seq), slab(LANES, seq)]
        out_shape += [jax.ShapeDtypeStruct((req, depth, 512, seq), F32),
                      jax.ShapeDtypeStruct((req, depth, seq * DIFF_HEADS, LANES), F32),
                      jax.ShapeDtypeStruct((req, depth, LANES, seq), F32),
                      jax.ShapeDtypeStruct((req, depth, LANES, seq), F32)]
        steps = n // tm
        for w in cast_weights:
            rows, cols = w.shape[1] // steps, w.shape[2]
            assert rows * steps == w.shape[1] and rows % 16 == 0
            in_specs.append(pl.BlockSpec((None, rows, cols), lambda i: (layer, i, 0)))
            args.append(w)
            out_specs.append(pl.BlockSpec((rows, cols), lambda i: (i, 0)))
            out_shape.append(jax.ShapeDtypeStruct(w.shape[1:], BF16))
        if prev_caches is not None:
            aliases = {len(args) + k: 6 + k for k in range(4)}
            in_specs += [pl.BlockSpec(memory_space=pl.ANY)] * 4
            args += list(prev_caches)
    return pl.pallas_call(
        functools.partial(_qkv_kernel, latent, len(cast_weights)),
        grid=(n // tm,),
        in_specs=in_specs,
        out_specs=out_specs,
        out_shape=out_shape,
        input_output_aliases=aliases,
        compiler_params=pltpu.CompilerParams(
            dimension_semantics=("arbitrary",), vmem_limit_bytes=VMEM_LIMIT),
        name="qkv_latent" if latent else "qkv_context",
    )(*args)


def _pad_rows(x, block):
    z = jnp.zeros_like(x)
    return jnp.concatenate([x, z] if block == 0 else [z, x], axis=0)


ONES_ROWS = 16
SHIFT_KEYS = 8
DENOM_RANGE = (2.0 ** -100, 2.0 ** 100)


def _numerators(st, exact):
    shift = (st if exact else st[:SHIFT_KEYS]).max(axis=0, keepdims=True)
    return jnp.exp2(st - shift).astype(BF16)


def _pv(vt, e):
    d = vt.shape[0]
    ext = _dot(jnp.concatenate([vt, jnp.ones((ONES_ROWS, vt.shape[1]), BF16)], axis=0), e)
    return ext[:d], ext[d:d + 1]


def _denominators_bad(denoms):
    bad = None
    for l in denoms:
        b = jnp.where(jnp.logical_and(l > DENOM_RANGE[0], l < DENOM_RANGE[1]), 0.0, 1.0)
        bad = b if bad is None else jnp.maximum(bad, b)
    return (jnp.max(bad) > 0.0).astype(jnp.int32)


KA_W = DIFF_HEADS * LANES


def _stage_kv(k_all, vt_all, ka, va, kb, vb, cache_refs):
    seq = ka.shape[0]
    k_all[0:seq, 0:KA_W] = ka
    k_all[0:seq, KA_W:] = kb
    vt_all[0:KA_W, 0:seq] = va.astype(F32).T.astype(BF16)
    vt_all[KA_W:, 0:seq] = vb.astype(F32).T.astype(BF16)
    if cache_refs is not None:
        cka_ref, cva_ref, ckb_ref, cvb_ref = cache_refs
        past = cka_ref.shape[2]
        for h in range(DIFF_HEADS):
            cs = slice(h * LANES, (h + 1) * LANES)
            k_all[seq:, cs] = cka_ref[h].T.astype(BF16)
            vt_all[cs, seq:] = cva_ref[pl.ds(h, past, stride=DIFF_HEADS), :].T.astype(BF16)
        k_all[seq:, KA_W:] = jnp.concatenate([ckb_ref[0], ckb_ref[1]], axis=0).T.astype(BF16)
        for g in range(GQA_KV_HEADS):
            vt_all[KA_W + g * HEAD_DIM:KA_W + (g + 1) * HEAD_DIM, seq:] = cvb_ref[g].astype(BF16)


def _lambda(lq1, lk1, lq2, lk2, lam_init):
    return (jnp.exp(jnp.sum(lq1[...] * lk1[...], axis=-1, keepdims=True))
            - jnp.exp(jnp.sum(lq2[...] * lk2[...], axis=-1, keepdims=True)) + lam_init)


def _attention_tasks(qa, qb, k_all, vt_all, lam, subln, lam_init, store, denoms=None):
    tq = qa.shape[0]
    qat = qa.astype(F32).T.astype(BF16)
    qbt = qb.astype(F32).T.astype(BF16)
    exact = denoms is None
    tasks = []

    def first(st):
        return _numerators(st, exact)

    def pv(vt, e):
        num, l = _pv(vt, e)
        if not exact:
            denoms.append(l)
        return num, 1.0 / l

    def diff_finish(e, h, cs):
        num, r = pv(vt_all[cs, :], e)
        ot = num[:, :tq] * r[:, :tq] - num[:, tq:] * (r[:, tq:] * lam)
        ms = jnp.mean(ot * ot, axis=0, keepdims=True)
        ot = ot * lax.rsqrt(ms + EPS) * subln * (1.0 - lam_init)
        store(cs, ot.T)

    def gqa_finish(e, g, os):
        num, r = pv(vt_all[KA_W + g * HEAD_DIM:KA_W + (g + 1) * HEAD_DIM, :], e)
        ot = num * r
        store(os, jnp.concatenate([ot[:, :tq], ot[:, tq:]], axis=0).T)

    for h in range(DIFF_HEADS):
        cs = slice(h * LANES, (h + 1) * LANES)

        def logits(h=h, cs=cs):
            c0 = qat[h * LANES:h * LANES + HEAD_DIM]
            c1 = qat[h * LANES + HEAD_DIM:(h + 1) * LANES]
            return first(_dot(k_all[:, cs], jnp.concatenate([_pad_rows(c0, 0), _pad_rows(c1, 1)], axis=1)))

        tasks.append((logits, lambda e, h=h, cs=cs: diff_finish(e, h, cs)))
    for g in range(GQA_KV_HEADS):
        for rp in range(2):
            r0 = (GQA_REP * g + 2 * rp) * HEAD_DIM
            os = slice(KA_W + r0, KA_W + r0 + LANES)

            def logits(g=g, r0=r0):
                a = qbt[r0:r0 + HEAD_DIM]
                b = qbt[r0 + HEAD_DIM:r0 + LANES]
                return first(_dot(k_all[:, KA_W:], jnp.concatenate([_pad_rows(a, g), _pad_rows(b, g)], axis=1)))

            tasks.append((logits, lambda e, g=g, os=os: gqa_finish(e, g, os)))
    return tasks


LOGITS_AHEAD = 2
GUARDED_AHEAD = 5
N_TASKS = DIFF_HEADS + 2 * GQA_KV_HEADS


def _attention_units(tasks, ahead=LOGITS_AHEAD):
    pending = []

    def prefill(t):
        pending.append(t[0]())

    def step(i):
        if i + ahead < len(tasks):
            pending.append(tasks[i + ahead][0]())
        tasks[i][1](pending.pop(0))

    units = [functools.partial(prefill, t) for t in tasks[:ahead]]
    return units + [functools.partial(step, i) for i in range(len(tasks))]


def _ffn_units(r, x_ref, mix_ref, mod_ref, wo_ref, nf_ref, wgu_ref, wd_ref, o_ref):
    g_a = mod_ref[:, 2 * D_MODEL:3 * D_MODEL]
    sh_f = mod_ref[:, 3 * D_MODEL:4 * D_MODEL]
    sc_f = mod_ref[:, 4 * D_MODEL:5 * D_MODEL]
    g_f = mod_ref[:, 5 * D_MODEL:6 * D_MODEL]
    state = {}

    def attn_residual():
        state["x"] = x_ref[r, :] + g_a * _dot(mix_ref[r, :], wo_ref[...])

    def gate():
        state["h"] = (_rms(state["x"], nf_ref[...]) * (1.0 + sc_f) + sh_f).astype(BF16)
        state["gate"] = _dot(state["h"], wgu_ref[:, :FFN_HIDDEN])

    def up():
        g = state.pop("gate")
        state["act"] = (g * jax.nn.sigmoid(g) * _dot(state.pop("h"), wgu_ref[:, FFN_HIDDEN:])).astype(BF16)

    def ffn_residual():
        o_ref[r, :] = state.pop("x") + g_f * _dot(state.pop("act"), wd_ref[...])

    return [attn_residual, gate, up, ffn_residual]


def _interleave(a, b):
    out, ia, ib = [], 0, 0
    while ia < len(a) or ib < len(b):
        if ib >= len(b) or (ia < len(a) and ia * len(b) <= ib * len(a)):
            out.append(a[ia])
            ia += 1
        else:
            out.append(b[ib])
            ib += 1
    return out


def _attn_ffn_kernel(lam_init, sub, *refs):
    (qa_ref, qb_ref, ka_ref, va_ref, kb_ref, vb_ref, lq1, lk1, lq2, lk2, subln_ref,
     x_ref, mod_ref, wo_ref, nf_ref, wgu_ref, wd_ref, o_ref, k_all, vt_all, mix) = refs
    seq = ka_ref.shape[0] // sub
    lam = _lambda(lq1, lk1, lq2, lk2, lam_init)
    subln = subln_ref[...]
    attn, ffn = [], []
    for j in range(sub):
        r = slice(j * seq, (j + 1) * seq)

        later = []

        def store(cols, tile, r=r):
            mix[r, cols] = tile.astype(mix.dtype)

        def stage(j=j, r=r, later=later, store=store):
            _stage_kv(k_all.at[j], vt_all.at[j], ka_ref[r, :], va_ref[r, :], kb_ref[r, :], vb_ref[r, :], None)
            later.extend(_attention_units(_attention_tasks(
                qa_ref[r, :], qb_ref[r, :], k_all.at[j], vt_all.at[j], lam, subln, lam_init, store)))

        attn.append([stage] + [lambda k=k, later=later: later[k]() for k in range(LOGITS_AHEAD + N_TASKS)])
        ffn.append(_ffn_units(r, x_ref, mix, mod_ref, wo_ref, nf_ref, wgu_ref, wd_ref, o_ref))
    units = attn[0]
    for j in range(1, sub):
        units = units + _interleave(attn[j], ffn[j - 1])
    for unit in units + ffn[sub - 1]:
        unit()


def _attn_ffn(layer, x, qa, qb, own, seq, lam_vecs, subln, mod4, w_out, norm_ffn, w_gu, w_down):
    n = x.shape[0]
    sub = 4
    tm = sub * seq
    lam_init = 0.8 - 0.6 * math.exp(-0.3 * layer)
    tok = lambda w: pl.BlockSpec((tm, w), lambda i: (i, 0))
    kv_w = own[0].shape[1] + own[2].shape[1]
    in_specs = ([tok(512), tok(512)] + [tok(a.shape[1]) for a in own]
                + [_layer_spec((1, HEAD_DIM), layer)] * 4 + [_layer_spec((2 * HEAD_DIM, 1), layer)]
                + [tok(D_MODEL), _mod_spec(layer, False, 1),
                   _const_spec((D_MODEL, D_MODEL)), _layer_spec((1, D_MODEL), layer),
                   _const_spec((D_MODEL, 2 * FFN_HIDDEN)), _const_spec((FFN_HIDDEN, D_MODEL))])
    return pl.pallas_call(
        functools.partial(_attn_ffn_kernel, lam_init, sub),
        grid=(n // tm,),
        in_specs=in_specs,
        out_specs=tok(D_MODEL),
        out_shape=jax.ShapeDtypeStruct((n, D_MODEL), F32),
        scratch_shapes=[pltpu.VMEM((sub, seq, kv_w), BF16), pltpu.VMEM((sub, kv_w, seq), BF16),
                        pltpu.VMEM((tm, D_MODEL), BF16)],
        compiler_params=pltpu.CompilerParams(
            dimension_semantics=("arbitrary",), vmem_limit_bytes=VMEM_LIMIT),
        name="attn_ffn_context",
    )(qa, qb, *own, *lam_vecs, subln, x, mod4, w_out, norm_ffn, w_gu, w_down)


def _attn_ffn_lagged_kernel(lam_init, tiles_per_request, n_tiles, *refs):
    (qa_ref, qb_ref, ka_ref, va_ref, kb_ref, vb_ref, cka_ref, cva_ref, ckb_ref, cvb_ref,
     lq1, lk1, lq2, lk2, subln_ref, x_ref, mod_ref, wo_hbm, nf_ref, wgu_hbm, wd_hbm,
     o_ref, k_all, vt_all, mix, wo_ref, wgu_ref, wd_ref, wsem, redo) = refs
    s = pl.program_id(0)
    tq = qa_ref.shape[0]

    def weight_copies():
        return [pltpu.make_async_copy(src, dst, wsem.at[k])
                for k, (src, dst) in enumerate(((wo_hbm, wo_ref), (wgu_hbm, wgu_ref), (wd_hbm, wd_ref)))]

    @pl.when(s == 0)
    def _():
        for cp in weight_copies():
            cp.start()

    @pl.when(s == 1)
    def _():
        for cp in weight_copies():
            cp.wait()

    lam = _lambda(lq1, lk1, lq2, lk2, lam_init)
    subln = subln_ref[...]

    def attention_units(slot, exact=False):
        def store(cols, tile):
            mix[slot, :, cols] = tile.astype(mix.dtype)

        denoms = None if exact else []
        units = _attention_units(_attention_tasks(
            qa_ref[...], qb_ref[...], k_all, vt_all, lam, subln, lam_init, store, denoms),
            LOGITS_AHEAD if exact else GUARDED_AHEAD)
        if exact:
            return units

        def check():
            redo[0] = _denominators_bad(denoms)

        return units + [check]

    def ffn_units(slot):
        return _ffn_units(slice(0, tq), x_ref, mix.at[slot], mod_ref, wo_ref, nf_ref, wgu_ref, wd_ref, o_ref)

    @pl.when(jnp.logical_and(s % tiles_per_request == 0, s < n_tiles))
    def _():
        _stage_kv(k_all, vt_all, ka_ref[...], va_ref[...], kb_ref[...], vb_ref[...],
                  (cka_ref, cva_ref, ckb_ref, cvb_ref))

    @pl.when(s == 0)
    def _():
        for unit in attention_units(0):
            unit()

    @pl.when(jnp.logical_and(s > 0, s < n_tiles))
    def _():
        for unit in _interleave(attention_units(s % 2), ffn_units((s - 1) % 2)):
            unit()

    @pl.when(jnp.logical_and(s < n_tiles, redo[0] != 0))
    def _():
        for unit in attention_units(s % 2, exact=True):
            unit()

    @pl.when(s == n_tiles)
    def _():
        for unit in ffn_units((n_tiles - 1) % 2):
            unit()


def _attn_ffn_lagged(layer, x, qa, qb, own, cache, seq, lam_vecs, subln, mod4, w_out, norm_ffn, w_gu, w_down):
    n = x.shape[0]
    tq = Q_TILE
    tpr = seq // tq
    n_tiles = n // tq
    lam_init = 0.8 - 0.6 * math.exp(-0.3 * layer)
    att = lambda s: jnp.minimum(s, n_tiles - 1)
    ffn = lambda s: jnp.maximum(s - 1, 0)
    once = pl.Buffered(1)
    in_specs = [pl.BlockSpec((tq, 512), lambda s: (att(s), 0))] * 2
    in_specs += [pl.BlockSpec((seq, a.shape[1]), lambda s: (att(s) // tpr, 0), pipeline_mode=once) for a in own]
    for a in cache:
        nd = a.ndim - 2
        in_specs.append(pl.BlockSpec((None, None) + a.shape[2:],
                                     lambda s, nd=nd: (att(s) // tpr, layer) + (0,) * nd, pipeline_mode=once))
    in_specs += [_layer_spec((1, HEAD_DIM), layer)] * 4 + [_layer_spec((2 * HEAD_DIM, 1), layer)]
    in_specs += [pl.BlockSpec((tq, D_MODEL), lambda s: (ffn(s), 0)),
                 pl.BlockSpec((None, None, 1, 6 * D_MODEL), lambda s: (layer, 1 + ffn(s) // tpr, 0, 0)),
                 pl.BlockSpec(memory_space=pl.ANY), _layer_spec((1, D_MODEL), layer),
                 pl.BlockSpec(memory_space=pl.ANY), pl.BlockSpec(memory_space=pl.ANY)]
    keys = seq + cache[0].shape[-1]
    kv_w = own[0].shape[1] + own[2].shape[1]
    return pl.pallas_call(
        functools.partial(_attn_ffn_lagged_kernel, lam_init, tpr, n_tiles),
        grid=(n_tiles + 1,),
        in_specs=in_specs,
        out_specs=pl.BlockSpec((tq, D_MODEL), lambda s: (ffn(s), 0)),
        out_shape=jax.ShapeDtypeStruct((n, D_MODEL), F32),
        scratch_shapes=[pltpu.VMEM((keys, kv_w), BF16), pltpu.VMEM((kv_w, keys), BF16),
                        pltpu.VMEM((2, tq, D_MODEL), BF16),
                        pltpu.VMEM(w_out.shape, BF16), pltpu.VMEM(w_gu.shape, BF16), pltpu.VMEM(w_down.shape, BF16),
                        pltpu.SemaphoreType.DMA((3,)), pltpu.SMEM((1,), jnp.int32)],
        compiler_params=pltpu.CompilerParams(
            dimension_semantics=("arbitrary",), vmem_limit_bytes=VMEM_LIMIT),
        name="attn_ffn_latent",
    )(qa, qb, *own, *cache, *lam_vecs, subln, x, mod4, w_out, norm_ffn, w_gu, w_down)


def _rope_tables(n_tokens):
    t = jnp.arange(n_tokens, dtype=jnp.int32)
    row = (t // GRID_W).astype(F32)
    col = (t % GRID_W).astype(F32)
    axis_dim = HEAD_DIM // 2
    freqs = ROPE_THETA ** (-jnp.arange(0, axis_dim, 2, dtype=F32) / axis_dim)
    ang = jnp.concatenate([row[:, None] * freqs, col[:, None] * freqs], axis=-1)
    c, s = jnp.cos(ang), jnp.sin(ang)
    return jnp.concatenate([c, c, c, c], axis=-1), jnp.concatenate([-s, s, -s, s], axis=-1)


def kernel(x_prompt, x_sample, cache_diff_k, cache_diff_v, cache_gqa_k, cache_gqa_v, c, c_ctx, w_mod, b_mod, norm_attn, w_in, q_norm_a, k_norm_a, lambda_q1, lambda_k1, lambda_q2, lambda_k2, subln, q_norm_b, k_norm_b, w_out, norm_ffn, w_gate_up, w_down):
    batch, seq, d = x_prompt.shape
    dec_batch, dec_seq, _ = x_sample.shape
    depth = w_mod.shape[0]
    past = cache_diff_k.shape[2]

    cond = jnp.concatenate([c_ctx[None], c, jnp.zeros((MOD_ROWS - 1 - dec_batch, d), F32)], axis=0)
    mod = _modulation(cond, w_mod, b_mod)
    mod4 = mod.reshape(depth, MOD_ROWS, 1, 6 * d)

    w_in_b = w_in.astype(BF16)
    seg_ones = jnp.kron(jnp.eye(256 // HEAD_DIM, dtype=F32), jnp.ones((HEAD_DIM, HEAD_DIM), F32)).astype(BF16)
    rope = _rope_tables(dec_seq)

    cache = (jnp.transpose(cache_diff_k, (0, 1, 3, 4, 5, 2)).reshape(dec_batch, depth, DIFF_HEADS, 2 * HEAD_DIM, past),
             cache_diff_v.reshape(dec_batch, depth, past * DIFF_HEADS, 2 * HEAD_DIM),
             jnp.transpose(cache_gqa_k, (0, 1, 3, 4, 2)),
             jnp.transpose(cache_gqa_v, (0, 1, 3, 4, 2)))

    row = lambda a: a.reshape(depth, 1, a.shape[-1])
    gains = (row(q_norm_a), row(k_norm_a), row(q_norm_b), row(k_norm_b))
    lam_vecs = (row(lambda_q1), row(lambda_k1), row(lambda_q2), row(lambda_k2))
    na, nf, sl = row(norm_attn), row(norm_ffn), subln.reshape(depth, 2 * HEAD_DIM, 1)

    yp = x_prompt.reshape(batch * seq, d)
    ys = x_sample.reshape(dec_batch * dec_seq, d)
    lat_tiles = dec_seq // TOKEN_TILE
    new_caches = None
    for l in range(depth):
        qa, ka, va, qb, kb, vb, *rest = _qkv(
            yp, mod4, l, False, seq // TOKEN_TILE, na, w_in_b, seg_ones, gains, None, new_caches,
            (w_out, w_gate_up, w_down))
        new_caches, (w_out_b, w_gu_b, w_down_b) = rest[:4], rest[4:]
        yp = _attn_ffn(l, yp, qa, qb, (ka, va, kb, vb), seq, lam_vecs, sl, mod4, w_out_b, nf, w_gu_b, w_down_b)

        qa, ka, va, qb, kb, vb = _qkv(
            ys, mod4, l, True, lat_tiles, na, w_in_b, seg_ones, gains, rope)
        ys = _attn_ffn_lagged(l, ys, qa, qb, (ka, va, kb, vb), cache, dec_seq, lam_vecs, sl,
                              mod4, w_out_b, nf, w_gu_b, w_down_b)

    ka_t, va_n, kb_t, vb_t = new_caches
    new_diff_k = jnp.transpose(ka_t.reshape(batch, depth, DIFF_HEADS, 2, HEAD_DIM, seq), (0, 1, 5, 2, 3, 4))
    new_gqa_k = jnp.transpose(kb_t.reshape(batch, depth, GQA_KV_HEADS, HEAD_DIM, seq), (0, 1, 4, 2, 3))
    new_gqa_v = jnp.transpose(vb_t.reshape(batch, depth, GQA_KV_HEADS, HEAD_DIM, seq), (0, 1, 4, 2, 3))
    return (yp.reshape(batch, seq, d), ys.reshape(dec_batch, dec_seq, d),
            new_diff_k, va_n.reshape(batch, depth, seq, DIFF_HEADS, 2 * HEAD_DIM), new_gqa_k, new_gqa_v)
```

```python
import functools
import math

import jax
import jax.numpy as jnp
from jax import lax
from jax.experimental import pallas as pl
from jax.experimental.pallas import tpu as pltpu

F32 = jnp.float32
BF16 = jnp.bfloat16

D_MODEL = 1024
HEAD_DIM = 64
GRID_W = 64
DIFF_HEADS = 4
GQA_KV_HEADS = 2
GQA_REP = 4
FFN_HIDDEN = 2816
IN_COLS = 2304
ROPE_THETA = 10000.0
EPS = 1e-6
LANES = 128
MOD_ROWS = 8
TOKEN_TILE = 256
Q_TILE = 256
VMEM_LIMIT = 56 * 1024 * 1024


def _dot(a, b):
    return jnp.dot(a, b, preferred_element_type=F32)


def _rms(x, gain):
    ms = jnp.mean(x * x, axis=-1, keepdims=True)
    return x * lax.rsqrt(ms + EPS) * gain


def _layer_spec(shape, layer):
    n = len(shape)
    return pl.BlockSpec((None,) + tuple(shape), lambda *_: (layer,) + (0,) * n,
                        pipeline_mode=pl.Buffered(1))


def _const_spec(shape):
    n = len(shape)
    return pl.BlockSpec(shape, lambda *_: (0,) * n, pipeline_mode=pl.Buffered(1))


def _mod_spec(layer, latent, tiles_per_request):
    if latent:
        return pl.BlockSpec((None, None, 1, 6 * D_MODEL),
                            lambda i: (layer, 1 + i // tiles_per_request, 0, 0))
    return pl.BlockSpec((None, None, 1, 6 * D_MODEL), lambda i: (layer, 0, 0, 0))


def _mod_kernel(cond_ref, w_ref, b_ref, o_ref):
    c = cond_ref[...]
    s = c * jax.nn.sigmoid(c)
    o_ref[...] = _dot(s.astype(BF16), w_ref[...].astype(BF16)) + b_ref[...]


def _modulation(cond, w_mod, b_mod):
    depth = w_mod.shape[0]
    tn = 1536
    return pl.pallas_call(
        _mod_kernel,
        grid=(depth, 6 * D_MODEL // tn),
        in_specs=[
            pl.BlockSpec((MOD_ROWS, D_MODEL), lambda l, j: (0, 0)),
            pl.BlockSpec((None, D_MODEL, tn), lambda l, j: (l, 0, j)),
            pl.BlockSpec((None, 1, tn), lambda l, j: (l, 0, j)),
        ],
        out_specs=pl.BlockSpec((None, MOD_ROWS, tn), lambda l, j: (l, 0, j)),
        out_shape=jax.ShapeDtypeStruct((depth, MOD_ROWS, 6 * D_MODEL), F32),
        compiler_params=pltpu.CompilerParams(
            dimension_semantics=("arbitrary", "arbitrary"), vmem_limit_bytes=VMEM_LIMIT),
        name="modulation",
    )(cond, w_mod, b_mod.reshape(depth, 1, 6 * D_MODEL))


def _head_norm(x, seg_ones, gain):
    ss = _dot((x * x).astype(BF16), seg_ones)
    return x * lax.rsqrt(ss * (1.0 / HEAD_DIM) + EPS) * gain


def _swap_halves(x, first_half):
    return jnp.where(first_half, pltpu.roll(x, 96, 1), pltpu.roll(x, 32, 1))


def _rope(x, cos, sin, first_half):
    out = []
    for j in range(x.shape[1] // LANES):
        xj = x[:, j * LANES:(j + 1) * LANES]
        out.append(xj * cos + _swap_halves(xj, first_half) * sin)
    return out[0] if len(out) == 1 else jnp.concatenate(out, axis=1)


def _tile_gain(g):
    return jnp.concatenate([g] * (256 // HEAD_DIM), axis=1)


def _qkv_kernel(latent, n_cast, *refs):
    if latent:
        (x_ref, mod_ref, na_ref, w_ref, seg_ref, gqa_ref, gka_ref, gqb_ref, gkb_ref,
         cos_ref, sin_ref, qa_o, ka_o, va_o, qb_o, kb_o, vb_o) = refs
    else:
        (x_ref, mod_ref, na_ref, w_ref, seg_ref, gqa_ref, gka_ref, gqb_ref, gkb_ref) = refs[:9]
        outs = refs[len(refs) - 10 - n_cast:]
        qa_o, ka_o, va_o, qb_o, kb_o, vb_o, kaf_o, vaf_o, kbf_o, vbf_o = outs[:10]
        for src, dst in zip(refs[9:9 + n_cast], outs[10:]):
            dst[...] = src[...].astype(BF16)
    tm = TOKEN_TILE
    sh = mod_ref[:, 0:D_MODEL]
    sc = mod_ref[:, D_MODEL:2 * D_MODEL]
    seg = seg_ref[...]
    lane = lax.broadcasted_iota(jnp.int32, (1, LANES), 1)
    first_half = (lane % HEAD_DIM) < (HEAD_DIM // 2)
    scale = HEAD_DIM ** -0.5 * math.log2(math.e)

    def project(r):
        h = _rms(x_ref[r, :], na_ref[...]) * (1.0 + sc) + sh
        return _dot(h.astype(BF16), w_ref[...])

    def finish(j, r, proj):
        def qk(lo_col, width, gain_ref):
            gain = _tile_gain(gain_ref[...])
            out = []
            for c0 in range(0, width, 256):
                w = min(256, width - c0)
                out.append(_head_norm(proj[:, lo_col + c0:lo_col + c0 + w], seg[:w, :w], gain[:, :w]))
            y = out[0] if len(out) == 1 else jnp.concatenate(out, axis=1)
            if latent:
                y = _rope(y, cos_ref[r, :], sin_ref[r, :], first_half)
            return y

        qa = qk(0, 512, gqa_ref)
        ka = qk(512, 512, gka_ref)
        va = proj[:, 1024:1536]
        qb = qk(1536, 512, gqb_ref)
        kb = qk(2048, 128, gkb_ref)
        vb = proj[:, 2176:2304]
        qa_o[r, :] = (qa * scale).astype(BF16)
        ka_o[r, :] = ka.astype(BF16)
        va_o[r, :] = va.astype(BF16)
        qb_o[r, :] = (qb * scale).astype(BF16)
        kb_o[r, :] = kb.astype(BF16)
        vb_o[r, :] = vb.astype(BF16)
        if not latent:
            kaf_o[j] = ka.T
            kbf_o[j] = kb.T
            vbf_o[j] = vb.T
            for h in range(DIFF_HEADS):
                vaf_o[j, pl.ds(h, tm, stride=DIFF_HEADS), :] = va[:, h * LANES:(h + 1) * LANES]

    rows = [slice(j * tm, (j + 1) * tm) for j in range(x_ref.shape[0] // tm)]
    pending = [project(rows[0])]
    for j, r in enumerate(rows):
        if j + 1 < len(rows):
            pending.append(project(rows[j + 1]))
        finish(j, r, pending.pop(0))


def _qkv(x, mod4, layer, latent, tiles_per_request, norm_attn, w_in, seg_ones, gains, rope,
         prev_caches=None, cast_weights=()):
    n = x.shape[0]
    sub = 4
    tm = sub * TOKEN_TILE
    depth = w_in.shape[0]
    tok = lambda w: pl.BlockSpec((tm, w), lambda i: (i, 0))
    in_specs = [
        tok(D_MODEL),
        _mod_spec(layer, latent, max(tiles_per_request // sub, 1)),
        _layer_spec((1, D_MODEL), layer),
        _layer_spec((D_MODEL, IN_COLS), layer),
        _const_spec((256, 256)),
    ] + [_layer_spec((1, HEAD_DIM), layer)] * 4
    args = [x, mod4, norm_attn, w_in, seg_ones, *gains]
    aliases = {}
    out_specs = [tok(512), tok(512), tok(512), tok(512), tok(LANES), tok(LANES)]
    out_shape = [jax.ShapeDtypeStruct((n, w), BF16) for w in (512, 512, 512, 512, LANES, LANES)]
    if latent:
        rope_spec = pl.BlockSpec((tm, LANES), lambda i: (i % (tiles_per_request // sub), 0))
        in_specs += [rope_spec, rope_spec]
        args += list(rope)
    else:
        assert tiles_per_request == 1, "context requests must be one token tile long"
        seq = TOKEN_TILE
        req = n // seq
        slab = lambda *dims: pl.BlockSpec((sub, None) + dims, lambda i: (i, layer) + (0,) * len(dims))
        out_specs += [slab(512, seq), slab(seq * DIFF_HEADS, LANES), slab(LANES, seq), slab(LANES, seq)]
        out_shape += [jax.ShapeDtypeStruct((req, depth, 512, seq), F32),
                      jax.ShapeDtypeStruct((req, depth, seq * DIFF_HEADS, LANES), F32),
                      jax.ShapeDtypeStruct((req, depth, LANES, seq), F32),
                      jax.ShapeDtypeStruct((req, depth, LANES, seq), F32)]
        steps = n // tm
        for w in cast_weights:
            rows, cols = w.shape[1] // steps, w.shape[2]
            assert rows * steps == w.shape[1] and rows % 16 == 0
            in_specs.append(pl.BlockSpec((None, rows, cols), lambda i: (layer, i, 0)))
            args.append(w)
            out_specs.append(pl.BlockSpec((rows, cols), lambda i: (i, 0)))
            out_shape.append(jax.ShapeDtypeStruct(w.shape[1:], BF16))
        if prev_caches is not None:
            aliases = {len(args) + k: 6 + k for k in range(4)}
            in_specs += [pl.BlockSpec(memory_space=pl.ANY)] * 4
            args += list(prev_caches)
    return pl.pallas_call(
        functools.partial(_qkv_kernel, latent, len(cast_weights)),
        grid=(n // tm,),
        in_specs=in_specs,
        out_specs=out_specs,
        out_shape=out_shape,
        input_output_aliases=aliases,
        compiler_params=pltpu.CompilerParams(
            dimension_semantics=("arbitrary",), vmem_limit_bytes=VMEM_LIMIT),
        name="qkv_latent" if latent else "qkv_context",
    )(*args)


def _pad_rows(x, block):
    z = jnp.zeros_like(x)
    return jnp.concatenate([x, z] if block == 0 else [z, x], axis=0)


ONES_ROWS = 16
SHIFT_KEYS = 8
DENOM_RANGE = (2.0 ** -100, 2.0 ** 100)


def _numerators(st, exact):
    shift = (st if exact else st[:SHIFT_KEYS]).max(axis=0, keepdims=True)
    return jnp.exp2(st - shift).astype(BF16)


def _pv(vt, e):
    d = vt.shape[0]
    ext = _dot(jnp.concatenate([vt, jnp.ones((ONES_ROWS, vt.shape[1]), BF16)], axis=0), e)
    return ext[:d], ext[d:d + 1]


def _denominators_bad(denoms):
    bad = None
    for l in denoms:
        b = jnp.where(jnp.logical_and(l > DENOM_RANGE[0], l < DENOM_RANGE[1]), 0.0, 1.0)
        bad = b if bad is None else jnp.maximum(bad, b)
    return (jnp.max(bad) > 0.0).astype(jnp.int32)


KA_W = DIFF_HEADS * LANES


def _stage_kv(k_all, vt_all, ka, va, kb, vb, cache_refs):
    seq = ka.shape[0]
    k_all[0:seq, 0:KA_W] = ka
    k_all[0:seq, KA_W:] = kb
    vt_all[0:KA_W, 0:seq] = va.astype(F32).T.astype(BF16)
    vt_all[KA_W:, 0:seq] = vb.astype(F32).T.astype(BF16)
    if cache_refs is not None:
        cka_ref, cva_ref, ckb_ref, cvb_ref = cache_refs
        past = cka_ref.shape[2]
        for h in range(DIFF_HEADS):
            cs = slice(h * LANES, (h + 1) * LANES)
            k_all[seq:, cs] = cka_ref[h].T.astype(BF16)
            vt_all[cs, seq:] = cva_ref[pl.ds(h, past, stride=DIFF_HEADS), :].T.astype(BF16)
        k_all[seq:, KA_W:] = jnp.concatenate([ckb_ref[0], ckb_ref[1]], axis=0).T.astype(BF16)
        for g in range(GQA_KV_HEADS):
            vt_all[KA_W + g * HEAD_DIM:KA_W + (g + 1) * HEAD_DIM, seq:] = cvb_ref[g].astype(BF16)


def _lambda(lq1, lk1, lq2, lk2, lam_init):
    return (jnp.exp(jnp.sum(lq1[...] * lk1[...], axis=-1, keepdims=True))
            - jnp.exp(jnp.sum(lq2[...] * lk2[...], axis=-1, keepdims=True)) + lam_init)


def _attention_tasks(qa, qb, k_all, vt_all, lam, subln, lam_init, store, denoms=None):
    tq = qa.shape[0]
    qat = qa.astype(F32).T.astype(BF16)
    qbt = qb.astype(F32).T.astype(BF16)
    exact = denoms is None
    tasks = []

    def first(st):
        return _numerators(st, exact)

    def pv(vt, e):
        num, l = _pv(vt, e)
        if not exact:
            denoms.append(l)
        return num, 1.0 / l

    def diff_finish(e, h, cs):
        num, r = pv(vt_all[cs, :], e)
        ot = num[:, :tq] * r[:, :tq] - num[:, tq:] * (r[:, tq:] * lam)
        ms = jnp.mean(ot * ot, axis=0, keepdims=True)
        ot = ot * lax.rsqrt(ms + EPS) * subln * (1.0 - lam_init)
        store(cs, ot.T)

    def gqa_finish(e, g, os):
        num, r = pv(vt_all[KA_W + g * HEAD_DIM:KA_W + (g + 1) * HEAD_DIM, :], e)
        ot = num * r
        store(os, jnp.concatenate([ot[:, :tq], ot[:, tq:]], axis=0).T)

    for h in range(DIFF_HEADS):
        cs = slice(h * LANES, (h + 1) * LANES)

        def logits(h=h, cs=cs):
            c0 = qat[h * LANES:h * LANES + HEAD_DIM]
            c1 = qat[h * LANES + HEAD_DIM:(h + 1) * LANES]
            return first(_dot(k_all[:, cs], jnp.concatenate([_pad_rows(c0, 0), _pad_rows(c1, 1)], axis=1)))

        tasks.append((logits, lambda e, h=h, cs=cs: diff_finish(e, h, cs)))
    for g in range(GQA_KV_HEADS):
        for rp in range(2):
            r0 = (GQA_REP * g + 2 * rp) * HEAD_DIM
            os = slice(KA_W + r0, KA_W + r0 + LANES)

            def logits(g=g, r0=r0):
                a = qbt[r0:r0 + HEAD_DIM]
                b = qbt[r0 + HEAD_DIM:r0 + LANES]
                return first(_dot(k_all[:, KA_W:], jnp.concatenate([_pad_rows(a, g), _pad_rows(b, g)], axis=1)))

            tasks.append((logits, lambda e, g=g, os=os: gqa_finish(e, g, os)))
    return tasks


LOGITS_AHEAD = 2
GUARDED_AHEAD = 5
N_TASKS = DIFF_HEADS + 2 * GQA_KV_HEADS


def _attention_units(tasks, ahead=LOGITS_AHEAD):
    pending = []

    def prefill(t):
        pending.append(t[0]())

    def step(i):
        if i + ahead < len(tasks):
            pending.append(tasks[i + ahead][0]())
        tasks[i][1](pending.pop(0))

    units = [functools.partial(prefill, t) for t in tasks[:ahead]]
    return units + [functools.partial(step, i) for i in range(len(tasks))]


def _ffn_units(r, x_ref, mix_ref, mod_ref, wo_ref, nf_ref, wgu_ref, wd_ref, o_ref):
    g_a = mod_ref[:, 2 * D_MODEL:3 * D_MODEL]
    sh_f = mod_ref[:, 3 * D_MODEL:4 * D_MODEL]
    sc_f = mod_ref[:, 4 * D_MODEL:5 * D_MODEL]
    g_f = mod_ref[:, 5 * D_MODEL:6 * D_MODEL]
    state = {}

    def attn_residual():
        state["x"] = x_ref[r, :] + g_a * _dot(mix_ref[r, :], wo_ref[...])

    def gate():
        state["h"] = (_rms(state["x"], nf_ref[...]) * (1.0 + sc_f) + sh_f).astype(BF16)
        state["gate"] = _dot(state["h"], wgu_ref[:, :FFN_HIDDEN])

    def up():
        g = state.pop("gate")
        state["act"] = (g * jax.nn.sigmoid(g) * _dot(state.pop("h"), wgu_ref[:, FFN_HIDDEN:])).astype(BF16)

    def ffn_residual():
        o_ref[r, :] = state.pop("x") + g_f * _dot(state.pop("act"), wd_ref[...])

    return [attn_residual, gate, up, ffn_residual]


def _interleave(a, b):
    out, ia, ib = [], 0, 0
    while ia < len(a) or ib < len(b):
        if ib >= len(b) or (ia < len(a) and ia * len(b) <= ib * len(a)):
            out.append(a[ia])
            ia += 1
        else:
            out.append(b[ib])
            ib += 1
    return out


def _attn_ffn_kernel(lam_init, sub, *refs):
    (qa_ref, qb_ref, ka_ref, va_ref, kb_ref, vb_ref, lq1, lk1, lq2, lk2, subln_ref,
     x_ref, mod_ref, wo_ref, nf_ref, wgu_ref, wd_ref, o_ref, k_all, vt_all, mix) = refs
    seq = ka_ref.shape[0] // sub
    lam = _lambda(lq1, lk1, lq2, lk2, lam_init)
    subln = subln_ref[...]
    attn, ffn = [], []
    for j in range(sub):
        r = slice(j * seq, (j + 1) * seq)

        later = []

        def store(cols, tile, r=r):
            mix[r, cols] = tile.astype(mix.dtype)

        def stage(j=j, r=r, later=later, store=store):
            _stage_kv(k_all.at[j], vt_all.at[j], ka_ref[r, :], va_ref[r, :], kb_ref[r, :], vb_ref[r, :], None)
            later.extend(_attention_units(_attention_tasks(
                qa_ref[r, :], qb_ref[r, :], k_all.at[j], vt_all.at[j], lam, subln, lam_init, store)))

        attn.append([stage] + [lambda k=k, later=later: later[k]() for k in range(LOGITS_AHEAD + N_TASKS)])
        ffn.append(_ffn_units(r, x_ref, mix, mod_ref, wo_ref, nf_ref, wgu_ref, wd_ref, o_ref))
    units = attn[0]
    for j in range(1, sub):
        units = units + _interleave(attn[j], ffn[j - 1])
    for unit in units + ffn[sub - 1]:
        unit()


def _attn_ffn(layer, x, qa, qb, own, seq, lam_vecs, subln, mod4, w_out, norm_ffn, w_gu, w_down):
    n = x.shape[0]
    sub = 4
    tm = sub * seq
    lam_init = 0.8 - 0.6 * math.exp(-0.3 * layer)
    tok = lambda w: pl.BlockSpec((tm, w), lambda i: (i, 0))
    kv_w = own[0].shape[1] + own[2].shape[1]
    in_specs = ([tok(512), tok(512)] + [tok(a.shape[1]) for a in own]
                + [_layer_spec((1, HEAD_DIM), layer)] * 4 + [_layer_spec((2 * HEAD_DIM, 1), layer)]
                + [tok(D_MODEL), _mod_spec(layer, False, 1),
                   _const_spec((D_MODEL, D_MODEL)), _layer_spec((1, D_MODEL), layer),
                   _const_spec((D_MODEL, 2 * FFN_HIDDEN)), _const_spec((FFN_HIDDEN, D_MODEL))])
    return pl.pallas_call(
        functools.partial(_attn_ffn_kernel, lam_init, sub),
        grid=(n // tm,),
        in_specs=in_specs,
        out_specs=tok(D_MODEL),
        out_shape=jax.ShapeDtypeStruct((n, D_MODEL), F32),
        scratch_shapes=[pltpu.VMEM((sub, seq, kv_w), BF16), pltpu.VMEM((sub, kv_w, seq), BF16),
                        pltpu.VMEM((tm, D_MODEL), BF16)],
        compiler_params=pltpu.CompilerParams(
            dimension_semantics=("arbitrary",), vmem_limit_bytes=VMEM_LIMIT),
        name="attn_ffn_context",
    )(qa, qb, *own, *lam_vecs, subln, x, mod4, w_out, norm_ffn, w_gu, w_down)


def _attn_ffn_lagged_kernel(lam_init, layer, tiles_per_request, n_tiles, *refs):
    (qa_ref, qb_ref, *kv_hbm) = refs[:10]
    (lq1, lk1, lq2, lk2, subln_ref, x_ref, mod_ref, wo_hbm, nf_ref, wgu_hbm, wd_hbm,
     o_ref, k_all, vt_all, mix, wo_ref, wgu_ref, wd_ref, wsem) = refs[10:29]
    kv_ref, kvsem, redo = refs[29:37], refs[37], refs[38]
    s = pl.program_id(0)
    tq = qa_ref.shape[0]
    seq = kv_ref[0].shape[0]
    requests = n_tiles // tiles_per_request

    def weight_copies():
        return [pltpu.make_async_copy(src, dst, wsem.at[k])
                for k, (src, dst) in enumerate(((wo_hbm, wo_ref), (wgu_hbm, wgu_ref), (wd_hbm, wd_ref)))]

    def kv_copies(b):
        srcs = [h.at[pl.ds(b * seq, seq), :] for h in kv_hbm[:4]] + [h.at[b, layer] for h in kv_hbm[4:]]
        return [pltpu.make_async_copy(src, dst, kvsem.at[k]) for k, (src, dst) in enumerate(zip(srcs, kv_ref))]

    @pl.when(s == 0)
    def _():
        for cp in kv_copies(0) + weight_copies():
            cp.start()

    @pl.when(s == 1)
    def _():
        for cp in weight_copies():
            cp.wait()

    nxt = s // tiles_per_request + 1

    @pl.when(jnp.logical_and(s % tiles_per_request == 1, nxt < requests))
    def _():
        for cp in kv_copies(nxt):
            cp.start()

    lam = _lambda(lq1, lk1, lq2, lk2, lam_init)
    subln = subln_ref[...]

    def attention_units(slot, exact=False):
        def store(cols, tile):
            mix[slot, :, cols] = tile.astype(mix.dtype)

        denoms = None if exact else []
        units = _attention_units(_attention_tasks(
            qa_ref[...], qb_ref[...], k_all, vt_all, lam, subln, lam_init, store, denoms),
            LOGITS_AHEAD if exact else GUARDED_AHEAD)
        if exact:
            return units

        def check():
            redo[0] = _denominators_bad(denoms)

        return units + [check]

    def ffn_units(slot):
        return _ffn_units(slice(0, tq), x_ref, mix.at[slot], mod_ref, wo_ref, nf_ref, wgu_ref, wd_ref, o_ref)

    @pl.when(jnp.logical_and(s % tiles_per_request == 0, s < n_tiles))
    def _():
        for cp in kv_copies(s // tiles_per_request):
            cp.wait()
        ka_ref, va_ref, kb_ref, vb_ref = kv_ref[:4]
        _stage_kv(k_all, vt_all, ka_ref[...], va_ref[...], kb_ref[...], vb_ref[...], kv_ref[4:])

    @pl.when(s == 0)
    def _():
        for unit in attention_units(0):
            unit()

    @pl.when(jnp.logical_and(s > 0, s < n_tiles))
    def _():
        for unit in _interleave(attention_units(s % 2), ffn_units((s - 1) % 2)):
            unit()

    @pl.when(jnp.logical_and(s < n_tiles, redo[0] != 0))
    def _():
        for unit in attention_units(s % 2, exact=True):
            unit()

    @pl.when(s == n_tiles)
    def _():
        for unit in ffn_units((n_tiles - 1) % 2):
            unit()


def _attn_ffn_lagged(layer, x, qa, qb, own, cache, seq, lam_vecs, subln, mod4, w_out, norm_ffn, w_gu, w_down):
    n = x.shape[0]
    tq = Q_TILE
    tpr = seq // tq
    n_tiles = n // tq
    lam_init = 0.8 - 0.6 * math.exp(-0.3 * layer)
    att = lambda s: jnp.minimum(s, n_tiles - 1)
    ffn = lambda s: jnp.maximum(s - 1, 0)
    assert tpr >= 2, "the next request's keys / values are fetched during a request's second tile"
    in_specs = [pl.BlockSpec((tq, 512), lambda s: (att(s), 0))] * 2
    in_specs += [pl.BlockSpec(memory_space=pl.ANY)] * (len(own) + len(cache))
    landing = ([pltpu.VMEM((seq, a.shape[1]), a.dtype) for a in own]
               + [pltpu.VMEM(a.shape[2:], a.dtype) for a in cache])
    in_specs += [_layer_spec((1, HEAD_DIM), layer)] * 4 + [_layer_spec((2 * HEAD_DIM, 1), layer)]
    in_specs += [pl.BlockSpec((tq, D_MODEL), lambda s: (ffn(s), 0)),
                 pl.BlockSpec((None, None, 1, 6 * D_MODEL), lambda s: (layer, 1 + ffn(s) // tpr, 0, 0)),
                 pl.BlockSpec(memory_space=pl.ANY), _layer_spec((1, D_MODEL), layer),
                 pl.BlockSpec(memory_space=pl.ANY), pl.BlockSpec(memory_space=pl.ANY)]
    keys = seq + cache[0].shape[-1]
    kv_w = own[0].shape[1] + own[2].shape[1]
    return pl.pallas_call(
        functools.partial(_attn_ffn_lagged_kernel, lam_init, layer, tpr, n_tiles),
        grid=(n_tiles + 1,),
        in_specs=in_specs,
        out_specs=pl.BlockSpec((tq, D_MODEL), lambda s: (ffn(s), 0)),
        out_shape=jax.ShapeDtypeStruct((n, D_MODEL), F32),
        scratch_shapes=[pltpu.VMEM((keys, kv_w), BF16), pltpu.VMEM((kv_w, keys), BF16),
                        pltpu.VMEM((2, tq, D_MODEL), BF16),
                        pltpu.VMEM(w_out.shape, BF16), pltpu.VMEM(w_gu.shape, BF16), pltpu.VMEM(w_down.shape, BF16),
                        pltpu.SemaphoreType.DMA((3,)), *landing, pltpu.SemaphoreType.DMA((len(landing),)),
                        pltpu.SMEM((1,), jnp.int32)],
        compiler_params=pltpu.CompilerParams(
            dimension_semantics=("arbitrary",), vmem_limit_bytes=VMEM_LIMIT),
        name="attn_ffn_latent",
    )(qa, qb, *own, *cache, *lam_vecs, subln, x, mod4, w_out, norm_ffn, w_gu, w_down)


def _rope_tables(n_tokens):
    t = jnp.arange(n_tokens, dtype=jnp.int32)
    row = (t // GRID_W).astype(F32)
    col = (t % GRID_W).astype(F32)
    axis_dim = HEAD_DIM // 2
    freqs = ROPE_THETA ** (-jnp.arange(0, axis_dim, 2, dtype=F32) / axis_dim)
    ang = jnp.concatenate([row[:, None] * freqs, col[:, None] * freqs], axis=-1)
    c, s = jnp.cos(ang), jnp.sin(ang)
    return jnp.concatenate([c, c, c, c], axis=-1), jnp.concatenate([-s, s, -s, s], axis=-1)


def kernel(x_prompt, x_sample, cache_diff_k, cache_diff_v, cache_gqa_k, cache_gqa_v, c, c_ctx, w_mod, b_mod, norm_attn, w_in, q_norm_a, k_norm_a, lambda_q1, lambda_k1, lambda_q2, lambda_k2, subln, q_norm_b, k_norm_b, w_out, norm_ffn, w_gate_up, w_down):
    batch, seq, d = x_prompt.shape
    dec_batch, dec_seq, _ = x_sample.shape
    depth = w_mod.shape[0]
    past = cache_diff_k.shape[2]

    cond = jnp.concatenate([c_ctx[None], c, jnp.zeros((MOD_ROWS - 1 - dec_batch, d), F32)], axis=0)
    mod = _modulation(cond, w_mod, b_mod)
    mod4 = mod.reshape(depth, MOD_ROWS, 1, 6 * d)

    w_in_b = w_in.astype(BF16)
    seg_ones = jnp.kron(jnp.eye(256 // HEAD_DIM, dtype=F32), jnp.ones((HEAD_DIM, HEAD_DIM), F32)).astype(BF16)
    rope = _rope_tables(dec_seq)

    cache = (jnp.transpose(cache_diff_k, (0, 1, 3, 4, 5, 2)).reshape(dec_batch, depth, DIFF_HEADS, 2 * HEAD_DIM, past),
             cache_diff_v.reshape(dec_batch, depth, past * DIFF_HEADS, 2 * HEAD_DIM),
             jnp.transpose(cache_gqa_k, (0, 1, 3, 4, 2)),
             jnp.transpose(cache_gqa_v, (0, 1, 3, 4, 2)))

    row = lambda a: a.reshape(depth, 1, a.shape[-1])
    gains = (row(q_norm_a), row(k_norm_a), row(q_norm_b), row(k_norm_b))
    lam_vecs = (row(lambda_q1), row(lambda_k1), row(lambda_q2), row(lambda_k2))
    na, nf, sl = row(norm_attn), row(norm_ffn), subln.reshape(depth, 2 * HEAD_DIM, 1)

    yp = x_prompt.reshape(batch * seq, d)
    ys = x_sample.reshape(dec_batch * dec_seq, d)
    lat_tiles = dec_seq // TOKEN_TILE
    new_caches = None
    for l in range(depth):
        qa, ka, va, qb, kb, vb, *rest = _qkv(
            yp, mod4, l, False, seq // TOKEN_TILE, na, w_in_b, seg_ones, gains, None, new_caches,
            (w_out, w_gate_up, w_down))
        new_caches, (w_out_b, w_gu_b, w_down_b) = rest[:4], rest[4:]
        yp = _attn_ffn(l, yp, qa, qb, (ka, va, kb, vb), seq, lam_vecs, sl, mod4, w_out_b, nf, w_gu_b, w_down_b)

        qa, ka, va, qb, kb, vb = _qkv(
            ys, mod4, l, True, lat_tiles, na, w_in_b, seg_ones, gains, rope)
        ys = _attn_ffn_lagged(l, ys, qa, qb, (ka, va, kb, vb), cache, dec_seq, lam_vecs, sl,
                              mod4, w_out_b, nf, w_gu_b, w_down_b)

    ka_t, va_n, kb_t, vb_t = new_caches
    new_diff_k = jnp.transpose(ka_t.reshape(batch, depth, DIFF_HEADS, 2, HEAD_DIM, seq), (0, 1, 5, 2, 3, 4))
    new_gqa_k = jnp.transpose(kb_t.reshape(batch, depth, GQA_KV_HEADS, HEAD_DIM, seq), (0, 1, 4, 2, 3))
    new_gqa_v = jnp.transpose(vb_t.reshape(batch, depth, GQA_KV_HEADS, HEAD_DIM, seq), (0, 1, 4, 2, 3))
    return (yp.reshape(batch, seq, d), ys.reshape(dec_batch, dec_seq, d),
            new_diff_k, va_n.reshape(batch, depth, seq, DIFF_HEADS, 2 * HEAD_DIM), new_gqa_k, new_gqa_v)
```

```python
import functools
import math

import jax
import jax.numpy as jnp
from jax import lax
from jax.experimental import pallas as pl
from jax.experimental.pallas import tpu as pltpu

F32 = jnp.float32
BF16 = jnp.bfloat16

D_MODEL = 1024
HEAD_DIM = 64
GRID_W = 64
DIFF_HEADS = 4
GQA_KV_HEADS = 2
GQA_REP = 4
FFN_HIDDEN = 2816
IN_COLS = 2304
ROPE_THETA = 10000.0
EPS = 1e-6
LANES = 128
MOD_ROWS = 8
TOKEN_TILE = 256
Q_TILE = 256
VMEM_LIMIT = 56 * 1024 * 1024


def _dot(a, b):
    return jnp.dot(a, b, preferred_element_type=F32)


def _rms(x, gain):
    ms = jnp.mean(x * x, axis=-1, keepdims=True)
    return x * lax.rsqrt(ms + EPS) * gain


def _layer_spec(shape, layer):
    n = len(shape)
    return pl.BlockSpec((None,) + tuple(shape), lambda *_: (layer,) + (0,) * n,
                        pipeline_mode=pl.Buffered(1))


def _const_spec(shape):
    n = len(shape)
    return pl.BlockSpec(shape, lambda *_: (0,) * n, pipeline_mode=pl.Buffered(1))


def _mod_spec(layer, latent, tiles_per_request):
    if latent:
        return pl.BlockSpec((None, None, 1, 6 * D_MODEL),
                            lambda i: (layer, 1 + i // tiles_per_request, 0, 0))
    return pl.BlockSpec((None, None, 1, 6 * D_MODEL), lambda i: (layer, 0, 0, 0))


def _mod_kernel(cond_ref, w_ref, b_ref, o_ref):
    c = cond_ref[...]
    s = c * jax.nn.sigmoid(c)
    o_ref[...] = _dot(s.astype(BF16), w_ref[...].astype(BF16)) + b_ref[...]


def _modulation(cond, w_mod, b_mod):
    depth = w_mod.shape[0]
    tn = 1536
    return pl.pallas_call(
        _mod_kernel,
        grid=(depth, 6 * D_MODEL // tn),
        in_specs=[
            pl.BlockSpec((MOD_ROWS, D_MODEL), lambda l, j: (0, 0)),
            pl.BlockSpec((None, D_MODEL, tn), lambda l, j: (l, 0, j)),
            pl.BlockSpec((None, 1, tn), lambda l, j: (l, 0, j)),
        ],
        out_specs=pl.BlockSpec((None, MOD_ROWS, tn), lambda l, j: (l, 0, j)),
        out_shape=jax.ShapeDtypeStruct((depth, MOD_ROWS, 6 * D_MODEL), F32),
        compiler_params=pltpu.CompilerParams(
            dimension_semantics=("arbitrary", "arbitrary"), vmem_limit_bytes=VMEM_LIMIT),
        name="modulation",
    )(cond, w_mod, b_mod.reshape(depth, 1, 6 * D_MODEL))


def _head_norm(x, seg_ones, gain):
    ss = _dot((x * x).astype(BF16), seg_ones)
    return x * lax.rsqrt(ss * (1.0 / HEAD_DIM) + EPS) * gain


def _swap_halves(x, first_half):
    return jnp.where(first_half, pltpu.roll(x, 96, 1), pltpu.roll(x, 32, 1))


def _rope(x, cos, sin, first_half):
    out = []
    for j in range(x.shape[1] // LANES):
        xj = x[:, j * LANES:(j + 1) * LANES]
        out.append(xj * cos + _swap_halves(xj, first_half) * sin)
    return out[0] if len(out) == 1 else jnp.concatenate(out, axis=1)


def _tile_gain(g):
    return jnp.concatenate([g] * (256 // HEAD_DIM), axis=1)


def _qkv_kernel(latent, n_cast, *refs):
    if latent:
        (x_ref, mod_ref, na_ref, w_ref, seg_ref, gqa_ref, gka_ref, gqb_ref, gkb_ref,
         cos_ref, sin_ref, qa_o, ka_o, va_o, qb_o, kb_o, vb_o) = refs
    else:
        (x_ref, mod_ref, na_ref, w_ref, seg_ref, gqa_ref, gka_ref, gqb_ref, gkb_ref) = refs[:9]
        outs = refs[len(refs) - 10 - n_cast:]
        qa_o, ka_o, va_o, qb_o, kb_o, vb_o, kaf_o, vaf_o, kbf_o, vbf_o = outs[:10]
        for src, dst in zip(refs[9:9 + n_cast], outs[10:]):
            dst[...] = src[...].astype(BF16)
    tm = TOKEN_TILE
    sh = mod_ref[:, 0:D_MODEL]
    sc = mod_ref[:, D_MODEL:2 * D_MODEL]
    seg = seg_ref[...]
    lane = lax.broadcasted_iota(jnp.int32, (1, LANES), 1)
    first_half = (lane % HEAD_DIM) < (HEAD_DIM // 2)
    scale = HEAD_DIM ** -0.5 * math.log2(math.e)

    def project(r):
        h = _rms(x_ref[r, :], na_ref[...]) * (1.0 + sc) + sh
        return _dot(h.astype(BF16), w_ref[...])

    def finish(j, r, proj):
        def qk(lo_col, width, gain_ref):
            gain = _tile_gain(gain_ref[...])
            out = []
            for c0 in range(0, width, 256):
                w = min(256, width - c0)
                out.append(_head_norm(proj[:, lo_col + c0:lo_col + c0 + w], seg[:w, :w], gain[:, :w]))
            y = out[0] if len(out) == 1 else jnp.concatenate(out, axis=1)
            if latent:
                y = _rope(y, cos_ref[r, :], sin_ref[r, :], first_half)
            return y

        qa = qk(0, 512, gqa_ref)
        ka = qk(512, 512, gka_ref)
        va = proj[:, 1024:1536]
        qb = qk(1536, 512, gqb_ref)
        kb = qk(2048, 128, gkb_ref)
        vb = proj[:, 2176:2304]
        qa_o[r, :] = (qa * scale).astype(BF16)
        ka_o[r, :] = ka.astype(BF16)
        va_o[r, :] = va.astype(BF16)
        qb_o[r, :] = (qb * scale).astype(BF16)
        kb_o[r, :] = kb.astype(BF16)
        vb_o[r, :] = vb.astype(BF16)
        if not latent:
            kaf_o[j] = ka.T
            kbf_o[j] = kb.T
            vbf_o[j] = vb.T
            for h in range(DIFF_HEADS):
                vaf_o[j, pl.ds(h, tm, stride=DIFF_HEADS), :] = va[:, h * LANES:(h + 1) * LANES]

    rows = [slice(j * tm, (j + 1) * tm) for j in range(x_ref.shape[0] // tm)]
    pending = [project(rows[0])]
    for j, r in enumerate(rows):
        if j + 1 < len(rows):
            pending.append(project(rows[j + 1]))
        finish(j, r, pending.pop(0))


def _qkv(x, mod4, layer, latent, tiles_per_request, norm_attn, w_in, seg_ones, gains, rope,
         prev_caches=None, cast_weights=()):
    n = x.shape[0]
    sub = 4
    tm = sub * TOKEN_TILE
    depth = w_in.shape[0]
    tok = lambda w: pl.BlockSpec((tm, w), lambda i: (i, 0))
    in_specs = [
        tok(D_MODEL),
        _mod_spec(layer, latent, max(tiles_per_request // sub, 1)),
        _layer_spec((1, D_MODEL), layer),
        _layer_spec((D_MODEL, IN_COLS), layer),
        _const_spec((256, 256)),
    ] + [_layer_spec((1, HEAD_DIM), layer)] * 4
    args = [x, mod4, norm_attn, w_in, seg_ones, *gains]
    aliases = {}
    out_specs = [tok(512), tok(512), tok(512), tok(512), tok(LANES), tok(LANES)]
    out_shape = [jax.ShapeDtypeStruct((n, w), BF16) for w in (512, 512, 512, 512, LANES, LANES)]
    if latent:
        rope_spec = pl.BlockSpec((tm, LANES), lambda i: (i % (tiles_per_request // sub), 0))
        in_specs += [rope_spec, rope_spec]
        args += list(rope)
    else:
        assert tiles_per_request == 1, "context requests must be one token tile long"
        seq = TOKEN_TILE
        req = n // seq
        slab = lambda *dims: pl.BlockSpec((sub, None) + dims, lambda i: (i, layer) + (0,) * len(dims))
        out_specs += [slab(512, seq), slab(seq * DIFF_HEADS, LANES), slab(LANES, seq), slab(LANES, seq)]
        out_shape += [jax.ShapeDtypeStruct((req, depth, 512, seq), F32),
                      jax.ShapeDtypeStruct((req, depth, seq * DIFF_HEADS, LANES), F32),
                      jax.ShapeDtypeStruct((req, depth, LANES, seq), F32),
                      jax.ShapeDtypeStruct((req, depth, LANES, seq), F32)]
        steps = n // tm
        for w in cast_weights:
            rows, cols = w.shape[1] // steps, w.shape[2]
            assert rows * steps == w.shape[1] and rows % 16 == 0
            in_specs.append(pl.BlockSpec((None, rows, cols), lambda i: (layer, i, 0)))
            args.append(w)
            out_specs.append(pl.BlockSpec((rows, cols), lambda i: (i, 0)))
            out_shape.append(jax.ShapeDtypeStruct(w.shape[1:], BF16))
        if prev_caches is not None:
            aliases = {len(args) + k: 6 + k for k in range(4)}
            in_specs += [pl.BlockSpec(memory_space=pl.ANY)] * 4
            args += list(prev_caches)
    return pl.pallas_call(
        functools.partial(_qkv_kernel, latent, len(cast_weights)),
        grid=(n // tm,),
        in_specs=in_specs,
        out_specs=out_specs,
        out_shape=out_shape,
        input_output_aliases=aliases,
        compiler_params=pltpu.CompilerParams(
            dimension_semantics=("arbitrary",), vmem_limit_bytes=VMEM_LIMIT),
        name="qkv_latent" if latent else "qkv_context",
    )(*args)


def _pad_rows(x, block):
    z = jnp.zeros_like(x)
    return jnp.concatenate([x, z] if block == 0 else [z, x], axis=0)


ONES_ROWS = 16
SHIFT_KEYS = 8
DENOM_RANGE = (2.0 ** -100, 2.0 ** 100)


def _numerators(st, exact):
    shift = (st if exact else st[:SHIFT_KEYS]).max(axis=0, keepdims=True)
    return jnp.exp2(st - shift).astype(BF16)


def _pv(vt, e):
    d = vt.shape[0]
    ext = _dot(jnp.concatenate([vt, jnp.ones((ONES_ROWS, vt.shape[1]), BF16)], axis=0), e)
    return ext[:d], ext[d:d + 1]


def _denominators_bad(denoms):
    bad = None
    for l in denoms:
        b = jnp.where(jnp.logical_and(l > DENOM_RANGE[0], l < DENOM_RANGE[1]), 0.0, 1.0)
        bad = b if bad is None else jnp.maximum(bad, b)
    return (jnp.max(bad) > 0.0).astype(jnp.int32)


KA_W = DIFF_HEADS * LANES


def _stage_kv(k_all, vt_all, ka, va, kb, vb, cache_refs):
    seq = ka.shape[0]
    k_all[0:seq, 0:KA_W] = ka
    k_all[0:seq, KA_W:] = kb
    vt_all[0:KA_W, 0:seq] = va.astype(F32).T.astype(BF16)
    vt_all[KA_W:, 0:seq] = vb.astype(F32).T.astype(BF16)
    if cache_refs is not None:
        cka_ref, cva_ref, ckb_ref, cvb_ref = cache_refs
        past = cka_ref.shape[2]
        for h in range(DIFF_HEADS):
            cs = slice(h * LANES, (h + 1) * LANES)
            k_all[seq:, cs] = cka_ref[h].T.astype(BF16)
            vt_all[cs, seq:] = cva_ref[pl.ds(h, past, stride=DIFF_HEADS), :].T.astype(BF16)
        k_all[seq:, KA_W:] = jnp.concatenate([ckb_ref[0], ckb_ref[1]], axis=0).T.astype(BF16)
        for g in range(GQA_KV_HEADS):
            vt_all[KA_W + g * HEAD_DIM:KA_W + (g + 1) * HEAD_DIM, seq:] = cvb_ref[g].astype(BF16)


def _lambda(lq1, lk1, lq2, lk2, lam_init):
    return (jnp.exp(jnp.sum(lq1[...] * lk1[...], axis=-1, keepdims=True))
            - jnp.exp(jnp.sum(lq2[...] * lk2[...], axis=-1, keepdims=True)) + lam_init)


def _attention_tasks(qa, qb, k_all, vt_all, lam, subln, lam_init, store, denoms=None):
    tq = qa.shape[0]
    qat = qa.astype(F32).T.astype(BF16)
    qbt = qb.astype(F32).T.astype(BF16)
    exact = denoms is None
    tasks = []

    def first(st):
        return _numerators(st, exact)

    def pv(vt, e):
        num, l = _pv(vt, e)
        if not exact:
            denoms.append(l)
        return num, 1.0 / l

    def diff_finish(e, h, cs):
        num, r = pv(vt_all[cs, :], e)
        ot = num[:, :tq] * r[:, :tq] - num[:, tq:] * (r[:, tq:] * lam)
        ms = jnp.mean(ot * ot, axis=0, keepdims=True)
        ot = ot * lax.rsqrt(ms + EPS) * subln * (1.0 - lam_init)
        store(cs, ot.T)

    def gqa_finish(e, g, os):
        num, r = pv(vt_all[KA_W + g * HEAD_DIM:KA_W + (g + 1) * HEAD_DIM, :], e)
        ot = num * r
        store(os, jnp.concatenate([ot[:, :tq], ot[:, tq:]], axis=0).T)

    for h in range(DIFF_HEADS):
        cs = slice(h * LANES, (h + 1) * LANES)

        def logits(h=h, cs=cs):
            c0 = qat[h * LANES:h * LANES + HEAD_DIM]
            c1 = qat[h * LANES + HEAD_DIM:(h + 1) * LANES]
            return first(_dot(k_all[:, cs], jnp.concatenate([_pad_rows(c0, 0), _pad_rows(c1, 1)], axis=1)))

        tasks.append((logits, lambda e, h=h, cs=cs: diff_finish(e, h, cs)))
    for g in range(GQA_KV_HEADS):
        for rp in range(2):
            r0 = (GQA_REP * g + 2 * rp) * HEAD_DIM
            os = slice(KA_W + r0, KA_W + r0 + LANES)

            def logits(g=g, r0=r0):
                a = qbt[r0:r0 + HEAD_DIM]
                b = qbt[r0 + HEAD_DIM:r0 + LANES]
                return first(_dot(k_all[:, KA_W:], jnp.concatenate([_pad_rows(a, g), _pad_rows(b, g)], axis=1)))

            tasks.append((logits, lambda e, g=g, os=os: gqa_finish(e, g, os)))
    return tasks


LOGITS_AHEAD = 2
GUARDED_AHEAD = 5
N_TASKS = DIFF_HEADS + 2 * GQA_KV_HEADS


def _attention_units(tasks, ahead=LOGITS_AHEAD):
    pending = []

    def prefill(t):
        pending.append(t[0]())

    def step(i):
        if i + ahead < len(tasks):
            pending.append(tasks[i + ahead][0]())
        tasks[i][1](pending.pop(0))

    units = [functools.partial(prefill, t) for t in tasks[:ahead]]
    return units + [functools.partial(step, i) for i in range(len(tasks))]


def _ffn_units(r, x_ref, mix_ref, mod_ref, wo_ref, nf_ref, wgu_ref, wd_ref, o_ref):
    g_a = mod_ref[:, 2 * D_MODEL:3 * D_MODEL]
    sh_f = mod_ref[:, 3 * D_MODEL:4 * D_MODEL]
    sc_f = mod_ref[:, 4 * D_MODEL:5 * D_MODEL]
    g_f = mod_ref[:, 5 * D_MODEL:6 * D_MODEL]
    state = {}

    def attn_residual():
        state["x"] = x_ref[r, :] + g_a * _dot(mix_ref[r, :], wo_ref[...])

    def gate():
        state["h"] = (_rms(state["x"], nf_ref[...]) * (1.0 + sc_f) + sh_f).astype(BF16)
        state["gate"] = _dot(state["h"], wgu_ref[:, :FFN_HIDDEN])

    def up():
        g = state.pop("gate")
        state["act"] = (g * jax.nn.sigmoid(g) * _dot(state.pop("h"), wgu_ref[:, FFN_HIDDEN:])).astype(BF16)

    def ffn_residual():
        o_ref[r, :] = state.pop("x") + g_f * _dot(state.pop("act"), wd_ref[...])

    return [attn_residual, gate, up, ffn_residual]


def _interleave(a, b):
    out, ia, ib = [], 0, 0
    while ia < len(a) or ib < len(b):
        if ib >= len(b) or (ia < len(a) and ia * len(b) <= ib * len(a)):
            out.append(a[ia])
            ia += 1
        else:
            out.append(b[ib])
            ib += 1
    return out


def _attn_ffn_kernel(lam_init, sub, *refs):
    (qa_ref, qb_ref, ka_ref, va_ref, kb_ref, vb_ref, lq1, lk1, lq2, lk2, subln_ref,
     x_ref, mod_ref, wo_hbm, nf_ref, wgu_hbm, wd_hbm, o_ref, k_all, vt_all, mix,
     wo_ref, wgu_ref, wd_ref, wsem) = refs
    first_step = pl.program_id(0) == 0

    def weight_copies():
        return [pltpu.make_async_copy(src, dst, wsem.at[k])
                for k, (src, dst) in enumerate(((wo_hbm, wo_ref), (wgu_hbm, wgu_ref), (wd_hbm, wd_ref)))]

    @pl.when(first_step)
    def _():
        for cp in weight_copies():
            cp.start()

    seq = ka_ref.shape[0] // sub
    lam = _lambda(lq1, lk1, lq2, lk2, lam_init)
    subln = subln_ref[...]
    attn, ffn = [], []
    for j in range(sub):
        r = slice(j * seq, (j + 1) * seq)

        later = []

        def store(cols, tile, r=r):
            mix[r, cols] = tile.astype(mix.dtype)

        def stage(j=j, r=r, later=later, store=store):
            _stage_kv(k_all.at[j], vt_all.at[j], ka_ref[r, :], va_ref[r, :], kb_ref[r, :], vb_ref[r, :], None)
            later.extend(_attention_units(_attention_tasks(
                qa_ref[r, :], qb_ref[r, :], k_all.at[j], vt_all.at[j], lam, subln, lam_init, store)))

        attn.append([stage] + [lambda k=k, later=later: later[k]() for k in range(LOGITS_AHEAD + N_TASKS)])
        ffn.append(_ffn_units(r, x_ref, mix, mod_ref, wo_ref, nf_ref, wgu_ref, wd_ref, o_ref))
    for unit in attn[0]:
        unit()

    @pl.when(first_step)
    def _():
        for cp in weight_copies():
            cp.wait()

    units = []
    for j in range(1, sub):
        units = units + _interleave(attn[j], ffn[j - 1])
    for unit in units + ffn[sub - 1]:
        unit()


def _attn_ffn(layer, x, qa, qb, own, seq, lam_vecs, subln, mod4, w_out, norm_ffn, w_gu, w_down):
    n = x.shape[0]
    sub = 4
    tm = sub * seq
    lam_init = 0.8 - 0.6 * math.exp(-0.3 * layer)
    tok = lambda w: pl.BlockSpec((tm, w), lambda i: (i, 0))
    kv_w = own[0].shape[1] + own[2].shape[1]
    in_specs = ([tok(512), tok(512)] + [tok(a.shape[1]) for a in own]
                + [_layer_spec((1, HEAD_DIM), layer)] * 4 + [_layer_spec((2 * HEAD_DIM, 1), layer)]
                + [tok(D_MODEL), _mod_spec(layer, False, 1),
                   pl.BlockSpec(memory_space=pl.ANY), _layer_spec((1, D_MODEL), layer),
                   pl.BlockSpec(memory_space=pl.ANY), pl.BlockSpec(memory_space=pl.ANY)])
    return pl.pallas_call(
        functools.partial(_attn_ffn_kernel, lam_init, sub),
        grid=(n // tm,),
        in_specs=in_specs,
        out_specs=tok(D_MODEL),
        out_shape=jax.ShapeDtypeStruct((n, D_MODEL), F32),
        scratch_shapes=[pltpu.VMEM((sub, seq, kv_w), BF16), pltpu.VMEM((sub, kv_w, seq), BF16),
                        pltpu.VMEM((tm, D_MODEL), BF16),
                        pltpu.VMEM(w_out.shape, BF16), pltpu.VMEM(w_gu.shape, BF16), pltpu.VMEM(w_down.shape, BF16),
                        pltpu.SemaphoreType.DMA((3,))],
        compiler_params=pltpu.CompilerParams(
            dimension_semantics=("arbitrary",), vmem_limit_bytes=VMEM_LIMIT),
        name="attn_ffn_context",
    )(qa, qb, *own, *lam_vecs, subln, x, mod4, w_out, norm_ffn, w_gu, w_down)


def _attn_ffn_lagged_kernel(lam_init, layer, tiles_per_request, n_tiles, *refs):
    (qa_ref, qb_ref, *kv_hbm) = refs[:10]
    (lq1, lk1, lq2, lk2, subln_ref, x_ref, mod_ref, wo_hbm, nf_ref, wgu_hbm, wd_hbm,
     o_ref, k_all, vt_all, mix, wo_ref, wgu_ref, wd_ref, wsem) = refs[10:29]
    kv_ref, kvsem, redo = refs[29:37], refs[37], refs[38]
    s = pl.program_id(0)
    tq = qa_ref.shape[0]
    seq = kv_ref[0].shape[0]
    requests = n_tiles // tiles_per_request

    def weight_copies():
        return [pltpu.make_async_copy(src, dst, wsem.at[k])
                for k, (src, dst) in enumerate(((wo_hbm, wo_ref), (wgu_hbm, wgu_ref), (wd_hbm, wd_ref)))]

    def kv_copies(b):
        srcs = [h.at[pl.ds(b * seq, seq), :] for h in kv_hbm[:4]] + [h.at[b, layer] for h in kv_hbm[4:]]
        return [pltpu.make_async_copy(src, dst, kvsem.at[k]) for k, (src, dst) in enumerate(zip(srcs, kv_ref))]

    @pl.when(s == 0)
    def _():
        for cp in kv_copies(0) + weight_copies():
            cp.start()

    @pl.when(s == 1)
    def _():
        for cp in weight_copies():
            cp.wait()

    nxt = s // tiles_per_request + 1

    @pl.when(jnp.logical_and(s % tiles_per_request == 1, nxt < requests))
    def _():
        for cp in kv_copies(nxt):
            cp.start()

    lam = _lambda(lq1, lk1, lq2, lk2, lam_init)
    subln = subln_ref[...]

    def attention_units(slot, exact=False):
        def store(cols, tile):
            mix[slot, :, cols] = tile.astype(mix.dtype)

        denoms = None if exact else []
        units = _attention_units(_attention_tasks(
            qa_ref[...], qb_ref[...], k_all, vt_all, lam, subln, lam_init, store, denoms),
            LOGITS_AHEAD if exact else GUARDED_AHEAD)
        if exact:
            return units

        def check():
            redo[0] = _denominators_bad(denoms)

        return units + [check]

    def ffn_units(slot):
        return _ffn_units(slice(0, tq), x_ref, mix.at[slot], mod_ref, wo_ref, nf_ref, wgu_ref, wd_ref, o_ref)

    @pl.when(jnp.logical_and(s % tiles_per_request == 0, s < n_tiles))
    def _():
        for cp in kv_copies(s // tiles_per_request):
            cp.wait()
        ka_ref, va_ref, kb_ref, vb_ref = kv_ref[:4]
        _stage_kv(k_all, vt_all, ka_ref[...], va_ref[...], kb_ref[...], vb_ref[...], kv_ref[4:])

    @pl.when(s == 0)
    def _():
        for unit in attention_units(0):
            unit()

    @pl.when(jnp.logical_and(s > 0, s < n_tiles))
    def _():
        for unit in _interleave(attention_units(s % 2), ffn_units((s - 1) % 2)):
            unit()

    @pl.when(jnp.logical_and(s < n_tiles, redo[0] != 0))
    def _():
        for unit in attention_units(s % 2, exact=True):
            unit()

    @pl.when(s == n_tiles)
    def _():
        for unit in ffn_units((n_tiles - 1) % 2):
            unit()


def _attn_ffn_lagged(layer, x, qa, qb, own, cache, seq, lam_vecs, subln, mod4, w_out, norm_ffn, w_gu, w_down):
    n = x.shape[0]
    tq = Q_TILE
    tpr = seq // tq
    n_tiles = n // tq
    lam_init = 0.8 - 0.6 * math.exp(-0.3 * layer)
    att = lambda s: jnp.minimum(s, n_tiles - 1)
    ffn = lambda s: jnp.maximum(s - 1, 0)
    assert tpr >= 2, "the next request's keys / values are fetched during a request's second tile"
    in_specs = [pl.BlockSpec((tq, 512), lambda s: (att(s), 0))] * 2
    in_specs += [pl.BlockSpec(memory_space=pl.ANY)] * (len(own) + len(cache))
    landing = ([pltpu.VMEM((seq, a.shape[1]), a.dtype) for a in own]
               + [pltpu.VMEM(a.shape[2:], a.dtype) for a in cache])
    in_specs += [_layer_spec((1, HEAD_DIM), layer)] * 4 + [_layer_spec((2 * HEAD_DIM, 1), layer)]
    in_specs += [pl.BlockSpec((tq, D_MODEL), lambda s: (ffn(s), 0)),
                 pl.BlockSpec((None, None, 1, 6 * D_MODEL), lambda s: (layer, 1 + ffn(s) // tpr, 0, 0)),
                 pl.BlockSpec(memory_space=pl.ANY), _layer_spec((1, D_MODEL), layer),
                 pl.BlockSpec(memory_space=pl.ANY), pl.BlockSpec(memory_space=pl.ANY)]
    keys = seq + cache[0].shape[-1]
    kv_w = own[0].shape[1] + own[2].shape[1]
    return pl.pallas_call(
        functools.partial(_attn_ffn_lagged_kernel, lam_init, layer, tpr, n_tiles),
        grid=(n_tiles + 1,),
        in_specs=in_specs,
        out_specs=pl.BlockSpec((tq, D_MODEL), lambda s: (ffn(s), 0)),
        out_shape=jax.ShapeDtypeStruct((n, D_MODEL), F32),
        scratch_shapes=[pltpu.VMEM((keys, kv_w), BF16), pltpu.VMEM((kv_w, keys), BF16),
                        pltpu.VMEM((2, tq, D_MODEL), BF16),
                        pltpu.VMEM(w_out.shape, BF16), pltpu.VMEM(w_gu.shape, BF16), pltpu.VMEM(w_down.shape, BF16),
                        pltpu.SemaphoreType.DMA((3,)), *landing, pltpu.SemaphoreType.DMA((len(landing),)),
                        pltpu.SMEM((1,), jnp.int32)],
        compiler_params=pltpu.CompilerParams(
            dimension_semantics=("arbitrary",), vmem_limit_bytes=VMEM_LIMIT),
        name="attn_ffn_latent",
    )(qa, qb, *own, *cache, *lam_vecs, subln, x, mod4, w_out, norm_ffn, w_gu, w_down)


def _rope_tables(n_tokens):
    t = jnp.arange(n_tokens, dtype=jnp.int32)
    row = (t // GRID_W).astype(F32)
    col = (t % GRID_W).astype(F32)
    axis_dim = HEAD_DIM // 2
    freqs = ROPE_THETA ** (-jnp.arange(0, axis_dim, 2, dtype=F32) / axis_dim)
    ang = jnp.concatenate([row[:, None] * freqs, col[:, None] * freqs], axis=-1)
    c, s = jnp.cos(ang), jnp.sin(ang)
    return jnp.concatenate([c, c, c, c], axis=-1), jnp.concatenate([-s, s, -s, s], axis=-1)


def kernel(x_prompt, x_sample, cache_diff_k, cache_diff_v, cache_gqa_k, cache_gqa_v, c, c_ctx, w_mod, b_mod, norm_attn, w_in, q_norm_a, k_norm_a, lambda_q1, lambda_k1, lambda_q2, lambda_k2, subln, q_norm_b, k_norm_b, w_out, norm_ffn, w_gate_up, w_down):
    batch, seq, d = x_prompt.shape
    dec_batch, dec_seq, _ = x_sample.shape
    depth = w_mod.shape[0]
    past = cache_diff_k.shape[2]

    cond = jnp.concatenate([c_ctx[None], c, jnp.zeros((MOD_ROWS - 1 - dec_batch, d), F32)], axis=0)
    mod = _modulation(cond, w_mod, b_mod)
    mod4 = mod.reshape(depth, MOD_ROWS, 1, 6 * d)

    w_in_b = w_in.astype(BF16)
    seg_ones = jnp.kron(jnp.eye(256 // HEAD_DIM, dtype=F32), jnp.ones((HEAD_DIM, HEAD_DIM), F32)).astype(BF16)
    rope = _rope_tables(dec_seq)

    cache = (jnp.transpose(cache_diff_k, (0, 1, 3, 4, 5, 2)).reshape(dec_batch, depth, DIFF_HEADS, 2 * HEAD_DIM, past),
             cache_diff_v.reshape(dec_batch, depth, past * DIFF_HEADS, 2 * HEAD_DIM),
             jnp.transpose(cache_gqa_k, (0, 1, 3, 4, 2)),
             jnp.transpose(cache_gqa_v, (0, 1, 3, 4, 2)))

    row = lambda a: a.reshape(depth, 1, a.shape[-1])
    gains = (row(q_norm_a), row(k_norm_a), row(q_norm_b), row(k_norm_b))
    lam_vecs = (row(lambda_q1), row(lambda_k1), row(lambda_q2), row(lambda_k2))
    na, nf, sl = row(norm_attn), row(norm_ffn), subln.reshape(depth, 2 * HEAD_DIM, 1)

    yp = x_prompt.reshape(batch * seq, d)
    ys = x_sample.reshape(dec_batch * dec_seq, d)
    lat_tiles = dec_seq // TOKEN_TILE
    new_caches = None
    for l in range(depth):
        qa, ka, va, qb, kb, vb, *rest = _qkv(
            yp, mod4, l, False, seq // TOKEN_TILE, na, w_in_b, seg_ones, gains, None, new_caches,
            (w_out, w_gate_up, w_down))
        new_caches, (w_out_b, w_gu_b, w_down_b) = rest[:4], rest[4:]
        yp = _attn_ffn(l, yp, qa, qb, (ka, va, kb, vb), seq, lam_vecs, sl, mod4, w_out_b, nf, w_gu_b, w_down_b)

        qa, ka, va, qb, kb, vb = _qkv(
            ys, mod4, l, True, lat_tiles, na, w_in_b, seg_ones, gains, rope)
        ys = _attn_ffn_lagged(l, ys, qa, qb, (ka, va, kb, vb), cache, dec_seq, lam_vecs, sl,
                              mod4, w_out_b, nf, w_gu_b, w_down_b)

    ka_t, va_n, kb_t, vb_t = new_caches
    new_diff_k = jnp.transpose(ka_t.reshape(batch, depth, DIFF_HEADS, 2, HEAD_DIM, seq), (0, 1, 5, 2, 3, 4))
    new_gqa_k = jnp.transpose(kb_t.reshape(batch, depth, GQA_KV_HEADS, HEAD_DIM, seq), (0, 1, 4, 2, 3))
    new_gqa_v = jnp.transpose(vb_t.reshape(batch, depth, GQA_KV_HEADS, HEAD_DIM, seq), (0, 1, 4, 2, 3))
    return (yp.reshape(batch, seq, d), ys.reshape(dec_batch, dec_seq, d),
            new_diff_k, va_n.reshape(batch, depth, seq, DIFF_HEADS, 2 * HEAD_DIM), new_gqa_k, new_gqa_v)
```

```python
import functools
import math

import jax
import jax.numpy as jnp
from jax import lax
from jax.experimental import pallas as pl
from jax.experimental.pallas import tpu as pltpu

F32 = jnp.float32
BF16 = jnp.bfloat16

D_MODEL = 1024
HEAD_DIM = 64
GRID_W = 64
DIFF_HEADS = 4
GQA_KV_HEADS = 2
GQA_REP = 4
FFN_HIDDEN = 2816
IN_COLS = 2304
ROPE_THETA = 10000.0
EPS = 1e-6
LANES = 128
MOD_ROWS = 8
TOKEN_TILE = 256
Q_TILE = 256
VMEM_LIMIT = 56 * 1024 * 1024


def _dot(a, b):
    return jnp.dot(a, b, preferred_element_type=F32)


def _rms(x, gain):
    ms = jnp.mean(x * x, axis=-1, keepdims=True)
    return x * lax.rsqrt(ms + EPS) * gain


def _layer_spec(shape, layer):
    n = len(shape)
    return pl.BlockSpec((None,) + tuple(shape), lambda *_: (layer,) + (0,) * n,
                        pipeline_mode=pl.Buffered(1))


def _const_spec(shape):
    n = len(shape)
    return pl.BlockSpec(shape, lambda *_: (0,) * n, pipeline_mode=pl.Buffered(1))


def _mod_spec(layer, latent, tiles_per_request):
    if latent:
        return pl.BlockSpec((None, None, 1, 6 * D_MODEL),
                            lambda i: (layer, 1 + i // tiles_per_request, 0, 0))
    return pl.BlockSpec((None, None, 1, 6 * D_MODEL), lambda i: (layer, 0, 0, 0))


def _mod_kernel(cond_ref, w_ref, b_ref, o_ref):
    c = cond_ref[...]
    s = c * jax.nn.sigmoid(c)
    o_ref[...] = _dot(s.astype(BF16), w_ref[...].astype(BF16)) + b_ref[...]


def _modulation(cond, w_mod, b_mod):
    depth = w_mod.shape[0]
    tn = 1536
    return pl.pallas_call(
        _mod_kernel,
        grid=(depth, 6 * D_MODEL // tn),
        in_specs=[
            pl.BlockSpec((MOD_ROWS, D_MODEL), lambda l, j: (0, 0)),
            pl.BlockSpec((None, D_MODEL, tn), lambda l, j: (l, 0, j)),
            pl.BlockSpec((None, 1, tn), lambda l, j: (l, 0, j)),
        ],
        out_specs=pl.BlockSpec((None, MOD_ROWS, tn), lambda l, j: (l, 0, j)),
        out_shape=jax.ShapeDtypeStruct((depth, MOD_ROWS, 6 * D_MODEL), F32),
        compiler_params=pltpu.CompilerParams(
            dimension_semantics=("arbitrary", "arbitrary"), vmem_limit_bytes=VMEM_LIMIT),
        name="modulation",
    )(cond, w_mod, b_mod.reshape(depth, 1, 6 * D_MODEL))


def _head_norm(x, seg_ones, gain):
    ss = _dot((x * x).astype(BF16), seg_ones)
    return x * lax.rsqrt(ss * (1.0 / HEAD_DIM) + EPS) * gain


def _swap_halves(x, first_half):
    return jnp.where(first_half, pltpu.roll(x, 96, 1), pltpu.roll(x, 32, 1))


def _rope(x, cos, sin, first_half):
    out = []
    for j in range(x.shape[1] // LANES):
        xj = x[:, j * LANES:(j + 1) * LANES]
        out.append(xj * cos + _swap_halves(xj, first_half) * sin)
    return out[0] if len(out) == 1 else jnp.concatenate(out, axis=1)


def _tile_gain(g):
    return jnp.concatenate([g] * (256 // HEAD_DIM), axis=1)


def _qkv_kernel(latent, n_cast, n_steps, *refs):
    if latent:
        (x_ref, mod_ref, na_ref, w_ref, seg_ref, gqa_ref, gka_ref, gqb_ref, gkb_ref,
         cos_ref, sin_ref, qa_o, ka_o, va_o, qb_o, kb_o, vb_o) = refs
    else:
        (x_hbm, mod_ref, na_ref, w_ref, seg_ref, gqa_ref, gka_ref, gqb_ref, gkb_ref) = refs[:9]
        xring, xsem = refs[-2:]
        outs = refs[len(refs) - 12 - n_cast:len(refs) - 2]
        qa_o, ka_o, va_o, qb_o, kb_o, vb_o, kaf_o, vaf_o, kbf_o, vbf_o = outs[:10]
        i = pl.program_id(0)
        slots, rows_per_step = xring.shape[0], xring.shape[1]

        def x_copy(k):
            return pltpu.make_async_copy(x_hbm.at[pl.ds(k * rows_per_step, rows_per_step), :],
                                         xring.at[k % slots], xsem.at[k % slots])

        @pl.when(i == 0)
        def _():
            for k in range(min(slots - 1, n_steps)):
                x_copy(k).start()

        @pl.when(i + slots - 1 < n_steps)
        def _():
            x_copy(i + slots - 1).start()

        x_copy(i).wait()
        x_ref = xring.at[i % slots]
        for src, dst in zip(refs[9:9 + n_cast], outs[10:]):
            dst[...] = src[...].astype(BF16)
    tm = TOKEN_TILE
    sh = mod_ref[:, 0:D_MODEL]
    sc = mod_ref[:, D_MODEL:2 * D_MODEL]
    seg = seg_ref[...]
    lane = lax.broadcasted_iota(jnp.int32, (1, LANES), 1)
    first_half = (lane % HEAD_DIM) < (HEAD_DIM // 2)
    scale = HEAD_DIM ** -0.5 * math.log2(math.e)

    def project(r):
        h = _rms(x_ref[r, :], na_ref[...]) * (1.0 + sc) + sh
        return _dot(h.astype(BF16), w_ref[...])

    def finish(j, r, proj):
        def qk(lo_col, width, gain_ref):
            gain = _tile_gain(gain_ref[...])
            out = []
            for c0 in range(0, width, 256):
                w = min(256, width - c0)
                out.append(_head_norm(proj[:, lo_col + c0:lo_col + c0 + w], seg[:w, :w], gain[:, :w]))
            y = out[0] if len(out) == 1 else jnp.concatenate(out, axis=1)
            if latent:
                y = _rope(y, cos_ref[r, :], sin_ref[r, :], first_half)
            return y

        qa = qk(0, 512, gqa_ref)
        ka = qk(512, 512, gka_ref)
        va = proj[:, 1024:1536]
        qb = qk(1536, 512, gqb_ref)
        kb = qk(2048, 128, gkb_ref)
        vb = proj[:, 2176:2304]
        qa_o[r, :] = (qa * scale).astype(BF16)
        ka_o[r, :] = ka.astype(BF16)
        va_o[r, :] = va.astype(BF16)
        qb_o[r, :] = (qb * scale).astype(BF16)
        kb_o[r, :] = kb.astype(BF16)
        vb_o[r, :] = vb.astype(BF16)
        if not latent:
            kaf_o[j] = ka.T
            kbf_o[j] = kb.T
            vbf_o[j] = vb.T
            for h in range(DIFF_HEADS):
                vaf_o[j, pl.ds(h, tm, stride=DIFF_HEADS), :] = va[:, h * LANES:(h + 1) * LANES]

    rows = [slice(j * tm, (j + 1) * tm) for j in range(x_ref.shape[0] // tm)]
    pending = [project(rows[0])]
    for j, r in enumerate(rows):
        if j + 1 < len(rows):
            pending.append(project(rows[j + 1]))
        finish(j, r, pending.pop(0))


def _qkv(x, mod4, layer, latent, tiles_per_request, norm_attn, w_in, seg_ones, gains, rope,
         prev_caches=None, cast_weights=()):
    n = x.shape[0]
    sub = 4
    tm = sub * TOKEN_TILE
    depth = w_in.shape[0]
    tok = lambda w: pl.BlockSpec((tm, w), lambda i: (i, 0))
    in_specs = [
        tok(D_MODEL),
        _mod_spec(layer, latent, max(tiles_per_request // sub, 1)),
        _layer_spec((1, D_MODEL), layer),
        _layer_spec((D_MODEL, IN_COLS), layer),
        _const_spec((256, 256)),
    ] + [_layer_spec((1, HEAD_DIM), layer)] * 4
    args = [x, mod4, norm_attn, w_in, seg_ones, *gains]
    aliases = {}
    scratch = []
    out_specs = [tok(512), tok(512), tok(512), tok(512), tok(LANES), tok(LANES)]
    out_shape = [jax.ShapeDtypeStruct((n, w), BF16) for w in (512, 512, 512, 512, LANES, LANES)]
    if latent:
        rope_spec = pl.BlockSpec((tm, LANES), lambda i: (i % (tiles_per_request // sub), 0))
        in_specs += [rope_spec, rope_spec]
        args += list(rope)
    else:
        assert tiles_per_request == 1, "context requests must be one token tile long"
        in_specs[0] = pl.BlockSpec(memory_space=pl.ANY)
        scratch = [pltpu.VMEM((3, tm, D_MODEL), F32), pltpu.SemaphoreType.DMA((3,))]
        seq = TOKEN_TILE
        req = n // seq
        slab = lambda *dims: pl.BlockSpec((sub, None) + dims, lambda i: (i, layer) + (0,) * len(dims))
        out_specs += [slab(512, seq), slab(seq * DIFF_HEADS, LANES), slab(LANES, seq), slab(LANES, seq)]
        out_shape += [jax.ShapeDtypeStruct((req, depth, 512, seq), F32),
                      jax.ShapeDtypeStruct((req, depth, seq * DIFF_HEADS, LANES), F32),
                      jax.ShapeDtypeStruct((req, depth, LANES, seq), F32),
                      jax.ShapeDtypeStruct((req, depth, LANES, seq), F32)]
        steps = n // tm
        for w in cast_weights:
            rows, cols = w.shape[1] // steps, w.shape[2]
            assert rows * steps == w.shape[1] and rows % 16 == 0
            in_specs.append(pl.BlockSpec((None, rows, cols), lambda i: (layer, i, 0)))
            args.append(w)
            out_specs.append(pl.BlockSpec((rows, cols), lambda i: (i, 0)))
            out_shape.append(jax.ShapeDtypeStruct(w.shape[1:], BF16))
        if prev_caches is not None:
            aliases = {len(args) + k: 6 + k for k in range(4)}
            in_specs += [pl.BlockSpec(memory_space=pl.ANY)] * 4
            args += list(prev_caches)
    return pl.pallas_call(
        functools.partial(_qkv_kernel, latent, len(cast_weights), n // tm),
        grid=(n // tm,),
        in_specs=in_specs,
        out_specs=out_specs,
        out_shape=out_shape,
        input_output_aliases=aliases,
        scratch_shapes=scratch,
        compiler_params=pltpu.CompilerParams(
            dimension_semantics=("arbitrary",), vmem_limit_bytes=VMEM_LIMIT),
        name="qkv_latent" if latent else "qkv_context",
    )(*args)


def _pad_rows(x, block):
    z = jnp.zeros_like(x)
    return jnp.concatenate([x, z] if block == 0 else [z, x], axis=0)


ONES_ROWS = 16
SHIFT_KEYS = 8
DENOM_RANGE = (2.0 ** -100, 2.0 ** 100)


def _numerators(st, exact):
    shift = (st if exact else st[:SHIFT_KEYS]).max(axis=0, keepdims=True)
    return jnp.exp2(st - shift).astype(BF16)


def _pv(vt, e):
    d = vt.shape[0]
    ext = _dot(jnp.concatenate([vt, jnp.ones((ONES_ROWS, vt.shape[1]), BF16)], axis=0), e)
    return ext[:d], ext[d:d + 1]


def _denominators_bad(denoms):
    bad = None
    for l in denoms:
        b = jnp.where(jnp.logical_and(l > DENOM_RANGE[0], l < DENOM_RANGE[1]), 0.0, 1.0)
        bad = b if bad is None else jnp.maximum(bad, b)
    return (jnp.max(bad) > 0.0).astype(jnp.int32)


KA_W = DIFF_HEADS * LANES


def _stage_kv(k_all, vt_all, ka, va, kb, vb, cache_refs):
    seq = ka.shape[0]
    k_all[0:seq, 0:KA_W] = ka
    k_all[0:seq, KA_W:] = kb
    vt_all[0:KA_W, 0:seq] = va.astype(F32).T.astype(BF16)
    vt_all[KA_W:, 0:seq] = vb.astype(F32).T.astype(BF16)
    if cache_refs is not None:
        cka_ref, cva_ref, ckb_ref, cvb_ref = cache_refs
        past = cka_ref.shape[2]
        for h in range(DIFF_HEADS):
            cs = slice(h * LANES, (h + 1) * LANES)
            k_all[seq:, cs] = cka_ref[h].T.astype(BF16)
            vt_all[cs, seq:] = cva_ref[pl.ds(h, past, stride=DIFF_HEADS), :].T.astype(BF16)
        k_all[seq:, KA_W:] = jnp.concatenate([ckb_ref[0], ckb_ref[1]], axis=0).T.astype(BF16)
        for g in range(GQA_KV_HEADS):
            vt_all[KA_W + g * HEAD_DIM:KA_W + (g + 1) * HEAD_DIM, seq:] = cvb_ref[g].astype(BF16)


def _lambda(lq1, lk1, lq2, lk2, lam_init):
    return (jnp.exp(jnp.sum(lq1[...] * lk1[...], axis=-1, keepdims=True))
            - jnp.exp(jnp.sum(lq2[...] * lk2[...], axis=-1, keepdims=True)) + lam_init)


def _attention_tasks(qa, qb, k_all, vt_all, lam, subln, lam_init, store, denoms=None):
    tq = qa.shape[0]
    qat = qa.astype(F32).T.astype(BF16)
    qbt = qb.astype(F32).T.astype(BF16)
    exact = denoms is None
    tasks = []

    def first(st):
        return _numerators(st, exact)

    def pv(vt, e):
        num, l = _pv(vt, e)
        if not exact:
            denoms.append(l)
        return num, 1.0 / l

    def diff_finish(e, h, cs):
        num, r = pv(vt_all[cs, :], e)
        ot = num[:, :tq] * r[:, :tq] - num[:, tq:] * (r[:, tq:] * lam)
        ms = jnp.mean(ot * ot, axis=0, keepdims=True)
        ot = ot * lax.rsqrt(ms + EPS) * subln * (1.0 - lam_init)
        store(cs, ot.T)

    def gqa_finish(e, g, os):
        num, r = pv(vt_all[KA_W + g * HEAD_DIM:KA_W + (g + 1) * HEAD_DIM, :], e)
        ot = num * r
        store(os, jnp.concatenate([ot[:, :tq], ot[:, tq:]], axis=0).T)

    for h in range(DIFF_HEADS):
        cs = slice(h * LANES, (h + 1) * LANES)

        def logits(h=h, cs=cs):
            c0 = qat[h * LANES:h * LANES + HEAD_DIM]
            c1 = qat[h * LANES + HEAD_DIM:(h + 1) * LANES]
            return first(_dot(k_all[:, cs], jnp.concatenate([_pad_rows(c0, 0), _pad_rows(c1, 1)], axis=1)))

        tasks.append((logits, lambda e, h=h, cs=cs: diff_finish(e, h, cs)))
    for g in range(GQA_KV_HEADS):
        for rp in range(2):
            r0 = (GQA_REP * g + 2 * rp) * HEAD_DIM
            os = slice(KA_W + r0, KA_W + r0 + LANES)

            def logits(g=g, r0=r0):
                a = qbt[r0:r0 + HEAD_DIM]
                b = qbt[r0 + HEAD_DIM:r0 + LANES]
                return first(_dot(k_all[:, KA_W:], jnp.concatenate([_pad_rows(a, g), _pad_rows(b, g)], axis=1)))

            tasks.append((logits, lambda e, g=g, os=os: gqa_finish(e, g, os)))
    return tasks


LOGITS_AHEAD = 2
GUARDED_AHEAD = 5
N_TASKS = DIFF_HEADS + 2 * GQA_KV_HEADS


def _attention_units(tasks, ahead=LOGITS_AHEAD):
    pending = []

    def prefill(t):
        pending.append(t[0]())

    def step(i):
        if i + ahead < len(tasks):
            pending.append(tasks[i + ahead][0]())
        tasks[i][1](pending.pop(0))

    units = [functools.partial(prefill, t) for t in tasks[:ahead]]
    return units + [functools.partial(step, i) for i in range(len(tasks))]


def _ffn_units(r, x_ref, mix_ref, mod_ref, wo_ref, nf_ref, wgu_ref, wd_ref, o_ref):
    g_a = mod_ref[:, 2 * D_MODEL:3 * D_MODEL]
    sh_f = mod_ref[:, 3 * D_MODEL:4 * D_MODEL]
    sc_f = mod_ref[:, 4 * D_MODEL:5 * D_MODEL]
    g_f = mod_ref[:, 5 * D_MODEL:6 * D_MODEL]
    state = {}

    def attn_residual():
        state["x"] = x_ref[r, :] + g_a * _dot(mix_ref[r, :], wo_ref[...])

    def gate():
        state["h"] = (_rms(state["x"], nf_ref[...]) * (1.0 + sc_f) + sh_f).astype(BF16)
        state["gate"] = _dot(state["h"], wgu_ref[:, :FFN_HIDDEN])

    def up():
        g = state.pop("gate")
        state["act"] = (g * jax.nn.sigmoid(g) * _dot(state.pop("h"), wgu_ref[:, FFN_HIDDEN:])).astype(BF16)

    def ffn_residual():
        o_ref[r, :] = state.pop("x") + g_f * _dot(state.pop("act"), wd_ref[...])

    return [attn_residual, gate, up, ffn_residual]


def _interleave(a, b):
    out, ia, ib = [], 0, 0
    while ia < len(a) or ib < len(b):
        if ib >= len(b) or (ia < len(a) and ia * len(b) <= ib * len(a)):
            out.append(a[ia])
            ia += 1
        else:
            out.append(b[ib])
            ib += 1
    return out


def _attn_ffn_kernel(lam_init, sub, *refs):
    (qa_ref, qb_ref, ka_ref, va_ref, kb_ref, vb_ref, lq1, lk1, lq2, lk2, subln_ref,
     x_ref, mod_ref, wo_ref, nf_ref, wgu_ref, wd_ref, o_ref, k_all, vt_all, mix) = refs
    seq = ka_ref.shape[0] // sub
    lam = _lambda(lq1, lk1, lq2, lk2, lam_init)
    subln = subln_ref[...]
    attn, ffn = [], []
    for j in range(sub):
        r = slice(j * seq, (j + 1) * seq)

        later = []

        def store(cols, tile, r=r):
            mix[r, cols] = tile.astype(mix.dtype)

        def stage(j=j, r=r, later=later, store=store):
            _stage_kv(k_all.at[j], vt_all.at[j], ka_ref[r, :], va_ref[r, :], kb_ref[r, :], vb_ref[r, :], None)
            later.extend(_attention_units(_attention_tasks(
                qa_ref[r, :], qb_ref[r, :], k_all.at[j], vt_all.at[j], lam, subln, lam_init, store)))

        attn.append([stage] + [lambda k=k, later=later: later[k]() for k in range(LOGITS_AHEAD + N_TASKS)])
        ffn.append(_ffn_units(r, x_ref, mix, mod_ref, wo_ref, nf_ref, wgu_ref, wd_ref, o_ref))
    units = attn[0]
    for j in range(1, sub):
        units = units + _interleave(attn[j], ffn[j - 1])
    for unit in units + ffn[sub - 1]:
        unit()


def _attn_ffn(layer, x, qa, qb, own, seq, lam_vecs, subln, mod4, w_out, norm_ffn, w_gu, w_down):
    n = x.shape[0]
    sub = 4
    tm = sub * seq
    lam_init = 0.8 - 0.6 * math.exp(-0.3 * layer)
    tok = lambda w: pl.BlockSpec((tm, w), lambda i: (i, 0))
    kv_w = own[0].shape[1] + own[2].shape[1]
    in_specs = ([tok(512), tok(512)] + [tok(a.shape[1]) for a in own]
                + [_layer_spec((1, HEAD_DIM), layer)] * 4 + [_layer_spec((2 * HEAD_DIM, 1), layer)]
                + [tok(D_MODEL), _mod_spec(layer, False, 1),
                   _const_spec((D_MODEL, D_MODEL)), _layer_spec((1, D_MODEL), layer),
                   _const_spec((D_MODEL, 2 * FFN_HIDDEN)), _const_spec((FFN_HIDDEN, D_MODEL))])
    return pl.pallas_call(
        functools.partial(_attn_ffn_kernel, lam_init, sub),
        grid=(n // tm,),
        in_specs=in_specs,
        out_specs=tok(D_MODEL),
        out_shape=jax.ShapeDtypeStruct((n, D_MODEL), F32),
        scratch_shapes=[pltpu.VMEM((sub, seq, kv_w), BF16), pltpu.VMEM((sub, kv_w, seq), BF16),
                        pltpu.VMEM((tm, D_MODEL), BF16)],
        compiler_params=pltpu.CompilerParams(
            dimension_semantics=("arbitrary",), vmem_limit_bytes=VMEM_LIMIT),
        name="attn_ffn_context",
    )(qa, qb, *own, *lam_vecs, subln, x, mod4, w_out, norm_ffn, w_gu, w_down)


def _attn_ffn_lagged_kernel(lam_init, layer, tiles_per_request, n_tiles, *refs):
    (qa_ref, qb_ref, *kv_hbm) = refs[:10]
    (lq1, lk1, lq2, lk2, subln_ref, x_ref, mod_ref, wo_hbm, nf_ref, wgu_hbm, wd_hbm,
     o_ref, k_all, vt_all, mix, wo_ref, wgu_ref, wd_ref, wsem) = refs[10:29]
    kv_ref, kvsem, redo = refs[29:37], refs[37], refs[38]
    s = pl.program_id(0)
    tq = qa_ref.shape[0]
    seq = kv_ref[0].shape[0]
    requests = n_tiles // tiles_per_request

    def weight_copies():
        return [pltpu.make_async_copy(src, dst, wsem.at[k])
                for k, (src, dst) in enumerate(((wo_hbm, wo_ref), (wgu_hbm, wgu_ref), (wd_hbm, wd_ref)))]

    def kv_copies(b):
        srcs = [h.at[pl.ds(b * seq, seq), :] for h in kv_hbm[:4]] + [h.at[b, layer] for h in kv_hbm[4:]]
        return [pltpu.make_async_copy(src, dst, kvsem.at[k]) for k, (src, dst) in enumerate(zip(srcs, kv_ref))]

    @pl.when(s == 0)
    def _():
        for cp in kv_copies(0) + weight_copies():
            cp.start()

    @pl.when(s == 1)
    def _():
        for cp in weight_copies():
            cp.wait()

    nxt = s // tiles_per_request + 1

    @pl.when(jnp.logical_and(s % tiles_per_request == 1, nxt < requests))
    def _():
        for cp in kv_copies(nxt):
            cp.start()

    lam = _lambda(lq1, lk1, lq2, lk2, lam_init)
    subln = subln_ref[...]

    def attention_units(slot, exact=False):
        def store(cols, tile):
            mix[slot, :, cols] = tile.astype(mix.dtype)

        denoms = None if exact else []
        units = _attention_units(_attention_tasks(
            qa_ref[...], qb_ref[...], k_all, vt_all, lam, subln, lam_init, store, denoms),
            LOGITS_AHEAD if exact else GUARDED_AHEAD)
        if exact:
            return units

        def check():
            redo[0] = _denominators_bad(denoms)

        return units + [check]

    def ffn_units(slot):
        return _ffn_units(slice(0, tq), x_ref, mix.at[slot], mod_ref, wo_ref, nf_ref, wgu_ref, wd_ref, o_ref)

    @pl.when(jnp.logical_and(s % tiles_per_request == 0, s < n_tiles))
    def _():
        for cp in kv_copies(s // tiles_per_request):
            cp.wait()
        ka_ref, va_ref, kb_ref, vb_ref = kv_ref[:4]
        _stage_kv(k_all, vt_all, ka_ref[...], va_ref[...], kb_ref[...], vb_ref[...], kv_ref[4:])

    @pl.when(s == 0)
    def _():
        for unit in attention_units(0):
            unit()

    @pl.when(jnp.logical_and(s > 0, s < n_tiles))
    def _():
        for unit in _interleave(attention_units(s % 2), ffn_units((s - 1) % 2)):
            unit()

    @pl.when(jnp.logical_and(s < n_tiles, redo[0] != 0))
    def _():
        for unit in attention_units(s % 2, exact=True):
            unit()

    @pl.when(s == n_tiles)
    def _():
        for unit in ffn_units((n_tiles - 1) % 2):
            unit()


def _attn_ffn_lagged(layer, x, qa, qb, own, cache, seq, lam_vecs, subln, mod4, w_out, norm_ffn, w_gu, w_down):
    n = x.shape[0]
    tq = Q_TILE
    tpr = seq // tq
    n_tiles = n // tq
    lam_init = 0.8 - 0.6 * math.exp(-0.3 * layer)
    att = lambda s: jnp.minimum(s, n_tiles - 1)
    ffn = lambda s: jnp.maximum(s - 1, 0)
    assert tpr >= 2, "the next request's keys / values are fetched during a request's second tile"
    in_specs = [pl.BlockSpec((tq, 512), lambda s: (att(s), 0))] * 2
    in_specs += [pl.BlockSpec(memory_space=pl.ANY)] * (len(own) + len(cache))
    landing = ([pltpu.VMEM((seq, a.shape[1]), a.dtype) for a in own]
               + [pltpu.VMEM(a.shape[2:], a.dtype) for a in cache])
    in_specs += [_layer_spec((1, HEAD_DIM), layer)] * 4 + [_layer_spec((2 * HEAD_DIM, 1), layer)]
    in_specs += [pl.BlockSpec((tq, D_MODEL), lambda s: (ffn(s), 0)),
                 pl.BlockSpec((None, None, 1, 6 * D_MODEL), lambda s: (layer, 1 + ffn(s) // tpr, 0, 0)),
                 pl.BlockSpec(memory_space=pl.ANY), _layer_spec((1, D_MODEL), layer),
                 pl.BlockSpec(memory_space=pl.ANY), pl.BlockSpec(memory_space=pl.ANY)]
    keys = seq + cache[0].shape[-1]
    kv_w = own[0].shape[1] + own[2].shape[1]
    return pl.pallas_call(
        functools.partial(_attn_ffn_lagged_kernel, lam_init, layer, tpr, n_tiles),
        grid=(n_tiles + 1,),
        in_specs=in_specs,
        out_specs=pl.BlockSpec((tq, D_MODEL), lambda s: (ffn(s), 0)),
        out_shape=jax.ShapeDtypeStruct((n, D_MODEL), F32),
        scratch_shapes=[pltpu.VMEM((keys, kv_w), BF16), pltpu.VMEM((kv_w, keys), BF16),
                        pltpu.VMEM((2, tq, D_MODEL), BF16),
                        pltpu.VMEM(w_out.shape, BF16), pltpu.VMEM(w_gu.shape, BF16), pltpu.VMEM(w_down.shape, BF16),
                        pltpu.SemaphoreType.DMA((3,)), *landing, pltpu.SemaphoreType.DMA((len(landing),)),
                        pltpu.SMEM((1,), jnp.int32)],
        compiler_params=pltpu.CompilerParams(
            dimension_semantics=("arbitrary",), vmem_limit_bytes=VMEM_LIMIT),
        name="attn_ffn_latent",
    )(qa, qb, *own, *cache, *lam_vecs, subln, x, mod4, w_out, norm_ffn, w_gu, w_down)


def _rope_tables(n_tokens):
    t = jnp.arange(n_tokens, dtype=jnp.int32)
    row = (t // GRID_W).astype(F32)
    col = (t % GRID_W).astype(F32)
    axis_dim = HEAD_DIM // 2
    freqs = ROPE_THETA ** (-jnp.arange(0, axis_dim, 2, dtype=F32) / axis_dim)
    ang = jnp.concatenate([row[:, None] * freqs, col[:, None] * freqs], axis=-1)
    c, s = jnp.cos(ang), jnp.sin(ang)
    return jnp.concatenate([c, c, c, c], axis=-1), jnp.concatenate([-s, s, -s, s], axis=-1)


def kernel(x_prompt, x_sample, cache_diff_k, cache_diff_v, cache_gqa_k, cache_gqa_v, c, c_ctx, w_mod, b_mod, norm_attn, w_in, q_norm_a, k_norm_a, lambda_q1, lambda_k1, lambda_q2, lambda_k2, subln, q_norm_b, k_norm_b, w_out, norm_ffn, w_gate_up, w_down):
    batch, seq, d = x_prompt.shape
    dec_batch, dec_seq, _ = x_sample.shape
    depth = w_mod.shape[0]
    past = cache_diff_k.shape[2]

    cond = jnp.concatenate([c_ctx[None], c, jnp.zeros((MOD_ROWS - 1 - dec_batch, d), F32)], axis=0)
    mod = _modulation(cond, w_mod, b_mod)
    mod4 = mod.reshape(depth, MOD_ROWS, 1, 6 * d)

    w_in_b = w_in.astype(BF16)
    seg_ones = jnp.kron(jnp.eye(256 // HEAD_DIM, dtype=F32), jnp.ones((HEAD_DIM, HEAD_DIM), F32)).astype(BF16)
    rope = _rope_tables(dec_seq)

    cache = (jnp.transpose(cache_diff_k, (0, 1, 3, 4, 5, 2)).reshape(dec_batch, depth, DIFF_HEADS, 2 * HEAD_DIM, past),
             cache_diff_v.reshape(dec_batch, depth, past * DIFF_HEADS, 2 * HEAD_DIM),
             jnp.transpose(cache_gqa_k, (0, 1, 3, 4, 2)),
             jnp.transpose(cache_gqa_v, (0, 1, 3, 4, 2)))

    row = lambda a: a.reshape(depth, 1, a.shape[-1])
    gains = (row(q_norm_a), row(k_norm_a), row(q_norm_b), row(k_norm_b))
    lam_vecs = (row(lambda_q1), row(lambda_k1), row(lambda_q2), row(lambda_k2))
    na, nf, sl = row(norm_attn), row(norm_ffn), subln.reshape(depth, 2 * HEAD_DIM, 1)

    yp = x_prompt.reshape(batch * seq, d)
    ys = x_sample.reshape(dec_batch * dec_seq, d)
    lat_tiles = dec_seq // TOKEN_TILE
    new_caches = None
    for l in range(depth):
        qa, ka, va, qb, kb, vb, *rest = _qkv(
            yp, mod4, l, False, seq // TOKEN_TILE, na, w_in_b, seg_ones, gains, None, new_caches,
            (w_out, w_gate_up, w_down))
        new_caches, (w_out_b, w_gu_b, w_down_b) = rest[:4], rest[4:]
        yp = _attn_ffn(l, yp, qa, qb, (ka, va, kb, vb), seq, lam_vecs, sl, mod4, w_out_b, nf, w_gu_b, w_down_b)

        qa, ka, va, qb, kb, vb = _qkv(
            ys, mod4, l, True, lat_tiles, na, w_in_b, seg_ones, gains, rope)
        ys = _attn_ffn_lagged(l, ys, qa, qb, (ka, va, kb, vb), cache, dec_seq, lam_vecs, sl,
                              mod4, w_out_b, nf, w_gu_b, w_down_b)

    ka_t, va_n, kb_t, vb_t = new_caches
    new_diff_k = jnp.transpose(ka_t.reshape(batch, depth, DIFF_HEADS, 2, HEAD_DIM, seq), (0, 1, 5, 2, 3, 4))
    new_gqa_k = jnp.transpose(kb_t.reshape(batch, depth, GQA_KV_HEADS, HEAD_DIM, seq), (0, 1, 4, 2, 3))
    new_gqa_v = jnp.transpose(vb_t.reshape(batch, depth, GQA_KV_HEADS, HEAD_DIM, seq), (0, 1, 4, 2, 3))
    return (yp.reshape(batch, seq, d), ys.reshape(dec_batch, dec_seq, d),
            new_diff_k, va_n.reshape(batch, depth, seq, DIFF_HEADS, 2 * HEAD_DIM), new_gqa_k, new_gqa_v)
```

```python
import functools
import math

import jax
import jax.numpy as jnp
from jax import lax
from jax.experimental import pallas as pl
from jax.experimental.pallas import tpu as pltpu

F32 = jnp.float32
BF16 = jnp.bfloat16

D_MODEL = 1024
HEAD_DIM = 64
GRID_W = 64
DIFF_HEADS = 4
GQA_KV_HEADS = 2
GQA_REP = 4
FFN_HIDDEN = 2816
IN_COLS = 2304
ROPE_THETA = 10000.0
EPS = 1e-6
LANES = 128
MOD_ROWS = 8
TOKEN_TILE = 256
Q_TILE = 256
VMEM_LIMIT = 56 * 1024 * 1024


def _dot(a, b):
    return jnp.dot(a, b, preferred_element_type=F32)


def _rms(x, gain):
    ms = jnp.mean(x * x, axis=-1, keepdims=True)
    return x * lax.rsqrt(ms + EPS) * gain


def _layer_spec(shape, layer):
    n = len(shape)
    return pl.BlockSpec((None,) + tuple(shape), lambda *_: (layer,) + (0,) * n,
                        pipeline_mode=pl.Buffered(1))


def _const_spec(shape):
    n = len(shape)
    return pl.BlockSpec(shape, lambda *_: (0,) * n, pipeline_mode=pl.Buffered(1))


def _mod_spec(layer, latent, tiles_per_request):
    if latent:
        return pl.BlockSpec((None, None, 1, 6 * D_MODEL),
                            lambda i: (layer, 1 + i // tiles_per_request, 0, 0))
    return pl.BlockSpec((None, None, 1, 6 * D_MODEL), lambda i: (layer, 0, 0, 0))


def _mod_kernel(cond_ref, w_ref, b_ref, o_ref):
    c = cond_ref[...]
    s = c * jax.nn.sigmoid(c)
    o_ref[...] = _dot(s.astype(BF16), w_ref[...].astype(BF16)) + b_ref[...]


def _modulation(cond, w_mod, b_mod):
    depth = w_mod.shape[0]
    tn = 1536
    return pl.pallas_call(
        _mod_kernel,
        grid=(depth, 6 * D_MODEL // tn),
        in_specs=[
            pl.BlockSpec((MOD_ROWS, D_MODEL), lambda l, j: (0, 0)),
            pl.BlockSpec((None, D_MODEL, tn), lambda l, j: (l, 0, j)),
            pl.BlockSpec((None, 1, tn), lambda l, j: (l, 0, j)),
        ],
        out_specs=pl.BlockSpec((None, MOD_ROWS, tn), lambda l, j: (l, 0, j)),
        out_shape=jax.ShapeDtypeStruct((depth, MOD_ROWS, 6 * D_MODEL), F32),
        compiler_params=pltpu.CompilerParams(
            dimension_semantics=("arbitrary", "arbitrary"), vmem_limit_bytes=VMEM_LIMIT),
        name="modulation",
    )(cond, w_mod, b_mod.reshape(depth, 1, 6 * D_MODEL))


def _head_norm(x, seg_ones, gain):
    ss = _dot((x * x).astype(BF16), seg_ones)
    return x * lax.rsqrt(ss * (1.0 / HEAD_DIM) + EPS) * gain


def _swap_halves(x, first_half):
    return jnp.where(first_half, pltpu.roll(x, 96, 1), pltpu.roll(x, 32, 1))


def _rope(x, cos, sin, first_half):
    out = []
    for j in range(x.shape[1] // LANES):
        xj = x[:, j * LANES:(j + 1) * LANES]
        out.append(xj * cos + _swap_halves(xj, first_half) * sin)
    return out[0] if len(out) == 1 else jnp.concatenate(out, axis=1)


def _tile_gain(g):
    return jnp.concatenate([g] * (256 // HEAD_DIM), axis=1)


def _qkv_kernel(latent, n_cast, *refs):
    if latent:
        (x_ref, mod_ref, na_ref, w_ref, seg_ref, gqa_ref, gka_ref, gqb_ref, gkb_ref,
         cos_ref, sin_ref, qa_o, ka_o, va_o, qb_o, kb_o, vb_o) = refs
    else:
        (x_ref, mod_ref, na_ref, w_ref, seg_ref, gqa_ref, gka_ref, gqb_ref, gkb_ref) = refs[:9]
        outs = refs[len(refs) - 10 - n_cast:]
        qa_o, ka_o, va_o, qb_o, kb_o, vb_o, kaf_o, vaf_o, kbf_o, vbf_o = outs[:10]
        for src, dst in zip(refs[9:9 + n_cast], outs[10:]):
            dst[...] = src[...].astype(BF16)
    tm = TOKEN_TILE
    sh = mod_ref[:, 0:D_MODEL]
    sc = mod_ref[:, D_MODEL:2 * D_MODEL]
    seg = seg_ref[...]
    lane = lax.broadcasted_iota(jnp.int32, (1, LANES), 1)
    first_half = (lane % HEAD_DIM) < (HEAD_DIM // 2)
    scale = HEAD_DIM ** -0.5 * math.log2(math.e)

    def project(r):
        h = _rms(x_ref[r, :], na_ref[...]) * (1.0 + sc) + sh
        return _dot(h.astype(BF16), w_ref[...])

    def finish(j, r, proj):
        def qk(lo_col, width, gain_ref):
            gain = _tile_gain(gain_ref[...])
            out = []
            for c0 in range(0, width, 256):
                w = min(256, width - c0)
                out.append(_head_norm(proj[:, lo_col + c0:lo_col + c0 + w], seg[:w, :w], gain[:, :w]))
            y = out[0] if len(out) == 1 else jnp.concatenate(out, axis=1)
            if latent:
                y = _rope(y, cos_ref[r, :], sin_ref[r, :], first_half)
            return y

        qa = qk(0, 512, gqa_ref)
        ka = qk(512, 512, gka_ref)
        va = proj[:, 1024:1536]
        qb = qk(1536, 512, gqb_ref)
        kb = qk(2048, 128, gkb_ref)
        vb = proj[:, 2176:2304]
        qa_o[r, :] = (qa * scale).astype(BF16)
        ka_o[r, :] = ka.astype(BF16)
        va_o[r, :] = va.astype(BF16)
        qb_o[r, :] = (qb * scale).astype(BF16)
        kb_o[r, :] = kb.astype(BF16)
        vb_o[r, :] = vb.astype(BF16)
        if not latent:
            kaf_o[j] = ka.T
            kbf_o[j] = kb.T
            vbf_o[j] = vb.T
            for h in range(DIFF_HEADS):
                vaf_o[j, pl.ds(h, tm, stride=DIFF_HEADS), :] = va[:, h * LANES:(h + 1) * LANES]

    rows = [slice(j * tm, (j + 1) * tm) for j in range(x_ref.shape[0] // tm)]
    pending = [project(rows[0])]
    for j, r in enumerate(rows):
        if j + 1 < len(rows):
            pending.append(project(rows[j + 1]))
        finish(j, r, pending.pop(0))


def _qkv(x, mod4, layer, latent, tiles_per_request, norm_attn, w_in, seg_ones, gains, rope,
         prev_caches=None, cast_weights=()):
    n = x.shape[0]
    sub = 4
    tm = sub * TOKEN_TILE
    depth = w_in.shape[0]
    tok = lambda w: pl.BlockSpec((tm, w), lambda i: (i, 0))
    in_specs = [
        tok(D_MODEL),
        _mod_spec(layer, latent, max(tiles_per_request // sub, 1)),
        _layer_spec((1, D_MODEL), layer),
        _layer_spec((D_MODEL, IN_COLS), layer),
        _const_spec((256, 256)),
    ] + [_layer_spec((1, HEAD_DIM), layer)] * 4
    args = [x, mod4, norm_attn, w_in, seg_ones, *gains]
    aliases = {}
    out_specs = [tok(512), tok(512), tok(512), tok(512), tok(LANES), tok(LANES)]
    out_shape = [jax.ShapeDtypeStruct((n, w), BF16) for w in (512, 512, 512, 512, LANES, LANES)]
    if latent:
        rope_spec = pl.BlockSpec((tm, LANES), lambda i: (i % (tiles_per_request // sub), 0))
        in_specs += [rope_spec, rope_spec]
        args += list(rope)
    else:
        assert tiles_per_request == 1, "context requests must be one token tile long"
        seq = TOKEN_TILE
        req = n // seq
        slab = lambda *dims: pl.BlockSpec((sub, None) + dims, lambda i: (i, layer) + (0,) * len(dims))
        out_specs += [slab(512, seq), slab(seq * DIFF_HEADS, LANES), slab(LANES, seq), slab(LANES, seq)]
        out_shape += [jax.ShapeDtypeStruct((req, depth, 512, seq), F32),
                      jax.ShapeDtypeStruct((req, depth, seq * DIFF_HEADS, LANES), F32),
                      jax.ShapeDtypeStruct((req, depth, LANES, seq), F32),
                      jax.ShapeDtypeStruct((req, depth, LANES, seq), F32)]
        steps = n // tm
        for w in cast_weights:
            rows, cols = w.shape[1] // steps, w.shape[2]
            assert rows * steps == w.shape[1] and rows % 16 == 0
            in_specs.append(pl.BlockSpec((None, rows, cols), lambda i: (layer, i, 0)))
            args.append(w)
            out_specs.append(pl.BlockSpec((rows, cols), lambda i: (i, 0)))
            out_shape.append(jax.ShapeDtypeStruct(w.shape[1:], BF16))
        if prev_caches is not None:
            aliases = {len(args) + k: 6 + k for k in range(4)}
            in_specs += [pl.BlockSpec(memory_space=pl.ANY)] * 4
            args += list(prev_caches)
    return pl.pallas_call(
        functools.partial(_qkv_kernel, latent, len(cast_weights)),
        grid=(n // tm,),
        in_specs=in_specs,
        out_specs=out_specs,
        out_shape=out_shape,
        input_output_aliases=aliases,
        compiler_params=pltpu.CompilerParams(
            dimension_semantics=("arbitrary",), vmem_limit_bytes=VMEM_LIMIT),
        name="qkv_latent" if latent else "qkv_context",
    )(*args)


def _pad_rows(x, block):
    z = jnp.zeros_like(x)
    return jnp.concatenate([x, z] if block == 0 else [z, x], axis=0)


ONES_ROWS = 16
SHIFT_KEYS = 8
DENOM_RANGE = (2.0 ** -100, 2.0 ** 100)


def _numerators(st, exact):
    shift = (st if exact else st[:SHIFT_KEYS]).max(axis=0, keepdims=True)
    return jnp.exp2(st - shift).astype(BF16)


def _pv(vt, e):
    d = vt.shape[0]
    ext = _dot(jnp.concatenate([vt, jnp.ones((ONES_ROWS, vt.shape[1]), BF16)], axis=0), e)
    return ext[:d], ext[d:d + 1]


def _denominators_bad(denoms):
    bad = None
    for l in denoms:
        b = jnp.where(jnp.logical_and(l > DENOM_RANGE[0], l < DENOM_RANGE[1]), 0.0, 1.0)
        bad = b if bad is None else jnp.maximum(bad, b)
    return (jnp.max(bad) > 0.0).astype(jnp.int32)


KA_W = DIFF_HEADS * LANES


def _stage_kv(k_all, vt_all, ka, va, kb, vb, cache_refs):
    seq = ka.shape[0]
    k_all[0:seq, 0:KA_W] = ka
    k_all[0:seq, KA_W:] = kb
    vt_all[0:KA_W, 0:seq] = va.astype(F32).T.astype(BF16)
    vt_all[KA_W:, 0:seq] = vb.astype(F32).T.astype(BF16)
    if cache_refs is not None:
        cka_ref, cva_ref, ckb_ref, cvb_ref = cache_refs
        past = cka_ref.shape[2]
        for h in range(DIFF_HEADS):
            cs = slice(h * LANES, (h + 1) * LANES)
            k_all[seq:, cs] = cka_ref[h].T.astype(BF16)
            vt_all[cs, seq:] = cva_ref[pl.ds(h, past, stride=DIFF_HEADS), :].T.astype(BF16)
        k_all[seq:, KA_W:] = jnp.concatenate([ckb_ref[0], ckb_ref[1]], axis=0).T.astype(BF16)
        for g in range(GQA_KV_HEADS):
            vt_all[KA_W + g * HEAD_DIM:KA_W + (g + 1) * HEAD_DIM, seq:] = cvb_ref[g].astype(BF16)


def _lambda(lq1, lk1, lq2, lk2, lam_init):
    return (jnp.exp(jnp.sum(lq1[...] * lk1[...], axis=-1, keepdims=True))
            - jnp.exp(jnp.sum(lq2[...] * lk2[...], axis=-1, keepdims=True)) + lam_init)


def _attention_tasks(qa, qb, k_all, vt_all, lam, subln, lam_init, store, denoms=None):
    tq = qa.shape[0]
    qat = qa.astype(F32).T.astype(BF16)
    qbt = qb.astype(F32).T.astype(BF16)
    exact = denoms is None
    tasks = []

    def first(st):
        return _numerators(st, exact)

    def pv(vt, e):
        num, l = _pv(vt, e)
        if not exact:
            denoms.append(l)
        return num, 1.0 / l

    def diff_finish(e, h, cs):
        num, r = pv(vt_all[cs, :], e)
        ot = num[:, :tq] * r[:, :tq] - num[:, tq:] * (r[:, tq:] * lam)
        ms = jnp.mean(ot * ot, axis=0, keepdims=True)
        ot = ot * lax.rsqrt(ms + EPS) * subln * (1.0 - lam_init)
        store(cs, ot.T)

    def gqa_finish(e, g, os):
        num, r = pv(vt_all[KA_W + g * HEAD_DIM:KA_W + (g + 1) * HEAD_DIM, :], e)
        ot = num * r
        store(os, jnp.concatenate([ot[:, :tq], ot[:, tq:]], axis=0).T)

    for h in range(DIFF_HEADS):
        cs = slice(h * LANES, (h + 1) * LANES)

        def logits(h=h, cs=cs):
            c0 = qat[h * LANES:h * LANES + HEAD_DIM]
            c1 = qat[h * LANES + HEAD_DIM:(h + 1) * LANES]
            return first(_dot(k_all[:, cs], jnp.concatenate([_pad_rows(c0, 0), _pad_rows(c1, 1)], axis=1)))

        tasks.append((logits, lambda e, h=h, cs=cs: diff_finish(e, h, cs)))
    for g in range(GQA_KV_HEADS):
        for rp in range(2):
            r0 = (GQA_REP * g + 2 * rp) * HEAD_DIM
            os = slice(KA_W + r0, KA_W + r0 + LANES)

            def logits(g=g, r0=r0):
                a = qbt[r0:r0 + HEAD_DIM]
                b = qbt[r0 + HEAD_DIM:r0 + LANES]
                return first(_dot(k_all[:, KA_W:], jnp.concatenate([_pad_rows(a, g), _pad_rows(b, g)], axis=1)))

            tasks.append((logits, lambda e, g=g, os=os: gqa_finish(e, g, os)))
    return tasks


LOGITS_AHEAD = 2
GUARDED_AHEAD = 5
N_TASKS = DIFF_HEADS + 2 * GQA_KV_HEADS


def _attention_units(tasks, ahead=LOGITS_AHEAD):
    pending = []

    def prefill(t):
        pending.append(t[0]())

    def step(i):
        if i + ahead < len(tasks):
            pending.append(tasks[i + ahead][0]())
        tasks[i][1](pending.pop(0))

    units = [functools.partial(prefill, t) for t in tasks[:ahead]]
    return units + [functools.partial(step, i) for i in range(len(tasks))]


def _ffn_units(r, x_ref, mix_ref, mod_ref, wo_ref, nf_ref, wgu_ref, wd_ref, o_ref):
    g_a = mod_ref[:, 2 * D_MODEL:3 * D_MODEL]
    sh_f = mod_ref[:, 3 * D_MODEL:4 * D_MODEL]
    sc_f = mod_ref[:, 4 * D_MODEL:5 * D_MODEL]
    g_f = mod_ref[:, 5 * D_MODEL:6 * D_MODEL]
    state = {}

    def attn_residual():
        state["x"] = x_ref[r, :] + g_a * _dot(mix_ref[r, :], wo_ref[...])

    def gate():
        state["h"] = (_rms(state["x"], nf_ref[...]) * (1.0 + sc_f) + sh_f).astype(BF16)
        state["gate"] = _dot(state["h"], wgu_ref[:, :FFN_HIDDEN])

    def up():
        g = state.pop("gate")
        state["act"] = (g * jax.nn.sigmoid(g) * _dot(state.pop("h"), wgu_ref[:, FFN_HIDDEN:])).astype(BF16)

    def ffn_residual():
        o_ref[r, :] = state.pop("x") + g_f * _dot(state.pop("act"), wd_ref[...])

    return [attn_residual, gate, up, ffn_residual]


def _interleave(a, b):
    out, ia, ib = [], 0, 0
    while ia < len(a) or ib < len(b):
        if ib >= len(b) or (ia < len(a) and ia * len(b) <= ib * len(a)):
            out.append(a[ia])
            ia += 1
        else:
            out.append(b[ib])
            ib += 1
    return out


def _attn_ffn_kernel(lam_init, sub, *refs):
    (qa_ref, qb_ref, ka_ref, va_ref, kb_ref, vb_ref, lq1, lk1, lq2, lk2, subln_ref,
     x_ref, mod_ref, wo_ref, nf_ref, wgu_ref, wd_ref, o_ref, k_all, vt_all, mix) = refs
    seq = ka_ref.shape[0] // sub
    lam = _lambda(lq1, lk1, lq2, lk2, lam_init)
    subln = subln_ref[...]
    attn, ffn = [], []
    for j in range(sub):
        r = slice(j * seq, (j + 1) * seq)

        later = []

        def store(cols, tile, r=r):
            mix[r, cols] = tile.astype(mix.dtype)

        def stage(j=j, r=r, later=later, store=store):
            _stage_kv(k_all.at[j], vt_all.at[j], ka_ref[r, :], va_ref[r, :], kb_ref[r, :], vb_ref[r, :], None)
            later.extend(_attention_units(_attention_tasks(
                qa_ref[r, :], qb_ref[r, :], k_all.at[j], vt_all.at[j], lam, subln, lam_init, store)))

        attn.append([stage] + [lambda k=k, later=later: later[k]() for k in range(LOGITS_AHEAD + N_TASKS)])
        ffn.append(_ffn_units(r, x_ref, mix, mod_ref, wo_ref, nf_ref, wgu_ref, wd_ref, o_ref))
    units = attn[0]
    for j in range(1, sub):
        units = units + _interleave(attn[j], ffn[j - 1])
    for unit in units + ffn[sub - 1]:
        unit()


def _attn_ffn(layer, x, qa, qb, own, seq, lam_vecs, subln, mod4, w_out, norm_ffn, w_gu, w_down):
    n = x.shape[0]
    sub = 4
    tm = sub * seq
    lam_init = 0.8 - 0.6 * math.exp(-0.3 * layer)
    tok = lambda w: pl.BlockSpec((tm, w), lambda i: (i, 0))
    kv_w = own[0].shape[1] + own[2].shape[1]
    in_specs = ([tok(512), tok(512)] + [tok(a.shape[1]) for a in own]
                + [_layer_spec((1, HEAD_DIM), layer)] * 4 + [_layer_spec((2 * HEAD_DIM, 1), layer)]
                + [tok(D_MODEL), _mod_spec(layer, False, 1),
                   _const_spec((D_MODEL, D_MODEL)), _layer_spec((1, D_MODEL), layer),
                   _const_spec((D_MODEL, 2 * FFN_HIDDEN)), _const_spec((FFN_HIDDEN, D_MODEL))])
    return pl.pallas_call(
        functools.partial(_attn_ffn_kernel, lam_init, sub),
        grid=(n // tm,),
        in_specs=in_specs,
        out_specs=tok(D_MODEL),
        out_shape=jax.ShapeDtypeStruct((n, D_MODEL), F32),
        scratch_shapes=[pltpu.VMEM((sub, seq, kv_w), BF16), pltpu.VMEM((sub, kv_w, seq), BF16),
                        pltpu.VMEM((tm, D_MODEL), BF16)],
        compiler_params=pltpu.CompilerParams(
            dimension_semantics=("arbitrary",), vmem_limit_bytes=VMEM_LIMIT),
        name="attn_ffn_context",
    )(qa, qb, *own, *lam_vecs, subln, x, mod4, w_out, norm_ffn, w_gu, w_down)


def _attn_ffn_lagged_kernel(lam_init, layer, tiles_per_request, n_tiles, *refs):
    (qa_ref, qb_ref, *kv_hbm) = refs[:10]
    (lq1, lk1, lq2, lk2, subln_ref, x_ref, mod_ref, wo_hbm, nf_ref, wgu_hbm, wd_hbm,
     o_ref, k_all, vt_all, mix, wo_ref, wgu_ref, wd_ref, wsem) = refs[10:29]
    kv_ref, kvsem, redo = refs[29:37], refs[37], refs[38]
    s = pl.program_id(0)
    tq = qa_ref.shape[0]
    seq = kv_ref[0].shape[0]
    requests = n_tiles // tiles_per_request

    def weight_copies():
        return [pltpu.make_async_copy(src, dst, wsem.at[k])
                for k, (src, dst) in enumerate(((wo_hbm, wo_ref), (wgu_hbm, wgu_ref), (wd_hbm, wd_ref)))]

    def kv_copies(b):
        srcs = [h.at[pl.ds(b * seq, seq), :] for h in kv_hbm[:4]] + [h.at[b, layer] for h in kv_hbm[4:]]
        return [pltpu.make_async_copy(src, dst, kvsem.at[k]) for k, (src, dst) in enumerate(zip(srcs, kv_ref))]

    @pl.when(s == 0)
    def _():
        for k, cp in enumerate(kv_copies(0)):
            cp.start(priority=k % 2)
        for cp in weight_copies():
            cp.start(priority=1)

    @pl.when(s == 1)
    def _():
        for cp in weight_copies():
            cp.wait()

    nxt = s // tiles_per_request + 1

    @pl.when(jnp.logical_and(s % tiles_per_request == 1, nxt < requests))
    def _():
        for cp in kv_copies(nxt):
            cp.start()

    lam = _lambda(lq1, lk1, lq2, lk2, lam_init)
    subln = subln_ref[...]

    def attention_units(slot, exact=False):
        def store(cols, tile):
            mix[slot, :, cols] = tile.astype(mix.dtype)

        denoms = None if exact else []
        units = _attention_units(_attention_tasks(
            qa_ref[...], qb_ref[...], k_all, vt_all, lam, subln, lam_init, store, denoms),
            LOGITS_AHEAD if exact else GUARDED_AHEAD)
        if exact:
            return units

        def check():
            redo[0] = _denominators_bad(denoms)

        return units + [check]

    def ffn_units(slot):
        return _ffn_units(slice(0, tq), x_ref, mix.at[slot], mod_ref, wo_ref, nf_ref, wgu_ref, wd_ref, o_ref)

    @pl.when(jnp.logical_and(s % tiles_per_request == 0, s < n_tiles))
    def _():
        for cp in kv_copies(s // tiles_per_request):
            cp.wait()
        ka_ref, va_ref, kb_ref, vb_ref = kv_ref[:4]
        _stage_kv(k_all, vt_all, ka_ref[...], va_ref[...], kb_ref[...], vb_ref[...], kv_ref[4:])

    @pl.when(s == 0)
    def _():
        for unit in attention_units(0):
            unit()

    @pl.when(jnp.logical_and(s > 0, s < n_tiles))
    def _():
        for unit in _interleave(attention_units(s % 2), ffn_units((s - 1) % 2)):
            unit()

    @pl.when(jnp.logical_and(s < n_tiles, redo[0] != 0))
    def _():
        for unit in attention_units(s % 2, exact=True):
            unit()

    @pl.when(s == n_tiles)
    def _():
        for unit in ffn_units((n_tiles - 1) % 2):
            unit()


def _attn_ffn_lagged(layer, x, qa, qb, own, cache, seq, lam_vecs, subln, mod4, w_out, norm_ffn, w_gu, w_down):
    n = x.shape[0]
    tq = Q_TILE
    tpr = seq // tq
    n_tiles = n // tq
    lam_init = 0.8 - 0.6 * math.exp(-0.3 * layer)
    att = lambda s: jnp.minimum(s, n_tiles - 1)
    ffn = lambda s: jnp.maximum(s - 1, 0)
    assert tpr >= 2, "the next request's keys / values are fetched during a request's second tile"
    in_specs = [pl.BlockSpec((tq, 512), lambda s: (att(s), 0))] * 2
    in_specs += [pl.BlockSpec(memory_space=pl.ANY)] * (len(own) + len(cache))
    landing = ([pltpu.VMEM((seq, a.shape[1]), a.dtype) for a in own]
               + [pltpu.VMEM(a.shape[2:], a.dtype) for a in cache])
    in_specs += [_layer_spec((1, HEAD_DIM), layer)] * 4 + [_layer_spec((2 * HEAD_DIM, 1), layer)]
    in_specs += [pl.BlockSpec((tq, D_MODEL), lambda s: (ffn(s), 0)),
                 pl.BlockSpec((None, None, 1, 6 * D_MODEL), lambda s: (layer, 1 + ffn(s) // tpr, 0, 0)),
                 pl.BlockSpec(memory_space=pl.ANY), _layer_spec((1, D_MODEL), layer),
                 pl.BlockSpec(memory_space=pl.ANY), pl.BlockSpec(memory_space=pl.ANY)]
    keys = seq + cache[0].shape[-1]
    kv_w = own[0].shape[1] + own[2].shape[1]
    return pl.pallas_call(
        functools.partial(_attn_ffn_lagged_kernel, lam_init, layer, tpr, n_tiles),
        grid=(n_tiles + 1,),
        in_specs=in_specs,
        out_specs=pl.BlockSpec((tq, D_MODEL), lambda s: (ffn(s), 0)),
        out_shape=jax.ShapeDtypeStruct((n, D_MODEL), F32),
        scratch_shapes=[pltpu.VMEM((keys, kv_w), BF16), pltpu.VMEM((kv_w, keys), BF16),
                        pltpu.VMEM((2, tq, D_MODEL), BF16),
                        pltpu.VMEM(w_out.shape, BF16), pltpu.VMEM(w_gu.shape, BF16), pltpu.VMEM(w_down.shape, BF16),
                        pltpu.SemaphoreType.DMA((3,)), *landing, pltpu.SemaphoreType.DMA((len(landing),)),
                        pltpu.SMEM((1,), jnp.int32)],
        compiler_params=pltpu.CompilerParams(
            dimension_semantics=("arbitrary",), vmem_limit_bytes=VMEM_LIMIT),
        name="attn_ffn_latent",
    )(qa, qb, *own, *cache, *lam_vecs, subln, x, mod4, w_out, norm_ffn, w_gu, w_down)


def _rope_tables(n_tokens):
    t = jnp.arange(n_tokens, dtype=jnp.int32)
    row = (t // GRID_W).astype(F32)
    col = (t % GRID_W).astype(F32)
    axis_dim = HEAD_DIM // 2
    freqs = ROPE_THETA ** (-jnp.arange(0, axis_dim, 2, dtype=F32) / axis_dim)
    ang = jnp.concatenate([row[:, None] * freqs, col[:, None] * freqs], axis=-1)
    c, s = jnp.cos(ang), jnp.sin(ang)
    return jnp.concatenate([c, c, c, c], axis=-1), jnp.concatenate([-s, s, -s, s], axis=-1)


def kernel(x_prompt, x_sample, cache_diff_k, cache_diff_v, cache_gqa_k, cache_gqa_v, c, c_ctx, w_mod, b_mod, norm_attn, w_in, q_norm_a, k_norm_a, lambda_q1, lambda_k1, lambda_q2, lambda_k2, subln, q_norm_b, k_norm_b, w_out, norm_ffn, w_gate_up, w_down):
    batch, seq, d = x_prompt.shape
    dec_batch, dec_seq, _ = x_sample.shape
    depth = w_mod.shape[0]
    past = cache_diff_k.shape[2]

    cond = jnp.concatenate([c_ctx[None], c, jnp.zeros((MOD_ROWS - 1 - dec_batch, d), F32)], axis=0)
    mod = _modulation(cond, w_mod, b_mod)
    mod4 = mod.reshape(depth, MOD_ROWS, 1, 6 * d)

    w_in_b = w_in.astype(BF16)
    seg_ones = jnp.kron(jnp.eye(256 // HEAD_DIM, dtype=F32), jnp.ones((HEAD_DIM, HEAD_DIM), F32)).astype(BF16)
    rope = _rope_tables(dec_seq)

    cache = (jnp.transpose(cache_diff_k, (0, 1, 3, 4, 5, 2)).reshape(dec_batch, depth, DIFF_HEADS, 2 * HEAD_DIM, past),
             cache_diff_v.reshape(dec_batch, depth, past * DIFF_HEADS, 2 * HEAD_DIM),
             jnp.transpose(cache_gqa_k, (0, 1, 3, 4, 2)),
             jnp.transpose(cache_gqa_v, (0, 1, 3, 4, 2)))

    row = lambda a: a.reshape(depth, 1, a.shape[-1])
    gains = (row(q_norm_a), row(k_norm_a), row(q_norm_b), row(k_norm_b))
    lam_vecs = (row(lambda_q1), row(lambda_k1), row(lambda_q2), row(lambda_k2))
    na, nf, sl = row(norm_attn), row(norm_ffn), subln.reshape(depth, 2 * HEAD_DIM, 1)

    yp = x_prompt.reshape(batch * seq, d)
    ys = x_sample.reshape(dec_batch * dec_seq, d)
    lat_tiles = dec_seq // TOKEN_TILE
    new_caches = None
    for l in range(depth):
        qa, ka, va, qb, kb, vb, *rest = _qkv(
            yp, mod4, l, False, seq // TOKEN_TILE, na, w_in_b, seg_ones, gains, None, new_caches,
            (w_out, w_gate_up, w_down))
        new_caches, (w_out_b, w_gu_b, w_down_b) = rest[:4], rest[4:]
        yp = _attn_ffn(l, yp, qa, qb, (ka, va, kb, vb), seq, lam_vecs, sl, mod4, w_out_b, nf, w_gu_b, w_down_b)

        qa, ka, va, qb, kb, vb = _qkv(
            ys, mod4, l, True, lat_tiles, na, w_in_b, seg_ones, gains, rope)
        ys = _attn_ffn_lagged(l, ys, qa, qb, (ka, va, kb, vb), cache, dec_seq, lam_vecs, sl,
                              mod4, w_out_b, nf, w_gu_b, w_down_b)

    ka_t, va_n, kb_t, vb_t = new_caches
    new_diff_k = jnp.transpose(ka_t.reshape(batch, depth, DIFF_HEADS, 2, HEAD_DIM, seq), (0, 1, 5, 2, 3, 4))
    new_gqa_k = jnp.transpose(kb_t.reshape(batch, depth, GQA_KV_HEADS, HEAD_DIM, seq), (0, 1, 4, 2, 3))
    new_gqa_v = jnp.transpose(vb_t.reshape(batch, depth, GQA_KV_HEADS, HEAD_DIM, seq), (0, 1, 4, 2, 3))
    return (yp.reshape(batch, seq, d), ys.reshape(dec_batch, dec_seq, d),
            new_diff_k, va_n.reshape(batch, depth, seq, DIFF_HEADS, 2 * HEAD_DIM), new_gqa_k, new_gqa_v)
```

```python
import functools
import math

import jax
import jax.numpy as jnp
from jax import lax
from jax.experimental import pallas as pl
from jax.experimental.pallas import tpu as pltpu

F32 = jnp.float32
BF16 = jnp.bfloat16

D_MODEL = 1024
HEAD_DIM = 64
GRID_W = 64
DIFF_HEADS = 4
GQA_KV_HEADS = 2
GQA_REP = 4
FFN_HIDDEN = 2816
IN_COLS = 2304
ROPE_THETA = 10000.0
EPS = 1e-6
LANES = 128
MOD_ROWS = 8
TOKEN_TILE = 256
Q_TILE = 256
VMEM_LIMIT = 56 * 1024 * 1024


def _dot(a, b):
    return jnp.dot(a, b, preferred_element_type=F32)


def _rms(x, gain):
    ms = jnp.mean(x * x, axis=-1, keepdims=True)
    return x * lax.rsqrt(ms + EPS) * gain


def _layer_spec(shape, layer):
    n = len(shape)
    return pl.BlockSpec((None,) + tuple(shape), lambda *_: (layer,) + (0,) * n,
                        pipeline_mode=pl.Buffered(1))


def _const_spec(shape):
    n = len(shape)
    return pl.BlockSpec(shape, lambda *_: (0,) * n, pipeline_mode=pl.Buffered(1))


def _mod_spec(layer, latent, tiles_per_request):
    if latent:
        return pl.BlockSpec((None, None, 1, 6 * D_MODEL),
                            lambda i: (layer, 1 + i // tiles_per_request, 0, 0))
    return pl.BlockSpec((None, None, 1, 6 * D_MODEL), lambda i: (layer, 0, 0, 0))


def _mod_kernel(cond_ref, w_ref, b_ref, o_ref):
    c = cond_ref[...]
    s = c * jax.nn.sigmoid(c)
    o_ref[...] = _dot(s.astype(BF16), w_ref[...].astype(BF16)) + b_ref[...]


def _modulation(cond, w_mod, b_mod):
    depth = w_mod.shape[0]
    tn = 1536
    return pl.pallas_call(
        _mod_kernel,
        grid=(depth, 6 * D_MODEL // tn),
        in_specs=[
            pl.BlockSpec((MOD_ROWS, D_MODEL), lambda l, j: (0, 0)),
            pl.BlockSpec((None, D_MODEL, tn), lambda l, j: (l, 0, j)),
            pl.BlockSpec((None, 1, tn), lambda l, j: (l, 0, j)),
        ],
        out_specs=pl.BlockSpec((None, MOD_ROWS, tn), lambda l, j: (l, 0, j)),
        out_shape=jax.ShapeDtypeStruct((depth, MOD_ROWS, 6 * D_MODEL), F32),
        compiler_params=pltpu.CompilerParams(
            dimension_semantics=("arbitrary", "arbitrary"), vmem_limit_bytes=VMEM_LIMIT),
        name="modulation",
    )(cond, w_mod, b_mod.reshape(depth, 1, 6 * D_MODEL))


def _head_norm(x, seg_ones, gain):
    ss = _dot((x * x).astype(BF16), seg_ones)
    return x * lax.rsqrt(ss * (1.0 / HEAD_DIM) + EPS) * gain


def _swap_halves(x, first_half):
    return jnp.where(first_half, pltpu.roll(x, 96, 1), pltpu.roll(x, 32, 1))


def _rope(x, cos, sin, first_half):
    out = []
    for j in range(x.shape[1] // LANES):
        xj = x[:, j * LANES:(j + 1) * LANES]
        out.append(xj * cos + _swap_halves(xj, first_half) * sin)
    return out[0] if len(out) == 1 else jnp.concatenate(out, axis=1)


def _tile_gain(g):
    return jnp.concatenate([g] * (256 // HEAD_DIM), axis=1)


def _qkv_kernel(latent, n_cast, *refs):
    if latent:
        (x_ref, mod_ref, na_ref, w_ref, seg_ref, gqa_ref, gka_ref, gqb_ref, gkb_ref,
         cos_ref, sin_ref, qa_o, ka_o, va_o, qb_o, kb_o, vb_o) = refs
    else:
        (x_ref, mod_ref, na_ref, w_ref, seg_ref, gqa_ref, gka_ref, gqb_ref, gkb_ref) = refs[:9]
        outs = refs[len(refs) - 10 - n_cast:]
        qa_o, ka_o, va_o, qb_o, kb_o, vb_o, kaf_o, vaf_o, kbf_o, vbf_o = outs[:10]
        for src, dst in zip(refs[9:9 + n_cast], outs[10:]):
            dst[...] = src[...].astype(BF16)
    tm = TOKEN_TILE
    sh = mod_ref[:, 0:D_MODEL]
    sc = mod_ref[:, D_MODEL:2 * D_MODEL]
    seg = seg_ref[...]
    lane = lax.broadcasted_iota(jnp.int32, (1, LANES), 1)
    first_half = (lane % HEAD_DIM) < (HEAD_DIM // 2)
    scale = HEAD_DIM ** -0.5 * math.log2(math.e)

    def project(r):
        h = _rms(x_ref[r, :], na_ref[...]) * (1.0 + sc) + sh
        return _dot(h.astype(BF16), w_ref[...])

    def finish(j, r, proj):
        def qk(lo_col, width, gain_ref):
            gain = _tile_gain(gain_ref[...])
            out = []
            for c0 in range(0, width, 256):
                w = min(256, width - c0)
                out.append(_head_norm(proj[:, lo_col + c0:lo_col + c0 + w], seg[:w, :w], gain[:, :w]))
            y = out[0] if len(out) == 1 else jnp.concatenate(out, axis=1)
            if latent:
                y = _rope(y, cos_ref[r, :], sin_ref[r, :], first_half)
            return y

        qa = qk(0, 512, gqa_ref)
        ka = qk(512, 512, gka_ref)
        va = proj[:, 1024:1536]
        qb = qk(1536, 512, gqb_ref)
        kb = qk(2048, 128, gkb_ref)
        vb = proj[:, 2176:2304]
        qa_o[r, :] = (qa * scale).astype(BF16)
        ka_o[r, :] = ka.astype(BF16)
        va_o[r, :] = va.astype(BF16)
        qb_o[r, :] = (qb * scale).astype(BF16)
        kb_o[r, :] = kb.astype(BF16)
        vb_o[r, :] = vb.astype(BF16)
        if not latent:
            kaf_o[j] = ka.T
            kbf_o[j] = kb.T
            vbf_o[j] = vb.T
            for h in range(DIFF_HEADS):
                vaf_o[j, pl.ds(h, tm, stride=DIFF_HEADS), :] = va[:, h * LANES:(h + 1) * LANES]

    rows = [slice(j * tm, (j + 1) * tm) for j in range(x_ref.shape[0] // tm)]
    pending = [project(rows[0])]
    for j, r in enumerate(rows):
        if j + 1 < len(rows):
            pending.append(project(rows[j + 1]))
        finish(j, r, pending.pop(0))


def _qkv(x, mod4, layer, latent, tiles_per_request, norm_attn, w_in, seg_ones, gains, rope,
         prev_caches=None, cast_weights=()):
    n = x.shape[0]
    sub = 4
    tm = sub * TOKEN_TILE
    depth = w_in.shape[0]
    tok = lambda w: pl.BlockSpec((tm, w), lambda i: (i, 0))
    in_specs = [
        tok(D_MODEL),
        _mod_spec(layer, latent, max(tiles_per_request // sub, 1)),
        _layer_spec((1, D_MODEL), layer),
        _layer_spec((D_MODEL, IN_COLS), layer),
        _const_spec((256, 256)),
    ] + [_layer_spec((1, HEAD_DIM), layer)] * 4
    args = [x, mod4, norm_attn, w_in, seg_ones, *gains]
    aliases = {}
    out_specs = [tok(512), tok(512), tok(512), tok(512), tok(LANES), tok(LANES)]
    out_shape = [jax.ShapeDtypeStruct((n, w), BF16) for w in (512, 512, 512, 512, LANES, LANES)]
    if latent:
        rope_spec = pl.BlockSpec((tm, LANES), lambda i: (i % (tiles_per_request // sub), 0))
        in_specs += [rope_spec, rope_spec]
        args += list(rope)
    else:
        assert tiles_per_request == 1, "context requests must be one token tile long"
        seq = TOKEN_TILE
        req = n // seq
        slab = lambda *dims: pl.BlockSpec((sub, None) + dims, lambda i: (i, layer) + (0,) * len(dims))
        out_specs += [slab(512, seq), slab(seq * DIFF_HEADS, LANES), slab(LANES, seq), slab(LANES, seq)]
        out_shape += [jax.ShapeDtypeStruct((req, depth, 512, seq), F32),
                      jax.ShapeDtypeStruct((req, depth, seq * DIFF_HEADS, LANES), F32),
                      jax.ShapeDtypeStruct((req, depth, LANES, seq), F32),
                      jax.ShapeDtypeStruct((req, depth, LANES, seq), F32)]
        steps = n // tm
        for w in cast_weights:
            rows, cols = w.shape[1] // steps, w.shape[2]
            assert rows * steps == w.shape[1] and rows % 16 == 0
            in_specs.append(pl.BlockSpec((None, rows, cols), lambda i: (layer, i, 0)))
            args.append(w)
            out_specs.append(pl.BlockSpec((rows, cols), lambda i: (i, 0)))
            out_shape.append(jax.ShapeDtypeStruct(w.shape[1:], BF16))
        if prev_caches is not None:
            aliases = {len(args) + k: 6 + k for k in range(4)}
            in_specs += [pl.BlockSpec(memory_space=pl.ANY)] * 4
            args += list(prev_caches)
    return pl.pallas_call(
        functools.partial(_qkv_kernel, latent, len(cast_weights)),
        grid=(n // tm,),
        in_specs=in_specs,
        out_specs=out_specs,
        out_shape=out_shape,
        input_output_aliases=aliases,
        compiler_params=pltpu.CompilerParams(
            dimension_semantics=("arbitrary",), vmem_limit_bytes=VMEM_LIMIT),
        name="qkv_latent" if latent else "qkv_context",
    )(*args)


def _pad_rows(x, block):
    z = jnp.zeros_like(x)
    return jnp.concatenate([x, z] if block == 0 else [z, x], axis=0)


ONES_ROWS = 16
SHIFT_KEYS = 8
DENOM_RANGE = (2.0 ** -100, 2.0 ** 100)


def _numerators(st, exact):
    shift = (st if exact else st[:SHIFT_KEYS]).max(axis=0, keepdims=True)
    return jnp.exp2(st - shift).astype(BF16)


def _pv(vt, e):
    d = vt.shape[0]
    ext = _dot(jnp.concatenate([vt, jnp.ones((ONES_ROWS, vt.shape[1]), BF16)], axis=0), e)
    return ext[:d], ext[d:d + 1]


def _denominators_bad(denoms):
    bad = None
    for l in denoms:
        b = jnp.where(jnp.logical_and(l > DENOM_RANGE[0], l < DENOM_RANGE[1]), 0.0, 1.0)
        bad = b if bad is None else jnp.maximum(bad, b)
    return (jnp.max(bad) > 0.0).astype(jnp.int32)


KA_W = DIFF_HEADS * LANES


def _stage_kv(k_all, vt_all, ka, va, kb, vb, cache_refs):
    seq = ka.shape[0]
    k_all[0:seq, 0:KA_W] = ka
    k_all[0:seq, KA_W:] = kb
    vt_all[0:KA_W, 0:seq] = va.astype(F32).T.astype(BF16)
    vt_all[KA_W:, 0:seq] = vb.astype(F32).T.astype(BF16)
    if cache_refs is not None:
        cka_ref, cva_ref, ckb_ref, cvb_ref = cache_refs
        past = cka_ref.shape[2]
        for h in range(DIFF_HEADS):
            cs = slice(h * LANES, (h + 1) * LANES)
            k_all[seq:, cs] = cka_ref[h].T.astype(BF16)
            vt_all[cs, seq:] = cva_ref[pl.ds(h, past, stride=DIFF_HEADS), :].T.astype(BF16)
        k_all[seq:, KA_W:] = jnp.concatenate([ckb_ref[0], ckb_ref[1]], axis=0).T.astype(BF16)
        for g in range(GQA_KV_HEADS):
            vt_all[KA_W + g * HEAD_DIM:KA_W + (g + 1) * HEAD_DIM, seq:] = cvb_ref[g].astype(BF16)


def _lambda(lq1, lk1, lq2, lk2, lam_init):
    return (jnp.exp(jnp.sum(lq1[...] * lk1[...], axis=-1, keepdims=True))
            - jnp.exp(jnp.sum(lq2[...] * lk2[...], axis=-1, keepdims=True)) + lam_init)


def _attention_tasks(qa, qb, k_all, vt_all, lam, subln, lam_init, store, denoms=None):
    tq = qa.shape[0]
    qat = qa.astype(F32).T.astype(BF16)
    qbt = qb.astype(F32).T.astype(BF16)
    exact = denoms is None
    tasks = []

    def first(st):
        return _numerators(st, exact)

    def pv(vt, e):
        num, l = _pv(vt, e)
        if not exact:
            denoms.append(l)
        return num, 1.0 / l

    def diff_finish(e, h, cs):
        num, r = pv(vt_all[cs, :], e)
        ot = num[:, :tq] * r[:, :tq] - num[:, tq:] * (r[:, tq:] * lam)
        ms = jnp.mean(ot * ot, axis=0, keepdims=True)
        ot = ot * lax.rsqrt(ms + EPS) * subln * (1.0 - lam_init)
        store(cs, ot.T)

    def gqa_finish(e, g, os):
        num, r = pv(vt_all[KA_W + g * HEAD_DIM:KA_W + (g + 1) * HEAD_DIM, :], e)
        ot = num * r
        store(os, jnp.concatenate([ot[:, :tq], ot[:, tq:]], axis=0).T)

    for h in range(DIFF_HEADS):
        cs = slice(h * LANES, (h + 1) * LANES)

        def logits(h=h, cs=cs):
            c0 = qat[h * LANES:h * LANES + HEAD_DIM]
            c1 = qat[h * LANES + HEAD_DIM:(h + 1) * LANES]
            return first(_dot(k_all[:, cs], jnp.concatenate([_pad_rows(c0, 0), _pad_rows(c1, 1)], axis=1)))

        tasks.append((logits, lambda e, h=h, cs=cs: diff_finish(e, h, cs)))
    for g in range(GQA_KV_HEADS):
        for rp in range(2):
            r0 = (GQA_REP * g + 2 * rp) * HEAD_DIM
            os = slice(KA_W + r0, KA_W + r0 + LANES)

            def logits(g=g, r0=r0):
                a = qbt[r0:r0 + HEAD_DIM]
                b = qbt[r0 + HEAD_DIM:r0 + LANES]
                return first(_dot(k_all[:, KA_W:], jnp.concatenate([_pad_rows(a, g), _pad_rows(b, g)], axis=1)))

            tasks.append((logits, lambda e, g=g, os=os: gqa_finish(e, g, os)))
    return tasks


LOGITS_AHEAD = 2
GUARDED_AHEAD = 5
N_TASKS = DIFF_HEADS + 2 * GQA_KV_HEADS


def _attention_units(tasks, ahead=LOGITS_AHEAD):
    pending = []

    def prefill(t):
        pending.append(t[0]())

    def step(i):
        if i + ahead < len(tasks):
            pending.append(tasks[i + ahead][0]())
        tasks[i][1](pending.pop(0))

    units = [functools.partial(prefill, t) for t in tasks[:ahead]]
    return units + [functools.partial(step, i) for i in range(len(tasks))]


def _ffn_units(r, x_ref, mix_ref, mod_ref, wo_ref, nf_ref, wgu_ref, wd_ref, o_ref):
    g_a = mod_ref[:, 2 * D_MODEL:3 * D_MODEL]
    sh_f = mod_ref[:, 3 * D_MODEL:4 * D_MODEL]
    sc_f = mod_ref[:, 4 * D_MODEL:5 * D_MODEL]
    g_f = mod_ref[:, 5 * D_MODEL:6 * D_MODEL]
    state = {}

    def attn_residual():
        state["x"] = x_ref[r, :] + g_a * _dot(mix_ref[r, :], wo_ref[...])

    def gate():
        state["h"] = (_rms(state["x"], nf_ref[...]) * (1.0 + sc_f) + sh_f).astype(BF16)
        state["gate"] = _dot(state["h"], wgu_ref[:, :FFN_HIDDEN])

    def up():
        g = state.pop("gate")
        state["act"] = (g * jax.nn.sigmoid(g) * _dot(state.pop("h"), wgu_ref[:, FFN_HIDDEN:])).astype(BF16)

    def ffn_residual():
        o_ref[r, :] = state.pop("x") + g_f * _dot(state.pop("act"), wd_ref[...])

    return [attn_residual, gate, up, ffn_residual]


def _interleave(a, b):
    out, ia, ib = [], 0, 0
    while ia < len(a) or ib < len(b):
        if ib >= len(b) or (ia < len(a) and ia * len(b) <= ib * len(a)):
            out.append(a[ia])
            ia += 1
        else:
            out.append(b[ib])
            ib += 1
    return out


def _attn_ffn_kernel(lam_init, sub, *refs):
    (qa_ref, qb_ref, ka_ref, va_ref, kb_ref, vb_ref, lq1, lk1, lq2, lk2, subln_ref,
     x_ref, mod_ref, wo_ref, nf_ref, wgu_ref, wd_ref, o_ref, k_all, vt_all, mix) = refs
    seq = ka_ref.shape[0] // sub
    lam = _lambda(lq1, lk1, lq2, lk2, lam_init)
    subln = subln_ref[...]
    attn, ffn = [], []
    for j in range(sub):
        r = slice(j * seq, (j + 1) * seq)

        later = []

        def store(cols, tile, r=r):
            mix[r, cols] = tile.astype(mix.dtype)

        def stage(j=j, r=r, later=later, store=store):
            _stage_kv(k_all.at[j], vt_all.at[j], ka_ref[r, :], va_ref[r, :], kb_ref[r, :], vb_ref[r, :], None)
            later.extend(_attention_units(_attention_tasks(
                qa_ref[r, :], qb_ref[r, :], k_all.at[j], vt_all.at[j], lam, subln, lam_init, store)))

        attn.append([stage] + [lambda k=k, later=later: later[k]() for k in range(LOGITS_AHEAD + N_TASKS)])
        ffn.append(_ffn_units(r, x_ref, mix, mod_ref, wo_ref, nf_ref, wgu_ref, wd_ref, o_ref))
    units = attn[0]
    for j in range(1, sub):
        units = units + _interleave(attn[j], ffn[j - 1])
    for unit in units + ffn[sub - 1]:
        unit()


def _attn_ffn(layer, x, qa, qb, own, seq, lam_vecs, subln, mod4, w_out, norm_ffn, w_gu, w_down):
    n = x.shape[0]
    sub = 4
    tm = sub * seq
    lam_init = 0.8 - 0.6 * math.exp(-0.3 * layer)
    tok = lambda w: pl.BlockSpec((tm, w), lambda i: (i, 0))
    kv_w = own[0].shape[1] + own[2].shape[1]
    in_specs = ([tok(512), tok(512)] + [tok(a.shape[1]) for a in own]
                + [_layer_spec((1, HEAD_DIM), layer)] * 4 + [_layer_spec((2 * HEAD_DIM, 1), layer)]
                + [tok(D_MODEL), _mod_spec(layer, False, 1),
                   _const_spec((D_MODEL, D_MODEL)), _layer_spec((1, D_MODEL), layer),
                   _const_spec((D_MODEL, 2 * FFN_HIDDEN)), _const_spec((FFN_HIDDEN, D_MODEL))])
    return pl.pallas_call(
        functools.partial(_attn_ffn_kernel, lam_init, sub),
        grid=(n // tm,),
        in_specs=in_specs,
        out_specs=tok(D_MODEL),
        out_shape=jax.ShapeDtypeStruct((n, D_MODEL), F32),
        scratch_shapes=[pltpu.VMEM((sub, seq, kv_w), BF16), pltpu.VMEM((sub, kv_w, seq), BF16),
                        pltpu.VMEM((tm, D_MODEL), BF16)],
        compiler_params=pltpu.CompilerParams(
            dimension_semantics=("arbitrary",), vmem_limit_bytes=VMEM_LIMIT),
        name="attn_ffn_context",
    )(qa, qb, *own, *lam_vecs, subln, x, mod4, w_out, norm_ffn, w_gu, w_down)


def _attn_ffn_lagged_kernel(lam_init, layer, tiles_per_request, n_tiles, *refs):
    (qa_ref, qb_ref, *kv_hbm) = refs[:10]
    (lq1, lk1, lq2, lk2, subln_ref, x_ref, mod_ref, wo_hbm, nf_ref, wgu_hbm, wd_hbm,
     o_ref, k_all, vt_all, mix, wo_ref, wgu_ref, wd_ref, wsem) = refs[10:29]
    kv_ref, kvsem, redo = refs[29:37], refs[37], refs[38]
    s = pl.program_id(0)
    tq = qa_ref.shape[0]
    seq = kv_ref[0].shape[0]
    requests = n_tiles // tiles_per_request

    def weight_copies():
        return [pltpu.make_async_copy(src, dst, wsem.at[k])
                for k, (src, dst) in enumerate(((wo_hbm, wo_ref), (wgu_hbm, wgu_ref), (wd_hbm, wd_ref)))]

    def kv_copies(b):
        srcs = [h.at[pl.ds(b * seq, seq), :] for h in kv_hbm[:4]] + [h.at[b, layer] for h in kv_hbm[4:]]
        return [pltpu.make_async_copy(src, dst, kvsem.at[k]) for k, (src, dst) in enumerate(zip(srcs, kv_ref))]

    @pl.when(s == 0)
    def _():
        for cp in kv_copies(0):
            cp.start()

    @pl.when(s == 1)
    def _():
        for cp in weight_copies():
            cp.wait()

    nxt = s // tiles_per_request + 1

    @pl.when(jnp.logical_and(s % tiles_per_request == 1, nxt < requests))
    def _():
        for cp in kv_copies(nxt):
            cp.start()

    lam = _lambda(lq1, lk1, lq2, lk2, lam_init)
    subln = subln_ref[...]

    def attention_units(slot, exact=False):
        def store(cols, tile):
            mix[slot, :, cols] = tile.astype(mix.dtype)

        denoms = None if exact else []
        units = _attention_units(_attention_tasks(
            qa_ref[...], qb_ref[...], k_all, vt_all, lam, subln, lam_init, store, denoms),
            LOGITS_AHEAD if exact else GUARDED_AHEAD)
        if exact:
            return units

        def check():
            redo[0] = _denominators_bad(denoms)

        return units + [check]

    def ffn_units(slot):
        return _ffn_units(slice(0, tq), x_ref, mix.at[slot], mod_ref, wo_ref, nf_ref, wgu_ref, wd_ref, o_ref)

    @pl.when(jnp.logical_and(s % tiles_per_request == 0, s < n_tiles))
    def _():
        for cp in kv_copies(s // tiles_per_request):
            cp.wait()
        ka_ref, va_ref, kb_ref, vb_ref = kv_ref[:4]
        _stage_kv(k_all, vt_all, ka_ref[...], va_ref[...], kb_ref[...], vb_ref[...], kv_ref[4:])

    @pl.when(s == 0)
    def _():
        for cp in weight_copies():
            cp.start()

        for unit in attention_units(0):
            unit()

    @pl.when(jnp.logical_and(s > 0, s < n_tiles))
    def _():
        for unit in _interleave(attention_units(s % 2), ffn_units((s - 1) % 2)):
            unit()

    @pl.when(jnp.logical_and(s < n_tiles, redo[0] != 0))
    def _():
        for unit in attention_units(s % 2, exact=True):
            unit()

    @pl.when(s == n_tiles)
    def _():
        for unit in ffn_units((n_tiles - 1) % 2):
            unit()


def _attn_ffn_lagged(layer, x, qa, qb, own, cache, seq, lam_vecs, subln, mod4, w_out, norm_ffn, w_gu, w_down):
    n = x.shape[0]
    tq = Q_TILE
    tpr = seq // tq
    n_tiles = n // tq
    lam_init = 0.8 - 0.6 * math.exp(-0.3 * layer)
    att = lambda s: jnp.minimum(s, n_tiles - 1)
    ffn = lambda s: jnp.maximum(s - 1, 0)
    assert tpr >= 2, "the next request's keys / values are fetched during a request's second tile"
    in_specs = [pl.BlockSpec((tq, 512), lambda s: (att(s), 0))] * 2
    in_specs += [pl.BlockSpec(memory_space=pl.ANY)] * (len(own) + len(cache))
    landing = ([pltpu.VMEM((seq, a.shape[1]), a.dtype) for a in own]
               + [pltpu.VMEM(a.shape[2:], a.dtype) for a in cache])
    in_specs += [_layer_spec((1, HEAD_DIM), layer)] * 4 + [_layer_spec((2 * HEAD_DIM, 1), layer)]
    in_specs += [pl.BlockSpec((tq, D_MODEL), lambda s: (ffn(s), 0)),
                 pl.BlockSpec((None, None, 1, 6 * D_MODEL), lambda s: (layer, 1 + ffn(s) // tpr, 0, 0)),
                 pl.BlockSpec(memory_space=pl.ANY), _layer_spec((1, D_MODEL), layer),
                 pl.BlockSpec(memory_space=pl.ANY), pl.BlockSpec(memory_space=pl.ANY)]
    keys = seq + cache[0].shape[-1]
    kv_w = own[0].shape[1] + own[2].shape[1]
    return pl.pallas_call(
        functools.partial(_attn_ffn_lagged_kernel, lam_init, layer, tpr, n_tiles),
        grid=(n_tiles + 1,),
        in_specs=in_specs,
        out_specs=pl.BlockSpec((tq, D_MODEL), lambda s: (ffn(s), 0)),
        out_shape=jax.ShapeDtypeStruct((n, D_MODEL), F32),
        scratch_shapes=[pltpu.VMEM((keys, kv_w), BF16), pltpu.VMEM((kv_w, keys), BF16),
                        pltpu.VMEM((2, tq, D_MODEL), BF16),
                        pltpu.VMEM(w_out.shape, BF16), pltpu.VMEM(w_gu.shape, BF16), pltpu.VMEM(w_down.shape, BF16),
                        pltpu.SemaphoreType.DMA((3,)), *landing, pltpu.SemaphoreType.DMA((len(landing),)),
                        pltpu.SMEM((1,), jnp.int32)],
        compiler_params=pltpu.CompilerParams(
            dimension_semantics=("arbitrary",), vmem_limit_bytes=VMEM_LIMIT),
        name="attn_ffn_latent",
    )(qa, qb, *own, *cache, *lam_vecs, subln, x, mod4, w_out, norm_ffn, w_gu, w_down)


def _rope_tables(n_tokens):
    t = jnp.arange(n_tokens, dtype=jnp.int32)
    row = (t // GRID_W).astype(F32)
    col = (t % GRID_W).astype(F32)
    axis_dim = HEAD_DIM // 2
    freqs = ROPE_THETA ** (-jnp.arange(0, axis_dim, 2, dtype=F32) / axis_dim)
    ang = jnp.concatenate([row[:, None] * freqs, col[:, None] * freqs], axis=-1)
    c, s = jnp.cos(ang), jnp.sin(ang)
    return jnp.concatenate([c, c, c, c], axis=-1), jnp.concatenate([-s, s, -s, s], axis=-1)


def kernel(x_prompt, x_sample, cache_diff_k, cache_diff_v, cache_gqa_k, cache_gqa_v, c, c_ctx, w_mod, b_mod, norm_attn, w_in, q_norm_a, k_norm_a, lambda_q1, lambda_k1, lambda_q2, lambda_k2, subln, q_norm_b, k_norm_b, w_out, norm_ffn, w_gate_up, w_down):
    batch, seq, d = x_prompt.shape
    dec_batch, dec_seq, _ = x_sample.shape
    depth = w_mod.shape[0]
    past = cache_diff_k.shape[2]

    cond = jnp.concatenate([c_ctx[None], c, jnp.zeros((MOD_ROWS - 1 - dec_batch, d), F32)], axis=0)
    mod = _modulation(cond, w_mod, b_mod)
    mod4 = mod.reshape(depth, MOD_ROWS, 1, 6 * d)

    w_in_b = w_in.astype(BF16)
    seg_ones = jnp.kron(jnp.eye(256 // HEAD_DIM, dtype=F32), jnp.ones((HEAD_DIM, HEAD_DIM), F32)).astype(BF16)
    rope = _rope_tables(dec_seq)

    cache = (jnp.transpose(cache_diff_k, (0, 1, 3, 4, 5, 2)).reshape(dec_batch, depth, DIFF_HEADS, 2 * HEAD_DIM, past),
             cache_diff_v.reshape(dec_batch, depth, past * DIFF_HEADS, 2 * HEAD_DIM),
             jnp.transpose(cache_gqa_k, (0, 1, 3, 4, 2)),
             jnp.transpose(cache_gqa_v, (0, 1, 3, 4, 2)))

    row = lambda a: a.reshape(depth, 1, a.shape[-1])
    gains = (row(q_norm_a), row(k_norm_a), row(q_norm_b), row(k_norm_b))
    lam_vecs = (row(lambda_q1), row(lambda_k1), row(lambda_q2), row(lambda_k2))
    na, nf, sl = row(norm_attn), row(norm_ffn), subln.reshape(depth, 2 * HEAD_DIM, 1)

    yp = x_prompt.reshape(batch * seq, d)
    ys = x_sample.reshape(dec_batch * dec_seq, d)
    lat_tiles = dec_seq // TOKEN_TILE
    new_caches = None
    for l in range(depth):
        qa, ka, va, qb, kb, vb, *rest = _qkv(
            yp, mod4, l, False, seq // TOKEN_TILE, na, w_in_b, seg_ones, gains, None, new_caches,
            (w_out, w_gate_up, w_down))
        new_caches, (w_out_b, w_gu_b, w_down_b) = rest[:4], rest[4:]
        yp = _attn_ffn(l, yp, qa, qb, (ka, va, kb, vb), seq, lam_vecs, sl, mod4, w_out_b, nf, w_gu_b, w_down_b)

        qa, ka, va, qb, kb, vb = _qkv(
            ys, mod4, l, True, lat_tiles, na, w_in_b, seg_ones, gains, rope)
        ys = _attn_ffn_lagged(l, ys, qa, qb, (ka, va, kb, vb), cache, dec_seq, lam_vecs, sl,
                              mod4, w_out_b, nf, w_gu_b, w_down_b)

    ka_t, va_n, kb_t, vb_t = new_caches
    new_diff_k = jnp.transpose(ka_t.reshape(batch, depth, DIFF_HEADS, 2, HEAD_DIM, seq), (0, 1, 5, 2, 3, 4))
    new_gqa_k = jnp.transpose(kb_t.reshape(batch, depth, GQA_KV_HEADS, HEAD_DIM, seq), (0, 1, 4, 2, 3))
    new_gqa_v = jnp.transpose(vb_t.reshape(batch, depth, GQA_KV_HEADS, HEAD_DIM, seq), (0, 1, 4, 2, 3))
    return (yp.reshape(batch, seq, d), ys.reshape(dec_batch, dec_seq, d),
            new_diff_k, va_n.reshape(batch, depth, seq, DIFF_HEADS, 2 * HEAD_DIM), new_gqa_k, new_gqa_v)
```
